```python
import math
import jax, jax.numpy as jnp
from jax import lax
import numpy as np

D_MODEL = 1024
BATCH = 8
SEQ = 2048
DEPTH = 1
DEC_BATCH = 128
DEC_SEQ = 4
PAST_LEN = 2048
PAGE_SIZE = 128

NSA_HEADS = 8
NSA_KV = 2
NSA_HD = 64
NSA_REP = NSA_HEADS // NSA_KV
CMP_LEN = 32
CMP_STRIDE = 16
CMP_HID = 64
SEL_BLOCK = 64
SEL_TOPN = 8
WINDOW = 512
Q_BLOCK = 128
HG_HEADS = 4
HG_DK = 128
HG_DV = 128
HG_CHUNK = 64
N_BUCKETS = 32
MAX_DIST = 128
EPS = 1e-6
NEG = -1e30
FORCED = 1e9

A_WIDTH = NSA_HEADS * NSA_HD
A_KV = NSA_KV * NSA_HD
B_WIDTH = HG_HEADS * HG_DV
B_KEY = HG_HEADS * HG_DK
SPLIT_SIZES = (A_WIDTH, 6 * A_KV, 3 * NSA_HEADS, A_WIDTH, B_KEY, B_KEY, B_WIDTH, B_WIDTH, D_MODEL, D_MODEL)
IN_COLS = sum(SPLIT_SIZES)

kernel_name = 'nsa_hgrn2_gated_hybrid_step'


def rmsnorm(x, g):
    x32 = x.astype(jnp.float32)
    return x32 * lax.rsqrt(jnp.mean(x32 * x32, axis=-1, keepdims=True) + EPS) * g.astype(jnp.float32)


def t5_bucket(rel):
    n = jnp.maximum(rel, 0)
    exact = N_BUCKETS // 2
    nf = jnp.maximum(n, 1).astype(jnp.float32)
    large = exact + (jnp.log(nf / exact) / math.log(MAX_DIST / exact) * (N_BUCKETS - exact)).astype(jnp.int32)
    large = jnp.minimum(large, N_BUCKETS - 1)
    return jnp.where(n < exact, n, large)


def masked_softmax(s, mask):
    s = jnp.where(mask, s.astype(jnp.float32), NEG)
    p = jax.nn.softmax(s, axis=-1)
    return jnp.where(mask, p, 0.0)


def compress(kv, pos, w1, w2):
    B, L = kv.shape[:2]
    nc = (L - CMP_LEN) // CMP_STRIDE + 1
    idx = np.arange(nc)[:, None] * CMP_STRIDE + np.arange(CMP_LEN)[None, :]
    blk = kv[:, idx] + pos[:, None, :]
    blk = jnp.swapaxes(blk, 2, 3).reshape(B, nc, NSA_KV, CMP_LEN * NSA_HD)
    out = jax.nn.silu(blk @ w1) @ w2
    end = jnp.asarray(np.arange(nc) * CMP_STRIDE + CMP_LEN - 1, jnp.int32)
    return out, end


def cover_matrix(nc, nb):
    cs = np.arange(nc) * CMP_STRIDE
    ce = cs + CMP_LEN
    bs = np.arange(nb) * SEL_BLOCK
    be = bs + SEL_BLOCK
    ov = np.clip(np.minimum(ce[:, None], be[None, :]) - np.maximum(cs[:, None], bs[None, :]), 0, None) / CMP_STRIDE
    return jnp.asarray(ov, jnp.float32)


def nsa_context(kc, vc, ks, vs, pos_k, w1_k, w2_k, pos_v, w1_v, w2_v):
    B, L = kc.shape[:2]
    k_cmp, cend = compress(kc, pos_k, w1_k, w2_k)
    v_cmp, _ = compress(vc, pos_v, w1_v, w2_v)
    pad = (-L) % SEL_BLOCK
    nb = (L + pad) // SEL_BLOCK
    padw = ((0, 0), (0, pad), (0, 0), (0, 0))
    ks_blk = jnp.pad(ks, padw).reshape(B, nb, SEL_BLOCK, NSA_KV, NSA_HD)
    vs_blk = jnp.pad(vs, padw).reshape(B, nb, SEL_BLOCK, NSA_KV, NSA_HD)
    return (k_cmp, v_cmp, cend, cover_matrix(k_cmp.shape[1], nb), ks_blk, vs_blk)


def nsa_attend(q, gates, qpos, kc, vc, cmp_end, cover, ks_blk, vs_blk, kw, vw, kwpos, rel_bias):
    B, T = q.shape[:2]
    qg = q.reshape(B, T, NSA_KV, NSA_REP, NSA_HD).transpose(0, 2, 3, 1, 4) * (NSA_HD ** -0.5)
    table = rel_bias.T.reshape(NSA_KV, NSA_REP, N_BUCKETS)
    sc = jnp.einsum('bgrtd,bngd->bgrtn', qg, kc) + table[:, :, t5_bucket(qpos[:, None] - cmp_end[None, :])]
    pc = masked_softmax(sc, cmp_end[None, :] <= qpos[:, None])
    o_cmp = jnp.einsum('bgrtn,bngd->bgrtd', pc, vc)
    imp = jnp.einsum('bgrtn,nj->bgtj', pc, cover)
    nb = ks_blk.shape[1]
    j = jnp.arange(nb)[None, :]
    cur = (qpos // SEL_BLOCK)[:, None]
    valid = j * SEL_BLOCK <= qpos[:, None]
    forced = (j == 0) | (j == cur) | (j == cur - 1)
    score = jnp.where(valid, jnp.where(forced, FORCED, imp), -1.0)
    _, idx = lax.top_k(score, min(SEL_TOPN, nb))
    n_sel = idx.shape[-1]
    sel_ok = idx * SEL_BLOCK <= qpos[None, None, :, None]
    b_ix = jnp.arange(B)[:, None, None, None]
    g_ix = jnp.arange(NSA_KV)[None, :, None, None]
    ksg = ks_blk.transpose(0, 3, 1, 2, 4)[b_ix, g_ix, idx]
    vsg = vs_blk.transpose(0, 3, 1, 2, 4)[b_ix, g_ix, idx]
    kpos = idx[..., None] * SEL_BLOCK + jnp.arange(SEL_BLOCK)
    ms = (kpos <= qpos[None, None, :, None, None]) & sel_ok[..., None]
    bk = t5_bucket(qpos[None, None, :, None, None] - kpos)
    bias_s = table[jnp.arange(NSA_KV).reshape(1, NSA_KV, 1, 1, 1, 1),
                   jnp.arange(NSA_REP).reshape(1, 1, NSA_REP, 1, 1, 1), bk[:, :, None]]
    ss = jnp.einsum('bgrtd,bgtnsd->bgrtns', qg, ksg) + bias_s
    nk = n_sel * SEL_BLOCK
    ps = masked_softmax(ss.reshape(B, NSA_KV, NSA_REP, T, nk), ms.reshape(B, NSA_KV, 1, T, nk))
    o_sel = jnp.einsum('bgrtk,bgtkd->bgrtd', ps, vsg.reshape(B, NSA_KV, T, nk, NSA_HD))
    rel_w = qpos[:, None] - kwpos[None, :]
    mw = (rel_w >= 0) & (rel_w < WINDOW) & (kwpos[None, :] >= 0)
    sw = jnp.einsum('bgrtd,blgd->bgrtl', qg, kw) + table[:, :, t5_bucket(rel_w)]
    o_win = jnp.einsum('bgrtl,blgd->bgrtd', masked_softmax(sw, mw), vw)
    g = jax.nn.sigmoid(gates.astype(jnp.float32)).reshape(B, T, 3, NSA_KV, NSA_REP).transpose(2, 0, 3, 4, 1)[..., None]
    o = g[0] * o_cmp + g[1] * o_sel + g[2] * o_win
    return o.transpose(0, 3, 1, 2, 4).reshape(B, T, A_WIDTH)


def hgrn2_scan(q, k, v, g, S0):
    B, T, H, _ = q.shape
    C = math.gcd(T, HG_CHUNK)
    nC = T // C
    def to_chunks(a):
        return a.reshape(B, nC, C, H, a.shape[-1]).transpose(1, 0, 3, 2, 4)
    tri = jnp.tril(jnp.ones((C, C), bool))[:, :, None]
    def step(S, xs):
        qc, kc, vc, gc = xs
        G = jnp.cumsum(gc, axis=2)
        o_inter = jnp.einsum('bhtd,bhde->bhte', qc * jnp.exp(G), S)
        D = jnp.where(tri, G[:, :, :, None, :] - G[:, :, None, :, :], -jnp.inf)
        A = jnp.einsum('bhtd,bhtsd,bhsd->bhts', qc, jnp.exp(D), kc)
        o = o_inter + jnp.einsum('bhts,bhse->bhte', A, vc)
        Gl = G[:, :, -1, :]
        S = jnp.exp(Gl)[..., None] * S + jnp.einsum('bhsd,bhse->bhde', kc * jnp.exp(Gl[:, :, None, :] - G), vc)
        return S, o
    S, o = lax.scan(step, S0, (to_chunks(q), to_chunks(k), to_chunks(v), to_chunks(g)))
    return o.transpose(1, 0, 3, 2, 4).reshape(B, T, H, -1), S


def hgrn2_branch(qB, fB, iB, zB, S0, lb, norm_g):
    B, T, _ = qB.shape
    shp = (B, T, HG_HEADS, HG_DK)
    lbh = lb.reshape(HG_HEADS, HG_DK)
    f = lbh + (1.0 - lbh) * jax.nn.sigmoid(fB.astype(jnp.float32).reshape(shp))
    o, S = hgrn2_scan(qB.astype(jnp.float32).reshape(shp), 1.0 - f,
                      iB.astype(jnp.float32).reshape(B, T, HG_HEADS, HG_DV), jnp.log(f), S0)
    o = o * lax.rsqrt(jnp.mean(o * o, axis=-1, keepdims=True) + EPS)
    return o.reshape(B, T, B_WIDTH) * norm_g * jax.nn.silu(zB), S


def project_inputs(x, norm_g, w_in):
    offs = np.cumsum(SPLIT_SIZES)[:-1].tolist()
    return jnp.split(rmsnorm(x, norm_g) @ w_in, offs, axis=-1)


def merge_out(x, oA, oB, mA, mB, w_ba, w_bb, w_out):
    y = jax.nn.sigmoid(mA) * (oA @ w_ba) + jax.nn.sigmoid(mB) * (oB @ w_bb)
    return x + y @ w_out


def layer_prompt(x, norm_g, w_in, pos_k, w1_k, w2_k, pos_v, w1_v, w2_v, rel_bias, lb, hg_norm_g, w_ba, w_bb, w_out):
    B, T, _ = x.shape
    qA, kvA, gA, zA, qB, fB, iB, zB, mA, mB = project_inputs(x, norm_g, w_in)
    q = qA.reshape(B, T, NSA_HEADS, NSA_HD)
    kc, vc, ks, vs, kw, vw = [a.reshape(B, T, NSA_KV, NSA_HD) for a in jnp.split(kvA, 6, axis=-1)]
    ctx = nsa_context(kc, vc, ks, vs, pos_k, w1_k, w2_k, pos_v, w1_v, w2_v)
    padw = ((0, 0), (WINDOW, 0), (0, 0), (0, 0))
    kw_pad = jnp.pad(kw, padw)
    vw_pad = jnp.pad(vw, padw)
    nq = T // Q_BLOCK
    qb = jnp.swapaxes(q.reshape(B, nq, Q_BLOCK, NSA_HEADS, NSA_HD), 0, 1)
    gb = jnp.swapaxes(gA.reshape(B, nq, Q_BLOCK, 3 * NSA_HEADS), 0, 1)
    def one_block(args):
        q_i, g_i, b = args
        start = b * Q_BLOCK
        qpos = start + jnp.arange(Q_BLOCK)
        kwb = lax.dynamic_slice_in_dim(kw_pad, start, WINDOW + Q_BLOCK, axis=1)
        vwb = lax.dynamic_slice_in_dim(vw_pad, start, WINDOW + Q_BLOCK, axis=1)
        kwpos = start - WINDOW + jnp.arange(WINDOW + Q_BLOCK)
        return nsa_attend(q_i, g_i, qpos, *ctx, kwb, vwb, kwpos, rel_bias)
    oA = lax.map(one_block, (qb, gb, jnp.arange(nq)))
    oA = jnp.swapaxes(oA, 0, 1).reshape(B, T, A_WIDTH) * jax.nn.silu(zA)
    S0 = jnp.zeros((B, HG_HEADS, HG_DK, HG_DV), jnp.float32)
    oB, S = hgrn2_branch(qB, fB, iB, zB, S0, lb, hg_norm_g)
    h = merge_out(x, oA, oB, mA, mB, w_ba, w_bb, w_out)
    wb = min(WINDOW, T)
    return h, (kc, vc, ks, vs, kw[:, T - wb:], vw[:, T - wb:], S)


def layer_sample(x, ck, cv, sk, sv, wk, wv, S0, page_table, norm_g, w_in, pos_k, w1_k, w2_k, pos_v, w1_v, w2_v,
                 rel_bias, lb, hg_norm_g, w_ba, w_bb, w_out):
    B, T, _ = x.shape
    qA, kvA, gA, zA, qB, fB, iB, zB, mA, mB = project_inputs(x, norm_g, w_in)
    q = qA.reshape(B, T, NSA_HEADS, NSA_HD)
    kc, vc, ks, vs, kw, vw = [a.reshape(B, T, NSA_KV, NSA_HD) for a in jnp.split(kvA, 6, axis=-1)]
    def with_past(cache, new):
        past = cache[page_table].reshape(B, -1, NSA_KV, NSA_HD)
        return jnp.concatenate([past, new], axis=1)
    ctx = nsa_context(with_past(ck, kc), with_past(cv, vc), with_past(sk, ks), with_past(sv, vs),
                      pos_k, w1_k, w2_k, pos_v, w1_v, w2_v)
    wb = wk.shape[1]
    kw_all = jnp.concatenate([wk, kw], axis=1)
    vw_all = jnp.concatenate([wv, vw], axis=1)
    kwpos = PAST_LEN - wb + jnp.arange(wb + T)
    qpos = PAST_LEN + jnp.arange(T)
    oA = nsa_attend(q, gA, qpos, *ctx, kw_all, vw_all, kwpos, rel_bias) * jax.nn.silu(zA)
    oB, S = hgrn2_branch(qB, fB, iB, zB, S0.astype(jnp.float32), lb, hg_norm_g)
    h = merge_out(x, oA, oB, mA, mB, w_ba, w_bb, w_out)
    return h, (kc, vc, ks, vs, kw_all[:, T:], vw_all[:, T:], S)


def setup_inputs(seed: int = 0) -> dict:
    key = jax.random.key(seed)
    k = jax.random.split(key, 32)
    n_pages = PAST_LEN // PAGE_SIZE
    n_pool = (DEC_BATCH * n_pages * 5) // 4
    win_buf = min(WINDOW, PAST_LEN)
    def nrm(kk, shape, s):
        return jax.random.normal(kk, shape, jnp.float32) * s
    pshape = (DEPTH, n_pool, PAGE_SIZE, NSA_KV, NSA_HD)
    wshape = (DEPTH, DEC_BATCH, win_buf, NSA_KV, NSA_HD)
    page_table = jax.random.permutation(k[9], n_pool)[:DEC_BATCH * n_pages].reshape(DEC_BATCH, n_pages).astype(jnp.int32)
    return {
        'x_prompt': nrm(k[0], (BATCH, SEQ, D_MODEL), 1.0),
        'x_sample': nrm(k[1], (DEC_BATCH, DEC_SEQ, D_MODEL), 1.0),
        'cache_cmp_k': nrm(k[2], pshape, 1.0),
        'cache_cmp_v': nrm(k[3], pshape, 1.0),
        'cache_sel_k': nrm(k[4], pshape, 1.0),
        'cache_sel_v': nrm(k[5], pshape, 1.0),
        'state_win_k': nrm(k[6], wshape, 1.0),
        'state_win_v': nrm(k[7], wshape, 1.0),
        'state_hgrn': nrm(k[8], (DEPTH, DEC_BATCH, HG_HEADS, HG_DK, HG_DV), 1.0),
        'page_table': page_table,
        'norm_g': 1.0 + nrm(k[10], (DEPTH, D_MODEL), 0.02),
        'w_in': nrm(k[11], (DEPTH, D_MODEL, IN_COLS), D_MODEL ** -0.5),
        'cmp_pos_k': nrm(k[12], (DEPTH, CMP_LEN, NSA_HD), 0.5),
        'cmp_w1_k': nrm(k[13], (DEPTH, CMP_LEN * NSA_HD, CMP_HID), (CMP_LEN * NSA_HD) ** -0.5),
        'cmp_w2_k': nrm(k[14], (DEPTH, CMP_HID, NSA_HD), 1.5 * CMP_HID ** -0.5),
        'cmp_pos_v': nrm(k[15], (DEPTH, CMP_LEN, NSA_HD), 0.5),
        'cmp_w1_v': nrm(k[16], (DEPTH, CMP_LEN * NSA_HD, CMP_HID), (CMP_LEN * NSA_HD) ** -0.5),
        'cmp_w2_v': nrm(k[17], (DEPTH, CMP_HID, NSA_HD), 1.5 * CMP_HID ** -0.5),
        'rel_bias': nrm(k[18], (N_BUCKETS, NSA_HEADS), 0.5),
        'hg_lower': nrm(k[19], (DEPTH + 1, B_KEY), 1.0),
        'hg_norm_g': 1.0 + nrm(k[20], (DEPTH, B_WIDTH), 0.02),
        'w_branch_a': nrm(k[21], (DEPTH, A_WIDTH, D_MODEL), A_WIDTH ** -0.5),
        'w_branch_b': nrm(k[22], (DEPTH, B_WIDTH, D_MODEL), B_WIDTH ** -0.5),
        'w_out': nrm(k[23], (DEPTH, D_MODEL, D_MODEL), D_MODEL ** -0.5),
        'final_g': 1.0 + nrm(k[24], (D_MODEL,), 0.02),
    }


def reference(x_prompt, x_sample, cache_cmp_k, cache_cmp_v, cache_sel_k, cache_sel_v, state_win_k, state_win_v,
              state_hgrn, page_table, norm_g, w_in, cmp_pos_k, cmp_w1_k, cmp_w2_k, cmp_pos_v, cmp_w1_v, cmp_w2_v,
              rel_bias, hg_lower, hg_norm_g, w_branch_a, w_branch_b, w_out, final_g):
    lower = jnp.cumsum(jax.nn.softmax(hg_lower.astype(jnp.float32), axis=0), axis=0)
    hp, hs = x_prompt, x_sample
    p_list, s_list = [], []
    for l in range(DEPTH):
        hp, st_p = layer_prompt(hp, norm_g[l], w_in[l], cmp_pos_k[l], cmp_w1_k[l], cmp_w2_k[l], cmp_pos_v[l],
                                cmp_w1_v[l], cmp_w2_v[l], rel_bias, lower[l], hg_norm_g[l],
                                w_branch_a[l], w_branch_b[l], w_out[l])
        hs, st_s = layer_sample(hs, cache_cmp_k[l], cache_cmp_v[l], cache_sel_k[l], cache_sel_v[l],
                                state_win_k[l], state_win_v[l], state_hgrn[l], page_table,
                                norm_g[l], w_in[l], cmp_pos_k[l], cmp_w1_k[l], cmp_w2_k[l], cmp_pos_v[l],
                                cmp_w1_v[l], cmp_w2_v[l], rel_bias, lower[l], hg_norm_g[l],
                                w_branch_a[l], w_branch_b[l], w_out[l])
        p_list.append(st_p)
        s_list.append(st_s)
    p_cmp_k, p_cmp_v, p_sel_k, p_sel_v, p_win_k, p_win_v, p_hgrn = [jnp.stack(s, axis=0) for s in zip(*p_list)]
    s_cmp_k, s_cmp_v, s_sel_k, s_sel_v, s_win_k, s_win_v, s_hgrn = [jnp.stack(s, axis=0) for s in zip(*s_list)]
    y_prompt = rmsnorm(hp, final_g).astype(x_prompt.dtype)
    y_sample = rmsnorm(hs, final_g).astype(x_sample.dtype)
    return (y_prompt, y_sample, p_cmp_k, p_cmp_v, p_sel_k, p_sel_v, p_win_k, p_win_v, p_hgrn,
            s_cmp_k, s_cmp_v, s_sel_k, s_sel_v, s_win_k, s_win_v, s_hgrn)
```

```python
import functools
import math

import numpy as np
import jax
import jax.numpy as jnp
from jax import lax
from jax.experimental import pallas as pl
from jax.experimental.pallas import tpu as pltpu

F32 = jnp.float32
BF16 = jnp.bfloat16

NSA_HEADS = 8
NSA_KV = 2
NSA_HD = 64
NSA_REP = NSA_HEADS // NSA_KV
CMP_LEN = 32
CMP_STRIDE = 16
CMP_HID = 64
SEL_BLOCK = 64
SEL_TOPN = 8
WINDOW = 512
HG_HEADS = 4
HG_DK = 128
HG_DV = 128
HG_CHUNK = 64
N_BUCKETS = 32
MAX_DIST = 128
EPS = 1e-6
NEG = -1e30
FORCED = 1e9
PAGE_SIZE = 128

A_WIDTH = NSA_HEADS * NSA_HD
A_KV = NSA_KV * NSA_HD
B_WIDTH = HG_HEADS * HG_DV
B_KEY = HG_HEADS * HG_DK

LANES = 128
SUBLANES = 8
VMEM_LIMIT_BYTES = 56 * 1024 * 1024

TQ = 128
T_PAD = SUBLANES
N_ROWBLK = NSA_HEADS

C_Q = 0
C_KV = C_Q + A_WIDTH
C_GATE = C_KV + 6 * A_KV
C_ZA = C_GATE + 2 * LANES
C_QB = C_ZA + A_WIDTH
C_FB = C_QB + B_KEY
C_IB = C_FB + B_KEY
C_ZB = C_IB + B_WIDTH
C_MA = C_ZB + B_WIDTH

CMP_SRC_ROWS = (LANES - 1) * CMP_STRIDE + CMP_LEN


def _nn(a, b):
    return jnp.dot(a, b, preferred_element_type=F32)


def _nt(a, b):
    return lax.dot_general(a, b, (((1,), (1,)), ((), ())), preferred_element_type=F32)


def _tn(a, b):
    return lax.dot_general(a, b, (((0,), (0,)), ((), ())), preferred_element_type=F32)


def _sigmoid(x):
    return 1.0 / (1.0 + jnp.exp(-x))


def _silu(x):
    return x * _sigmoid(x)


def _split2(x):
    hi = x.astype(BF16)
    lo = (x - hi.astype(F32)).astype(BF16)
    return hi, lo


def _cparams(n_grid):
    return pltpu.CompilerParams(dimension_semantics=("arbitrary",) * n_grid,
                                vmem_limit_bytes=VMEM_LIMIT_BYTES)


def _t5_bucket(rel):
    n = jnp.maximum(rel, 0)
    exact = N_BUCKETS // 2
    nf = jnp.maximum(n, 1).astype(F32)
    large = exact + (jnp.log(nf / exact) / math.log(MAX_DIST / exact) * (N_BUCKETS - exact)).astype(jnp.int32)
    large = jnp.minimum(large, N_BUCKETS - 1)
    return jnp.where(n < exact, n, large)


def _rowblk_head(rb):
    return NSA_REP * (rb % NSA_KV) + rb // NSA_KV


def _bias_body(tab_ref, o_ref, *, base, col_stride, tr):
    shape = o_ref.shape[1:]
    row = lax.broadcasted_iota(jnp.int32, shape, 0) + pl.program_id(0) * tr
    col = lax.broadcasted_iota(jnp.int32, shape, 1)
    bucket = _t5_bucket(base + row - col_stride * col)
    for rb in range(N_ROWBLK):
        h = _rowblk_head(rb)
        acc = jnp.zeros(shape, F32)
        for k in range(N_BUCKETS):
            acc = jnp.where(bucket == k, tab_ref[k, h], acc)
        o_ref[rb] = acc


def _bias_table(rel_bias, base, col_stride, rows, cols):
    tr = min(rows, TQ)
    return pl.pallas_call(
        functools.partial(_bias_body, base=base, col_stride=col_stride, tr=tr),
        grid=(rows // tr,),
        in_specs=[pl.BlockSpec(memory_space=pltpu.SMEM)],
        out_specs=pl.BlockSpec((N_ROWBLK, tr, cols), lambda i: (0, i, 0)),
        out_shape=jax.ShapeDtypeStruct((N_ROWBLK, rows, cols), F32),
        compiler_params=_cparams(1),
        name="bias_table",
    )(rel_bias)


def _proj_body(x_ref, g_ref, w_ref, o_ref, xn_ref):
    @pl.when(pl.program_id(1) == 0)
    def _():
        x = x_ref[...]
        ms = jnp.mean(x * x, axis=-1, keepdims=True)
        xn_ref[...] = (x * lax.rsqrt(ms + EPS) * g_ref[...]).astype(BF16)

    o_ref[...] = _nn(xn_ref[...], w_ref[...])


def _project(x2d, norm_g, w):
    m, d = x2d.shape
    n = w.shape[1]
    tm = min(m, 1024)
    tn = 512
    return pl.pallas_call(
        _proj_body,
        grid=(m // tm, n // tn),
        in_specs=[pl.BlockSpec((tm, d), lambda i, j: (i, 0)),
                  pl.BlockSpec((1, d), lambda i, j: (0, 0)),
                  pl.BlockSpec((d, tn), lambda i, j: (0, j))],
        out_specs=pl.BlockSpec((tm, tn), lambda i, j: (i, j)),
        out_shape=jax.ShapeDtypeStruct((m, n), F32),
        scratch_shapes=[pltpu.VMEM((tm, d), BF16)],
        compiler_params=_cparams(2),
        name="in_proj",
    )(x2d, norm_g.reshape(1, d), w)


def _permute_heads_cols(w):
    lead = w.shape[:-1]
    return w.reshape(lead + (NSA_KV, NSA_REP, NSA_HD)).swapaxes(-3, -2).reshape(lead + (A_WIDTH,))


def _arrange_w_in(w_in):
    d = w_in.shape[0]
    sizes = (A_WIDTH, 6 * A_KV, 3 * NSA_HEADS, A_WIDTH, B_KEY, B_KEY, B_WIDTH, B_WIDTH, d, d)
    offs = np.concatenate([[0], np.cumsum(sizes)])
    q_a, kv_a, g_a, z_a, q_b, f_b, i_b, z_b, m_a, m_b = [w_in[:, offs[i]:offs[i + 1]] for i in range(10)]
    gate_pad = jnp.zeros((d, 2 * LANES - 3 * NSA_HEADS), w_in.dtype)
    w = jnp.concatenate([_permute_heads_cols(q_a), kv_a, g_a, gate_pad, _permute_heads_cols(z_a),
                         q_b, f_b, i_b, z_b, m_a, m_b], axis=1)
    return w.astype(BF16)


def _compress(src_ref, pos_ref, w1_ref, w2_ref):
    acc = jnp.zeros((LANES, LANES), F32)
    for j in range(CMP_LEN):
        xj = src_ref[pl.ds(j, LANES, stride=CMP_STRIDE), :] + pos_ref[j:j + 1, :]
        acc = acc + _nn(xj.astype(BF16), w1_ref[j])
    return _nn(_silu(acc).astype(BF16), w2_ref[...])


def _compress_prompt_body(kc_ref, vc_ref, posk_ref, w1k_ref, w2k_ref, posv_ref, w1v_ref, w2v_ref,
                          ko_ref, vo_ref, buf_ref):
    t = kc_ref.shape[0]
    buf_ref[t:, :] = jnp.zeros((buf_ref.shape[0] - t, LANES), F32)
    buf_ref[:t, :] = kc_ref[...]
    ko_ref[...] = _compress(buf_ref, posk_ref, w1k_ref, w2k_ref)
    buf_ref[:t, :] = vc_ref[...]
    vo_ref[...] = _compress(buf_ref, posv_ref, w1v_ref, w2v_ref)


def _block_diag2(w):
    z = jnp.zeros_like(w)
    return jnp.concatenate([jnp.concatenate([w, z], axis=-1), jnp.concatenate([z, w], axis=-1)], axis=-2)


def _compress_weights(pos, w1, w2):
    pos2 = jnp.concatenate([pos, pos], axis=-1)
    w1bd = _block_diag2(w1.reshape(CMP_LEN, NSA_HD, CMP_HID)).astype(BF16)
    w2bd = _block_diag2(w2).astype(BF16)
    return pos2, w1bd, w2bd


def _const_spec(shape):
    nd = len(shape)
    return pl.BlockSpec(shape, lambda *_: (0,) * nd)


def _compress_prompt(proj3, cw_k, cw_v):
    b, t, _ = proj3.shape
    col = C_KV // LANES
    w_specs = [_const_spec(a.shape) for a in cw_k + cw_v]
    return pl.pallas_call(
        _compress_prompt_body,
        grid=(b,),
        in_specs=[pl.BlockSpec((None, t, LANES), lambda i: (i, 0, col)),
                  pl.BlockSpec((None, t, LANES), lambda i: (i, 0, col + 1))] + w_specs,
        out_specs=[pl.BlockSpec((None, LANES, LANES), lambda i: (i, 0, 0))] * 2,
        out_shape=[jax.ShapeDtypeStruct((b, LANES, LANES), F32)] * 2,
        scratch_shapes=[pltpu.VMEM((max(t + CMP_LEN, CMP_SRC_ROWS), LANES), F32)],
        compiler_params=_cparams(1),
        name="compress_prompt",
    )(proj3, proj3, *cw_k, *cw_v)


def _select_blocks(imp, qpos, nb):
    j = lax.broadcasted_iota(jnp.int32, imp.shape, imp.ndim - 1)
    cur = qpos // SEL_BLOCK
    valid = j * SEL_BLOCK <= qpos
    forced = (j == 0) | (j == cur) | (j == cur - 1)
    score = jnp.where(valid, jnp.where(forced, FORCED, imp), -1.0)
    score = jnp.where(j < nb, score, -2.0)
    sel = jnp.zeros(imp.shape, F32)
    for _ in range(min(SEL_TOPN, nb)):
        m = jnp.max(score, axis=-1, keepdims=True)
        first = jnp.min(jnp.where(score == m, j, LANES), axis=-1, keepdims=True)
        hit = j == first
        sel = jnp.where(hit, 1.0, sel)
        score = jnp.where(hit, -3.0, score)
    return sel


def _masked_softmax(s, mask):
    s = jnp.where(mask, s, NEG)
    m = jnp.max(s, axis=-1, keepdims=True)
    e = jnp.exp(s - m)
    p = e / jnp.sum(e, axis=-1, keepdims=True)
    return jnp.where(mask, p, 0.0)


def _stack_queries(q_ref, qbd_ref, rows):
    lane = lax.broadcasted_iota(jnp.int32, (rows, LANES), 1)
    for r in range(NSA_REP):
        blk = q_ref[:, r * LANES:(r + 1) * LANES] * (NSA_HD ** -0.5)
        qbd_ref[(2 * r) * rows:(2 * r + 1) * rows, :] = jnp.where(lane < NSA_HD, blk, 0.0).astype(BF16)
        qbd_ref[(2 * r + 1) * rows:(2 * r + 2) * rows, :] = jnp.where(lane >= NSA_HD, blk, 0.0).astype(BF16)


def _gate_and_store(gate_ref, za_ref, o_ref, o_cmp, o_sel, o_win, rows):
    gs = _sigmoid(gate_ref[...])
    lane = lax.broadcasted_iota(jnp.int32, (rows, LANES), 1)
    for r in range(NSA_REP):
        parts = []
        for g in range(NSA_KV):
            rb = 2 * r + g
            h = _rowblk_head(rb)
            o = (gs[:, h:h + 1] * o_cmp[rb] + gs[:, NSA_HEADS + h:NSA_HEADS + h + 1] * o_sel[rb]
                 + gs[:, 2 * NSA_HEADS + h:2 * NSA_HEADS + h + 1] * o_win[rb])
            parts.append(o)
        blk = jnp.where(lane < NSA_HD, parts[0], parts[1])
        o_ref[:, r * LANES:(r + 1) * LANES] = blk * _silu(za_ref[:, r * LANES:(r + 1) * LANES])


def _nsa_prompt_body(q_ref, gate_ref, za_ref, ks_ref, vs_ref, kw_ref, vw_ref, kcmp_ref, vcmp_ref,
                     bcmp_ref, btile_ref, cover_ref, emat_ref, o_ref,
                     qbd_ref, mask_ref, m_ref, l_ref, acc_ref, res_ref, *, nc, nb):
    qb = pl.program_id(1)
    nkt = mask_ref.shape[0]
    _stack_queries(q_ref, qbd_ref, TQ)
    row = lax.broadcasted_iota(jnp.int32, (TQ, LANES), 0)
    lane = lax.broadcasted_iota(jnp.int32, (TQ, LANES), 1)
    qpos = qb * TQ + row

    s = _nt(qbd_ref[...], kcmp_ref[...].astype(BF16)).reshape(N_ROWBLK, TQ, LANES) + bcmp_ref[...]
    mask_c = ((CMP_STRIDE * lane + CMP_LEN - 1) <= qpos) & (lane < nc)
    p = _masked_softmax(s, mask_c[None])
    res_ref[0] = _nn(p.reshape(N_ROWBLK * TQ, LANES).astype(BF16), vcmp_ref[...].astype(BF16)).reshape(N_ROWBLK, TQ, LANES)

    psum = p.reshape(NSA_REP, NSA_KV * TQ, LANES).sum(axis=0)
    p_hi, p_lo = _split2(psum)
    imp = _nn(p_hi, cover_ref[...]) + _nn(p_lo, cover_ref[...])
    sel = _select_blocks(imp.reshape(NSA_KV, TQ, LANES), qpos[None], nb)
    selb = sel.reshape(NSA_KV * TQ, LANES).astype(BF16)
    for kt in range(nkt):
        mask_ref[kt] = _nn(selb, emat_ref[:, kt * LANES:(kt + 1) * LANES]).reshape(NSA_KV, TQ, LANES)

    def attend(k_ref, v_ref, kt_lo, use_sel):
        m_ref[...] = jnp.full(m_ref.shape, NEG, F32)
        l_ref[...] = jnp.zeros(l_ref.shape, F32)
        acc_ref[...] = jnp.zeros(acc_ref.shape, F32)

        def body(kt, carry):
            st = pl.multiple_of(kt * LANES, LANES)
            k_t = k_ref[pl.ds(st, LANES), :].astype(BF16)
            v_t = v_ref[pl.ds(st, LANES), :].astype(BF16)
            d = jnp.minimum(qb - kt, 2)
            rel = (qb - kt) * LANES + row - lane
            if use_sel:
                mk = (rel >= 0)[None] & (mask_ref[kt] > 0.5)
            else:
                mk = jnp.broadcast_to(((rel >= 0) & (rel < WINDOW))[None], (NSA_KV, TQ, LANES))
            for r in range(NSA_REP):
                sc = _nt(qbd_ref[r * 2 * TQ:(r + 1) * 2 * TQ, :], k_t).reshape(NSA_KV, TQ, LANES)
                sc = jnp.where(mk, sc + btile_ref[d, 2 * r:2 * r + 2], NEG)
                m_old = m_ref[r]
                m_new = jnp.maximum(m_old, jnp.max(sc, axis=-1, keepdims=True))
                pt = jnp.where(mk, jnp.exp(sc - m_new), 0.0)
                alpha = jnp.exp(m_old - m_new)
                l_ref[r] = alpha * l_ref[r] + jnp.sum(pt, axis=-1, keepdims=True)
                pv = _nn(pt.reshape(NSA_KV * TQ, LANES).astype(BF16), v_t).reshape(NSA_KV, TQ, LANES)
                acc_ref[r] = alpha * acc_ref[r] + pv
                m_ref[r] = m_new
            return carry

        lax.fori_loop(kt_lo, qb + 1, body, 0)
        return (acc_ref[...] / l_ref[...]).reshape(N_ROWBLK, TQ, LANES)

    res_ref[1] = attend(ks_ref, vs_ref, 0, True)
    res_ref[2] = attend(kw_ref, vw_ref, jnp.maximum(qb - WINDOW // TQ, 0), False)
    _gate_and_store(gate_ref, za_ref, o_ref, res_ref[0], res_ref[1], res_ref[2], TQ)


def _cover_matrix(nc, nb):
    cs = np.arange(nc) * CMP_STRIDE
    ce = cs + CMP_LEN
    bs = np.arange(nb) * SEL_BLOCK
    be = bs + SEL_BLOCK
    ov = np.clip(np.minimum(ce[:, None], be[None, :]) - np.maximum(cs[:, None], bs[None, :]), 0, None) / CMP_STRIDE
    out = np.zeros((LANES, LANES), np.float32)
    out[:nc, :nb] = ov
    return jnp.asarray(out, BF16)


def _expand_matrix(nkeys):
    j = np.arange(LANES)[:, None]
    k = np.arange(nkeys)[None, :]
    return jnp.asarray((k // SEL_BLOCK == j).astype(np.float32), BF16)


def _nsa_prompt(proj3, kcmp, vcmp, bias_cmp, bias_tiles):
    b, t, _ = proj3.shape
    nq = t // TQ
    nc = (t - CMP_LEN) // CMP_STRIDE + 1
    nb = t // SEL_BLOCK
    assert nc <= LANES and nb <= LANES and t % TQ == 0
    kv_col = C_KV // LANES
    kv_spec = lambda c: pl.BlockSpec((None, t, LANES), lambda i, j: (i, 0, kv_col + c))
    cover = _cover_matrix(nc, nb)
    emat = _expand_matrix(t)
    return pl.pallas_call(
        functools.partial(_nsa_prompt_body, nc=nc, nb=nb),
        grid=(b, nq),
        in_specs=[pl.BlockSpec((None, TQ, A_WIDTH), lambda i, j: (i, j, C_Q // A_WIDTH)),
                  pl.BlockSpec((None, TQ, LANES), lambda i, j: (i, j, C_GATE // LANES)),
                  pl.BlockSpec((None, TQ, A_WIDTH), lambda i, j: (i, j, C_ZA // A_WIDTH)),
                  kv_spec(2), kv_spec(3), kv_spec(4), kv_spec(5),
                  pl.BlockSpec((None, LANES, LANES), lambda i, j: (i, 0, 0)),
                  pl.BlockSpec((None, LANES, LANES), lambda i, j: (i, 0, 0)),
                  pl.BlockSpec((N_ROWBLK, TQ, LANES), lambda i, j: (0, j, 0)),
                  _const_spec(bias_tiles.shape), _const_spec(cover.shape), _const_spec(emat.shape)],
        out_specs=pl.BlockSpec((None, TQ, A_WIDTH), lambda i, j: (i, j, 0)),
        out_shape=jax.ShapeDtypeStruct((b, t, A_WIDTH), F32),
        scratch_shapes=[pltpu.VMEM((N_ROWBLK * TQ, LANES), BF16),
                        pltpu.VMEM((nq, NSA_KV, TQ, LANES), F32),
                        pltpu.VMEM((NSA_REP, NSA_KV, TQ, 1), F32),
                        pltpu.VMEM((NSA_REP, NSA_KV, TQ, 1), F32),
                        pltpu.VMEM((NSA_REP, NSA_KV, TQ, LANES), F32),
                        pltpu.VMEM((3, N_ROWBLK, TQ, LANES), F32)],
        compiler_params=_cparams(2),
        name="nsa_prompt",
    )(proj3, proj3, proj3, proj3, proj3, proj3, proj3, kcmp, vcmp, bias_cmp, bias_tiles, cover, emat)


def _nsa_sample_body(pt_ref, *refs, n_pages, past, wbuf, nc, nb):
    del pt_ref
    pages = [refs[c * n_pages:(c + 1) * n_pages] for c in range(4)]
    (q_ref, kc_ref, vc_ref, ks_ref, vs_ref, kw_ref, vw_ref, gate_ref, za_ref, wk_ref, wv_ref,
     posk_ref, w1k_ref, w2k_ref, posv_ref, w1v_ref, w2v_ref, bcmp_ref, bsel_ref, bwin_ref,
     cover_ref, emat_ref) = refs[4 * n_pages:4 * n_pages + 22]
    o_ref, swk_ref, swv_ref = refs[4 * n_pages + 22:4 * n_pages + 25]
    kc_all, vc_all, ks_all, vs_all, kw_all, vw_all, qbd_ref = refs[4 * n_pages + 25:]
    n_keys = ks_all.shape[0]
    n_win = kw_all.shape[0]

    def assemble(dst, page_refs, new_ref):
        for pg in range(n_pages):
            dst[pg * PAGE_SIZE:(pg + 1) * PAGE_SIZE, :] = page_refs[pg][...]
        dst[past:past + T_PAD, :] = new_ref[...]
        dst[past + T_PAD:, :] = jnp.zeros((dst.shape[0] - past - T_PAD, LANES), F32)

    assemble(kc_all, pages[0], kc_ref)
    assemble(vc_all, pages[1], vc_ref)
    assemble(ks_all, pages[2], ks_ref)
    assemble(vs_all, pages[3], vs_ref)
    for dst, w_ref, new_ref, out_ref in ((kw_all, wk_ref, kw_ref, swk_ref), (vw_all, wv_ref, vw_ref, swv_ref)):
        dst[:wbuf, :] = w_ref[...]
        dst[wbuf:wbuf + T_PAD, :] = new_ref[...]
        dst[wbuf + T_PAD:, :] = jnp.zeros((n_win - wbuf - T_PAD, LANES), F32)
        out_ref[...] = dst[pl.ds(T_PAD // 2, wbuf), :]

    kcmp = _compress(kc_all, posk_ref, w1k_ref, w2k_ref).astype(BF16)
    vcmp = _compress(vc_all, posv_ref, w1v_ref, w2v_ref).astype(BF16)

    _stack_queries(q_ref, qbd_ref, T_PAD)
    qbd = qbd_ref[...]
    nrow = N_ROWBLK * T_PAD
    tok = lax.broadcasted_iota(jnp.int32, (T_PAD, LANES), 0)
    lane = lax.broadcasted_iota(jnp.int32, (T_PAD, LANES), 1)
    qpos = past + tok

    s = _nt(qbd, kcmp).reshape(N_ROWBLK, T_PAD, LANES) + bcmp_ref[...]
    mask_c = ((CMP_STRIDE * lane + CMP_LEN - 1) <= qpos) & (lane < nc)
    p = _masked_softmax(s, mask_c[None])
    o_cmp = _nn(p.reshape(nrow, LANES).astype(BF16), vcmp).reshape(N_ROWBLK, T_PAD, LANES)

    psum = p.reshape(NSA_REP, NSA_KV * T_PAD, LANES).sum(axis=0)
    p_hi, p_lo = _split2(psum)
    imp = _nn(p_hi, cover_ref[...]) + _nn(p_lo, cover_ref[...])
    sel = _select_blocks(imp.reshape(NSA_KV, T_PAD, LANES), qpos[None], nb)
    selk = _nn(sel.reshape(NSA_KV * T_PAD, LANES).astype(BF16), emat_ref[...]).reshape(NSA_KV, T_PAD, n_keys)

    kpos = lax.broadcasted_iota(jnp.int32, (T_PAD, n_keys), 1)
    tok_k = lax.broadcasted_iota(jnp.int32, (T_PAD, n_keys), 0)
    mk = ((kpos <= past + tok_k)[None] & (selk > 0.5))[None]
    s = _nt(qbd, ks_all[...].astype(BF16)).reshape(NSA_REP, NSA_KV, T_PAD, n_keys)
    s = s + bsel_ref[...].reshape(NSA_REP, NSA_KV, T_PAD, n_keys)
    p = _masked_softmax(s, mk)
    o_sel = _nn(p.reshape(nrow, n_keys).astype(BF16), vs_all[...].astype(BF16)).reshape(N_ROWBLK, T_PAD, LANES)

    ki = lax.broadcasted_iota(jnp.int32, (T_PAD, n_win), 1)
    tok_w = lax.broadcasted_iota(jnp.int32, (T_PAD, n_win), 0)
    rel_w = wbuf + tok_w - ki
    mw = ((rel_w >= 0) & (rel_w < WINDOW) & (ki < wbuf + T_PAD))[None]
    s = _nt(qbd, kw_all[...].astype(BF16)).reshape(N_ROWBLK, T_PAD, n_win) + bwin_ref[...]
    p = _masked_softmax(s, mw)
    o_win = _nn(p.reshape(nrow, n_win).astype(BF16), vw_all[...].astype(BF16)).reshape(N_ROWBLK, T_PAD, LANES)

    _gate_and_store(gate_ref, za_ref, o_ref, o_cmp, o_sel, o_win, T_PAD)


def _round_up(x, m):
    return (x + m - 1) // m * m


def _nsa_sample(projs3, caches, wk, wv, page_table, cw_k, cw_v, rel_bias, n_new):
    nseq, n_pages = page_table.shape
    past = n_pages * PAGE_SIZE
    wbuf = wk.shape[1]
    length = past + n_new
    nc = (length - CMP_LEN) // CMP_STRIDE + 1
    nb = -(-length // SEL_BLOCK)
    n_keys = _round_up(max(past + T_PAD, nb * SEL_BLOCK), LANES)
    n_cmp_rows = max(past + T_PAD + CMP_LEN, CMP_SRC_ROWS)
    n_win = _round_up(wbuf + T_PAD, LANES)
    assert nc <= LANES and nb <= LANES and n_new == T_PAD // 2

    bias_cmp = _bias_table(rel_bias, past - (CMP_LEN - 1), CMP_STRIDE, T_PAD, LANES)
    bias_sel = _bias_table(rel_bias, past, 1, T_PAD, n_keys)
    bias_win = _bias_table(rel_bias, wbuf, 1, T_PAD, n_win)
    cover = _cover_matrix(nc, nb)
    emat = _expand_matrix(n_keys)

    def page_spec(pg):
        return pl.BlockSpec((None, PAGE_SIZE, LANES), lambda i, pt: (pt[i, pg], 0, 0))

    def new_spec(width, col):
        return pl.BlockSpec((None, T_PAD, width), lambda i, pt: (i, 0, col))

    kv_col = C_KV // LANES
    in_specs = [page_spec(pg) for _ in range(4) for pg in range(n_pages)]
    in_specs += [new_spec(A_WIDTH, C_Q // A_WIDTH)]
    in_specs += [new_spec(LANES, kv_col + c) for c in range(6)]
    in_specs += [new_spec(LANES, C_GATE // LANES), new_spec(A_WIDTH, C_ZA // A_WIDTH)]
    in_specs += [pl.BlockSpec((None, wbuf, LANES), lambda i, pt: (i, 0, 0))] * 2
    consts = list(cw_k) + list(cw_v) + [bias_cmp, bias_sel, bias_win, cover, emat]
    in_specs += [_const_spec(a.shape) for a in consts]
    operands = [c for c in caches for _ in range(n_pages)] + [projs3] * 9 + [wk, wv] + consts

    grid_spec = pltpu.PrefetchScalarGridSpec(
        num_scalar_prefetch=1,
        grid=(nseq,),
        in_specs=in_specs,
        out_specs=[pl.BlockSpec((None, T_PAD, A_WIDTH), lambda i, pt: (i, 0, 0)),
                   pl.BlockSpec((None, wbuf, LANES), lambda i, pt: (i, 0, 0)),
                   pl.BlockSpec((None, wbuf, LANES), lambda i, pt: (i, 0, 0))],
        scratch_shapes=[pltpu.VMEM((n_cmp_rows, LANES), F32), pltpu.VMEM((n_cmp_rows, LANES), F32),
                        pltpu.VMEM((n_keys, LANES), F32), pltpu.VMEM((n_keys, LANES), F32),
                        pltpu.VMEM((n_win, LANES), F32), pltpu.VMEM((n_win, LANES), F32),
                        pltpu.VMEM((N_ROWBLK * T_PAD, LANES), BF16)])
    return pl.pallas_call(
        functools.partial(_nsa_sample_body, n_pages=n_pages, past=past, wbuf=wbuf, nc=nc, nb=nb),
        grid_spec=grid_spec,
        out_shape=[jax.ShapeDtypeStruct((nseq, T_PAD, A_WIDTH), F32),
                   jax.ShapeDtypeStruct((nseq, wbuf, LANES), F32),
                   jax.ShapeDtypeStruct((nseq, wbuf, LANES), F32)],
        compiler_params=_cparams(1),
        name="nsa_sample",
    )(page_table, *operands)


def _hgrn_constants(rows, seg_stride, seg_len):
    idx = np.arange(rows)
    seg = idx // seg_stride
    pos = idx % seg_stride
    same = seg[:, None] == seg[None, :]
    pt, pu = pos[:, None], pos[None, :]
    mats = [same & (pu <= pt),
            same & (pu > pt) & (pu <= seg_len - 1)]
    levels = []
    m = seg_len // 2
    while m >= 2:
        levels.append(m)
        m //= 2
    for m in levels:
        mats.append(same & (pu >= (pt // m) * m) & (pu <= pt))
        mats.append(same & (pu > pt) & (pu <= (pt // m) * m + m - 1))
    wstack = np.concatenate(mats, axis=0).astype(np.float32)
    wstack = np.concatenate([wstack] * 3, axis=1)
    masks = []
    for m in levels + [1]:
        masks.append(same & (pt // (2 * m) == pu // (2 * m)) & ((pt // m) % 2 == 1) & ((pu // m) % 2 == 0))
    masks.append(idx[:, None] == idx[None, :])
    masks = np.stack(masks).astype(np.float32)
    return jnp.asarray(wstack, BF16), jnp.asarray(masks, F32), len(levels)


def _hgrn_chunk(q, f_pre, hl_ref, wst_ref, n_levels):
    rows = q.shape[0]
    hl = hl_ref[...]
    e = jnp.exp(hl - jnp.max(hl, axis=0, keepdims=True))
    lb = e[0:1] / jnp.sum(e, axis=0, keepdims=True)
    f = lb + (1.0 - lb) * _sigmoid(f_pre)
    g = jnp.log(f)
    k = 1.0 - f
    g_hi = g.astype(BF16)
    r1 = g - g_hi.astype(F32)
    g_mid = r1.astype(BF16)
    g_lo = (r1 - g_mid.astype(F32)).astype(BF16)
    sums = _nn(wst_ref[...], jnp.concatenate([g_hi, g_mid, g_lo], axis=0))
    big_g = sums[0:rows]
    qg = q * jnp.exp(big_g)
    kd = k * jnp.exp(sums[rows:2 * rows])
    q_lv, k_lv = [], []
    for li in range(n_levels):
        q_lv.append(q * jnp.exp(sums[(2 + 2 * li) * rows:(3 + 2 * li) * rows]))
        k_lv.append(k * jnp.exp(sums[(3 + 2 * li) * rows:(4 + 2 * li) * rows]))
    q_lv += [q * f, q]
    k_lv += [k, k]
    return big_g, qg, kd, q_lv, k_lv


def _hgrn_intra(q_lv, k_lv, masks_ref, hs):
    a = None
    for li in range(len(q_lv)):
        term = jnp.where(masks_ref[li] > 0.5, _nt(q_lv[li][:, hs].astype(BF16), k_lv[li][:, hs].astype(BF16)), 0.0)
        a = term if a is None else a + term
    return a


def _hgrn_out(o, ng, z):
    o = o * lax.rsqrt(jnp.mean(o * o, axis=-1, keepdims=True) + EPS)
    return o * ng * _silu(z)


def _hgrn_prompt_body(q_ref, f_ref, i_ref, z_ref, hl_ref, ng_ref, wst_ref, masks_ref, o_ref, s_ref, st_ref,
                      *, n_levels):
    c = pl.program_id(1)

    @pl.when(c == 0)
    def _():
        st_ref[...] = jnp.zeros(st_ref.shape, F32)

    rows = q_ref.shape[0]
    v = i_ref[...]
    big_g, qg, kd, q_lv, k_lv = _hgrn_chunk(q_ref[...], f_ref[...], hl_ref, wst_ref, n_levels)
    for h in range(HG_HEADS):
        hs = slice(h * HG_DK, (h + 1) * HG_DK)
        vb = v[:, hs].astype(BF16)
        a = _hgrn_intra(q_lv, k_lv, masks_ref, hs)
        st = st_ref[h]
        o = _nt(qg[:, hs].astype(BF16), st.astype(BF16)) + _nn(a.astype(BF16), vb)
        o_ref[:, hs] = _hgrn_out(o, ng_ref[:, hs], z_ref[:, hs])
        st_new = st * jnp.exp(big_g[rows - 1:rows, hs]) + _tn(vb, kd[:, hs].astype(BF16))
        st_ref[h] = st_new

        @pl.when(c == pl.num_programs(1) - 1)
        def _():
            s_ref[h] = st_new.T


def _hgrn_prompt(proj3, hg_lower, hg_norm_g):
    b, t, _ = proj3.shape
    rows = math.gcd(t, HG_CHUNK)
    wst, masks, n_levels = _hgrn_constants(rows, rows, rows)
    col = lambda c: pl.BlockSpec((None, rows, B_KEY), lambda i, j: (i, j, c // B_KEY))
    return pl.pallas_call(
        functools.partial(_hgrn_prompt_body, n_levels=n_levels),
        grid=(b, t // rows),
        in_specs=[col(C_QB), col(C_FB), col(C_IB), col(C_ZB),
                  _const_spec(hg_lower.shape), _const_spec(hg_norm_g.shape),
                  _const_spec(wst.shape), _const_spec(masks.shape)],
        out_specs=[pl.BlockSpec((None, rows, B_WIDTH), lambda i, j: (i, j, 0)),
                   pl.BlockSpec((None, HG_HEADS, HG_DK, HG_DV), lambda i, j: (i, 0, 0, 0))],
        out_shape=[jax.ShapeDtypeStruct((b, t, B_WIDTH), F32),
                   jax.ShapeDtypeStruct((b, HG_HEADS, HG_DK, HG_DV), F32)],
        scratch_shapes=[pltpu.VMEM((HG_HEADS, HG_DV, HG_DK), F32)],
        compiler_params=_cparams(2),
        name="hgrn_prompt",
    )(proj3, proj3, proj3, proj3, hg_lower, hg_norm_g, wst, masks)


SEQ_PER_STEP = 8


def _hgrn_sample_body(q_ref, f_ref, i_ref, z_ref, s0_ref, hl_ref, ng_ref, wst_ref, masks_ref, o_ref, s_ref,
                      *, n_levels, n_new):
    v = i_ref[...]
    big_g, qg, kd, q_lv, k_lv = _hgrn_chunk(q_ref[...], f_ref[...], hl_ref, wst_ref, n_levels)
    for h in range(HG_HEADS):
        hs = slice(h * HG_DK, (h + 1) * HG_DK)
        vb = v[:, hs].astype(BF16)
        a = _hgrn_intra(q_lv, k_lv, masks_ref, hs)
        o_intra = _nn(a.astype(BF16), vb)
        for s in range(SEQ_PER_STEP):
            rs = slice(s * T_PAD, (s + 1) * T_PAD)
            s0 = s0_ref[s, h]
            o = _nn(qg[rs, hs].astype(BF16), s0.astype(BF16)) + o_intra[rs]
            o_ref[rs, hs] = _hgrn_out(o, ng_ref[:, hs], z_ref[rs, hs])
            decay = jnp.exp(big_g[rs, hs]).T[:, n_new - 1:n_new]
            s_ref[s, h] = s0 * decay + _tn(kd[rs, hs].astype(BF16), v[rs, hs].astype(BF16))


def _hgrn_sample(projs, state, hg_lower, hg_norm_g, n_new):
    nseq = state.shape[0]
    rows = SEQ_PER_STEP * T_PAD
    wst, masks, n_levels = _hgrn_constants(rows, T_PAD, n_new)
    col = lambda c: pl.BlockSpec((rows, B_KEY), lambda i: (i, c // B_KEY))
    st_spec = pl.BlockSpec((SEQ_PER_STEP, HG_HEADS, HG_DK, HG_DV), lambda i: (i, 0, 0, 0))
    return pl.pallas_call(
        functools.partial(_hgrn_sample_body, n_levels=n_levels, n_new=n_new),
        grid=(nseq // SEQ_PER_STEP,),
        in_specs=[col(C_QB), col(C_FB), col(C_IB), col(C_ZB), st_spec,
                  _const_spec(hg_lower.shape), _const_spec(hg_norm_g.shape),
                  _const_spec(wst.shape), _const_spec(masks.shape)],
        out_specs=[pl.BlockSpec((rows, B_WIDTH), lambda i: (i, 0)), st_spec],
        out_shape=[jax.ShapeDtypeStruct((nseq * T_PAD, B_WIDTH), F32),
                   jax.ShapeDtypeStruct(state.shape, F32)],
        compiler_params=_cparams(1),
        name="hgrn_sample",
    )(projs, projs, projs, projs, state, hg_lower, hg_norm_g, wst, masks)


def _merge_body(oa_ref, ob_ref, ma_ref, mb_ref, x_ref, wba_ref, wbb_ref, wout_ref, fg_ref, y_ref):
    a = _nn(oa_ref[...].astype(BF16), wba_ref[...])
    b = _nn(ob_ref[...].astype(BF16), wbb_ref[...])
    y = _sigmoid(ma_ref[...]) * a + _sigmoid(mb_ref[...]) * b
    h = x_ref[...] + _nn(y.astype(BF16), wout_ref[...])
    ms = jnp.mean(h * h, axis=-1, keepdims=True)
    y_ref[...] = h * lax.rsqrt(ms + EPS) * fg_ref[...]


def _merge(o_a, o_b, proj, x2d, w_ba, w_bb, w_out, final_g):
    m, d = x2d.shape
    tm = min(m, 512)
    row = lambda w, c: pl.BlockSpec((tm, w), lambda i: (i, c))
    return pl.pallas_call(
        _merge_body,
        grid=(m // tm,),
        in_specs=[row(A_WIDTH, 0), row(B_WIDTH, 0), row(d, C_MA // d), row(d, C_MA // d + 1), row(d, 0),
                  _const_spec(w_ba.shape), _const_spec(w_bb.shape), _const_spec(w_out.shape),
                  _const_spec((1, d))],
        out_specs=row(d, 0),
        out_shape=jax.ShapeDtypeStruct((m, d), F32),
        compiler_params=_cparams(1),
        name="merge_out",
    )(o_a, o_b, proj, proj, x2d, w_ba, w_bb, w_out, final_g.reshape(1, d))


def kernel(x_prompt, x_sample, cache_cmp_k, cache_cmp_v, cache_sel_k, cache_sel_v, state_win_k, state_win_v,
           state_hgrn, page_table, norm_g, w_in, cmp_pos_k, cmp_w1_k, cmp_w2_k, cmp_pos_v, cmp_w1_v, cmp_w2_v,
           rel_bias, hg_lower, hg_norm_g, w_branch_a, w_branch_b, w_out, final_g):
    depth = norm_g.shape[0]
    assert depth == 1, "single-layer trunk"
    b, t, d = x_prompt.shape
    nseq, n_new, _ = x_sample.shape
    assert C_MA % d == 0

    w = _arrange_w_in(w_in[0])
    n_cols = w.shape[1]
    w_ba = _permute_heads_cols(w_branch_a[0].T).T.astype(BF16)
    w_bb = w_branch_b[0].astype(BF16)
    w_o = w_out[0].astype(BF16)
    cw_k = _compress_weights(cmp_pos_k[0], cmp_w1_k[0], cmp_w2_k[0])
    cw_v = _compress_weights(cmp_pos_v[0], cmp_w1_v[0], cmp_w2_v[0])
    ng = hg_norm_g[0].reshape(1, B_WIDTH)
    kv_cols = lambda a, c: a[..., C_KV + c * A_KV:C_KV + (c + 1) * A_KV]

    xp2 = x_prompt.reshape(b * t, d)
    proj = _project(xp2, norm_g[0], w)
    proj3 = proj.reshape(b, t, n_cols)
    nc_p = (t - CMP_LEN) // CMP_STRIDE + 1
    kcmp, vcmp = _compress_prompt(proj3, cw_k, cw_v)
    bias_cmp = _bias_table(rel_bias, -(CMP_LEN - 1), CMP_STRIDE, t, LANES)
    bias_tiles = _bias_table(rel_bias, 0, 1, 3 * TQ, LANES).reshape(N_ROWBLK, 3, TQ, LANES).swapaxes(0, 1)
    o_a = _nsa_prompt(proj3, kcmp, vcmp, bias_cmp, bias_tiles)
    o_b, p_hgrn = _hgrn_prompt(proj3, hg_lower, ng)
    y_prompt = _merge(o_a.reshape(b * t, A_WIDTH), o_b.reshape(b * t, B_WIDTH), proj, xp2,
                      w_ba, w_bb, w_o, final_g).reshape(b, t, d)
    wb_p = min(WINDOW, t)
    p_kv = [kv_cols(proj3, c).reshape(1, b, t, NSA_KV, NSA_HD) for c in range(6)]
    p_state = p_kv[:4] + [p_kv[4][:, :, t - wb_p:], p_kv[5][:, :, t - wb_p:], p_hgrn[None]]

    xs_pad = jnp.pad(x_sample, ((0, 0), (0, T_PAD - n_new), (0, 0))).reshape(nseq * T_PAD, d)
    projs = _project(xs_pad, norm_g[0], w)
    projs3 = projs.reshape(nseq, T_PAD, n_cols)
    pool = cache_cmp_k.shape[1]
    caches = [c[0].reshape(pool, PAGE_SIZE, A_KV) for c in (cache_cmp_k, cache_cmp_v, cache_sel_k, cache_sel_v)]
    wbuf = state_win_k.shape[2]
    o_as, s_win_k, s_win_v = _nsa_sample(projs3, caches, state_win_k[0].reshape(nseq, wbuf, A_KV),
                                         state_win_v[0].reshape(nseq, wbuf, A_KV), page_table, cw_k, cw_v,
                                         rel_bias, n_new)
    o_bs, s_hgrn = _hgrn_sample(projs, state_hgrn[0], hg_lower, ng, n_new)
    y_s = _merge(o_as.reshape(nseq * T_PAD, A_WIDTH), o_bs, projs, xs_pad, w_ba, w_bb, w_o, final_g)
    y_sample = y_s.reshape(nseq, T_PAD, d)[:, :n_new]
    s_kv = [kv_cols(projs3, c)[:, :n_new].reshape(1, nseq, n_new, NSA_KV, NSA_HD) for c in range(4)]
    s_state = s_kv + [s_win_k.reshape(1, nseq, wbuf, NSA_KV, NSA_HD),
                      s_win_v.reshape(1, nseq, wbuf, NSA_KV, NSA_HD), s_hgrn[None]]

    return (y_prompt, y_sample, *p_state, *s_state)
```

```python
import functools
import math

import numpy as np
import jax
import jax.numpy as jnp
from jax import lax
from jax.experimental import pallas as pl
from jax.experimental.pallas import tpu as pltpu

F32 = jnp.float32
BF16 = jnp.bfloat16

NSA_HEADS = 8
NSA_KV = 2
NSA_HD = 64
NSA_REP = NSA_HEADS // NSA_KV
CMP_LEN = 32
CMP_STRIDE = 16
CMP_HID = 64
SEL_BLOCK = 64
SEL_TOPN = 8
WINDOW = 512
HG_HEADS = 4
HG_DK = 128
HG_DV = 128
HG_CHUNK = 64
N_BUCKETS = 32
MAX_DIST = 128
EPS = 1e-6
NEG = -1e30
FORCED = 1e9
PAGE_SIZE = 128

A_WIDTH = NSA_HEADS * NSA_HD
A_KV = NSA_KV * NSA_HD
B_WIDTH = HG_HEADS * HG_DV
B_KEY = HG_HEADS * HG_DK

LANES = 128
SUBLANES = 8
VMEM_LIMIT_BYTES = 56 * 1024 * 1024

TQ = 128
T_PAD = SUBLANES
N_ROWBLK = NSA_HEADS

C_Q = 0
C_KV = C_Q + A_WIDTH
C_GATE = C_KV + 6 * A_KV
C_ZA = C_GATE + 2 * LANES
C_QB = C_ZA + A_WIDTH
C_FB = C_QB + B_KEY
C_IB = C_FB + B_KEY
C_ZB = C_IB + B_WIDTH
C_MA = C_ZB + B_WIDTH

CMP_SRC_ROWS = (LANES - 1) * CMP_STRIDE + CMP_LEN


def _nn(a, b):
    return jnp.dot(a, b, preferred_element_type=F32)


def _nt(a, b):
    return lax.dot_general(a, b, (((1,), (1,)), ((), ())), preferred_element_type=F32)


def _tn(a, b):
    return lax.dot_general(a, b, (((0,), (0,)), ((), ())), preferred_element_type=F32)


def _sigmoid(x):
    return 1.0 / (1.0 + jnp.exp(-x))


def _silu(x):
    return x * _sigmoid(x)


def _split2(x):
    hi = x.astype(BF16)
    lo = (x - hi.astype(F32)).astype(BF16)
    return hi, lo


def _cparams(n_grid):
    return pltpu.CompilerParams(dimension_semantics=("arbitrary",) * n_grid,
                                vmem_limit_bytes=VMEM_LIMIT_BYTES)


def _t5_bucket(rel):
    n = jnp.maximum(rel, 0)
    exact = N_BUCKETS // 2
    nf = jnp.maximum(n, 1).astype(F32)
    large = exact + (jnp.log(nf / exact) / math.log(MAX_DIST / exact) * (N_BUCKETS - exact)).astype(jnp.int32)
    large = jnp.minimum(large, N_BUCKETS - 1)
    return jnp.where(n < exact, n, large)


def _rowblk_head(rb):
    return NSA_REP * (rb % NSA_KV) + rb // NSA_KV


def _bias_body(tab_ref, o_ref, *, base, step, row_stride, col_stride, lo, hi, cols):
    shape = (o_ref.shape[0], cols)
    row = lax.broadcasted_iota(jnp.int32, shape, 0)
    col = lax.broadcasted_iota(jnp.int32, shape, 1)
    rel = base + step * pl.program_id(0) + row_stride * row + col_stride * col
    bucket = _t5_bucket(rel)
    for rb in range(N_ROWBLK):
        h = _rowblk_head(rb)
        acc = jnp.zeros(shape, F32)
        for k in range(N_BUCKETS):
            acc = jnp.where(bucket == k, tab_ref[k, h], acc)
        if lo is not None:
            acc = jnp.where((rel >= lo) & (rel < hi), acc, NEG)
        o_ref[:, rb * cols:(rb + 1) * cols] = acc


def _bias_table(rel_bias, *, base, step, row_stride, col_stride, rows, cols, steps, lo=None, hi=None):
    return pl.pallas_call(
        functools.partial(_bias_body, base=base, step=step, row_stride=row_stride, col_stride=col_stride,
                          lo=lo, hi=hi, cols=cols),
        grid=(steps,),
        in_specs=[pl.BlockSpec(memory_space=pltpu.SMEM)],
        out_specs=pl.BlockSpec((None, rows, N_ROWBLK * cols), lambda i: (i, 0, 0)),
        out_shape=jax.ShapeDtypeStruct((steps, rows, N_ROWBLK * cols), F32),
        compiler_params=_cparams(1),
        name="bias_table",
    )(rel_bias)


def _bias_table_sample(rel_bias, base, col_stride, cols):
    t = _bias_table(rel_bias, base=base, step=0, row_stride=1, col_stride=-col_stride, rows=T_PAD, cols=cols, steps=1)
    return t.reshape(T_PAD, N_ROWBLK, cols).swapaxes(0, 1)


def _proj_body(x_ref, g_ref, w_ref, o_ref, xn_ref):
    @pl.when(pl.program_id(1) == 0)
    def _():
        x = x_ref[...]
        ms = jnp.mean(x * x, axis=-1, keepdims=True)
        xn_ref[...] = (x * lax.rsqrt(ms + EPS) * g_ref[...]).astype(BF16)

    o_ref[...] = _nn(xn_ref[...], w_ref[...])


def _project(x2d, norm_g, w):
    m, d = x2d.shape
    n = w.shape[1]
    tm = min(m, 1024)
    tn = 512
    return pl.pallas_call(
        _proj_body,
        grid=(m // tm, n // tn),
        in_specs=[pl.BlockSpec((tm, d), lambda i, j: (i, 0)),
                  pl.BlockSpec((1, d), lambda i, j: (0, 0)),
                  pl.BlockSpec((d, tn), lambda i, j: (0, j))],
        out_specs=pl.BlockSpec((tm, tn), lambda i, j: (i, j)),
        out_shape=jax.ShapeDtypeStruct((m, n), F32),
        scratch_shapes=[pltpu.VMEM((tm, d), BF16)],
        compiler_params=_cparams(2),
        name="in_proj",
    )(x2d, norm_g.reshape(1, d), w)


def _permute_heads_cols(w):
    lead = w.shape[:-1]
    return w.reshape(lead + (NSA_KV, NSA_REP, NSA_HD)).swapaxes(-3, -2).reshape(lead + (A_WIDTH,))


def _arrange_w_in(w_in):
    d = w_in.shape[0]
    sizes = (A_WIDTH, 6 * A_KV, 3 * NSA_HEADS, A_WIDTH, B_KEY, B_KEY, B_WIDTH, B_WIDTH, d, d)
    offs = np.concatenate([[0], np.cumsum(sizes)])
    q_a, kv_a, g_a, z_a, q_b, f_b, i_b, z_b, m_a, m_b = [w_in[:, offs[i]:offs[i + 1]] for i in range(10)]
    gate_pad = jnp.zeros((d, 2 * LANES - 3 * NSA_HEADS), w_in.dtype)
    w = jnp.concatenate([_permute_heads_cols(q_a), kv_a, g_a, gate_pad, _permute_heads_cols(z_a),
                         q_b, f_b, i_b, z_b, m_a, m_b], axis=1)
    return w.astype(BF16)


def _compress(src_ref, pos_ref, w1_ref, w2_ref):
    acc = jnp.zeros((LANES, LANES), F32)
    for j in range(CMP_LEN):
        xj = src_ref[pl.ds(j, LANES, stride=CMP_STRIDE), :] + pos_ref[j:j + 1, :]
        acc = acc + _nn(xj.astype(BF16), w1_ref[j])
    return _nn(_silu(acc).astype(BF16), w2_ref[...])


def _compress_prompt_body(kc_ref, vc_ref, posk_ref, w1k_ref, w2k_ref, posv_ref, w1v_ref, w2v_ref,
                          ko_ref, vo_ref, buf_ref):
    t = kc_ref.shape[0]
    buf_ref[t:, :] = jnp.zeros((buf_ref.shape[0] - t, LANES), F32)
    buf_ref[:t, :] = kc_ref[...]
    ko_ref[...] = _compress(buf_ref, posk_ref, w1k_ref, w2k_ref)
    buf_ref[:t, :] = vc_ref[...]
    vo_ref[...] = _compress(buf_ref, posv_ref, w1v_ref, w2v_ref)


def _block_diag2(w):
    z = jnp.zeros_like(w)
    return jnp.concatenate([jnp.concatenate([w, z], axis=-1), jnp.concatenate([z, w], axis=-1)], axis=-2)


def _compress_weights(pos, w1, w2):
    pos2 = jnp.concatenate([pos, pos], axis=-1)
    w1bd = _block_diag2(w1.reshape(CMP_LEN, NSA_HD, CMP_HID)).astype(BF16)
    w2bd = _block_diag2(w2).astype(BF16)
    return pos2, w1bd, w2bd


def _const_spec(shape):
    nd = len(shape)
    return pl.BlockSpec(shape, lambda *_: (0,) * nd)


def _compress_prompt(proj3, cw_k, cw_v):
    b, t, _ = proj3.shape
    col = C_KV // LANES
    w_specs = [_const_spec(a.shape) for a in cw_k + cw_v]
    return pl.pallas_call(
        _compress_prompt_body,
        grid=(b,),
        in_specs=[pl.BlockSpec((None, t, LANES), lambda i: (i, 0, col)),
                  pl.BlockSpec((None, t, LANES), lambda i: (i, 0, col + 1))] + w_specs,
        out_specs=[pl.BlockSpec((None, LANES, LANES), lambda i: (i, 0, 0))] * 2,
        out_shape=[jax.ShapeDtypeStruct((b, LANES, LANES), F32)] * 2,
        scratch_shapes=[pltpu.VMEM((max(t + CMP_LEN, CMP_SRC_ROWS), LANES), F32)],
        compiler_params=_cparams(1),
        name="compress_prompt",
    )(proj3, proj3, *cw_k, *cw_v)


def _select_blocks(imp, qpos, nb, axis):
    j = lax.broadcasted_iota(jnp.int32, imp.shape, axis)
    cur = qpos // SEL_BLOCK
    valid = j * SEL_BLOCK <= qpos
    forced = (j == 0) | (j == cur) | (j == cur - 1)
    score = jnp.where(valid, jnp.where(forced, FORCED, imp), -1.0)
    score = jnp.where(j < nb, score, -2.0)
    sel = jnp.zeros(imp.shape, F32)
    for _ in range(min(SEL_TOPN, nb)):
        m = jnp.max(score, axis=axis, keepdims=True)
        first = jnp.min(jnp.where(score == m, j, LANES), axis=axis, keepdims=True)
        hit = j == first
        sel = jnp.where(hit, 1.0, sel)
        score = jnp.where(hit, -3.0, score)
    return sel


def _masked_softmax(s, mask):
    s = jnp.where(mask, s, NEG)
    m = jnp.max(s, axis=-1, keepdims=True)
    e = jnp.exp(s - m)
    p = e / jnp.sum(e, axis=-1, keepdims=True)
    return jnp.where(mask, p, 0.0)


def _stack_queries(q_ref, qbd_ref, rows):
    lane = lax.broadcasted_iota(jnp.int32, (rows, LANES), 1)
    for r in range(NSA_REP):
        blk = q_ref[:, r * LANES:(r + 1) * LANES] * (NSA_HD ** -0.5)
        qbd_ref[(2 * r) * rows:(2 * r + 1) * rows, :] = jnp.where(lane < NSA_HD, blk, 0.0).astype(BF16)
        qbd_ref[(2 * r + 1) * rows:(2 * r + 2) * rows, :] = jnp.where(lane >= NSA_HD, blk, 0.0).astype(BF16)


def _gate_and_store(gate_ref, za_ref, o_ref, o_cmp, o_sel, o_win, rows):
    gs = _sigmoid(gate_ref[...])
    lane = lax.broadcasted_iota(jnp.int32, (rows, LANES), 1)
    for r in range(NSA_REP):
        parts = []
        for g in range(NSA_KV):
            rb = 2 * r + g
            h = _rowblk_head(rb)
            o = (gs[:, h:h + 1] * o_cmp[rb] + gs[:, NSA_HEADS + h:NSA_HEADS + h + 1] * o_sel[rb]
                 + gs[:, 2 * NSA_HEADS + h:2 * NSA_HEADS + h + 1] * o_win[rb])
            parts.append(o)
        blk = jnp.where(lane < NSA_HD, parts[0], parts[1])
        o_ref[:, r * LANES:(r + 1) * LANES] = blk * _silu(za_ref[:, r * LANES:(r + 1) * LANES])


N_COL = N_ROWBLK * TQ
MASK_COLS = NSA_KV * TQ
COL_CHUNK = TQ
SEL_TILES_PER_ITER = 2


def _nsa_prompt_body(q_ref, gate_ref, za_ref, ks_ref, vs_ref, kw_ref, vw_ref, kcmp_ref, vcmp_ref,
                     bcmp_ref, btile_ref, covert_ref, o_ref,
                     qt_ref, ksb_ref, vst_ref, kwb_ref, vwt_ref, vcmpt_ref, mask_ref, m_ref, l_ref, acc_ref, res_ref,
                     *, nc, nb):
    qb = pl.program_id(1)
    nkt = ksb_ref.shape[0]
    sub = lax.broadcasted_iota(jnp.int32, (LANES, TQ), 0)
    tok = lax.broadcasted_iota(jnp.int32, (LANES, TQ), 1)

    @pl.when(qb == 0)
    def _():
        for kt in range(nkt):
            rows = slice(kt * TQ, (kt + 1) * TQ)
            ksb_ref[kt] = ks_ref[rows, :].astype(BF16)
            kwb_ref[kt] = kw_ref[rows, :].astype(BF16)
            vst_ref[kt] = vs_ref[rows, :].T.astype(BF16)
            vwt_ref[kt] = vw_ref[rows, :].T.astype(BF16)
        vcmpt_ref[...] = vcmp_ref[...].T.astype(BF16)

    for r in range(NSA_REP):
        qt = (q_ref[:, r * LANES:(r + 1) * LANES] * (NSA_HD ** -0.5)).T
        qt_ref[:, (2 * r) * TQ:(2 * r + 1) * TQ] = jnp.where(sub < NSA_HD, qt, 0.0).astype(BF16)
        qt_ref[:, (2 * r + 1) * TQ:(2 * r + 2) * TQ] = jnp.where(sub >= NSA_HD, qt, 0.0).astype(BF16)

    valid = ((CMP_STRIDE * sub + CMP_LEN - 1) <= qb * TQ + tok) & (sub < nc)
    kcb = kcmp_ref[...].astype(BF16)
    psum = [None] * NSA_KV
    for rb in range(N_ROWBLK):
        cols = slice(rb * TQ, (rb + 1) * TQ)
        s = jnp.where(valid, _nn(kcb, qt_ref[:, cols]) + bcmp_ref[:, cols], NEG)
        e = jnp.exp(s - jnp.max(s, axis=0, keepdims=True))
        p = jnp.where(valid, e / jnp.sum(e, axis=0, keepdims=True), 0.0)
        res_ref[0, :, cols] = _nn(vcmpt_ref[...], p.astype(BF16))
        g = rb % NSA_KV
        psum[g] = p if psum[g] is None else psum[g] + p

    p_hi, p_lo = _split2(jnp.concatenate(psum, axis=1))
    imp = _nn(covert_ref[...], p_hi) + _nn(covert_ref[...], p_lo)
    nbp = _round_up(nb, SUBLANES)
    tok2 = lax.broadcasted_iota(jnp.int32, (1, MASK_COLS), 1) % TQ
    sel = _select_blocks(imp[:nbp], qb * TQ + tok2, nb, 0)
    upper = lax.broadcasted_iota(jnp.int32, (TQ, MASK_COLS), 0) < SEL_BLOCK
    for kt in range(nkt):
        picked = jnp.where(upper, sel[2 * kt:2 * kt + 1, :], sel[2 * kt + 1:2 * kt + 2, :]) > 0.5
        mask_ref[kt] = jnp.where(kt <= qb, jnp.where(picked, 0.0, NEG), NEG)

    m_ref[...] = jnp.full(m_ref.shape, NEG, F32)
    l_ref[...] = jnp.zeros(l_ref.shape, F32)
    acc_ref[...] = jnp.zeros(acc_ref.shape, F32)

    def sel_body(it, carry):
        tiles = []
        for u in range(SEL_TILES_PER_ITER):
            kt = it * SEL_TILES_PER_ITER + u
            tiles.append((ksb_ref[kt], vst_ref[kt], jnp.clip(qb - kt, 0, 2), kt))
        for c in range(N_COL // COL_CHUNK):
            cols = slice(c * COL_CHUNK, (c + 1) * COL_CHUNK)
            mcols = slice(c * COL_CHUNK % MASK_COLS, c * COL_CHUNK % MASK_COLS + COL_CHUNK)
            qc = qt_ref[:, cols]
            ss = [_nn(k_t, qc) + btile_ref[bidx, :, cols] + mask_ref[kt, :, mcols] for k_t, _, bidx, kt in tiles]
            m_old = m_ref[:, cols]
            m_new = m_old
            for s in ss:
                m_new = jnp.maximum(m_new, jnp.max(s, axis=0, keepdims=True))
            alpha = jnp.exp(m_old - m_new)
            l_new = alpha * l_ref[:, cols]
            acc = alpha * acc_ref[:, cols]
            for s, (_, v_t, _, _) in zip(ss, tiles):
                p = jnp.exp(s - m_new)
                l_new = l_new + jnp.sum(p, axis=0, keepdims=True)
                acc = acc + _nn(v_t, p.astype(BF16))
            l_ref[:, cols] = l_new
            acc_ref[:, cols] = acc
            m_ref[:, cols] = m_new
        return carry

    lax.fori_loop(0, qb // SEL_TILES_PER_ITER + 1, sel_body, 0)
    res_ref[1] = acc_ref[...] / l_ref[...]

    n_wt = WINDOW // TQ + 1
    for c in range(N_COL // COL_CHUNK):
        cols = slice(c * COL_CHUNK, (c + 1) * COL_CHUNK)
        qc = qt_ref[:, cols]
        ss = []
        for dist in range(n_wt):
            kt = jnp.maximum(qb - dist, 0)
            bidx = jnp.where(qb >= dist, dist, n_wt)
            ss.append(_nn(kwb_ref[kt], qc) + btile_ref[bidx, :, cols])
        m = jnp.max(ss[0], axis=0, keepdims=True)
        for s in ss[1:]:
            m = jnp.maximum(m, jnp.max(s, axis=0, keepdims=True))
        l_w = None
        acc = None
        for dist, s in enumerate(ss):
            p = jnp.exp(s - m)
            ps = jnp.sum(p, axis=0, keepdims=True)
            pv = _nn(vwt_ref[jnp.maximum(qb - dist, 0)], p.astype(BF16))
            l_w = ps if l_w is None else l_w + ps
            acc = pv if acc is None else acc + pv
        res_ref[2, :, cols] = acc / l_w

    gt = _sigmoid(gate_ref[...]).T
    for r in range(NSA_REP):
        parts = []
        for g in range(NSA_KV):
            rb = 2 * r + g
            h = _rowblk_head(rb)
            cols = slice(rb * TQ, (rb + 1) * TQ)
            parts.append(gt[h:h + 1, :] * res_ref[0, :, cols]
                         + gt[NSA_HEADS + h:NSA_HEADS + h + 1, :] * res_ref[1, :, cols]
                         + gt[2 * NSA_HEADS + h:2 * NSA_HEADS + h + 1, :] * res_ref[2, :, cols])
        blk = jnp.where(sub < NSA_HD, parts[0], parts[1]).T
        o_ref[:, r * LANES:(r + 1) * LANES] = blk * _silu(za_ref[:, r * LANES:(r + 1) * LANES])


def _cover_matrix(nc, nb):
    cs = np.arange(nc) * CMP_STRIDE
    ce = cs + CMP_LEN
    bs = np.arange(nb) * SEL_BLOCK
    be = bs + SEL_BLOCK
    ov = np.clip(np.minimum(ce[:, None], be[None, :]) - np.maximum(cs[:, None], bs[None, :]), 0, None) / CMP_STRIDE
    out = np.zeros((LANES, LANES), np.float32)
    out[:nc, :nb] = ov
    return jnp.asarray(out, BF16)


def _expand_matrix(nkeys):
    j = np.arange(LANES)[:, None]
    k = np.arange(nkeys)[None, :]
    return jnp.asarray((k // SEL_BLOCK == j).astype(np.float32), BF16)


def _nsa_prompt(proj3, kcmp, vcmp, bias_cmp, bias_tiles):
    b, t, _ = proj3.shape
    nq = t // TQ
    nc = (t - CMP_LEN) // CMP_STRIDE + 1
    nb = t // SEL_BLOCK
    assert nc <= LANES and nb <= LANES and t % (TQ * SEL_TILES_PER_ITER) == 0
    kv_col = C_KV // LANES
    kv_spec = lambda c: pl.BlockSpec((None, t, LANES), lambda i, j: (i, 0, kv_col + c))
    cover_t = _cover_matrix(nc, nb).T
    tile_bf16 = pltpu.VMEM((nq, TQ, LANES), BF16)
    return pl.pallas_call(
        functools.partial(_nsa_prompt_body, nc=nc, nb=nb),
        grid=(b, nq),
        in_specs=[pl.BlockSpec((None, TQ, A_WIDTH), lambda i, j: (i, j, C_Q // A_WIDTH)),
                  pl.BlockSpec((None, TQ, LANES), lambda i, j: (i, j, C_GATE // LANES)),
                  pl.BlockSpec((None, TQ, A_WIDTH), lambda i, j: (i, j, C_ZA // A_WIDTH)),
                  kv_spec(2), kv_spec(3), kv_spec(4), kv_spec(5),
                  pl.BlockSpec((None, LANES, LANES), lambda i, j: (i, 0, 0)),
                  pl.BlockSpec((None, LANES, LANES), lambda i, j: (i, 0, 0)),
                  pl.BlockSpec((None, LANES, N_COL), lambda i, j: (j, 0, 0)),
                  _const_spec(bias_tiles.shape), _const_spec(cover_t.shape)],
        out_specs=pl.BlockSpec((None, TQ, A_WIDTH), lambda i, j: (i, j, 0)),
        out_shape=jax.ShapeDtypeStruct((b, t, A_WIDTH), F32),
        scratch_shapes=[pltpu.VMEM((LANES, N_COL), BF16),
                        tile_bf16, tile_bf16, tile_bf16, tile_bf16,
                        pltpu.VMEM((LANES, LANES), BF16),
                        pltpu.VMEM((nq, TQ, MASK_COLS), F32),
                        pltpu.VMEM((1, N_COL), F32),
                        pltpu.VMEM((1, N_COL), F32),
                        pltpu.VMEM((LANES, N_COL), F32),
                        pltpu.VMEM((3, LANES, N_COL), F32)],
        compiler_params=_cparams(2),
        name="nsa_prompt",
    )(proj3, proj3, proj3, proj3, proj3, proj3, proj3, kcmp, vcmp, bias_cmp, bias_tiles, cover_t)


def _nsa_sample_body(pt_ref, *refs, n_pages, past, wbuf, nc, nb):
    del pt_ref
    pages = [refs[c * n_pages:(c + 1) * n_pages] for c in range(4)]
    (q_ref, kc_ref, vc_ref, ks_ref, vs_ref, kw_ref, vw_ref, gate_ref, za_ref, wk_ref, wv_ref,
     posk_ref, w1k_ref, w2k_ref, posv_ref, w1v_ref, w2v_ref, bcmp_ref, bsel_ref, bwin_ref,
     cover_ref, emat_ref) = refs[4 * n_pages:4 * n_pages + 22]
    o_ref, swk_ref, swv_ref = refs[4 * n_pages + 22:4 * n_pages + 25]
    kc_all, vc_all, ks_all, vs_all, kw_all, vw_all, qbd_ref = refs[4 * n_pages + 25:]
    n_keys = ks_all.shape[0]
    n_win = kw_all.shape[0]

    def assemble(dst, page_refs, new_ref):
        for pg in range(n_pages):
            dst[pg * PAGE_SIZE:(pg + 1) * PAGE_SIZE, :] = page_refs[pg][...]
        dst[past:past + T_PAD, :] = new_ref[...]
        dst[past + T_PAD:, :] = jnp.zeros((dst.shape[0] - past - T_PAD, LANES), F32)

    assemble(kc_all, pages[0], kc_ref)
    assemble(vc_all, pages[1], vc_ref)
    assemble(ks_all, pages[2], ks_ref)
    assemble(vs_all, pages[3], vs_ref)
    for dst, w_ref, new_ref, out_ref in ((kw_all, wk_ref, kw_ref, swk_ref), (vw_all, wv_ref, vw_ref, swv_ref)):
        dst[:wbuf, :] = w_ref[...]
        dst[wbuf:wbuf + T_PAD, :] = new_ref[...]
        dst[wbuf + T_PAD:, :] = jnp.zeros((n_win - wbuf - T_PAD, LANES), F32)
        out_ref[...] = dst[pl.ds(T_PAD // 2, wbuf), :]

    kcmp = _compress(kc_all, posk_ref, w1k_ref, w2k_ref).astype(BF16)
    vcmp = _compress(vc_all, posv_ref, w1v_ref, w2v_ref).astype(BF16)

    _stack_queries(q_ref, qbd_ref, T_PAD)
    qbd = qbd_ref[...]
    nrow = N_ROWBLK * T_PAD
    tok = lax.broadcasted_iota(jnp.int32, (T_PAD, LANES), 0)
    lane = lax.broadcasted_iota(jnp.int32, (T_PAD, LANES), 1)
    qpos = past + tok

    s = _nt(qbd, kcmp).reshape(N_ROWBLK, T_PAD, LANES) + bcmp_ref[...]
    mask_c = ((CMP_STRIDE * lane + CMP_LEN - 1) <= qpos) & (lane < nc)
    p = _masked_softmax(s, mask_c[None])
    o_cmp = _nn(p.reshape(nrow, LANES).astype(BF16), vcmp).reshape(N_ROWBLK, T_PAD, LANES)

    psum = p.reshape(NSA_REP, NSA_KV * T_PAD, LANES).sum(axis=0)
    p_hi, p_lo = _split2(psum)
    imp = _nn(p_hi, cover_ref[...]) + _nn(p_lo, cover_ref[...])
    sel = _select_blocks(imp.reshape(NSA_KV, T_PAD, LANES), qpos[None], nb, 2)
    selk = _nn(sel.reshape(NSA_KV * T_PAD, LANES).astype(BF16), emat_ref[...]).reshape(NSA_KV, T_PAD, n_keys)

    kpos = lax.broadcasted_iota(jnp.int32, (T_PAD, n_keys), 1)
    tok_k = lax.broadcasted_iota(jnp.int32, (T_PAD, n_keys), 0)
    mk = ((kpos <= past + tok_k)[None] & (selk > 0.5))[None]
    s = _nt(qbd, ks_all[...].astype(BF16)).reshape(NSA_REP, NSA_KV, T_PAD, n_keys)
    s = s + bsel_ref[...].reshape(NSA_REP, NSA_KV, T_PAD, n_keys)
    p = _masked_softmax(s, mk)
    o_sel = _nn(p.reshape(nrow, n_keys).astype(BF16), vs_all[...].astype(BF16)).reshape(N_ROWBLK, T_PAD, LANES)

    ki = lax.broadcasted_iota(jnp.int32, (T_PAD, n_win), 1)
    tok_w = lax.broadcasted_iota(jnp.int32, (T_PAD, n_win), 0)
    rel_w = wbuf + tok_w - ki
    mw = ((rel_w >= 0) & (rel_w < WINDOW) & (ki < wbuf + T_PAD))[None]
    s = _nt(qbd, kw_all[...].astype(BF16)).reshape(N_ROWBLK, T_PAD, n_win) + bwin_ref[...]
    p = _masked_softmax(s, mw)
    o_win = _nn(p.reshape(nrow, n_win).astype(BF16), vw_all[...].astype(BF16)).reshape(N_ROWBLK, T_PAD, LANES)

    _gate_and_store(gate_ref, za_ref, o_ref, o_cmp, o_sel, o_win, T_PAD)


def _round_up(x, m):
    return (x + m - 1) // m * m


def _nsa_sample(projs3, caches, wk, wv, page_table, cw_k, cw_v, rel_bias, n_new):
    nseq, n_pages = page_table.shape
    past = n_pages * PAGE_SIZE
    wbuf = wk.shape[1]
    length = past + n_new
    nc = (length - CMP_LEN) // CMP_STRIDE + 1
    nb = -(-length // SEL_BLOCK)
    n_keys = _round_up(max(past + T_PAD, nb * SEL_BLOCK), LANES)
    n_cmp_rows = max(past + T_PAD + CMP_LEN, CMP_SRC_ROWS)
    n_win = _round_up(wbuf + T_PAD, LANES)
    assert nc <= LANES and nb <= LANES and n_new == T_PAD // 2

    bias_cmp = _bias_table_sample(rel_bias, past - (CMP_LEN - 1), CMP_STRIDE, LANES)
    bias_sel = _bias_table_sample(rel_bias, past, 1, n_keys)
    bias_win = _bias_table_sample(rel_bias, wbuf, 1, n_win)
    cover = _cover_matrix(nc, nb)
    emat = _expand_matrix(n_keys)

    def page_spec(pg):
        return pl.BlockSpec((None, PAGE_SIZE, LANES), lambda i, pt: (pt[i, pg], 0, 0))

    def new_spec(width, col):
        return pl.BlockSpec((None, T_PAD, width), lambda i, pt: (i, 0, col))

    kv_col = C_KV // LANES
    in_specs = [page_spec(pg) for _ in range(4) for pg in range(n_pages)]
    in_specs += [new_spec(A_WIDTH, C_Q // A_WIDTH)]
    in_specs += [new_spec(LANES, kv_col + c) for c in range(6)]
    in_specs += [new_spec(LANES, C_GATE // LANES), new_spec(A_WIDTH, C_ZA // A_WIDTH)]
    in_specs += [pl.BlockSpec((None, wbuf, LANES), lambda i, pt: (i, 0, 0))] * 2
    consts = list(cw_k) + list(cw_v) + [bias_cmp, bias_sel, bias_win, cover, emat]
    in_specs += [_const_spec(a.shape) for a in consts]
    operands = [c for c in caches for _ in range(n_pages)] + [projs3] * 9 + [wk, wv] + consts

    grid_spec = pltpu.PrefetchScalarGridSpec(
        num_scalar_prefetch=1,
        grid=(nseq,),
        in_specs=in_specs,
        out_specs=[pl.BlockSpec((None, T_PAD, A_WIDTH), lambda i, pt: (i, 0, 0)),
                   pl.BlockSpec((None, wbuf, LANES), lambda i, pt: (i, 0, 0)),
                   pl.BlockSpec((None, wbuf, LANES), lambda i, pt: (i, 0, 0))],
        scratch_shapes=[pltpu.VMEM((n_cmp_rows, LANES), F32), pltpu.VMEM((n_cmp_rows, LANES), F32),
                        pltpu.VMEM((n_keys, LANES), F32), pltpu.VMEM((n_keys, LANES), F32),
                        pltpu.VMEM((n_win, LANES), F32), pltpu.VMEM((n_win, LANES), F32),
                        pltpu.VMEM((N_ROWBLK * T_PAD, LANES), BF16)])
    return pl.pallas_call(
        functools.partial(_nsa_sample_body, n_pages=n_pages, past=past, wbuf=wbuf, nc=nc, nb=nb),
        grid_spec=grid_spec,
        out_shape=[jax.ShapeDtypeStruct((nseq, T_PAD, A_WIDTH), F32),
                   jax.ShapeDtypeStruct((nseq, wbuf, LANES), F32),
                   jax.ShapeDtypeStruct((nseq, wbuf, LANES), F32)],
        compiler_params=_cparams(1),
        name="nsa_sample",
    )(page_table, *operands)


def _hgrn_constants(rows, seg_stride, seg_len):
    idx = np.arange(rows)
    seg = idx // seg_stride
    pos = idx % seg_stride
    same = seg[:, None] == seg[None, :]
    pt, pu = pos[:, None], pos[None, :]
    mats = [same & (pu <= pt),
            same & (pu > pt) & (pu <= seg_len - 1)]
    levels = []
    m = seg_len // 2
    while m >= 2:
        levels.append(m)
        m //= 2
    for m in levels:
        mats.append(same & (pu >= (pt // m) * m) & (pu <= pt))
        mats.append(same & (pu > pt) & (pu <= (pt // m) * m + m - 1))
    wstack = np.concatenate(mats, axis=0).astype(np.float32)
    wstack = np.concatenate([wstack] * 3, axis=1)
    masks = []
    for m in levels + [1]:
        masks.append(same & (pt // (2 * m) == pu // (2 * m)) & ((pt // m) % 2 == 1) & ((pu // m) % 2 == 0))
    masks.append(idx[:, None] == idx[None, :])
    masks = np.stack(masks).astype(np.float32)
    return jnp.asarray(wstack, BF16), jnp.asarray(masks, F32), len(levels)


def _hgrn_chunk(q, f_pre, hl_ref, wst_ref, n_levels):
    rows = q.shape[0]
    hl = hl_ref[...]
    e = jnp.exp(hl - jnp.max(hl, axis=0, keepdims=True))
    lb = e[0:1] / jnp.sum(e, axis=0, keepdims=True)
    f = lb + (1.0 - lb) * _sigmoid(f_pre)
    g = jnp.log(f)
    k = 1.0 - f
    g_hi = g.astype(BF16)
    r1 = g - g_hi.astype(F32)
    g_mid = r1.astype(BF16)
    g_lo = (r1 - g_mid.astype(F32)).astype(BF16)
    sums = _nn(wst_ref[...], jnp.concatenate([g_hi, g_mid, g_lo], axis=0))
    big_g = sums[0:rows]
    qg = q * jnp.exp(big_g)
    kd = k * jnp.exp(sums[rows:2 * rows])
    q_lv, k_lv = [], []
    for li in range(n_levels):
        q_lv.append(q * jnp.exp(sums[(2 + 2 * li) * rows:(3 + 2 * li) * rows]))
        k_lv.append(k * jnp.exp(sums[(3 + 2 * li) * rows:(4 + 2 * li) * rows]))
    q_lv += [q * f, q]
    k_lv += [k, k]
    return big_g, qg, kd, q_lv, k_lv


def _hgrn_intra(q_lv, k_lv, masks_ref, hs):
    a = None
    for li in range(len(q_lv)):
        term = jnp.where(masks_ref[li] > 0.5, _nt(q_lv[li][:, hs].astype(BF16), k_lv[li][:, hs].astype(BF16)), 0.0)
        a = term if a is None else a + term
    return a


def _hgrn_out(o, ng, z):
    o = o * lax.rsqrt(jnp.mean(o * o, axis=-1, keepdims=True) + EPS)
    return o * ng * _silu(z)


def _hgrn_prompt_body(q_ref, f_ref, i_ref, z_ref, hl_ref, ng_ref, wst_ref, masks_ref, o_ref, s_ref, st_ref,
                      *, n_levels):
    c = pl.program_id(1)

    @pl.when(c == 0)
    def _():
        st_ref[...] = jnp.zeros(st_ref.shape, F32)

    rows = q_ref.shape[0]
    v = i_ref[...]
    big_g, qg, kd, q_lv, k_lv = _hgrn_chunk(q_ref[...], f_ref[...], hl_ref, wst_ref, n_levels)
    for h in range(HG_HEADS):
        hs = slice(h * HG_DK, (h + 1) * HG_DK)
        vb = v[:, hs].astype(BF16)
        a = _hgrn_intra(q_lv, k_lv, masks_ref, hs)
        st = st_ref[h]
        o = _nt(qg[:, hs].astype(BF16), st.astype(BF16)) + _nn(a.astype(BF16), vb)
        o_ref[:, hs] = _hgrn_out(o, ng_ref[:, hs], z_ref[:, hs])
        st_new = st * jnp.exp(big_g[rows - 1:rows, hs]) + _tn(vb, kd[:, hs].astype(BF16))
        st_ref[h] = st_new

        @pl.when(c == pl.num_programs(1) - 1)
        def _():
            s_ref[h] = st_new.T


def _hgrn_prompt(proj3, hg_lower, hg_norm_g):
    b, t, _ = proj3.shape
    rows = math.gcd(t, HG_CHUNK)
    wst, masks, n_levels = _hgrn_constants(rows, rows, rows)
    col = lambda c: pl.BlockSpec((None, rows, B_KEY), lambda i, j: (i, j, c // B_KEY))
    return pl.pallas_call(
        functools.partial(_hgrn_prompt_body, n_levels=n_levels),
        grid=(b, t // rows),
        in_specs=[col(C_QB), col(C_FB), col(C_IB), col(C_ZB),
                  _const_spec(hg_lower.shape), _const_spec(hg_norm_g.shape),
                  _const_spec(wst.shape), _const_spec(masks.shape)],
        out_specs=[pl.BlockSpec((None, rows, B_WIDTH), lambda i, j: (i, j, 0)),
                   pl.BlockSpec((None, HG_HEADS, HG_DK, HG_DV), lambda i, j: (i, 0, 0, 0))],
        out_shape=[jax.ShapeDtypeStruct((b, t, B_WIDTH), F32),
                   jax.ShapeDtypeStruct((b, HG_HEADS, HG_DK, HG_DV), F32)],
        scratch_shapes=[pltpu.VMEM((HG_HEADS, HG_DV, HG_DK), F32)],
        compiler_params=_cparams(2),
        name="hgrn_prompt",
    )(proj3, proj3, proj3, proj3, hg_lower, hg_norm_g, wst, masks)


SEQ_PER_STEP = 8


def _hgrn_sample_body(q_ref, f_ref, i_ref, z_ref, s0_ref, hl_ref, ng_ref, wst_ref, masks_ref, o_ref, s_ref,
                      *, n_levels, n_new):
    v = i_ref[...]
    big_g, qg, kd, q_lv, k_lv = _hgrn_chunk(q_ref[...], f_ref[...], hl_ref, wst_ref, n_levels)
    for h in range(HG_HEADS):
        hs = slice(h * HG_DK, (h + 1) * HG_DK)
        vb = v[:, hs].astype(BF16)
        a = _hgrn_intra(q_lv, k_lv, masks_ref, hs)
        o_intra = _nn(a.astype(BF16), vb)
        for s in range(SEQ_PER_STEP):
            rs = slice(s * T_PAD, (s + 1) * T_PAD)
            s0 = s0_ref[s, h]
            o = _nn(qg[rs, hs].astype(BF16), s0.astype(BF16)) + o_intra[rs]
            o_ref[rs, hs] = _hgrn_out(o, ng_ref[:, hs], z_ref[rs, hs])
            decay = jnp.exp(big_g[rs, hs]).T[:, n_new - 1:n_new]
            s_ref[s, h] = s0 * decay + _tn(kd[rs, hs].astype(BF16), v[rs, hs].astype(BF16))


def _hgrn_sample(projs, state, hg_lower, hg_norm_g, n_new):
    nseq = state.shape[0]
    rows = SEQ_PER_STEP * T_PAD
    wst, masks, n_levels = _hgrn_constants(rows, T_PAD, n_new)
    col = lambda c: pl.BlockSpec((rows, B_KEY), lambda i: (i, c // B_KEY))
    st_spec = pl.BlockSpec((SEQ_PER_STEP, HG_HEADS, HG_DK, HG_DV), lambda i: (i, 0, 0, 0))
    return pl.pallas_call(
        functools.partial(_hgrn_sample_body, n_levels=n_levels, n_new=n_new),
        grid=(nseq // SEQ_PER_STEP,),
        in_specs=[col(C_QB), col(C_FB), col(C_IB), col(C_ZB), st_spec,
                  _const_spec(hg_lower.shape), _const_spec(hg_norm_g.shape),
                  _const_spec(wst.shape), _const_spec(masks.shape)],
        out_specs=[pl.BlockSpec((rows, B_WIDTH), lambda i: (i, 0)), st_spec],
        out_shape=[jax.ShapeDtypeStruct((nseq * T_PAD, B_WIDTH), F32),
                   jax.ShapeDtypeStruct(state.shape, F32)],
        compiler_params=_cparams(1),
        name="hgrn_sample",
    )(projs, projs, projs, projs, state, hg_lower, hg_norm_g, wst, masks)


def _merge_body(oa_ref, ob_ref, ma_ref, mb_ref, x_ref, wba_ref, wbb_ref, wout_ref, fg_ref, y_ref):
    a = _nn(oa_ref[...].astype(BF16), wba_ref[...])
    b = _nn(ob_ref[...].astype(BF16), wbb_ref[...])
    y = _sigmoid(ma_ref[...]) * a + _sigmoid(mb_ref[...]) * b
    h = x_ref[...] + _nn(y.astype(BF16), wout_ref[...])
    ms = jnp.mean(h * h, axis=-1, keepdims=True)
    y_ref[...] = h * lax.rsqrt(ms + EPS) * fg_ref[...]


def _merge(o_a, o_b, proj, x2d, w_ba, w_bb, w_out, final_g):
    m, d = x2d.shape
    tm = min(m, 512)
    row = lambda w, c: pl.BlockSpec((tm, w), lambda i: (i, c))
    return pl.pallas_call(
        _merge_body,
        grid=(m // tm,),
        in_specs=[row(A_WIDTH, 0), row(B_WIDTH, 0), row(d, C_MA // d), row(d, C_MA // d + 1), row(d, 0),
                  _const_spec(w_ba.shape), _const_spec(w_bb.shape), _const_spec(w_out.shape),
                  _const_spec((1, d))],
        out_specs=row(d, 0),
        out_shape=jax.ShapeDtypeStruct((m, d), F32),
        compiler_params=_cparams(1),
        name="merge_out",
    )(o_a, o_b, proj, proj, x2d, w_ba, w_bb, w_out, final_g.reshape(1, d))


def kernel(x_prompt, x_sample, cache_cmp_k, cache_cmp_v, cache_sel_k, cache_sel_v, state_win_k, state_win_v,
           state_hgrn, page_table, norm_g, w_in, cmp_pos_k, cmp_w1_k, cmp_w2_k, cmp_pos_v, cmp_w1_v, cmp_w2_v,
           rel_bias, hg_lower, hg_norm_g, w_branch_a, w_branch_b, w_out, final_g):
    depth = norm_g.shape[0]
    assert depth == 1, "single-layer trunk"
    b, t, d = x_prompt.shape
    nseq, n_new, _ = x_sample.shape
    assert C_MA % d == 0

    w = _arrange_w_in(w_in[0])
    n_cols = w.shape[1]
    w_ba = _permute_heads_cols(w_branch_a[0].T).T.astype(BF16)
    w_bb = w_branch_b[0].astype(BF16)
    w_o = w_out[0].astype(BF16)
    cw_k = _compress_weights(cmp_pos_k[0], cmp_w1_k[0], cmp_w2_k[0])
    cw_v = _compress_weights(cmp_pos_v[0], cmp_w1_v[0], cmp_w2_v[0])
    ng = hg_norm_g[0].reshape(1, B_WIDTH)
    kv_cols = lambda a, c: a[..., C_KV + c * A_KV:C_KV + (c + 1) * A_KV]

    xp2 = x_prompt.reshape(b * t, d)
    proj = _project(xp2, norm_g[0], w)
    proj3 = proj.reshape(b, t, n_cols)
    nc_p = (t - CMP_LEN) // CMP_STRIDE + 1
    kcmp, vcmp = _compress_prompt(proj3, cw_k, cw_v)
    bias_cmp = _bias_table(rel_bias, base=-(CMP_LEN - 1), step=TQ, row_stride=-CMP_STRIDE, col_stride=1,
                           rows=LANES, cols=TQ, steps=t // TQ)
    bias_tiles = _bias_table(rel_bias, base=0, step=TQ, row_stride=-1, col_stride=1, rows=TQ, cols=TQ,
                             steps=WINDOW // TQ + 2, lo=0, hi=WINDOW)
    o_a = _nsa_prompt(proj3, kcmp, vcmp, bias_cmp, bias_tiles)
    o_b, p_hgrn = _hgrn_prompt(proj3, hg_lower, ng)
    y_prompt = _merge(o_a.reshape(b * t, A_WIDTH), o_b.reshape(b * t, B_WIDTH), proj, xp2,
                      w_ba, w_bb, w_o, final_g).reshape(b, t, d)
    wb_p = min(WINDOW, t)
    p_kv = [kv_cols(proj3, c).reshape(1, b, t, NSA_KV, NSA_HD) for c in range(6)]
    p_state = p_kv[:4] + [p_kv[4][:, :, t - wb_p:], p_kv[5][:, :, t - wb_p:], p_hgrn[None]]

    xs_pad = jnp.pad(x_sample, ((0, 0), (0, T_PAD - n_new), (0, 0))).reshape(nseq * T_PAD, d)
    projs = _project(xs_pad, norm_g[0], w)
    projs3 = projs.reshape(nseq, T_PAD, n_cols)
    pool = cache_cmp_k.shape[1]
    caches = [c[0].reshape(pool, PAGE_SIZE, A_KV) for c in (cache_cmp_k, cache_cmp_v, cache_sel_k, cache_sel_v)]
    wbuf = state_win_k.shape[2]
    o_as, s_win_k, s_win_v = _nsa_sample(projs3, caches, state_win_k[0].reshape(nseq, wbuf, A_KV),
                                         state_win_v[0].reshape(nseq, wbuf, A_KV), page_table, cw_k, cw_v,
                                         rel_bias, n_new)
    o_bs, s_hgrn = _hgrn_sample(projs, state_hgrn[0], hg_lower, ng, n_new)
    y_s = _merge(o_as.reshape(nseq * T_PAD, A_WIDTH), o_bs, projs, xs_pad, w_ba, w_bb, w_o, final_g)
    y_sample = y_s.reshape(nseq, T_PAD, d)[:, :n_new]
    s_kv = [kv_cols(projs3, c)[:, :n_new].reshape(1, nseq, n_new, NSA_KV, NSA_HD) for c in range(4)]
    s_state = s_kv + [s_win_k.reshape(1, nseq, wbuf, NSA_KV, NSA_HD),
                      s_win_v.reshape(1, nseq, wbuf, NSA_KV, NSA_HD), s_hgrn[None]]

    return (y_prompt, y_sample, *p_state, *s_state)
```

```python
import functools
import math

import numpy as np
import jax
import jax.numpy as jnp
from jax import lax
from jax.experimental import pallas as pl
from jax.experimental.pallas import tpu as pltpu

F32 = jnp.float32
BF16 = jnp.bfloat16

NSA_HEADS = 8
NSA_KV = 2
NSA_HD = 64
NSA_REP = NSA_HEADS // NSA_KV
CMP_LEN = 32
CMP_STRIDE = 16
CMP_HID = 64
SEL_BLOCK = 64
SEL_TOPN = 8
WINDOW = 512
HG_HEADS = 4
HG_DK = 128
HG_DV = 128
HG_CHUNK = 64
N_BUCKETS = 32
MAX_DIST = 128
EPS = 1e-6
NEG = -1e30
FORCED = 1e9
PAGE_SIZE = 128

A_WIDTH = NSA_HEADS * NSA_HD
A_KV = NSA_KV * NSA_HD
B_WIDTH = HG_HEADS * HG_DV
B_KEY = HG_HEADS * HG_DK

LANES = 128
SUBLANES = 8
VMEM_LIMIT_BYTES = 56 * 1024 * 1024

TQ = 128
T_PAD = SUBLANES
N_ROWBLK = NSA_HEADS

C_Q = 0
C_KV = C_Q + A_WIDTH
C_GATE = C_KV + 6 * A_KV
C_ZA = C_GATE + 2 * LANES
C_QB = C_ZA + A_WIDTH
C_FB = C_QB + B_KEY
C_IB = C_FB + B_KEY
C_ZB = C_IB + B_WIDTH
C_MA = C_ZB + B_WIDTH

CMP_SRC_ROWS = (LANES - 1) * CMP_STRIDE + CMP_LEN
CMP_PITCH = CMP_STRIDE + 1


def _nn(a, b):
    return jnp.dot(a, b, preferred_element_type=F32)


def _nt(a, b):
    return lax.dot_general(a, b, (((1,), (1,)), ((), ())), preferred_element_type=F32)


def _tn(a, b):
    return lax.dot_general(a, b, (((0,), (0,)), ((), ())), preferred_element_type=F32)


def _sigmoid(x):
    return 1.0 / (1.0 + jnp.exp(-x))


def _silu(x):
    return x * _sigmoid(x)


def _split2(x):
    hi = x.astype(BF16)
    lo = (x - hi.astype(F32)).astype(BF16)
    return hi, lo


def _cparams(n_grid):
    return pltpu.CompilerParams(dimension_semantics=("arbitrary",) * n_grid,
                                vmem_limit_bytes=VMEM_LIMIT_BYTES)


def _t5_bucket(rel):
    n = jnp.maximum(rel, 0)
    exact = N_BUCKETS // 2
    nf = jnp.maximum(n, 1).astype(F32)
    scaled = jnp.log(nf / exact) / math.log(MAX_DIST / exact) * (N_BUCKETS - exact)
    large = exact + jnp.floor(scaled).astype(jnp.int32)
    large = jnp.minimum(large, N_BUCKETS - 1)
    return jnp.where(n < exact, n, large)


def _rowblk_head(rb):
    return NSA_REP * (rb % NSA_KV) + rb // NSA_KV


def _bias_body(tab_ref, o_ref, *, base, step, row_stride, col_stride, lo, hi, cols):
    shape = (o_ref.shape[0], cols)
    row = lax.broadcasted_iota(jnp.int32, shape, 0)
    col = lax.broadcasted_iota(jnp.int32, shape, 1)
    rel = base + step * pl.program_id(0) + row_stride * row + col_stride * col
    bucket = _t5_bucket(rel)
    for rb in range(N_ROWBLK):
        h = _rowblk_head(rb)
        acc = jnp.zeros(shape, F32)
        for k in range(N_BUCKETS):
            acc = jnp.where(bucket == k, tab_ref[k, h], acc)
        if lo is not None:
            acc = jnp.where((rel >= lo) & (rel < hi), acc, NEG)
        o_ref[:, rb * cols:(rb + 1) * cols] = acc


def _bias_table(rel_bias, *, base, step, row_stride, col_stride, rows, cols, steps, lo=None, hi=None):
    return pl.pallas_call(
        functools.partial(_bias_body, base=base, step=step, row_stride=row_stride, col_stride=col_stride,
                          lo=lo, hi=hi, cols=cols),
        grid=(steps,),
        in_specs=[pl.BlockSpec(memory_space=pltpu.SMEM)],
        out_specs=pl.BlockSpec((None, rows, N_ROWBLK * cols), lambda i: (i, 0, 0)),
        out_shape=jax.ShapeDtypeStruct((steps, rows, N_ROWBLK * cols), F32),
        compiler_params=_cparams(1),
        name="bias_table",
    )(rel_bias)


def _bias_table_sample(rel_bias, base, col_stride, cols):
    t = _bias_table(rel_bias, base=base, step=0, row_stride=1, col_stride=-col_stride, rows=T_PAD, cols=cols, steps=1)
    return t.reshape(T_PAD, N_ROWBLK, cols).swapaxes(0, 1)


def _proj_body(x_ref, g_ref, w_ref, o_ref, xn_ref):
    @pl.when(pl.program_id(1) == 0)
    def _():
        x = x_ref[...]
        ms = jnp.mean(x * x, axis=-1, keepdims=True)
        xn_ref[...] = (x * lax.rsqrt(ms + EPS) * g_ref[...]).astype(BF16)

    o_ref[...] = _nn(xn_ref[...], w_ref[...])


def _project(x2d, norm_g, w):
    m, d = x2d.shape
    n = w.shape[1]
    tm = min(m, 1024)
    tn = 512
    return pl.pallas_call(
        _proj_body,
        grid=(m // tm, n // tn),
        in_specs=[pl.BlockSpec((tm, d), lambda i, j: (i, 0)),
                  pl.BlockSpec((1, d), lambda i, j: (0, 0)),
                  pl.BlockSpec((d, tn), lambda i, j: (0, j))],
        out_specs=pl.BlockSpec((tm, tn), lambda i, j: (i, j)),
        out_shape=jax.ShapeDtypeStruct((m, n), F32),
        scratch_shapes=[pltpu.VMEM((tm, d), BF16)],
        compiler_params=_cparams(2),
        name="in_proj",
    )(x2d, norm_g.reshape(1, d), w)


def _permute_heads_cols(w):
    lead = w.shape[:-1]
    return w.reshape(lead + (NSA_KV, NSA_REP, NSA_HD)).swapaxes(-3, -2).reshape(lead + (A_WIDTH,))


def _arrange_w_in(w_in):
    d = w_in.shape[0]
    sizes = (A_WIDTH, 6 * A_KV, 3 * NSA_HEADS, A_WIDTH, B_KEY, B_KEY, B_WIDTH, B_WIDTH, d, d)
    offs = np.concatenate([[0], np.cumsum(sizes)])
    q_a, kv_a, g_a, z_a, q_b, f_b, i_b, z_b, m_a, m_b = [w_in[:, offs[i]:offs[i + 1]] for i in range(10)]
    gate_pad = jnp.zeros((d, 2 * LANES - 3 * NSA_HEADS), w_in.dtype)
    w = jnp.concatenate([_permute_heads_cols(q_a), kv_a, g_a, gate_pad, _permute_heads_cols(z_a),
                         q_b, f_b, i_b, z_b, m_a, m_b], axis=1)
    return w.astype(BF16)


def _compress(src_ref, pos_ref, w1_ref, w2_ref, pitch):
    acc = jnp.zeros((LANES, LANES), F32)
    for j in range(CMP_LEN):
        start = (j // CMP_STRIDE) * pitch + j % CMP_STRIDE
        xj = src_ref[pl.ds(start, LANES, stride=pitch), :] + pos_ref[j:j + 1, :]
        acc = acc + _nn(xj.astype(BF16), w1_ref[j])
    return _nn(_silu(acc).astype(BF16), w2_ref[...])


def _compress_prompt_body(kc_ref, vc_ref, posk_ref, w1k_ref, w2k_ref, posv_ref, w1v_ref, w2v_ref,
                          ko_ref, vo_ref, buf_ref):
    t = kc_ref.shape[0]
    buf_ref[t:, :] = jnp.zeros((buf_ref.shape[0] - t, LANES), F32)
    buf_ref[:t, :] = kc_ref[...]
    ko_ref[...] = _compress(buf_ref, posk_ref, w1k_ref, w2k_ref, CMP_STRIDE)
    buf_ref[:t, :] = vc_ref[...]
    vo_ref[...] = _compress(buf_ref, posv_ref, w1v_ref, w2v_ref, CMP_STRIDE)


def _block_diag2(w):
    z = jnp.zeros_like(w)
    return jnp.concatenate([jnp.concatenate([w, z], axis=-1), jnp.concatenate([z, w], axis=-1)], axis=-2)


def _compress_weights(pos, w1, w2):
    pos2 = jnp.concatenate([pos, pos], axis=-1)
    w1bd = _block_diag2(w1.reshape(CMP_LEN, NSA_HD, CMP_HID)).astype(BF16)
    w2bd = _block_diag2(w2).astype(BF16)
    return pos2, w1bd, w2bd


def _const_spec(shape):
    nd = len(shape)
    return pl.BlockSpec(shape, lambda *_: (0,) * nd)


def _compress_prompt(proj3, cw_k, cw_v):
    b, t, _ = proj3.shape
    col = C_KV // LANES
    w_specs = [_const_spec(a.shape) for a in cw_k + cw_v]
    return pl.pallas_call(
        _compress_prompt_body,
        grid=(b,),
        in_specs=[pl.BlockSpec((None, t, LANES), lambda i: (i, 0, col)),
                  pl.BlockSpec((None, t, LANES), lambda i: (i, 0, col + 1))] + w_specs,
        out_specs=[pl.BlockSpec((None, LANES, LANES), lambda i: (i, 0, 0))] * 2,
        out_shape=[jax.ShapeDtypeStruct((b, LANES, LANES), F32)] * 2,
        scratch_shapes=[pltpu.VMEM((max(t + CMP_LEN, CMP_SRC_ROWS), LANES), F32)],
        compiler_params=_cparams(1),
        name="compress_prompt",
    )(proj3, proj3, *cw_k, *cw_v)


def _select_blocks(imp, qpos, nb, axis):
    j = lax.broadcasted_iota(jnp.int32, imp.shape, axis)
    cur = qpos // SEL_BLOCK
    valid = j * SEL_BLOCK <= qpos
    forced = (j == 0) | (j == cur) | (j == cur - 1)
    score = jnp.where(valid, jnp.where(forced, FORCED, imp), -1.0)
    score = jnp.where(j < nb, score, -2.0)
    rank = jnp.zeros(imp.shape, F32)
    for i in range(nb):
        s_i = lax.slice_in_dim(score, i, i + 1, axis=axis)
        tie = jnp.where(i < j, 1.0, 0.0)
        rank = rank + jnp.where(s_i > score, 1.0, jnp.where(s_i == score, tie, 0.0))
    return jnp.where((rank < min(SEL_TOPN, nb)) & (j < nb), 1.0, 0.0)


def _masked_softmax(s, mask):
    s = jnp.where(mask, s, NEG)
    m = jnp.max(s, axis=-1, keepdims=True)
    e = jnp.exp(s - m)
    p = e / jnp.sum(e, axis=-1, keepdims=True)
    return jnp.where(mask, p, 0.0)


def _stack_queries(q_ref, qbd_ref, rows):
    lane = lax.broadcasted_iota(jnp.int32, (rows, LANES), 1)
    for r in range(NSA_REP):
        blk = q_ref[:, r * LANES:(r + 1) * LANES] * (NSA_HD ** -0.5)
        qbd_ref[(2 * r) * rows:(2 * r + 1) * rows, :] = jnp.where(lane < NSA_HD, blk, 0.0).astype(BF16)
        qbd_ref[(2 * r + 1) * rows:(2 * r + 2) * rows, :] = jnp.where(lane >= NSA_HD, blk, 0.0).astype(BF16)


def _gate_and_store(gate_ref, za_ref, o_ref, o_cmp, o_sel, o_win, rows):
    gs = _sigmoid(gate_ref[...])
    lane = lax.broadcasted_iota(jnp.int32, (rows, LANES), 1)
    for r in range(NSA_REP):
        parts = []
        for g in range(NSA_KV):
            rb = 2 * r + g
            h = _rowblk_head(rb)
            o = (gs[:, h:h + 1] * o_cmp[rb] + gs[:, NSA_HEADS + h:NSA_HEADS + h + 1] * o_sel[rb]
                 + gs[:, 2 * NSA_HEADS + h:2 * NSA_HEADS + h + 1] * o_win[rb])
            parts.append(o)
        blk = jnp.where(lane < NSA_HD, parts[0], parts[1])
        o_ref[:, r * LANES:(r + 1) * LANES] = blk * _silu(za_ref[:, r * LANES:(r + 1) * LANES])


N_COL = N_ROWBLK * TQ
MASK_COLS = NSA_KV * TQ
COL_CHUNK = TQ
SEL_TILES_PER_ITER = 2


def _nsa_prompt_body(q_ref, gate_ref, za_ref, ks_ref, vs_ref, kw_ref, vw_ref, kcmp_ref, vcmp_ref,
                     bcmp_ref, btile_ref, covert_ref, o_ref,
                     qt_ref, ksb_ref, vst_ref, kwb_ref, vwt_ref, vcmpt_ref, mask_ref, m_ref, l_ref, acc_ref, res_ref,
                     *, nc, nb):
    qb = pl.program_id(1)
    nkt = ksb_ref.shape[0]
    sub = lax.broadcasted_iota(jnp.int32, (LANES, TQ), 0)
    tok = lax.broadcasted_iota(jnp.int32, (LANES, TQ), 1)

    @pl.when(qb == 0)
    def _():
        for kt in range(nkt):
            rows = slice(kt * TQ, (kt + 1) * TQ)
            ksb_ref[kt] = ks_ref[rows, :].astype(BF16)
            kwb_ref[kt] = kw_ref[rows, :].astype(BF16)
            vst_ref[kt] = vs_ref[rows, :].T.astype(BF16)
            vwt_ref[kt] = vw_ref[rows, :].T.astype(BF16)
        vcmpt_ref[...] = vcmp_ref[...].T.astype(BF16)

    for r in range(NSA_REP):
        qt = (q_ref[:, r * LANES:(r + 1) * LANES] * (NSA_HD ** -0.5)).T
        qt_ref[:, (2 * r) * TQ:(2 * r + 1) * TQ] = jnp.where(sub < NSA_HD, qt, 0.0).astype(BF16)
        qt_ref[:, (2 * r + 1) * TQ:(2 * r + 2) * TQ] = jnp.where(sub >= NSA_HD, qt, 0.0).astype(BF16)

    valid = ((CMP_STRIDE * sub + CMP_LEN - 1) <= qb * TQ + tok) & (sub < nc)
    kcb = kcmp_ref[...].astype(BF16)
    psum = [None] * NSA_KV
    for rb in range(N_ROWBLK):
        cols = slice(rb * TQ, (rb + 1) * TQ)
        s = jnp.where(valid, _nn(kcb, qt_ref[:, cols]) + bcmp_ref[:, cols], NEG)
        e = jnp.exp(s - jnp.max(s, axis=0, keepdims=True))
        p = jnp.where(valid, e / jnp.sum(e, axis=0, keepdims=True), 0.0)
        res_ref[0, :, cols] = _nn(vcmpt_ref[...], p.astype(BF16))
        g = rb % NSA_KV
        psum[g] = p if psum[g] is None else psum[g] + p

    p_hi, p_lo = _split2(jnp.concatenate(psum, axis=1))
    imp = _nn(covert_ref[...], p_hi) + _nn(covert_ref[...], p_lo)
    nbp = _round_up(nb, SUBLANES)
    tok2 = lax.broadcasted_iota(jnp.int32, (1, MASK_COLS), 1) % TQ
    sel = _select_blocks(imp[:nbp], qb * TQ + tok2, nb, 0)
    upper = lax.broadcasted_iota(jnp.int32, (TQ, MASK_COLS), 0) < SEL_BLOCK
    for kt in range(nkt):
        picked = jnp.where(upper, sel[2 * kt:2 * kt + 1, :], sel[2 * kt + 1:2 * kt + 2, :]) > 0.5
        mask_ref[kt] = jnp.where(kt <= qb, jnp.where(picked, 0.0, NEG), NEG)

    m_ref[...] = jnp.full(m_ref.shape, NEG, F32)
    l_ref[...] = jnp.zeros(l_ref.shape, F32)
    acc_ref[...] = jnp.zeros(acc_ref.shape, F32)

    def sel_body(it, carry):
        tiles = []
        for u in range(SEL_TILES_PER_ITER):
            kt = it * SEL_TILES_PER_ITER + u
            tiles.append((ksb_ref[kt], vst_ref[kt], jnp.clip(qb - kt, 0, 2), kt))
        for c in range(N_COL // COL_CHUNK):
            cols = slice(c * COL_CHUNK, (c + 1) * COL_CHUNK)
            mcols = slice(c * COL_CHUNK % MASK_COLS, c * COL_CHUNK % MASK_COLS + COL_CHUNK)
            qc = qt_ref[:, cols]
            ss = [_nn(k_t, qc) + btile_ref[bidx, :, cols] + mask_ref[kt, :, mcols] for k_t, _, bidx, kt in tiles]
            m_old = m_ref[:, cols]
            m_new = m_old
            for s in ss:
                m_new = jnp.maximum(m_new, jnp.max(s, axis=0, keepdims=True))
            alpha = jnp.exp(m_old - m_new)
            l_new = alpha * l_ref[:, cols]
            acc = alpha * acc_ref[:, cols]
            for s, (_, v_t, _, _) in zip(ss, tiles):
                p = jnp.exp(s - m_new)
                l_new = l_new + jnp.sum(p, axis=0, keepdims=True)
                acc = acc + _nn(v_t, p.astype(BF16))
            l_ref[:, cols] = l_new
            acc_ref[:, cols] = acc
            m_ref[:, cols] = m_new
        return carry

    lax.fori_loop(0, qb // SEL_TILES_PER_ITER + 1, sel_body, 0)
    res_ref[1] = acc_ref[...] / l_ref[...]

    n_wt = WINDOW // TQ + 1
    for c in range(N_COL // COL_CHUNK):
        cols = slice(c * COL_CHUNK, (c + 1) * COL_CHUNK)
        qc = qt_ref[:, cols]
        ss = []
        for dist in range(n_wt):
            kt = jnp.maximum(qb - dist, 0)
            bidx = jnp.where(qb >= dist, dist, n_wt)
            ss.append(_nn(kwb_ref[kt], qc) + btile_ref[bidx, :, cols])
        m = jnp.max(ss[0], axis=0, keepdims=True)
        for s in ss[1:]:
            m = jnp.maximum(m, jnp.max(s, axis=0, keepdims=True))
        l_w = None
        acc = None
        for dist, s in enumerate(ss):
            p = jnp.exp(s - m)
            ps = jnp.sum(p, axis=0, keepdims=True)
            pv = _nn(vwt_ref[jnp.maximum(qb - dist, 0)], p.astype(BF16))
            l_w = ps if l_w is None else l_w + ps
            acc = pv if acc is None else acc + pv
        res_ref[2, :, cols] = acc / l_w

    gt = _sigmoid(gate_ref[...]).T
    for r in range(NSA_REP):
        parts = []
        for g in range(NSA_KV):
            rb = 2 * r + g
            h = _rowblk_head(rb)
            cols = slice(rb * TQ, (rb + 1) * TQ)
            parts.append(gt[h:h + 1, :] * res_ref[0, :, cols]
                         + gt[NSA_HEADS + h:NSA_HEADS + h + 1, :] * res_ref[1, :, cols]
                         + gt[2 * NSA_HEADS + h:2 * NSA_HEADS + h + 1, :] * res_ref[2, :, cols])
        blk = jnp.where(sub < NSA_HD, parts[0], parts[1]).T
        o_ref[:, r * LANES:(r + 1) * LANES] = blk * _silu(za_ref[:, r * LANES:(r + 1) * LANES])


def _cover_matrix(nc, nb):
    cs = np.arange(nc) * CMP_STRIDE
    ce = cs + CMP_LEN
    bs = np.arange(nb) * SEL_BLOCK
    be = bs + SEL_BLOCK
    ov = np.clip(np.minimum(ce[:, None], be[None, :]) - np.maximum(cs[:, None], bs[None, :]), 0, None) / CMP_STRIDE
    out = np.zeros((LANES, LANES), np.float32)
    out[:nc, :nb] = ov
    return jnp.asarray(out, BF16)


def _expand_matrix(nkeys):
    j = np.arange(LANES)[:, None]
    k = np.arange(nkeys)[None, :]
    return jnp.asarray((k // SEL_BLOCK == j).astype(np.float32), BF16)


def _nsa_prompt(proj3, kcmp, vcmp, bias_cmp, bias_tiles):
    b, t, _ = proj3.shape
    nq = t // TQ
    nc = (t - CMP_LEN) // CMP_STRIDE + 1
    nb = t // SEL_BLOCK
    assert nc <= LANES and nb <= LANES and t % (TQ * SEL_TILES_PER_ITER) == 0
    kv_col = C_KV // LANES
    kv_spec = lambda c: pl.BlockSpec((None, t, LANES), lambda i, j: (i, 0, kv_col + c))
    cover_t = _cover_matrix(nc, nb).T
    tile_bf16 = pltpu.VMEM((nq, TQ, LANES), BF16)
    return pl.pallas_call(
        functools.partial(_nsa_prompt_body, nc=nc, nb=nb),
        grid=(b, nq),
        in_specs=[pl.BlockSpec((None, TQ, A_WIDTH), lambda i, j: (i, j, C_Q // A_WIDTH)),
                  pl.BlockSpec((None, TQ, LANES), lambda i, j: (i, j, C_GATE // LANES)),
                  pl.BlockSpec((None, TQ, A_WIDTH), lambda i, j: (i, j, C_ZA // A_WIDTH)),
                  kv_spec(2), kv_spec(3), kv_spec(4), kv_spec(5),
                  pl.BlockSpec((None, LANES, LANES), lambda i, j: (i, 0, 0)),
                  pl.BlockSpec((None, LANES, LANES), lambda i, j: (i, 0, 0)),
                  pl.BlockSpec((None, LANES, N_COL), lambda i, j: (j, 0, 0)),
                  _const_spec(bias_tiles.shape), _const_spec(cover_t.shape)],
        out_specs=pl.BlockSpec((None, TQ, A_WIDTH), lambda i, j: (i, j, 0)),
        out_shape=jax.ShapeDtypeStruct((b, t, A_WIDTH), F32),
        scratch_shapes=[pltpu.VMEM((LANES, N_COL), BF16),
                        tile_bf16, tile_bf16, tile_bf16, tile_bf16,
                        pltpu.VMEM((LANES, LANES), BF16),
                        pltpu.VMEM((nq, TQ, MASK_COLS), F32),
                        pltpu.VMEM((1, N_COL), F32),
                        pltpu.VMEM((1, N_COL), F32),
                        pltpu.VMEM((LANES, N_COL), F32),
                        pltpu.VMEM((3, LANES, N_COL), F32)],
        compiler_params=_cparams(2),
        name="nsa_prompt",
    )(proj3, proj3, proj3, proj3, proj3, proj3, proj3, kcmp, vcmp, bias_cmp, bias_tiles, cover_t)


NSA_SEQ_PER_STEP = 1


def _nsa_sample_body(pt_ref, *refs, n_pages, **static):
    del pt_ref
    n_pg = 4 * n_pages * NSA_SEQ_PER_STEP
    per_seq = refs[n_pg:n_pg + 11]
    consts = refs[n_pg + 11:n_pg + 22]
    outs = refs[n_pg + 22:n_pg + 25]
    scratch = refs[n_pg + 25:]
    for s in range(NSA_SEQ_PER_STEP):
        base = 4 * n_pages * s
        pages = [refs[base + c * n_pages:base + (c + 1) * n_pages] for c in range(4)]
        _nsa_sample_one(pages, [r.at[s] for r in per_seq], consts, [r.at[s] for r in outs],
                        [r.at[s] for r in scratch], n_pages=n_pages, **static)


def _nsa_sample_one(pages, per_seq, consts, outs, scratch, *, n_pages, past, wbuf, nc, nb):
    q_ref, kc_ref, vc_ref, ks_ref, vs_ref, kw_ref, vw_ref, gate_ref, za_ref, wk_ref, wv_ref = per_seq
    (posk_ref, w1k_ref, w2k_ref, posv_ref, w1v_ref, w2v_ref, bcmp_ref, bsel_ref, bwin_ref,
     cover_ref, emat_ref) = consts
    o_ref, swk_ref, swv_ref = outs
    kc_all, vc_all, newt_ref, qbd_ref = scratch
    n_keys = bsel_ref.shape[2]
    n_new = T_PAD // 2
    chunks_per_page = PAGE_SIZE // CMP_STRIDE

    for dst, page_refs, new_ref in ((kc_all, pages[0], kc_ref), (vc_all, pages[1], vc_ref)):
        for pg in range(n_pages):
            rows = page_refs[pg][...].T
            for c in range(chunks_per_page):
                r0 = (pg * chunks_per_page + c) * CMP_PITCH
                dst[r0:r0 + CMP_STRIDE, :] = rows[c * CMP_STRIDE:(c + 1) * CMP_STRIDE, :]
        r0 = (past // CMP_STRIDE) * CMP_PITCH
        dst[r0:r0 + T_PAD, :] = new_ref[...]
        dst[r0 + T_PAD:, :] = jnp.zeros((dst.shape[0] - r0 - T_PAD, LANES), F32)
    kcmp = _compress(kc_all, posk_ref, w1k_ref, w2k_ref, CMP_PITCH).astype(BF16)
    vcmp = _compress(vc_all, posv_ref, w1v_ref, w2v_ref, CMP_PITCH).astype(BF16)

    newt_ref[...] = jnp.zeros(newt_ref.shape, F32)
    for idx, new_ref in enumerate((ks_ref, vs_ref, kw_ref, vw_ref)):
        newt_ref[idx, 0:T_PAD, :] = new_ref[...]

    _stack_queries(q_ref, qbd_ref, T_PAD)
    qbd = qbd_ref[...]
    nrow = N_ROWBLK * T_PAD
    tok = lax.broadcasted_iota(jnp.int32, (T_PAD, LANES), 0)
    lane = lax.broadcasted_iota(jnp.int32, (T_PAD, LANES), 1)
    qpos = past + tok
    by_head = lambda x: x.reshape(NSA_REP, NSA_KV, T_PAD, LANES)

    s = _nt(qbd, kcmp).reshape(N_ROWBLK, T_PAD, LANES) + bcmp_ref[...]
    mask_c = ((CMP_STRIDE * lane + CMP_LEN - 1) <= qpos) & (lane < nc)
    p = _masked_softmax(s, mask_c[None])
    o_cmp = _nn(p.reshape(nrow, LANES).astype(BF16), vcmp).reshape(N_ROWBLK, T_PAD, LANES)

    psum = p.reshape(NSA_REP, NSA_KV * T_PAD, LANES).sum(axis=0)
    p_hi, p_lo = _split2(psum)
    imp = _nn(p_hi, cover_ref[...]) + _nn(p_lo, cover_ref[...])
    sel = _select_blocks(imp.reshape(NSA_KV, T_PAD, LANES), qpos[None], nb, 2)
    selk = _nn(sel.reshape(NSA_KV * T_PAD, LANES).astype(BF16), emat_ref[...]).reshape(NSA_KV, T_PAD, n_keys)

    def softmax_pv(scores, values):
        m = scores[0]
        for s_t in scores[1:]:
            m = jnp.maximum(m, s_t)
        m = jnp.max(m, axis=-1, keepdims=True)
        acc = None
        tot = None
        for s_t, (v_t, channel_major) in zip(scores, values):
            p_t = jnp.exp(s_t - m)
            pb = p_t.reshape(nrow, p_t.shape[-1]).astype(BF16)
            pv = _nt(pb, v_t) if channel_major else _nn(pb, v_t)
            acc = pv if acc is None else acc + pv
            tot = p_t if tot is None else tot + p_t
        return (acc / jnp.sum(tot, axis=-1, keepdims=True).reshape(nrow, 1)).reshape(N_ROWBLK, T_PAD, LANES)

    scores, values = [], []
    for t in range(n_pages + 1):
        cols = slice(t * LANES, (t + 1) * LANES)
        allowed = ((t * LANES + lane) <= qpos)[None] & (selk[:, :, cols] > 0.5)
        madd = jnp.where(allowed, 0.0, NEG)[None]
        if t < n_pages:
            raw = _nn(qbd, pages[2][t][...].astype(BF16))
            values.append((pages[3][t][...].astype(BF16), True))
        else:
            raw = _nt(qbd, newt_ref[0].astype(BF16))
            values.append((newt_ref[1].astype(BF16), False))
        scores.append(by_head(raw) + by_head(bsel_ref[:, :, cols]) + madd)
    o_sel = softmax_pv(scores, values)

    scores, values = [], []
    kwt = wk_ref[...].astype(BF16)
    vwt = wv_ref[...].astype(BF16)
    for t in range(wbuf // LANES + 1):
        cols = slice(t * LANES, (t + 1) * LANES)
        rel_w = wbuf + tok - (t * LANES + lane)
        madd = jnp.where((rel_w >= 0) & (rel_w < WINDOW), 0.0, NEG)[None, None]
        if t < wbuf // LANES:
            raw = _nn(qbd, kwt[:, cols])
            values.append((vwt[:, cols], True))
        else:
            raw = _nt(qbd, newt_ref[2].astype(BF16))
            values.append((newt_ref[3].astype(BF16), False))
        scores.append(by_head(raw) + by_head(bwin_ref[:, :, cols]) + madd)
    o_win = softmax_pv(scores, values)

    _gate_and_store(gate_ref, za_ref, o_ref, o_cmp, o_sel, o_win, T_PAD)

    lane_full = lax.broadcasted_iota(jnp.int32, (LANES, LANES), 1)
    for idx, w_ref, out_ref in ((2, wk_ref, swk_ref), (3, wv_ref, swv_ref)):
        rolled = pltpu.roll(w_ref[...], wbuf - n_new, axis=1)
        tail = pltpu.roll(newt_ref[idx], LANES - n_new, axis=0).T
        out_ref[:, :wbuf - LANES] = rolled[:, :wbuf - LANES]
        out_ref[:, wbuf - LANES:] = jnp.where(lane_full >= LANES - n_new, tail, rolled[:, wbuf - LANES:])


def _round_up(x, m):
    return (x + m - 1) // m * m


def _nsa_sample(projs3, caches, wk, wv, page_table, cw_k, cw_v, rel_bias, n_new):
    nseq, n_pages = page_table.shape
    past = n_pages * PAGE_SIZE
    wbuf = wk.shape[2]
    length = past + n_new
    nc = (length - CMP_LEN) // CMP_STRIDE + 1
    nb = -(-length // SEL_BLOCK)
    n_keys = past + LANES
    n_win = wbuf + LANES
    pitched = lambda p: (p // CMP_STRIDE) * CMP_PITCH + p % CMP_STRIDE
    n_cmp_rows = _round_up(max(pitched(past) + T_PAD, pitched(CMP_SRC_ROWS - 1) + 1), SUBLANES)
    assert nc <= LANES and nb <= LANES and n_new == T_PAD // 2 and nb * SEL_BLOCK <= n_keys
    assert wbuf % LANES == 0 and PAGE_SIZE == LANES

    bias_cmp = _bias_table_sample(rel_bias, past - (CMP_LEN - 1), CMP_STRIDE, LANES)
    bias_sel = _bias_table_sample(rel_bias, past, 1, n_keys)
    bias_win = _bias_table_sample(rel_bias, wbuf, 1, n_win)
    cover = _cover_matrix(nc, nb)
    emat = _expand_matrix(n_keys)

    sps = NSA_SEQ_PER_STEP
    assert nseq % sps == 0

    def page_spec(s, pg):
        return pl.BlockSpec((None, LANES, PAGE_SIZE), lambda i, pt: (pt[i * sps + s, pg], 0, 0))

    def new_spec(width, col):
        return pl.BlockSpec((sps, T_PAD, width), lambda i, pt: (i, 0, col))

    kv_col = C_KV // LANES
    in_specs = [page_spec(s, pg) for s in range(sps) for _ in range(4) for pg in range(n_pages)]
    in_specs += [new_spec(A_WIDTH, C_Q // A_WIDTH)]
    in_specs += [new_spec(LANES, kv_col + c) for c in range(6)]
    in_specs += [new_spec(LANES, C_GATE // LANES), new_spec(A_WIDTH, C_ZA // A_WIDTH)]
    win_spec = pl.BlockSpec((sps, LANES, wbuf), lambda i, pt: (i, 0, 0))
    in_specs += [win_spec] * 2
    consts = list(cw_k) + list(cw_v) + [bias_cmp, bias_sel, bias_win, cover, emat]
    in_specs += [_const_spec(a.shape) for a in consts]
    operands = [c for c in caches for _ in range(n_pages)] * sps + [projs3] * 9 + [wk, wv] + consts

    grid_spec = pltpu.PrefetchScalarGridSpec(
        num_scalar_prefetch=1,
        grid=(nseq // sps,),
        in_specs=in_specs,
        out_specs=[pl.BlockSpec((sps, T_PAD, A_WIDTH), lambda i, pt: (i, 0, 0)), win_spec, win_spec],
        scratch_shapes=[pltpu.VMEM((sps, n_cmp_rows, LANES), F32), pltpu.VMEM((sps, n_cmp_rows, LANES), F32),
                        pltpu.VMEM((sps, 4, LANES, LANES), F32),
                        pltpu.VMEM((sps, N_ROWBLK * T_PAD, LANES), BF16)])
    return pl.pallas_call(
        functools.partial(_nsa_sample_body, n_pages=n_pages, past=past, wbuf=wbuf, nc=nc, nb=nb),
        grid_spec=grid_spec,
        out_shape=[jax.ShapeDtypeStruct((nseq, T_PAD, A_WIDTH), F32),
                   jax.ShapeDtypeStruct((nseq, LANES, wbuf), F32),
                   jax.ShapeDtypeStruct((nseq, LANES, wbuf), F32)],
        compiler_params=_cparams(1),
        name="nsa_sample",
    )(page_table, *operands)


def _hgrn_constants(rows, seg_stride, seg_len):
    idx = np.arange(rows)
    seg = idx // seg_stride
    pos = idx % seg_stride
    same = seg[:, None] == seg[None, :]
    pt, pu = pos[:, None], pos[None, :]
    mats = [same & (pu <= pt),
            same & (pu > pt) & (pu <= seg_len - 1)]
    levels = []
    m = seg_len // 2
    while m >= 2:
        levels.append(m)
        m //= 2
    for m in levels:
        mats.append(same & (pu >= (pt // m) * m) & (pu <= pt))
        mats.append(same & (pu > pt) & (pu <= (pt // m) * m + m - 1))
    wstack = np.concatenate(mats, axis=0).astype(np.float32)
    wstack = np.concatenate([wstack] * 3, axis=1)
    masks = []
    for m in levels + [1]:
        masks.append(same & (pt // (2 * m) == pu // (2 * m)) & ((pt // m) % 2 == 1) & ((pu // m) % 2 == 0))
    masks.append(idx[:, None] == idx[None, :])
    masks = np.stack(masks).astype(np.float32)
    return jnp.asarray(wstack, BF16), jnp.asarray(masks, F32), len(levels)


def _hgrn_chunk(q, f_pre, hl_ref, wst_ref, n_levels):
    rows = q.shape[0]
    hl = hl_ref[...]
    e = jnp.exp(hl - jnp.max(hl, axis=0, keepdims=True))
    lb = e[0:1] / jnp.sum(e, axis=0, keepdims=True)
    f = lb + (1.0 - lb) * _sigmoid(f_pre)
    g = jnp.log(f)
    k = 1.0 - f
    g_hi = g.astype(BF16)
    r1 = g - g_hi.astype(F32)
    g_mid = r1.astype(BF16)
    g_lo = (r1 - g_mid.astype(F32)).astype(BF16)
    sums = _nn(wst_ref[...], jnp.concatenate([g_hi, g_mid, g_lo], axis=0))
    big_g = sums[0:rows]
    qg = q * jnp.exp(big_g)
    kd = k * jnp.exp(sums[rows:2 * rows])
    q_lv, k_lv = [], []
    for li in range(n_levels):
        q_lv.append(q * jnp.exp(sums[(2 + 2 * li) * rows:(3 + 2 * li) * rows]))
        k_lv.append(k * jnp.exp(sums[(3 + 2 * li) * rows:(4 + 2 * li) * rows]))
    q_lv += [q * f, q]
    k_lv += [k, k]
    return big_g, qg, kd, q_lv, k_lv


def _hgrn_intra(q_lv, k_lv, masks_ref, hs):
    a = None
    for li in range(len(q_lv)):
        term = jnp.where(masks_ref[li] > 0.5, _nt(q_lv[li][:, hs].astype(BF16), k_lv[li][:, hs].astype(BF16)), 0.0)
        a = term if a is None else a + term
    return a


def _hgrn_out(o, ng, z):
    o = o * lax.rsqrt(jnp.mean(o * o, axis=-1, keepdims=True) + EPS)
    return o * ng * _silu(z)


def _hgrn_prompt_body(q_ref, f_ref, i_ref, z_ref, hl_ref, ng_ref, wst_ref, masks_ref, o_ref, s_ref, st_ref,
                      *, n_levels):
    c = pl.program_id(1)

    @pl.when(c == 0)
    def _():
        st_ref[...] = jnp.zeros(st_ref.shape, F32)

    rows = q_ref.shape[0]
    v = i_ref[...]
    big_g, qg, kd, q_lv, k_lv = _hgrn_chunk(q_ref[...], f_ref[...], hl_ref, wst_ref, n_levels)
    for h in range(HG_HEADS):
        hs = slice(h * HG_DK, (h + 1) * HG_DK)
        vb = v[:, hs].astype(BF16)
        a = _hgrn_intra(q_lv, k_lv, masks_ref, hs)
        st = st_ref[h]
        o = _nt(qg[:, hs].astype(BF16), st.astype(BF16)) + _nn(a.astype(BF16), vb)
        o_ref[:, hs] = _hgrn_out(o, ng_ref[:, hs], z_ref[:, hs])
        st_new = st * jnp.exp(big_g[rows - 1:rows, hs]) + _tn(vb, kd[:, hs].astype(BF16))
        st_ref[h] = st_new

        @pl.when(c == pl.num_programs(1) - 1)
        def _():
            s_ref[h] = st_new.T


def _hgrn_prompt(proj3, hg_lower, hg_norm_g):
    b, t, _ = proj3.shape
    rows = math.gcd(t, HG_CHUNK)
    wst, masks, n_levels = _hgrn_constants(rows, rows, rows)
    col = lambda c: pl.BlockSpec((None, rows, B_KEY), lambda i, j: (i, j, c // B_KEY))
    return pl.pallas_call(
        functools.partial(_hgrn_prompt_body, n_levels=n_levels),
        grid=(b, t // rows),
        in_specs=[col(C_QB), col(C_FB), col(C_IB), col(C_ZB),
                  _const_spec(hg_lower.shape), _const_spec(hg_norm_g.shape),
                  _const_spec(wst.shape), _const_spec(masks.shape)],
        out_specs=[pl.BlockSpec((None, rows, B_WIDTH), lambda i, j: (i, j, 0)),
                   pl.BlockSpec((None, HG_HEADS, HG_DK, HG_DV), lambda i, j: (i, 0, 0, 0))],
        out_shape=[jax.ShapeDtypeStruct((b, t, B_WIDTH), F32),
                   jax.ShapeDtypeStruct((b, HG_HEADS, HG_DK, HG_DV), F32)],
        scratch_shapes=[pltpu.VMEM((HG_HEADS, HG_DV, HG_DK), F32)],
        compiler_params=_cparams(2),
        name="hgrn_prompt",
    )(proj3, proj3, proj3, proj3, hg_lower, hg_norm_g, wst, masks)


SEQ_PER_STEP = 8


def _hgrn_sample_body(q_ref, f_ref, i_ref, z_ref, s0_ref, hl_ref, ng_ref, wst_ref, masks_ref, o_ref, s_ref,
                      *, n_levels, n_new):
    v = i_ref[...]
    big_g, qg, kd, q_lv, k_lv = _hgrn_chunk(q_ref[...], f_ref[...], hl_ref, wst_ref, n_levels)
    for h in range(HG_HEADS):
        hs = slice(h * HG_DK, (h + 1) * HG_DK)
        vb = v[:, hs].astype(BF16)
        a = _hgrn_intra(q_lv, k_lv, masks_ref, hs)
        o_intra = _nn(a.astype(BF16), vb)
        for s in range(SEQ_PER_STEP):
            rs = slice(s * T_PAD, (s + 1) * T_PAD)
            s0 = s0_ref[s, h]
            o = _nn(qg[rs, hs].astype(BF16), s0.astype(BF16)) + o_intra[rs]
            o_ref[rs, hs] = _hgrn_out(o, ng_ref[:, hs], z_ref[rs, hs])
            decay = jnp.exp(big_g[rs, hs]).T[:, n_new - 1:n_new]
            s_ref[s, h] = s0 * decay + _tn(kd[rs, hs].astype(BF16), v[rs, hs].astype(BF16))


def _hgrn_sample(projs, state, hg_lower, hg_norm_g, n_new):
    nseq = state.shape[0]
    rows = SEQ_PER_STEP * T_PAD
    wst, masks, n_levels = _hgrn_constants(rows, T_PAD, n_new)
    col = lambda c: pl.BlockSpec((rows, B_KEY), lambda i: (i, c // B_KEY))
    st_spec = pl.BlockSpec((SEQ_PER_STEP, HG_HEADS, HG_DK, HG_DV), lambda i: (i, 0, 0, 0))
    return pl.pallas_call(
        functools.partial(_hgrn_sample_body, n_levels=n_levels, n_new=n_new),
        grid=(nseq // SEQ_PER_STEP,),
        in_specs=[col(C_QB), col(C_FB), col(C_IB), col(C_ZB), st_spec,
                  _const_spec(hg_lower.shape), _const_spec(hg_norm_g.shape),
                  _const_spec(wst.shape), _const_spec(masks.shape)],
        out_specs=[pl.BlockSpec((rows, B_WIDTH), lambda i: (i, 0)), st_spec],
        out_shape=[jax.ShapeDtypeStruct((nseq * T_PAD, B_WIDTH), F32),
                   jax.ShapeDtypeStruct(state.shape, F32)],
        compiler_params=_cparams(1),
        name="hgrn_sample",
    )(projs, projs, projs, projs, state, hg_lower, hg_norm_g, wst, masks)


def _merge_body(oa_ref, ob_ref, ma_ref, mb_ref, x_ref, wba_ref, wbb_ref, wout_ref, fg_ref, y_ref):
    a = _nn(oa_ref[...].astype(BF16), wba_ref[...])
    b = _nn(ob_ref[...].astype(BF16), wbb_ref[...])
    y = _sigmoid(ma_ref[...]) * a + _sigmoid(mb_ref[...]) * b
    h = x_ref[...] + _nn(y.astype(BF16), wout_ref[...])
    ms = jnp.mean(h * h, axis=-1, keepdims=True)
    y_ref[...] = h * lax.rsqrt(ms + EPS) * fg_ref[...]


def _merge(o_a, o_b, proj, x2d, w_ba, w_bb, w_out, final_g):
    m, d = x2d.shape
    tm = min(m, 512)
    row = lambda w, c: pl.BlockSpec((tm, w), lambda i: (i, c))
    return pl.pallas_call(
        _merge_body,
        grid=(m // tm,),
        in_specs=[row(A_WIDTH, 0), row(B_WIDTH, 0), row(d, C_MA // d), row(d, C_MA // d + 1), row(d, 0),
                  _const_spec(w_ba.shape), _const_spec(w_bb.shape), _const_spec(w_out.shape),
                  _const_spec((1, d))],
        out_specs=row(d, 0),
        out_shape=jax.ShapeDtypeStruct((m, d), F32),
        compiler_params=_cparams(1),
        name="merge_out",
    )(o_a, o_b, proj, proj, x2d, w_ba, w_bb, w_out, final_g.reshape(1, d))


def kernel(x_prompt, x_sample, cache_cmp_k, cache_cmp_v, cache_sel_k, cache_sel_v, state_win_k, state_win_v,
           state_hgrn, page_table, norm_g, w_in, cmp_pos_k, cmp_w1_k, cmp_w2_k, cmp_pos_v, cmp_w1_v, cmp_w2_v,
           rel_bias, hg_lower, hg_norm_g, w_branch_a, w_branch_b, w_out, final_g):
    depth = norm_g.shape[0]
    assert depth == 1, "single-layer trunk"
    b, t, d = x_prompt.shape
    nseq, n_new, _ = x_sample.shape
    assert C_MA % d == 0

    w = _arrange_w_in(w_in[0])
    n_cols = w.shape[1]
    w_ba = _permute_heads_cols(w_branch_a[0].T).T.astype(BF16)
    w_bb = w_branch_b[0].astype(BF16)
    w_o = w_out[0].astype(BF16)
    cw_k = _compress_weights(cmp_pos_k[0], cmp_w1_k[0], cmp_w2_k[0])
    cw_v = _compress_weights(cmp_pos_v[0], cmp_w1_v[0], cmp_w2_v[0])
    ng = hg_norm_g[0].reshape(1, B_WIDTH)
    kv_cols = lambda a, c: a[..., C_KV + c * A_KV:C_KV + (c + 1) * A_KV]

    xp2 = x_prompt.reshape(b * t, d)
    proj = _project(xp2, norm_g[0], w)
    proj3 = proj.reshape(b, t, n_cols)
    nc_p = (t - CMP_LEN) // CMP_STRIDE + 1
    kcmp, vcmp = _compress_prompt(proj3, cw_k, cw_v)
    bias_cmp = _bias_table(rel_bias, base=-(CMP_LEN - 1), step=TQ, row_stride=-CMP_STRIDE, col_stride=1,
                           rows=LANES, cols=TQ, steps=t // TQ)
    bias_tiles = _bias_table(rel_bias, base=0, step=TQ, row_stride=-1, col_stride=1, rows=TQ, cols=TQ,
                             steps=WINDOW // TQ + 2, lo=0, hi=WINDOW)
    o_a = _nsa_prompt(proj3, kcmp, vcmp, bias_cmp, bias_tiles)
    o_b, p_hgrn = _hgrn_prompt(proj3, hg_lower, ng)
    y_prompt = _merge(o_a.reshape(b * t, A_WIDTH), o_b.reshape(b * t, B_WIDTH), proj, xp2,
                      w_ba, w_bb, w_o, final_g).reshape(b, t, d)
    wb_p = min(WINDOW, t)
    p_kv = [kv_cols(proj3, c).reshape(1, b, t, NSA_KV, NSA_HD) for c in range(6)]
    p_state = p_kv[:4] + [p_kv[4][:, :, t - wb_p:], p_kv[5][:, :, t - wb_p:], p_hgrn[None]]

    xs_pad = jnp.pad(x_sample, ((0, 0), (0, T_PAD - n_new), (0, 0))).reshape(nseq * T_PAD, d)
    projs = _project(xs_pad, norm_g[0], w)
    projs3 = projs.reshape(nseq, T_PAD, n_cols)
    pool = cache_cmp_k.shape[1]
    chan_major = lambda a: jnp.transpose(a, (0, 2, 3, 1)).reshape(a.shape[0], A_KV, a.shape[1])
    row_major = lambda a: jnp.transpose(a.reshape(a.shape[0], NSA_KV, NSA_HD, a.shape[2]), (0, 3, 1, 2))
    caches = [chan_major(c[0]) for c in (cache_cmp_k, cache_cmp_v, cache_sel_k, cache_sel_v)]
    wbuf = state_win_k.shape[2]
    o_as, s_win_k, s_win_v = _nsa_sample(projs3, caches, chan_major(state_win_k[0]), chan_major(state_win_v[0]),
                                         page_table, cw_k, cw_v, rel_bias, n_new)
    s_win_k, s_win_v = row_major(s_win_k), row_major(s_win_v)
    o_bs, s_hgrn = _hgrn_sample(projs, state_hgrn[0], hg_lower, ng, n_new)
    y_s = _merge(o_as.reshape(nseq * T_PAD, A_WIDTH), o_bs, projs, xs_pad, w_ba, w_bb, w_o, final_g)
    y_sample = y_s.reshape(nseq, T_PAD, d)[:, :n_new]
    s_kv = [kv_cols(projs3, c)[:, :n_new].reshape(1, nseq, n_new, NSA_KV, NSA_HD) for c in range(4)]
    s_state = s_kv + [s_win_k.reshape(1, nseq, wbuf, NSA_KV, NSA_HD),
                      s_win_v.reshape(1, nseq, wbuf, NSA_KV, NSA_HD), s_hgrn[None]]

    return (y_prompt, y_sample, *p_state, *s_state)
```

```python
import functools
import math

import numpy as np
import jax
import jax.numpy as jnp
from jax import lax
from jax.experimental import pallas as pl
from jax.experimental.pallas import tpu as pltpu

F32 = jnp.float32
BF16 = jnp.bfloat16

NSA_HEADS = 8
NSA_KV = 2
NSA_HD = 64
NSA_REP = NSA_HEADS // NSA_KV
CMP_LEN = 32
CMP_STRIDE = 16
CMP_HID = 64
SEL_BLOCK = 64
SEL_TOPN = 8
WINDOW = 512
HG_HEADS = 4
HG_DK = 128
HG_DV = 128
HG_CHUNK = 64
N_BUCKETS = 32
MAX_DIST = 128
EPS = 1e-6
NEG = -1e30
FORCED = 1e9
PAGE_SIZE = 128

A_WIDTH = NSA_HEADS * NSA_HD
A_KV = NSA_KV * NSA_HD
B_WIDTH = HG_HEADS * HG_DV
B_KEY = HG_HEADS * HG_DK

LANES = 128
SUBLANES = 8
VMEM_LIMIT_BYTES = 56 * 1024 * 1024

TQ = 128
T_PAD = SUBLANES
N_ROWBLK = NSA_HEADS

C_Q = 0
C_KV = C_Q + A_WIDTH
C_GATE = C_KV + 6 * A_KV
C_ZA = C_GATE + 2 * LANES
C_QB = C_ZA + A_WIDTH
C_FB = C_QB + B_KEY
C_IB = C_FB + B_KEY
C_ZB = C_IB + B_WIDTH
C_MA = C_ZB + B_WIDTH

CMP_SRC_ROWS = (LANES - 1) * CMP_STRIDE + CMP_LEN
CMP_PITCH = CMP_STRIDE + 1


def _nn(a, b):
    return jnp.dot(a, b, preferred_element_type=F32)


def _nt(a, b):
    return lax.dot_general(a, b, (((1,), (1,)), ((), ())), preferred_element_type=F32)


def _tn(a, b):
    return lax.dot_general(a, b, (((0,), (0,)), ((), ())), preferred_element_type=F32)


def _sigmoid(x):
    return 1.0 / (1.0 + jnp.exp(-x))


def _silu(x):
    return x * _sigmoid(x)


def _split2(x):
    hi = x.astype(BF16)
    lo = (x - hi.astype(F32)).astype(BF16)
    return hi, lo


def _cparams(n_grid):
    return pltpu.CompilerParams(dimension_semantics=("arbitrary",) * n_grid,
                                vmem_limit_bytes=VMEM_LIMIT_BYTES)


def _t5_bucket(rel):
    n = jnp.maximum(rel, 0)
    exact = N_BUCKETS // 2
    nf = jnp.maximum(n, 1).astype(F32)
    scaled = jnp.log(nf / exact) / math.log(MAX_DIST / exact) * (N_BUCKETS - exact)
    large = exact + jnp.floor(scaled).astype(jnp.int32)
    large = jnp.minimum(large, N_BUCKETS - 1)
    return jnp.where(n < exact, n, large)


def _rowblk_head(rb):
    return NSA_REP * (rb % NSA_KV) + rb // NSA_KV


def _bias_body(tab_ref, o_ref, *, base, step, row_stride, col_stride, lo, hi, cols):
    shape = (o_ref.shape[0], cols)
    row = lax.broadcasted_iota(jnp.int32, shape, 0)
    col = lax.broadcasted_iota(jnp.int32, shape, 1)
    rel = base + step * pl.program_id(0) + row_stride * row + col_stride * col
    bucket = _t5_bucket(rel)
    for rb in range(N_ROWBLK):
        h = _rowblk_head(rb)
        acc = jnp.zeros(shape, F32)
        for k in range(N_BUCKETS):
            acc = jnp.where(bucket == k, tab_ref[k, h], acc)
        if lo is not None:
            acc = jnp.where((rel >= lo) & (rel < hi), acc, NEG)
        o_ref[:, rb * cols:(rb + 1) * cols] = acc


def _bias_table(rel_bias, *, base, step, row_stride, col_stride, rows, cols, steps, lo=None, hi=None):
    return pl.pallas_call(
        functools.partial(_bias_body, base=base, step=step, row_stride=row_stride, col_stride=col_stride,
                          lo=lo, hi=hi, cols=cols),
        grid=(steps,),
        in_specs=[pl.BlockSpec(memory_space=pltpu.SMEM)],
        out_specs=pl.BlockSpec((None, rows, N_ROWBLK * cols), lambda i: (i, 0, 0)),
        out_shape=jax.ShapeDtypeStruct((steps, rows, N_ROWBLK * cols), F32),
        compiler_params=_cparams(1),
        name="bias_table",
    )(rel_bias)


def _bias_table_sample(rel_bias, base, col_stride, cols):
    t = _bias_table(rel_bias, base=base, step=0, row_stride=1, col_stride=-col_stride, rows=T_PAD, cols=cols, steps=1)
    return t.reshape(T_PAD, N_ROWBLK, cols).swapaxes(0, 1)


def _proj_body(x_ref, g_ref, w_ref, o_ref):
    x = x_ref[...]
    ms = jnp.mean(x * x, axis=-1, keepdims=True)
    xn = (x * lax.rsqrt(ms + EPS) * g_ref[...]).astype(BF16)
    o_ref[...] = _nn(xn, w_ref[...])


PROJ_ROWS = 512


def _project(x2d, norm_g, w):
    m, d = x2d.shape
    n = w.shape[1]
    tm = min(m, PROJ_ROWS)
    return pl.pallas_call(
        _proj_body,
        grid=(m // tm,),
        in_specs=[pl.BlockSpec((tm, d), lambda i: (i, 0)),
                  pl.BlockSpec((1, d), lambda i: (0, 0)),
                  pl.BlockSpec((d, n), lambda i: (0, 0), pipeline_mode=pl.Buffered(1))],
        out_specs=pl.BlockSpec((tm, n), lambda i: (i, 0)),
        out_shape=jax.ShapeDtypeStruct((m, n), F32),
        compiler_params=_cparams(1),
        name="in_proj",
    )(x2d, norm_g.reshape(1, d), w)


def _permute_heads_cols(w):
    lead = w.shape[:-1]
    return w.reshape(lead + (NSA_KV, NSA_REP, NSA_HD)).swapaxes(-3, -2).reshape(lead + (A_WIDTH,))


def _arrange_w_in(w_in):
    d = w_in.shape[0]
    sizes = (A_WIDTH, 6 * A_KV, 3 * NSA_HEADS, A_WIDTH, B_KEY, B_KEY, B_WIDTH, B_WIDTH, d, d)
    offs = np.concatenate([[0], np.cumsum(sizes)])
    q_a, kv_a, g_a, z_a, q_b, f_b, i_b, z_b, m_a, m_b = [w_in[:, offs[i]:offs[i + 1]] for i in range(10)]
    gate_pad = jnp.zeros((d, 2 * LANES - 3 * NSA_HEADS), w_in.dtype)
    w = jnp.concatenate([_permute_heads_cols(q_a), kv_a, g_a, gate_pad, _permute_heads_cols(z_a),
                         q_b, f_b, i_b, z_b, m_a, m_b], axis=1)
    return w.astype(BF16)


def _compress(src_ref, pos_ref, w1_ref, w2_ref, pitch):
    acc = jnp.zeros((LANES, LANES), F32)
    for j in range(CMP_LEN):
        start = (j // CMP_STRIDE) * pitch + j % CMP_STRIDE
        xj = src_ref[pl.ds(start, LANES, stride=pitch), :] + pos_ref[j:j + 1, :]
        acc = acc + _nn(xj.astype(BF16), w1_ref[j])
    return _nn(_silu(acc).astype(BF16), w2_ref[...])


def _compress_prompt_body(kc_ref, vc_ref, posk_ref, w1k_ref, w2k_ref, posv_ref, w1v_ref, w2v_ref,
                          ko_ref, vo_ref, buf_ref):
    t = kc_ref.shape[0]
    buf_ref[t:, :] = jnp.zeros((buf_ref.shape[0] - t, LANES), F32)
    buf_ref[:t, :] = kc_ref[...]
    ko_ref[...] = _compress(buf_ref, posk_ref, w1k_ref, w2k_ref, CMP_STRIDE)
    buf_ref[:t, :] = vc_ref[...]
    vo_ref[...] = _compress(buf_ref, posv_ref, w1v_ref, w2v_ref, CMP_STRIDE)


def _block_diag2(w):
    z = jnp.zeros_like(w)
    return jnp.concatenate([jnp.concatenate([w, z], axis=-1), jnp.concatenate([z, w], axis=-1)], axis=-2)


def _compress_weights(pos, w1, w2):
    pos2 = jnp.concatenate([pos, pos], axis=-1)
    w1bd = _block_diag2(w1.reshape(CMP_LEN, NSA_HD, CMP_HID)).astype(BF16)
    w2bd = _block_diag2(w2).astype(BF16)
    return pos2, w1bd, w2bd


def _const_spec(shape):
    nd = len(shape)
    return pl.BlockSpec(shape, lambda *_: (0,) * nd)


def _compress_prompt(proj3, cw_k, cw_v):
    b, t, _ = proj3.shape
    col = C_KV // LANES
    w_specs = [_const_spec(a.shape) for a in cw_k + cw_v]
    return pl.pallas_call(
        _compress_prompt_body,
        grid=(b,),
        in_specs=[pl.BlockSpec((None, t, LANES), lambda i: (i, 0, col)),
                  pl.BlockSpec((None, t, LANES), lambda i: (i, 0, col + 1))] + w_specs,
        out_specs=[pl.BlockSpec((None, LANES, LANES), lambda i: (i, 0, 0))] * 2,
        out_shape=[jax.ShapeDtypeStruct((b, LANES, LANES), F32)] * 2,
        scratch_shapes=[pltpu.VMEM((max(t + CMP_LEN, CMP_SRC_ROWS), LANES), F32)],
        compiler_params=_cparams(1),
        name="compress_prompt",
    )(proj3, proj3, *cw_k, *cw_v)


def _select_blocks(imp, qpos, nb, axis):
    j = lax.broadcasted_iota(jnp.int32, imp.shape, axis)
    cur = qpos // SEL_BLOCK
    valid = j * SEL_BLOCK <= qpos
    forced = (j == 0) | (j == cur) | (j == cur - 1)
    score = jnp.where(valid, jnp.where(forced, FORCED, imp), -1.0)
    score = jnp.where(j < nb, score, -2.0)
    rank = jnp.zeros(imp.shape, F32)
    for i in range(nb):
        s_i = lax.slice_in_dim(score, i, i + 1, axis=axis)
        tie = jnp.where(i < j, 1.0, 0.0)
        rank = rank + jnp.where(s_i > score, 1.0, jnp.where(s_i == score, tie, 0.0))
    return jnp.where((rank < min(SEL_TOPN, nb)) & (j < nb), 1.0, 0.0)


def _masked_softmax(s, mask):
    s = jnp.where(mask, s, NEG)
    m = jnp.max(s, axis=-1, keepdims=True)
    e = jnp.exp(s - m)
    p = e / jnp.sum(e, axis=-1, keepdims=True)
    return jnp.where(mask, p, 0.0)


def _stack_queries(q_ref, qbd_ref, rows):
    lane = lax.broadcasted_iota(jnp.int32, (rows, LANES), 1)
    for r in range(NSA_REP):
        blk = q_ref[:, r * LANES:(r + 1) * LANES] * (NSA_HD ** -0.5)
        qbd_ref[(2 * r) * rows:(2 * r + 1) * rows, :] = jnp.where(lane < NSA_HD, blk, 0.0).astype(BF16)
        qbd_ref[(2 * r + 1) * rows:(2 * r + 2) * rows, :] = jnp.where(lane >= NSA_HD, blk, 0.0).astype(BF16)


def _gate_and_store(gate_ref, za_ref, o_ref, o_cmp, o_sel, o_win, rows):
    gs = _sigmoid(gate_ref[...])
    lane = lax.broadcasted_iota(jnp.int32, (rows, LANES), 1)
    for r in range(NSA_REP):
        parts = []
        for g in range(NSA_KV):
            rb = 2 * r + g
            h = _rowblk_head(rb)
            o = (gs[:, h:h + 1] * o_cmp[rb] + gs[:, NSA_HEADS + h:NSA_HEADS + h + 1] * o_sel[rb]
                 + gs[:, 2 * NSA_HEADS + h:2 * NSA_HEADS + h + 1] * o_win[rb])
            parts.append(o)
        blk = jnp.where(lane < NSA_HD, parts[0], parts[1])
        o_ref[:, r * LANES:(r + 1) * LANES] = blk * _silu(za_ref[:, r * LANES:(r + 1) * LANES])


N_COL = N_ROWBLK * TQ
MASK_COLS = NSA_KV * TQ
COL_CHUNK = TQ
SEL_TILES_PER_ITER = 2


def _nsa_prompt_body(q_ref, gate_ref, za_ref, ks_ref, vs_ref, kw_ref, vw_ref, kcmp_ref, vcmp_ref,
                     bcmp_ref, btile_ref, covert_ref, o_ref,
                     qt_ref, ksb_ref, vst_ref, kwb_ref, vwt_ref, vcmpt_ref, mask_ref, m_ref, l_ref, acc_ref, res_ref,
                     *, nc, nb):
    qb = pl.program_id(1)
    nkt = ksb_ref.shape[0]
    sub = lax.broadcasted_iota(jnp.int32, (LANES, TQ), 0)
    tok = lax.broadcasted_iota(jnp.int32, (LANES, TQ), 1)

    @pl.when(qb == 0)
    def _():
        for kt in range(nkt):
            rows = slice(kt * TQ, (kt + 1) * TQ)
            ksb_ref[kt] = ks_ref[rows, :].astype(BF16)
            kwb_ref[kt] = kw_ref[rows, :].astype(BF16)
            vst_ref[kt] = vs_ref[rows, :].T.astype(BF16)
            vwt_ref[kt] = vw_ref[rows, :].T.astype(BF16)
        vcmpt_ref[...] = vcmp_ref[...].T.astype(BF16)

    for r in range(NSA_REP):
        qt = (q_ref[:, r * LANES:(r + 1) * LANES] * (NSA_HD ** -0.5)).T
        qt_ref[:, (2 * r) * TQ:(2 * r + 1) * TQ] = jnp.where(sub < NSA_HD, qt, 0.0).astype(BF16)
        qt_ref[:, (2 * r + 1) * TQ:(2 * r + 2) * TQ] = jnp.where(sub >= NSA_HD, qt, 0.0).astype(BF16)

    valid = ((CMP_STRIDE * sub + CMP_LEN - 1) <= qb * TQ + tok) & (sub < nc)
    kcb = kcmp_ref[...].astype(BF16)
    psum = [None] * NSA_KV
    for rb in range(N_ROWBLK):
        cols = slice(rb * TQ, (rb + 1) * TQ)
        s = jnp.where(valid, _nn(kcb, qt_ref[:, cols]) + bcmp_ref[:, cols], NEG)
        e = jnp.exp(s - jnp.max(s, axis=0, keepdims=True))
        p = jnp.where(valid, e / jnp.sum(e, axis=0, keepdims=True), 0.0)
        res_ref[0, :, cols] = _nn(vcmpt_ref[...], p.astype(BF16))
        g = rb % NSA_KV
        psum[g] = p if psum[g] is None else psum[g] + p

    p_hi, p_lo = _split2(jnp.concatenate(psum, axis=1))
    imp = _nn(covert_ref[...], p_hi) + _nn(covert_ref[...], p_lo)
    nbp = _round_up(nb, SUBLANES)
    tok2 = lax.broadcasted_iota(jnp.int32, (1, MASK_COLS), 1) % TQ
    sel = _select_blocks(imp[:nbp], qb * TQ + tok2, nb, 0)
    upper = lax.broadcasted_iota(jnp.int32, (TQ, MASK_COLS), 0) < SEL_BLOCK
    for kt in range(nkt):
        picked = jnp.where(upper, sel[2 * kt:2 * kt + 1, :], sel[2 * kt + 1:2 * kt + 2, :]) > 0.5
        mask_ref[kt] = jnp.where(kt <= qb, jnp.where(picked, 0.0, NEG), NEG)

    m_ref[...] = jnp.full(m_ref.shape, NEG, F32)
    l_ref[...] = jnp.zeros(l_ref.shape, F32)
    acc_ref[...] = jnp.zeros(acc_ref.shape, F32)

    def sel_body(it, carry):
        tiles = []
        for u in range(SEL_TILES_PER_ITER):
            kt = it * SEL_TILES_PER_ITER + u
            tiles.append((ksb_ref[kt], vst_ref[kt], jnp.clip(qb - kt, 0, 2), kt))
        for c in range(N_COL // COL_CHUNK):
            cols = slice(c * COL_CHUNK, (c + 1) * COL_CHUNK)
            mcols = slice(c * COL_CHUNK % MASK_COLS, c * COL_CHUNK % MASK_COLS + COL_CHUNK)
            qc = qt_ref[:, cols]
            ss = [_nn(k_t, qc) + btile_ref[bidx, :, cols] + mask_ref[kt, :, mcols] for k_t, _, bidx, kt in tiles]
            m_old = m_ref[:, cols]
            m_new = m_old
            for s in ss:
                m_new = jnp.maximum(m_new, jnp.max(s, axis=0, keepdims=True))
            alpha = jnp.exp(m_old - m_new)
            l_new = alpha * l_ref[:, cols]
            acc = alpha * acc_ref[:, cols]
            for s, (_, v_t, _, _) in zip(ss, tiles):
                p = jnp.exp(s - m_new)
                l_new = l_new + jnp.sum(p, axis=0, keepdims=True)
                acc = acc + _nn(v_t, p.astype(BF16))
            l_ref[:, cols] = l_new
            acc_ref[:, cols] = acc
            m_ref[:, cols] = m_new
        return carry

    lax.fori_loop(0, qb // SEL_TILES_PER_ITER + 1, sel_body, 0)
    res_ref[1] = acc_ref[...] / l_ref[...]

    n_wt = WINDOW // TQ + 1
    for c in range(N_COL // COL_CHUNK):
        cols = slice(c * COL_CHUNK, (c + 1) * COL_CHUNK)
        qc = qt_ref[:, cols]
        ss = []
        for dist in range(n_wt):
            kt = jnp.maximum(qb - dist, 0)
            bidx = jnp.where(qb >= dist, dist, n_wt)
            ss.append(_nn(kwb_ref[kt], qc) + btile_ref[bidx, :, cols])
        m = jnp.max(ss[0], axis=0, keepdims=True)
        for s in ss[1:]:
            m = jnp.maximum(m, jnp.max(s, axis=0, keepdims=True))
        l_w = None
        acc = None
        for dist, s in enumerate(ss):
            p = jnp.exp(s - m)
            ps = jnp.sum(p, axis=0, keepdims=True)
            pv = _nn(vwt_ref[jnp.maximum(qb - dist, 0)], p.astype(BF16))
            l_w = ps if l_w is None else l_w + ps
            acc = pv if acc is None else acc + pv
        res_ref[2, :, cols] = acc / l_w

    gt = _sigmoid(gate_ref[...]).T
    for r in range(NSA_REP):
        parts = []
        for g in range(NSA_KV):
            rb = 2 * r + g
            h = _rowblk_head(rb)
            cols = slice(rb * TQ, (rb + 1) * TQ)
            parts.append(gt[h:h + 1, :] * res_ref[0, :, cols]
                         + gt[NSA_HEADS + h:NSA_HEADS + h + 1, :] * res_ref[1, :, cols]
                         + gt[2 * NSA_HEADS + h:2 * NSA_HEADS + h + 1, :] * res_ref[2, :, cols])
        blk = jnp.where(sub < NSA_HD, parts[0], parts[1]).T
        o_ref[:, r * LANES:(r + 1) * LANES] = blk * _silu(za_ref[:, r * LANES:(r + 1) * LANES])


def _cover_matrix(nc, nb):
    cs = np.arange(nc) * CMP_STRIDE
    ce = cs + CMP_LEN
    bs = np.arange(nb) * SEL_BLOCK
    be = bs + SEL_BLOCK
    ov = np.clip(np.minimum(ce[:, None], be[None, :]) - np.maximum(cs[:, None], bs[None, :]), 0, None) / CMP_STRIDE
    out = np.zeros((LANES, LANES), np.float32)
    out[:nc, :nb] = ov
    return jnp.asarray(out, BF16)


def _expand_matrix(nkeys):
    j = np.arange(LANES)[:, None]
    k = np.arange(nkeys)[None, :]
    return jnp.asarray((k // SEL_BLOCK == j).astype(np.float32), BF16)


def _nsa_prompt(proj3, kcmp, vcmp, bias_cmp, bias_tiles):
    b, t, _ = proj3.shape
    nq = t // TQ
    nc = (t - CMP_LEN) // CMP_STRIDE + 1
    nb = t // SEL_BLOCK
    assert nc <= LANES and nb <= LANES and t % (TQ * SEL_TILES_PER_ITER) == 0
    kv_col = C_KV // LANES
    kv_spec = lambda c: pl.BlockSpec((None, t, LANES), lambda i, j: (i, 0, kv_col + c))
    cover_t = _cover_matrix(nc, nb).T
    tile_bf16 = pltpu.VMEM((nq, TQ, LANES), BF16)
    return pl.pallas_call(
        functools.partial(_nsa_prompt_body, nc=nc, nb=nb),
        grid=(b, nq),
        in_specs=[pl.BlockSpec((None, TQ, A_WIDTH), lambda i, j: (i, j, C_Q // A_WIDTH)),
                  pl.BlockSpec((None, TQ, LANES), lambda i, j: (i, j, C_GATE // LANES)),
                  pl.BlockSpec((None, TQ, A_WIDTH), lambda i, j: (i, j, C_ZA // A_WIDTH)),
                  kv_spec(2), kv_spec(3), kv_spec(4), kv_spec(5),
                  pl.BlockSpec((None, LANES, LANES), lambda i, j: (i, 0, 0)),
                  pl.BlockSpec((None, LANES, LANES), lambda i, j: (i, 0, 0)),
                  pl.BlockSpec((None, LANES, N_COL), lambda i, j: (j, 0, 0)),
                  _const_spec(bias_tiles.shape), _const_spec(cover_t.shape)],
        out_specs=pl.BlockSpec((None, TQ, A_WIDTH), lambda i, j: (i, j, 0)),
        out_shape=jax.ShapeDtypeStruct((b, t, A_WIDTH), F32),
        scratch_shapes=[pltpu.VMEM((LANES, N_COL), BF16),
                        tile_bf16, tile_bf16, tile_bf16, tile_bf16,
                        pltpu.VMEM((LANES, LANES), BF16),
                        pltpu.VMEM((nq, TQ, MASK_COLS), F32),
                        pltpu.VMEM((1, N_COL), F32),
                        pltpu.VMEM((1, N_COL), F32),
                        pltpu.VMEM((LANES, N_COL), F32),
                        pltpu.VMEM((3, LANES, N_COL), F32)],
        compiler_params=_cparams(2),
        name="nsa_prompt",
    )(proj3, proj3, proj3, proj3, proj3, proj3, proj3, kcmp, vcmp, bias_cmp, bias_tiles, cover_t)


N_CACHES = 4


def _page_copy(cache_refs, pt_ref, pbuf, sems, seq, slot, c, pg, n_pages):
    return pltpu.make_async_copy(cache_refs[c].at[pt_ref[seq, pg]], pbuf.at[slot, c * n_pages + pg], sems.at[slot])


def _nsa_sample_body(pt_ref, *refs, n_pages, **static):
    cache_refs = refs[:N_CACHES]
    per_seq = refs[N_CACHES:N_CACHES + 11]
    consts = refs[N_CACHES + 11:N_CACHES + 22]
    outs = refs[N_CACHES + 22:N_CACHES + 25]
    scratch = refs[N_CACHES + 25:-2]
    pbuf, sems = refs[-2:]
    i = pl.program_id(0)
    slot = i % 2

    def all_pages(seq, slot_):
        return [_page_copy(cache_refs, pt_ref, pbuf, sems, seq, slot_, c, pg, n_pages)
                for c in range(N_CACHES) for pg in range(n_pages)]

    @pl.when(i == 0)
    def _():
        for cp in all_pages(0, 0):
            cp.start()

    @pl.when(i + 1 < pl.num_programs(0))
    def _():
        for cp in all_pages(i + 1, 1 - slot):
            cp.start()

    for cp in all_pages(i, slot):
        cp.wait()
    pages = [[pbuf.at[slot, c * n_pages + pg] for pg in range(n_pages)] for c in range(N_CACHES)]
    _nsa_sample_one(pages, per_seq, consts, outs, scratch, n_pages=n_pages, **static)


def _nsa_sample_one(pages, per_seq, consts, outs, scratch, *, n_pages, past, wbuf, nc, nb):
    q_ref, kc_ref, vc_ref, ks_ref, vs_ref, kw_ref, vw_ref, gate_ref, za_ref, wk_ref, wv_ref = per_seq
    (posk_ref, w1k_ref, w2k_ref, posv_ref, w1v_ref, w2v_ref, bcmp_ref, bsel_ref, bwin_ref,
     cover_ref, emat_ref) = consts
    o_ref, swk_ref, swv_ref = outs
    kc_all, vc_all, newt_ref, qbd_ref = scratch
    n_keys = bsel_ref.shape[2]
    n_new = T_PAD // 2
    chunks_per_page = PAGE_SIZE // CMP_STRIDE

    for dst, page_refs, new_ref in ((kc_all, pages[0], kc_ref), (vc_all, pages[1], vc_ref)):
        for pg in range(n_pages):
            rows = page_refs[pg][...].T
            for c in range(chunks_per_page):
                r0 = (pg * chunks_per_page + c) * CMP_PITCH
                dst[r0:r0 + CMP_STRIDE, :] = rows[c * CMP_STRIDE:(c + 1) * CMP_STRIDE, :]
        r0 = (past // CMP_STRIDE) * CMP_PITCH
        dst[r0:r0 + T_PAD, :] = new_ref[...]
        dst[r0 + T_PAD:, :] = jnp.zeros((dst.shape[0] - r0 - T_PAD, LANES), F32)
    kcmp = _compress(kc_all, posk_ref, w1k_ref, w2k_ref, CMP_PITCH).astype(BF16)
    vcmp = _compress(vc_all, posv_ref, w1v_ref, w2v_ref, CMP_PITCH).astype(BF16)

    newt_ref[...] = jnp.zeros(newt_ref.shape, F32)
    for idx, new_ref in enumerate((ks_ref, vs_ref, kw_ref, vw_ref)):
        newt_ref[idx, 0:T_PAD, :] = new_ref[...]

    _stack_queries(q_ref, qbd_ref, T_PAD)
    qbd = qbd_ref[...]
    nrow = N_ROWBLK * T_PAD
    tok = lax.broadcasted_iota(jnp.int32, (T_PAD, LANES), 0)
    lane = lax.broadcasted_iota(jnp.int32, (T_PAD, LANES), 1)
    qpos = past + tok
    by_head = lambda x: x.reshape(NSA_REP, NSA_KV, T_PAD, LANES)

    s = _nt(qbd, kcmp).reshape(N_ROWBLK, T_PAD, LANES) + bcmp_ref[...]
    mask_c = ((CMP_STRIDE * lane + CMP_LEN - 1) <= qpos) & (lane < nc)
    p = _masked_softmax(s, mask_c[None])
    o_cmp = _nn(p.reshape(nrow, LANES).astype(BF16), vcmp).reshape(N_ROWBLK, T_PAD, LANES)

    psum = p.reshape(NSA_REP, NSA_KV * T_PAD, LANES).sum(axis=0)
    p_hi, p_lo = _split2(psum)
    imp = _nn(p_hi, cover_ref[...]) + _nn(p_lo, cover_ref[...])
    sel = _select_blocks(imp.reshape(NSA_KV, T_PAD, LANES), qpos[None], nb, 2)
    selk = _nn(sel.reshape(NSA_KV * T_PAD, LANES).astype(BF16), emat_ref[...]).reshape(NSA_KV, T_PAD, n_keys)

    def softmax_pv(scores, values):
        m = scores[0]
        for s_t in scores[1:]:
            m = jnp.maximum(m, s_t)
        m = jnp.max(m, axis=-1, keepdims=True)
        acc = None
        tot = None
        for s_t, (v_t, channel_major) in zip(scores, values):
            p_t = jnp.exp(s_t - m)
            pb = p_t.reshape(nrow, p_t.shape[-1]).astype(BF16)
            pv = _nt(pb, v_t) if channel_major else _nn(pb, v_t)
            acc = pv if acc is None else acc + pv
            tot = p_t if tot is None else tot + p_t
        return (acc / jnp.sum(tot, axis=-1, keepdims=True).reshape(nrow, 1)).reshape(N_ROWBLK, T_PAD, LANES)

    scores, values = [], []
    for t in range(n_pages + 1):
        cols = slice(t * LANES, (t + 1) * LANES)
        allowed = ((t * LANES + lane) <= qpos)[None] & (selk[:, :, cols] > 0.5)
        madd = jnp.where(allowed, 0.0, NEG)[None]
        if t < n_pages:
            raw = _nn(qbd, pages[2][t][...].astype(BF16))
            values.append((pages[3][t][...].astype(BF16), True))
        else:
            raw = _nt(qbd, newt_ref[0].astype(BF16))
            values.append((newt_ref[1].astype(BF16), False))
        scores.append(by_head(raw) + by_head(bsel_ref[:, :, cols]) + madd)
    o_sel = softmax_pv(scores, values)

    scores, values = [], []
    kwt = wk_ref[...].astype(BF16)
    vwt = wv_ref[...].astype(BF16)
    for t in range(wbuf // LANES + 1):
        cols = slice(t * LANES, (t + 1) * LANES)
        rel_w = wbuf + tok - (t * LANES + lane)
        madd = jnp.where((rel_w >= 0) & (rel_w < WINDOW), 0.0, NEG)[None, None]
        if t < wbuf // LANES:
            raw = _nn(qbd, kwt[:, cols])
            values.append((vwt[:, cols], True))
        else:
            raw = _nt(qbd, newt_ref[2].astype(BF16))
            values.append((newt_ref[3].astype(BF16), False))
        scores.append(by_head(raw) + by_head(bwin_ref[:, :, cols]) + madd)
    o_win = softmax_pv(scores, values)

    _gate_and_store(gate_ref, za_ref, o_ref, o_cmp, o_sel, o_win, T_PAD)

    lane_full = lax.broadcasted_iota(jnp.int32, (LANES, LANES), 1)
    for idx, w_ref, out_ref in ((2, wk_ref, swk_ref), (3, wv_ref, swv_ref)):
        rolled = pltpu.roll(w_ref[...], wbuf - n_new, axis=1)
        tail = pltpu.roll(newt_ref[idx], LANES - n_new, axis=0).T
        out_ref[:, :wbuf - LANES] = rolled[:, :wbuf - LANES]
        out_ref[:, wbuf - LANES:] = jnp.where(lane_full >= LANES - n_new, tail, rolled[:, wbuf - LANES:])


def _round_up(x, m):
    return (x + m - 1) // m * m


def _nsa_sample(projs3, caches, wk, wv, page_table, cw_k, cw_v, rel_bias, n_new):
    nseq, n_pages = page_table.shape
    past = n_pages * PAGE_SIZE
    wbuf = wk.shape[2]
    length = past + n_new
    nc = (length - CMP_LEN) // CMP_STRIDE + 1
    nb = -(-length // SEL_BLOCK)
    n_keys = past + LANES
    n_win = wbuf + LANES
    pitched = lambda p: (p // CMP_STRIDE) * CMP_PITCH + p % CMP_STRIDE
    n_cmp_rows = _round_up(max(pitched(past) + T_PAD, pitched(CMP_SRC_ROWS - 1) + 1), SUBLANES)
    assert nc <= LANES and nb <= LANES and n_new == T_PAD // 2 and nb * SEL_BLOCK <= n_keys
    assert wbuf % LANES == 0 and PAGE_SIZE == LANES

    bias_cmp = _bias_table_sample(rel_bias, past - (CMP_LEN - 1), CMP_STRIDE, LANES)
    bias_sel = _bias_table_sample(rel_bias, past, 1, n_keys)
    bias_win = _bias_table_sample(rel_bias, wbuf, 1, n_win)
    cover = _cover_matrix(nc, nb)
    emat = _expand_matrix(n_keys)

    def new_spec(width, col):
        return pl.BlockSpec((None, T_PAD, width), lambda i, pt: (i, 0, col))

    kv_col = C_KV // LANES
    in_specs = [pl.BlockSpec(memory_space=pl.ANY)] * N_CACHES
    in_specs += [new_spec(A_WIDTH, C_Q // A_WIDTH)]
    in_specs += [new_spec(LANES, kv_col + c) for c in range(6)]
    in_specs += [new_spec(LANES, C_GATE // LANES), new_spec(A_WIDTH, C_ZA // A_WIDTH)]
    win_spec = pl.BlockSpec((None, LANES, wbuf), lambda i, pt: (i, 0, 0))
    in_specs += [win_spec] * 2
    consts = list(cw_k) + list(cw_v) + [bias_cmp, bias_sel, bias_win, cover, emat]
    in_specs += [_const_spec(a.shape) for a in consts]
    operands = list(caches) + [projs3] * 9 + [wk, wv] + consts

    grid_spec = pltpu.PrefetchScalarGridSpec(
        num_scalar_prefetch=1,
        grid=(nseq,),
        in_specs=in_specs,
        out_specs=[pl.BlockSpec((None, T_PAD, A_WIDTH), lambda i, pt: (i, 0, 0)), win_spec, win_spec],
        scratch_shapes=[pltpu.VMEM((n_cmp_rows, LANES), F32), pltpu.VMEM((n_cmp_rows, LANES), F32),
                        pltpu.VMEM((4, LANES, LANES), F32),
                        pltpu.VMEM((N_ROWBLK * T_PAD, LANES), BF16),
                        pltpu.VMEM((2, N_CACHES * n_pages, LANES, PAGE_SIZE), F32),
                        pltpu.SemaphoreType.DMA((2,))])
    return pl.pallas_call(
        functools.partial(_nsa_sample_body, n_pages=n_pages, past=past, wbuf=wbuf, nc=nc, nb=nb),
        grid_spec=grid_spec,
        out_shape=[jax.ShapeDtypeStruct((nseq, T_PAD, A_WIDTH), F32),
                   jax.ShapeDtypeStruct((nseq, LANES, wbuf), F32),
                   jax.ShapeDtypeStruct((nseq, LANES, wbuf), F32)],
        compiler_params=_cparams(1),
        name="nsa_sample",
    )(page_table, *operands)


def _hgrn_constants(rows, seg_stride, seg_len):
    idx = np.arange(rows)
    seg = idx // seg_stride
    pos = idx % seg_stride
    same = seg[:, None] == seg[None, :]
    pt, pu = pos[:, None], pos[None, :]
    mats = [same & (pu <= pt),
            same & (pu > pt) & (pu <= seg_len - 1)]
    levels = []
    m = seg_len // 2
    while m >= 2:
        levels.append(m)
        m //= 2
    for m in levels:
        mats.append(same & (pu >= (pt // m) * m) & (pu <= pt))
        mats.append(same & (pu > pt) & (pu <= (pt // m) * m + m - 1))
    wstack = np.concatenate(mats, axis=0).astype(np.float32)
    wstack = np.concatenate([wstack] * 3, axis=1)
    masks = []
    for m in levels + [1]:
        masks.append(same & (pt // (2 * m) == pu // (2 * m)) & ((pt // m) % 2 == 1) & ((pu // m) % 2 == 0))
    masks.append(idx[:, None] == idx[None, :])
    masks = np.stack(masks).astype(np.float32)
    return jnp.asarray(wstack, BF16), jnp.asarray(masks, F32), len(levels)


def _hgrn_chunk(q, f_pre, hl_ref, wst_ref, n_levels):
    rows = q.shape[0]
    hl = hl_ref[...]
    e = jnp.exp(hl - jnp.max(hl, axis=0, keepdims=True))
    lb = e[0:1] / jnp.sum(e, axis=0, keepdims=True)
    f = lb + (1.0 - lb) * _sigmoid(f_pre)
    g = jnp.log(f)
    k = 1.0 - f
    g_hi = g.astype(BF16)
    r1 = g - g_hi.astype(F32)
    g_mid = r1.astype(BF16)
    g_lo = (r1 - g_mid.astype(F32)).astype(BF16)
    sums = _nn(wst_ref[...], jnp.concatenate([g_hi, g_mid, g_lo], axis=0))
    big_g = sums[0:rows]
    qg = q * jnp.exp(big_g)
    kd = k * jnp.exp(sums[rows:2 * rows])
    q_lv, k_lv = [], []
    for li in range(n_levels):
        q_lv.append(q * jnp.exp(sums[(2 + 2 * li) * rows:(3 + 2 * li) * rows]))
        k_lv.append(k * jnp.exp(sums[(3 + 2 * li) * rows:(4 + 2 * li) * rows]))
    q_lv += [q * f, q]
    k_lv += [k, k]
    return big_g, qg, kd, q_lv, k_lv


def _hgrn_intra(q_lv, k_lv, masks_ref, hs):
    a = None
    for li in range(len(q_lv)):
        term = masks_ref[li] * _nt(q_lv[li][:, hs].astype(BF16), k_lv[li][:, hs].astype(BF16))
        a = term if a is None else a + term
    return a


def _hgrn_out(o, ng, z):
    o = o * lax.rsqrt(jnp.mean(o * o, axis=-1, keepdims=True) + EPS)
    return o * ng * _silu(z)


def _hgrn_prompt_body(q_ref, f_ref, i_ref, z_ref, hl_ref, ng_ref, wst_ref, masks_ref, o_ref, s_ref, st_ref,
                      *, n_levels, rows):
    c = pl.program_id(1)

    @pl.when(c == 0)
    def _():
        st_ref[...] = jnp.zeros(st_ref.shape, F32)

    chunks = []
    for ci in range(q_ref.shape[0] // rows):
        rs = slice(ci * rows, (ci + 1) * rows)
        v = i_ref[rs, :]
        big_g, qg, kd, q_lv, k_lv = _hgrn_chunk(q_ref[rs, :], f_ref[rs, :], hl_ref, wst_ref, n_levels)
        per_head = []
        for h in range(HG_HEADS):
            hs = slice(h * HG_DK, (h + 1) * HG_DK)
            vb = v[:, hs].astype(BF16)
            a = _hgrn_intra(q_lv, k_lv, masks_ref, hs)
            per_head.append((qg[:, hs].astype(BF16), _nn(a.astype(BF16), vb),
                             jnp.exp(big_g[rows - 1:rows, hs]), _tn(vb, kd[:, hs].astype(BF16))))
        chunks.append(per_head)

    for h in range(HG_HEADS):
        hs = slice(h * HG_DK, (h + 1) * HG_DK)
        st = st_ref[h]
        for ci, per_head in enumerate(chunks):
            rs = slice(ci * rows, (ci + 1) * rows)
            qgb, o_intra, decay, kv = per_head[h]
            o = _nt(qgb, st.astype(BF16)) + o_intra
            o_ref[rs, hs] = _hgrn_out(o, ng_ref[:, hs], z_ref[rs, hs])
            st = st * decay + kv
        st_ref[h] = st

        @pl.when(c == pl.num_programs(1) - 1)
        def _():
            s_ref[h] = st.T


HGRN_CHUNKS_PER_STEP = 8


def _hgrn_prompt(proj3, hg_lower, hg_norm_g):
    b, t, _ = proj3.shape
    chunk = math.gcd(t, HG_CHUNK)
    wst, masks, n_levels = _hgrn_constants(chunk, chunk, chunk)
    rows = chunk * math.gcd(t // chunk, HGRN_CHUNKS_PER_STEP)
    col = lambda c: pl.BlockSpec((None, rows, B_KEY), lambda i, j: (i, j, c // B_KEY))
    return pl.pallas_call(
        functools.partial(_hgrn_prompt_body, n_levels=n_levels, rows=chunk),
        grid=(b, t // rows),
        in_specs=[col(C_QB), col(C_FB), col(C_IB), col(C_ZB),
                  _const_spec(hg_lower.shape), _const_spec(hg_norm_g.shape),
                  _const_spec(wst.shape), _const_spec(masks.shape)],
        out_specs=[pl.BlockSpec((None, rows, B_WIDTH), lambda i, j: (i, j, 0)),
                   pl.BlockSpec((None, HG_HEADS, HG_DK, HG_DV), lambda i, j: (i, 0, 0, 0))],
        out_shape=[jax.ShapeDtypeStruct((b, t, B_WIDTH), F32),
                   jax.ShapeDtypeStruct((b, HG_HEADS, HG_DK, HG_DV), F32)],
        scratch_shapes=[pltpu.VMEM((HG_HEADS, HG_DV, HG_DK), F32)],
        compiler_params=_cparams(2),
        name="hgrn_prompt",
    )(proj3, proj3, proj3, proj3, hg_lower, hg_norm_g, wst, masks)


SEQ_PER_STEP = 8


def _hgrn_sample_body(q_ref, f_ref, i_ref, z_ref, s0_ref, hl_ref, ng_ref, wst_ref, masks_ref, o_ref, s_ref,
                      *, n_levels, n_new):
    v = i_ref[...]
    big_g, qg, kd, q_lv, k_lv = _hgrn_chunk(q_ref[...], f_ref[...], hl_ref, wst_ref, n_levels)
    for h in range(HG_HEADS):
        hs = slice(h * HG_DK, (h + 1) * HG_DK)
        vb = v[:, hs].astype(BF16)
        a = _hgrn_intra(q_lv, k_lv, masks_ref, hs)
        o_intra = _nn(a.astype(BF16), vb)
        for s in range(SEQ_PER_STEP):
            rs = slice(s * T_PAD, (s + 1) * T_PAD)
            s0 = s0_ref[s, h]
            o = _nn(qg[rs, hs].astype(BF16), s0.astype(BF16)) + o_intra[rs]
            o_ref[rs, hs] = _hgrn_out(o, ng_ref[:, hs], z_ref[rs, hs])
            decay = jnp.exp(big_g[rs, hs]).T[:, n_new - 1:n_new]
            s_ref[s, h] = s0 * decay + _tn(kd[rs, hs].astype(BF16), v[rs, hs].astype(BF16))


def _hgrn_sample(projs, state, hg_lower, hg_norm_g, n_new):
    nseq = state.shape[0]
    rows = SEQ_PER_STEP * T_PAD
    wst, masks, n_levels = _hgrn_constants(rows, T_PAD, n_new)
    col = lambda c: pl.BlockSpec((rows, B_KEY), lambda i: (i, c // B_KEY))
    st_spec = pl.BlockSpec((SEQ_PER_STEP, HG_HEADS, HG_DK, HG_DV), lambda i: (i, 0, 0, 0))
    return pl.pallas_call(
        functools.partial(_hgrn_sample_body, n_levels=n_levels, n_new=n_new),
        grid=(nseq // SEQ_PER_STEP,),
        in_specs=[col(C_QB), col(C_FB), col(C_IB), col(C_ZB), st_spec,
                  _const_spec(hg_lower.shape), _const_spec(hg_norm_g.shape),
                  _const_spec(wst.shape), _const_spec(masks.shape)],
        out_specs=[pl.BlockSpec((rows, B_WIDTH), lambda i: (i, 0)), st_spec],
        out_shape=[jax.ShapeDtypeStruct((nseq * T_PAD, B_WIDTH), F32),
                   jax.ShapeDtypeStruct(state.shape, F32)],
        compiler_params=_cparams(1),
        name="hgrn_sample",
    )(projs, projs, projs, projs, state, hg_lower, hg_norm_g, wst, masks)


def _merge_body(oa_ref, ob_ref, ma_ref, mb_ref, x_ref, wba_ref, wbb_ref, wout_ref, fg_ref, y_ref):
    a = _nn(oa_ref[...].astype(BF16), wba_ref[...])
    b = _nn(ob_ref[...].astype(BF16), wbb_ref[...])
    y = _sigmoid(ma_ref[...]) * a + _sigmoid(mb_ref[...]) * b
    h = x_ref[...] + _nn(y.astype(BF16), wout_ref[...])
    ms = jnp.mean(h * h, axis=-1, keepdims=True)
    y_ref[...] = h * lax.rsqrt(ms + EPS) * fg_ref[...]


def _merge(o_a, o_b, proj, x2d, w_ba, w_bb, w_out, final_g):
    m, d = x2d.shape
    tm = min(m, 512)
    row = lambda w, c: pl.BlockSpec((tm, w), lambda i: (i, c))
    return pl.pallas_call(
        _merge_body,
        grid=(m // tm,),
        in_specs=[row(A_WIDTH, 0), row(B_WIDTH, 0), row(d, C_MA // d), row(d, C_MA // d + 1), row(d, 0),
                  _const_spec(w_ba.shape), _const_spec(w_bb.shape), _const_spec(w_out.shape),
                  _const_spec((1, d))],
        out_specs=row(d, 0),
        out_shape=jax.ShapeDtypeStruct((m, d), F32),
        compiler_params=_cparams(1),
        name="merge_out",
    )(o_a, o_b, proj, proj, x2d, w_ba, w_bb, w_out, final_g.reshape(1, d))


def kernel(x_prompt, x_sample, cache_cmp_k, cache_cmp_v, cache_sel_k, cache_sel_v, state_win_k, state_win_v,
           state_hgrn, page_table, norm_g, w_in, cmp_pos_k, cmp_w1_k, cmp_w2_k, cmp_pos_v, cmp_w1_v, cmp_w2_v,
           rel_bias, hg_lower, hg_norm_g, w_branch_a, w_branch_b, w_out, final_g):
    depth = norm_g.shape[0]
    assert depth == 1, "single-layer trunk"
    b, t, d = x_prompt.shape
    nseq, n_new, _ = x_sample.shape
    assert C_MA % d == 0

    w = _arrange_w_in(w_in[0])
    n_cols = w.shape[1]
    w_ba = _permute_heads_cols(w_branch_a[0].T).T.astype(BF16)
    w_bb = w_branch_b[0].astype(BF16)
    w_o = w_out[0].astype(BF16)
    cw_k = _compress_weights(cmp_pos_k[0], cmp_w1_k[0], cmp_w2_k[0])
    cw_v = _compress_weights(cmp_pos_v[0], cmp_w1_v[0], cmp_w2_v[0])
    ng = hg_norm_g[0].reshape(1, B_WIDTH)
    kv_cols = lambda a, c: a[..., C_KV + c * A_KV:C_KV + (c + 1) * A_KV]

    xp2 = x_prompt.reshape(b * t, d)
    proj = _project(xp2, norm_g[0], w)
    proj3 = proj.reshape(b, t, n_cols)
    nc_p = (t - CMP_LEN) // CMP_STRIDE + 1
    kcmp, vcmp = _compress_prompt(proj3, cw_k, cw_v)
    bias_cmp = _bias_table(rel_bias, base=-(CMP_LEN - 1), step=TQ, row_stride=-CMP_STRIDE, col_stride=1,
                           rows=LANES, cols=TQ, steps=t // TQ)
    bias_tiles = _bias_table(rel_bias, base=0, step=TQ, row_stride=-1, col_stride=1, rows=TQ, cols=TQ,
                             steps=WINDOW // TQ + 2, lo=0, hi=WINDOW)
    o_a = _nsa_prompt(proj3, kcmp, vcmp, bias_cmp, bias_tiles)
    o_b, p_hgrn = _hgrn_prompt(proj3, hg_lower, ng)
    y_prompt = _merge(o_a.reshape(b * t, A_WIDTH), o_b.reshape(b * t, B_WIDTH), proj, xp2,
                      w_ba, w_bb, w_o, final_g).reshape(b, t, d)
    wb_p = min(WINDOW, t)
    p_kv = [kv_cols(proj3, c).reshape(1, b, t, NSA_KV, NSA_HD) for c in range(6)]
    p_state = p_kv[:4] + [p_kv[4][:, :, t - wb_p:], p_kv[5][:, :, t - wb_p:], p_hgrn[None]]

    xs_pad = jnp.pad(x_sample, ((0, 0), (0, T_PAD - n_new), (0, 0))).reshape(nseq * T_PAD, d)
    projs = _project(xs_pad, norm_g[0], w)
    projs3 = projs.reshape(nseq, T_PAD, n_cols)
    pool = cache_cmp_k.shape[1]
    chan_major = lambda a: jnp.transpose(a, (0, 2, 3, 1)).reshape(a.shape[0], A_KV, a.shape[1])
    row_major = lambda a: jnp.transpose(a.reshape(a.shape[0], NSA_KV, NSA_HD, a.shape[2]), (0, 3, 1, 2))
    caches = [chan_major(c[0]) for c in (cache_cmp_k, cache_cmp_v, cache_sel_k, cache_sel_v)]
    wbuf = state_win_k.shape[2]
    o_as, s_win_k, s_win_v = _nsa_sample(projs3, caches, chan_major(state_win_k[0]), chan_major(state_win_v[0]),
                                         page_table, cw_k, cw_v, rel_bias, n_new)
    s_win_k, s_win_v = row_major(s_win_k), row_major(s_win_v)
    o_bs, s_hgrn = _hgrn_sample(projs, state_hgrn[0], hg_lower, ng, n_new)
    y_s = _merge(o_as.reshape(nseq * T_PAD, A_WIDTH), o_bs, projs, xs_pad, w_ba, w_bb, w_o, final_g)
    y_sample = y_s.reshape(nseq, T_PAD, d)[:, :n_new]
    s_kv = [kv_cols(projs3, c)[:, :n_new].reshape(1, nseq, n_new, NSA_KV, NSA_HD) for c in range(4)]
    s_state = s_kv + [s_win_k.reshape(1, nseq, wbuf, NSA_KV, NSA_HD),
                      s_win_v.reshape(1, nseq, wbuf, NSA_KV, NSA_HD), s_hgrn[None]]

    return (y_prompt, y_sample, *p_state, *s_state)
```

```python
import functools
import math

import numpy as np
import jax
import jax.numpy as jnp
from jax import lax
from jax.experimental import pallas as pl
from jax.experimental.pallas import tpu as pltpu

F32 = jnp.float32
BF16 = jnp.bfloat16

NSA_HEADS = 8
NSA_KV = 2
NSA_HD = 64
NSA_REP = NSA_HEADS // NSA_KV
CMP_LEN = 32
CMP_STRIDE = 16
CMP_HID = 64
SEL_BLOCK = 64
SEL_TOPN = 8
WINDOW = 512
HG_HEADS = 4
HG_DK = 128
HG_DV = 128
HG_CHUNK = 64
N_BUCKETS = 32
MAX_DIST = 128
EPS = 1e-6
NEG = -1e30
FORCED = 1e9
PAGE_SIZE = 128

A_WIDTH = NSA_HEADS * NSA_HD
A_KV = NSA_KV * NSA_HD
B_WIDTH = HG_HEADS * HG_DV
B_KEY = HG_HEADS * HG_DK

LANES = 128
SUBLANES = 8
VMEM_LIMIT_BYTES = 56 * 1024 * 1024

TQ = 128
T_PAD = SUBLANES
N_ROWBLK = NSA_HEADS

C_Q = 0
C_KV = C_Q + A_WIDTH
C_GATE = C_KV + 6 * A_KV
C_ZA = C_GATE + 2 * LANES
C_QB = C_ZA + A_WIDTH
C_FB = C_QB + B_KEY
C_IB = C_FB + B_KEY
C_ZB = C_IB + B_WIDTH
C_MA = C_ZB + B_WIDTH

CMP_SRC_ROWS = (LANES - 1) * CMP_STRIDE + CMP_LEN
CMP_PITCH = CMP_STRIDE + 1


def _nn(a, b):
    return jnp.dot(a, b, preferred_element_type=F32)


def _nt(a, b):
    return lax.dot_general(a, b, (((1,), (1,)), ((), ())), preferred_element_type=F32)


def _tn(a, b):
    return lax.dot_general(a, b, (((0,), (0,)), ((), ())), preferred_element_type=F32)


def _sigmoid(x):
    return 1.0 / (1.0 + jnp.exp(-x))


def _silu(x):
    return x * _sigmoid(x)


def _split2(x):
    hi = x.astype(BF16)
    lo = (x - hi.astype(F32)).astype(BF16)
    return hi, lo


def _cparams(n_grid):
    return pltpu.CompilerParams(dimension_semantics=("arbitrary",) * n_grid,
                                vmem_limit_bytes=VMEM_LIMIT_BYTES)


def _t5_bucket(rel):
    n = jnp.maximum(rel, 0)
    exact = N_BUCKETS // 2
    nf = jnp.maximum(n, 1).astype(F32)
    scaled = jnp.log(nf / exact) / math.log(MAX_DIST / exact) * (N_BUCKETS - exact)
    large = exact + jnp.floor(scaled).astype(jnp.int32)
    large = jnp.minimum(large, N_BUCKETS - 1)
    return jnp.where(n < exact, n, large)


def _rowblk_head(rb):
    return NSA_REP * (rb % NSA_KV) + rb // NSA_KV


def _bias_body(tab_ref, o_ref, *, base, step, row_stride, col_stride, lo, hi, cols):
    sub_r = min(o_ref.shape[0], 4 * SUBLANES)
    sub_c = min(cols, LANES)
    shape = (sub_r, sub_c)
    row = lax.broadcasted_iota(jnp.int32, shape, 0)
    col = lax.broadcasted_iota(jnp.int32, shape, 1)
    for r0 in range(0, o_ref.shape[0], sub_r):
        for c0 in range(0, cols, sub_c):
            rel = base + step * pl.program_id(0) + row_stride * (row + r0) + col_stride * (col + c0)
            bucket = _t5_bucket(rel)
            accs = [jnp.zeros(shape, F32)] * N_ROWBLK
            for k in range(N_BUCKETS):
                hit = bucket == k
                accs = [jnp.where(hit, tab_ref[k, _rowblk_head(rb)], accs[rb]) for rb in range(N_ROWBLK)]
            for rb in range(N_ROWBLK):
                acc = accs[rb]
                if lo is not None:
                    acc = jnp.where((rel >= lo) & (rel < hi), acc, NEG)
                o_ref[r0:r0 + sub_r, rb * cols + c0:rb * cols + c0 + sub_c] = acc


def _bias_table(rel_bias, *, base, step, row_stride, col_stride, rows, cols, steps, lo=None, hi=None):
    return pl.pallas_call(
        functools.partial(_bias_body, base=base, step=step, row_stride=row_stride, col_stride=col_stride,
                          lo=lo, hi=hi, cols=cols),
        grid=(steps,),
        in_specs=[pl.BlockSpec(memory_space=pltpu.SMEM)],
        out_specs=pl.BlockSpec((None, rows, N_ROWBLK * cols), lambda i: (i, 0, 0)),
        out_shape=jax.ShapeDtypeStruct((steps, rows, N_ROWBLK * cols), F32),
        compiler_params=_cparams(1),
        name="bias_table",
    )(rel_bias)


def _bias_table_sample(rel_bias, base, col_stride, cols):
    t = _bias_table(rel_bias, base=base, step=0, row_stride=1, col_stride=-col_stride, rows=T_PAD, cols=cols, steps=1)
    return t.reshape(T_PAD, N_ROWBLK, cols).swapaxes(0, 1)


def _proj_body(x_ref, g_ref, w_ref, o_ref):
    x = x_ref[...]
    ms = jnp.mean(x * x, axis=-1, keepdims=True)
    xn = (x * lax.rsqrt(ms + EPS) * g_ref[...]).astype(BF16)
    o_ref[...] = _nn(xn, w_ref[...])


PROJ_ROWS = 512


def _project(x2d, norm_g, w):
    m, d = x2d.shape
    n = w.shape[1]
    tm = min(m, PROJ_ROWS)
    return pl.pallas_call(
        _proj_body,
        grid=(m // tm,),
        in_specs=[pl.BlockSpec((tm, d), lambda i: (i, 0)),
                  pl.BlockSpec((1, d), lambda i: (0, 0)),
                  pl.BlockSpec((d, n), lambda i: (0, 0), pipeline_mode=pl.Buffered(1))],
        out_specs=pl.BlockSpec((tm, n), lambda i: (i, 0)),
        out_shape=jax.ShapeDtypeStruct((m, n), F32),
        compiler_params=_cparams(1),
        name="in_proj",
    )(x2d, norm_g.reshape(1, d), w)


def _permute_heads_cols(w):
    lead = w.shape[:-1]
    return w.reshape(lead + (NSA_KV, NSA_REP, NSA_HD)).swapaxes(-3, -2).reshape(lead + (A_WIDTH,))


def _arrange_w_in(w_in):
    d = w_in.shape[0]
    sizes = (A_WIDTH, 6 * A_KV, 3 * NSA_HEADS, A_WIDTH, B_KEY, B_KEY, B_WIDTH, B_WIDTH, d, d)
    offs = np.concatenate([[0], np.cumsum(sizes)])
    q_a, kv_a, g_a, z_a, q_b, f_b, i_b, z_b, m_a, m_b = [w_in[:, offs[i]:offs[i + 1]] for i in range(10)]
    gate_pad = jnp.zeros((d, 2 * LANES - 3 * NSA_HEADS), w_in.dtype)
    w = jnp.concatenate([_permute_heads_cols(q_a), kv_a, g_a, gate_pad, _permute_heads_cols(z_a),
                         q_b, f_b, i_b, z_b, m_a, m_b], axis=1)
    return w.astype(BF16)


def _compress(src_ref, pos_ref, w1_ref, w2_ref, pitch):
    acc = jnp.zeros((LANES, LANES), F32)
    for j in range(CMP_LEN):
        start = (j // CMP_STRIDE) * pitch + j % CMP_STRIDE
        xj = src_ref[pl.ds(start, LANES, stride=pitch), :] + pos_ref[j:j + 1, :]
        acc = acc + _nn(xj.astype(BF16), w1_ref[j])
    return _nn(_silu(acc).astype(BF16), w2_ref[...])


def _compress_prompt_body(kc_ref, vc_ref, posk_ref, w1k_ref, w2k_ref, posv_ref, w1v_ref, w2v_ref,
                          ko_ref, vo_ref, buf_ref):
    t = kc_ref.shape[0]
    buf_ref[t:, :] = jnp.zeros((buf_ref.shape[0] - t, LANES), F32)
    buf_ref[:t, :] = kc_ref[...]
    ko_ref[...] = _compress(buf_ref, posk_ref, w1k_ref, w2k_ref, CMP_STRIDE)
    buf_ref[:t, :] = vc_ref[...]
    vo_ref[...] = _compress(buf_ref, posv_ref, w1v_ref, w2v_ref, CMP_STRIDE)


def _block_diag2(w):
    z = jnp.zeros_like(w)
    return jnp.concatenate([jnp.concatenate([w, z], axis=-1), jnp.concatenate([z, w], axis=-1)], axis=-2)


def _compress_weights(pos, w1, w2):
    pos2 = jnp.concatenate([pos, pos], axis=-1)
    w1bd = _block_diag2(w1.reshape(CMP_LEN, NSA_HD, CMP_HID)).astype(BF16)
    w2bd = _block_diag2(w2).astype(BF16)
    return pos2, w1bd, w2bd


def _const_spec(shape):
    nd = len(shape)
    return pl.BlockSpec(shape, lambda *_: (0,) * nd)


def _compress_prompt(proj3, cw_k, cw_v):
    b, t, _ = proj3.shape
    col = C_KV // LANES
    w_specs = [_const_spec(a.shape) for a in cw_k + cw_v]
    return pl.pallas_call(
        _compress_prompt_body,
        grid=(b,),
        in_specs=[pl.BlockSpec((None, t, LANES), lambda i: (i, 0, col)),
                  pl.BlockSpec((None, t, LANES), lambda i: (i, 0, col + 1))] + w_specs,
        out_specs=[pl.BlockSpec((None, LANES, LANES), lambda i: (i, 0, 0))] * 2,
        out_shape=[jax.ShapeDtypeStruct((b, LANES, LANES), F32)] * 2,
        scratch_shapes=[pltpu.VMEM((max(t + CMP_LEN, CMP_SRC_ROWS), LANES), F32)],
        compiler_params=_cparams(1),
        name="compress_prompt",
    )(proj3, proj3, *cw_k, *cw_v)


def _select_blocks(imp, qpos, nb, axis):
    j = lax.broadcasted_iota(jnp.int32, imp.shape, axis)
    cur = qpos // SEL_BLOCK
    valid = j * SEL_BLOCK <= qpos
    forced = (j == 0) | (j == cur) | (j == cur - 1)
    score = jnp.where(valid, jnp.where(forced, FORCED, imp), -1.0)
    score = jnp.where(j < nb, score, -2.0)
    rank = jnp.zeros(imp.shape, F32)
    for i in range(nb):
        s_i = lax.slice_in_dim(score, i, i + 1, axis=axis)
        tie = jnp.where(i < j, 1.0, 0.0)
        rank = rank + jnp.where(s_i > score, 1.0, jnp.where(s_i == score, tie, 0.0))
    return jnp.where((rank < min(SEL_TOPN, nb)) & (j < nb), 1.0, 0.0)


def _masked_softmax(s, mask):
    s = jnp.where(mask, s, NEG)
    m = jnp.max(s, axis=-1, keepdims=True)
    e = jnp.exp(s - m)
    p = e / jnp.sum(e, axis=-1, keepdims=True)
    return jnp.where(mask, p, 0.0)


def _stack_queries(q_ref, qbd_ref, rows):
    lane = lax.broadcasted_iota(jnp.int32, (rows, LANES), 1)
    for r in range(NSA_REP):
        blk = q_ref[:, r * LANES:(r + 1) * LANES] * (NSA_HD ** -0.5)
        qbd_ref[(2 * r) * rows:(2 * r + 1) * rows, :] = jnp.where(lane < NSA_HD, blk, 0.0).astype(BF16)
        qbd_ref[(2 * r + 1) * rows:(2 * r + 2) * rows, :] = jnp.where(lane >= NSA_HD, blk, 0.0).astype(BF16)


def _gate_and_store(gate_ref, za_ref, o_ref, o_cmp, o_sel, o_win, rows):
    gs = _sigmoid(gate_ref[...])
    lane = lax.broadcasted_iota(jnp.int32, (rows, LANES), 1)
    for r in range(NSA_REP):
        parts = []
        for g in range(NSA_KV):
            rb = 2 * r + g
            h = _rowblk_head(rb)
            o = (gs[:, h:h + 1] * o_cmp[rb] + gs[:, NSA_HEADS + h:NSA_HEADS + h + 1] * o_sel[rb]
                 + gs[:, 2 * NSA_HEADS + h:2 * NSA_HEADS + h + 1] * o_win[rb])
            parts.append(o)
        blk = jnp.where(lane < NSA_HD, parts[0], parts[1])
        o_ref[:, r * LANES:(r + 1) * LANES] = blk * _silu(za_ref[:, r * LANES:(r + 1) * LANES])


N_COL = N_ROWBLK * TQ
MASK_COLS = NSA_KV * TQ
COL_CHUNK = TQ
SEL_TILES_PER_ITER = 2


def _nsa_prompt_body(q_ref, gate_ref, za_ref, ks_ref, vs_ref, kw_ref, vw_ref, kcmp_ref, vcmp_ref,
                     bcmp_ref, btile_ref, covert_ref, o_ref,
                     qt_ref, ksb_ref, vst_ref, kwb_ref, vwt_ref, vcmpt_ref, mask_ref, s_ref, m_ref, l_ref, acc_ref, res_ref,
                     *, nc, nb):
    qb = pl.program_id(1)
    nkt = ksb_ref.shape[0]
    sub = lax.broadcasted_iota(jnp.int32, (LANES, TQ), 0)
    tok = lax.broadcasted_iota(jnp.int32, (LANES, TQ), 1)

    @pl.when(qb == 0)
    def _():
        for kt in range(nkt):
            rows = slice(kt * TQ, (kt + 1) * TQ)
            ksb_ref[kt] = ks_ref[rows, :].astype(BF16)
            kwb_ref[kt] = kw_ref[rows, :].astype(BF16)
            vst_ref[kt] = vs_ref[rows, :].T.astype(BF16)
            vwt_ref[kt] = vw_ref[rows, :].T.astype(BF16)
        vcmpt_ref[...] = vcmp_ref[...].T.astype(BF16)

    for r in range(NSA_REP):
        qt = (q_ref[:, r * LANES:(r + 1) * LANES] * (NSA_HD ** -0.5)).T
        qt_ref[:, (2 * r) * TQ:(2 * r + 1) * TQ] = jnp.where(sub < NSA_HD, qt, 0.0).astype(BF16)
        qt_ref[:, (2 * r + 1) * TQ:(2 * r + 2) * TQ] = jnp.where(sub >= NSA_HD, qt, 0.0).astype(BF16)

    valid = ((CMP_STRIDE * sub + CMP_LEN - 1) <= qb * TQ + tok) & (sub < nc)
    kcb = kcmp_ref[...].astype(BF16)
    psum = [None] * NSA_KV
    for rb in range(N_ROWBLK):
        cols = slice(rb * TQ, (rb + 1) * TQ)
        s = jnp.where(valid, _nn(kcb, qt_ref[:, cols]) + bcmp_ref[:, cols], NEG)
        e = jnp.exp(s - jnp.max(s, axis=0, keepdims=True))
        p = jnp.where(valid, e / jnp.sum(e, axis=0, keepdims=True), 0.0)
        res_ref[0, :, cols] = _nn(vcmpt_ref[...], p.astype(BF16))
        g = rb % NSA_KV
        psum[g] = p if psum[g] is None else psum[g] + p

    p_hi, p_lo = _split2(jnp.concatenate(psum, axis=1))
    imp = _nn(covert_ref[...], p_hi) + _nn(covert_ref[...], p_lo)
    nbp = _round_up(nb, SUBLANES)
    tok2 = lax.broadcasted_iota(jnp.int32, (1, MASK_COLS), 1) % TQ
    sel = _select_blocks(imp[:nbp], qb * TQ + tok2, nb, 0)
    upper = lax.broadcasted_iota(jnp.int32, (TQ, MASK_COLS), 0) < SEL_BLOCK
    for kt in range(nkt):
        picked = jnp.where(upper, sel[2 * kt:2 * kt + 1, :], sel[2 * kt + 1:2 * kt + 2, :]) > 0.5
        mask_ref[kt] = jnp.where(kt <= qb, jnp.where(picked, 0.0, NEG), NEG)

    m_ref[...] = jnp.full(m_ref.shape, NEG, F32)
    l_ref[...] = jnp.zeros(l_ref.shape, F32)
    acc_ref[...] = jnp.zeros(acc_ref.shape, F32)

    def sel_scores(pair, buf):
        for u in range(SEL_TILES_PER_ITER):
            kt = jnp.minimum(pair * SEL_TILES_PER_ITER + u, nkt - 1)
            k_t = ksb_ref[kt]
            bidx = jnp.clip(qb - kt, 0, 2)
            for c in range(N_COL // COL_CHUNK):
                cols = slice(c * COL_CHUNK, (c + 1) * COL_CHUNK)
                mcols = slice(c * COL_CHUNK % MASK_COLS, c * COL_CHUNK % MASK_COLS + COL_CHUNK)
                s_ref[buf, u, :, cols] = (_nn(k_t, qt_ref[:, cols]) + btile_ref[bidx, :, cols]
                                          + mask_ref[kt, :, mcols])

    def sel_update(pair, buf):
        v_ts = [vst_ref[pair * SEL_TILES_PER_ITER + u] for u in range(SEL_TILES_PER_ITER)]
        for c in range(N_COL // COL_CHUNK):
            cols = slice(c * COL_CHUNK, (c + 1) * COL_CHUNK)
            ss = [s_ref[buf, u, :, cols] for u in range(SEL_TILES_PER_ITER)]
            m_old = m_ref[:, cols]
            m_new = m_old
            for s in ss:
                m_new = jnp.maximum(m_new, jnp.max(s, axis=0, keepdims=True))
            alpha = jnp.exp(m_old - m_new)
            l_new = alpha * l_ref[:, cols]
            acc = alpha * acc_ref[:, cols]
            for s, v_t in zip(ss, v_ts):
                p = jnp.exp(s - m_new)
                l_new = l_new + jnp.sum(p, axis=0, keepdims=True)
                acc = acc + _nn(v_t, p.astype(BF16))
            l_ref[:, cols] = l_new
            acc_ref[:, cols] = acc
            m_ref[:, cols] = m_new

    def sel_body(it, carry):
        sel_scores(2 * it + 1, 1)
        sel_update(2 * it, 0)
        sel_scores(2 * it + 2, 0)
        sel_update(2 * it + 1, 1)
        return carry

    sel_scores(0, 0)
    lax.fori_loop(0, qb // (2 * SEL_TILES_PER_ITER) + 1, sel_body, 0)
    res_ref[1] = acc_ref[...] / l_ref[...]

    n_wt = WINDOW // TQ + 1
    for c in range(N_COL // COL_CHUNK):
        cols = slice(c * COL_CHUNK, (c + 1) * COL_CHUNK)
        qc = qt_ref[:, cols]
        ss = []
        for dist in range(n_wt):
            kt = jnp.maximum(qb - dist, 0)
            bidx = jnp.where(qb >= dist, dist, n_wt)
            ss.append(_nn(kwb_ref[kt], qc) + btile_ref[bidx, :, cols])
        m = jnp.max(ss[0], axis=0, keepdims=True)
        for s in ss[1:]:
            m = jnp.maximum(m, jnp.max(s, axis=0, keepdims=True))
        l_w = None
        acc = None
        for dist, s in enumerate(ss):
            p = jnp.exp(s - m)
            ps = jnp.sum(p, axis=0, keepdims=True)
            pv = _nn(vwt_ref[jnp.maximum(qb - dist, 0)], p.astype(BF16))
            l_w = ps if l_w is None else l_w + ps
            acc = pv if acc is None else acc + pv
        res_ref[2, :, cols] = acc / l_w

    gt = _sigmoid(gate_ref[...]).T
    for r in range(NSA_REP):
        parts = []
        for g in range(NSA_KV):
            rb = 2 * r + g
            h = _rowblk_head(rb)
            cols = slice(rb * TQ, (rb + 1) * TQ)
            parts.append(gt[h:h + 1, :] * res_ref[0, :, cols]
                         + gt[NSA_HEADS + h:NSA_HEADS + h + 1, :] * res_ref[1, :, cols]
                         + gt[2 * NSA_HEADS + h:2 * NSA_HEADS + h + 1, :] * res_ref[2, :, cols])
        blk = jnp.where(sub < NSA_HD, parts[0], parts[1]).T
        o_ref[:, r * LANES:(r + 1) * LANES] = blk * _silu(za_ref[:, r * LANES:(r + 1) * LANES])


def _cover_matrix(nc, nb):
    cs = np.arange(nc) * CMP_STRIDE
    ce = cs + CMP_LEN
    bs = np.arange(nb) * SEL_BLOCK
    be = bs + SEL_BLOCK
    ov = np.clip(np.minimum(ce[:, None], be[None, :]) - np.maximum(cs[:, None], bs[None, :]), 0, None) / CMP_STRIDE
    out = np.zeros((LANES, LANES), np.float32)
    out[:nc, :nb] = ov
    return jnp.asarray(out, BF16)


def _expand_matrix(nkeys):
    j = np.arange(LANES)[:, None]
    k = np.arange(nkeys)[None, :]
    return jnp.asarray((k // SEL_BLOCK == j).astype(np.float32), BF16)


def _nsa_prompt(proj3, kcmp, vcmp, bias_cmp, bias_tiles):
    b, t, _ = proj3.shape
    nq = t // TQ
    nc = (t - CMP_LEN) // CMP_STRIDE + 1
    nb = t // SEL_BLOCK
    assert nc <= LANES and nb <= LANES and t % (TQ * 2 * SEL_TILES_PER_ITER) == 0
    kv_col = C_KV // LANES
    kv_spec = lambda c: pl.BlockSpec((None, t, LANES), lambda i, j: (i, 0, kv_col + c))
    cover_t = _cover_matrix(nc, nb).T
    tile_bf16 = pltpu.VMEM((nq, TQ, LANES), BF16)
    return pl.pallas_call(
        functools.partial(_nsa_prompt_body, nc=nc, nb=nb),
        grid=(b, nq),
        in_specs=[pl.BlockSpec((None, TQ, A_WIDTH), lambda i, j: (i, j, C_Q // A_WIDTH)),
                  pl.BlockSpec((None, TQ, LANES), lambda i, j: (i, j, C_GATE // LANES)),
                  pl.BlockSpec((None, TQ, A_WIDTH), lambda i, j: (i, j, C_ZA // A_WIDTH)),
                  kv_spec(2), kv_spec(3), kv_spec(4), kv_spec(5),
                  pl.BlockSpec((None, LANES, LANES), lambda i, j: (i, 0, 0)),
                  pl.BlockSpec((None, LANES, LANES), lambda i, j: (i, 0, 0)),
                  pl.BlockSpec((None, LANES, N_COL), lambda i, j: (j, 0, 0)),
                  _const_spec(bias_tiles.shape), _const_spec(cover_t.shape)],
        out_specs=pl.BlockSpec((None, TQ, A_WIDTH), lambda i, j: (i, j, 0)),
        out_shape=jax.ShapeDtypeStruct((b, t, A_WIDTH), F32),
        scratch_shapes=[pltpu.VMEM((LANES, N_COL), BF16),
                        tile_bf16, tile_bf16, tile_bf16, tile_bf16,
                        pltpu.VMEM((LANES, LANES), BF16),
                        pltpu.VMEM((nq, TQ, MASK_COLS), F32),
                        pltpu.VMEM((2, SEL_TILES_PER_ITER, TQ, N_COL), F32),
                        pltpu.VMEM((1, N_COL), F32),
                        pltpu.VMEM((1, N_COL), F32),
                        pltpu.VMEM((LANES, N_COL), F32),
                        pltpu.VMEM((3, LANES, N_COL), F32)],
        compiler_params=_cparams(2),
        name="nsa_prompt",
    )(proj3, proj3, proj3, proj3, proj3, proj3, proj3, kcmp, vcmp, bias_cmp, bias_tiles, cover_t)


N_CACHES = 4


def _page_copy(cache_refs, pt_ref, pbuf, sems, seq, slot, c, pg, n_pages):
    return pltpu.make_async_copy(cache_refs[c].at[pt_ref[seq, pg]], pbuf.at[slot, c * n_pages + pg], sems.at[slot])


def _nsa_sample_body(pt_ref, *refs, n_pages, **static):
    cache_refs = refs[:N_CACHES]
    per_seq = refs[N_CACHES:N_CACHES + 11]
    consts = refs[N_CACHES + 11:N_CACHES + 22]
    outs = refs[N_CACHES + 22:N_CACHES + 25]
    scratch = refs[N_CACHES + 25:-2]
    pbuf, sems = refs[-2:]
    i = pl.program_id(0)
    slot = i % 2

    def all_pages(seq, slot_):
        return [_page_copy(cache_refs, pt_ref, pbuf, sems, seq, slot_, c, pg, n_pages)
                for c in range(N_CACHES) for pg in range(n_pages)]

    @pl.when(i == 0)
    def _():
        for cp in all_pages(0, 0):
            cp.start()

    @pl.when(i + 1 < pl.num_programs(0))
    def _():
        for cp in all_pages(i + 1, 1 - slot):
            cp.start()

    for cp in all_pages(i, slot):
        cp.wait()
    pages = [[pbuf.at[slot, c * n_pages + pg] for pg in range(n_pages)] for c in range(N_CACHES)]
    _nsa_sample_one(pages, per_seq, consts, outs, scratch, n_pages=n_pages, **static)


def _nsa_sample_one(pages, per_seq, consts, outs, scratch, *, n_pages, past, wbuf, nc, nb):
    q_ref, kc_ref, vc_ref, ks_ref, vs_ref, kw_ref, vw_ref, gate_ref, za_ref, wk_ref, wv_ref = per_seq
    (posk_ref, w1k_ref, w2k_ref, posv_ref, w1v_ref, w2v_ref, bcmp_ref, bsel_ref, bwin_ref,
     cover_ref, emat_ref) = consts
    o_ref, swk_ref, swv_ref = outs
    kc_all, vc_all, newt_ref, qbd_ref = scratch
    n_keys = bsel_ref.shape[2]
    n_new = T_PAD // 2
    chunks_per_page = PAGE_SIZE // CMP_STRIDE

    for dst, page_refs, new_ref in ((kc_all, pages[0], kc_ref), (vc_all, pages[1], vc_ref)):
        for pg in range(n_pages):
            rows = page_refs[pg][...].T
            for c in range(chunks_per_page):
                r0 = (pg * chunks_per_page + c) * CMP_PITCH
                dst[r0:r0 + CMP_STRIDE, :] = rows[c * CMP_STRIDE:(c + 1) * CMP_STRIDE, :]
        r0 = (past // CMP_STRIDE) * CMP_PITCH
        dst[r0:r0 + T_PAD, :] = new_ref[...]
        dst[r0 + T_PAD:, :] = jnp.zeros((dst.shape[0] - r0 - T_PAD, LANES), F32)
    kcmp = _compress(kc_all, posk_ref, w1k_ref, w2k_ref, CMP_PITCH).astype(BF16)
    vcmp = _compress(vc_all, posv_ref, w1v_ref, w2v_ref, CMP_PITCH).astype(BF16)

    newt_ref[...] = jnp.zeros(newt_ref.shape, F32)
    for idx, new_ref in enumerate((ks_ref, vs_ref, kw_ref, vw_ref)):
        newt_ref[idx, 0:T_PAD, :] = new_ref[...]

    _stack_queries(q_ref, qbd_ref, T_PAD)
    qbd = qbd_ref[...]
    nrow = N_ROWBLK * T_PAD
    tok = lax.broadcasted_iota(jnp.int32, (T_PAD, LANES), 0)
    lane = lax.broadcasted_iota(jnp.int32, (T_PAD, LANES), 1)
    qpos = past + tok
    by_head = lambda x: x.reshape(NSA_REP, NSA_KV, T_PAD, LANES)

    s = _nt(qbd, kcmp).reshape(N_ROWBLK, T_PAD, LANES) + bcmp_ref[...]
    mask_c = ((CMP_STRIDE * lane + CMP_LEN - 1) <= qpos) & (lane < nc)
    p = _masked_softmax(s, mask_c[None])
    o_cmp = _nn(p.reshape(nrow, LANES).astype(BF16), vcmp).reshape(N_ROWBLK, T_PAD, LANES)

    psum = p.reshape(NSA_REP, NSA_KV * T_PAD, LANES).sum(axis=0)
    p_hi, p_lo = _split2(psum)
    imp = _nn(p_hi, cover_ref[...]) + _nn(p_lo, cover_ref[...])
    sel = _select_blocks(imp.reshape(NSA_KV, T_PAD, LANES), qpos[None], nb, 2)
    selk = _nn(sel.reshape(NSA_KV * T_PAD, LANES).astype(BF16), emat_ref[...]).reshape(NSA_KV, T_PAD, n_keys)

    def softmax_pv(scores, values):
        m = scores[0]
        for s_t in scores[1:]:
            m = jnp.maximum(m, s_t)
        m = jnp.max(m, axis=-1, keepdims=True)
        acc = None
        tot = None
        for s_t, (v_t, channel_major) in zip(scores, values):
            p_t = jnp.exp(s_t - m)
            pb = p_t.reshape(nrow, p_t.shape[-1]).astype(BF16)
            pv = _nt(pb, v_t) if channel_major else _nn(pb, v_t)
            acc = pv if acc is None else acc + pv
            tot = p_t if tot is None else tot + p_t
        return (acc / jnp.sum(tot, axis=-1, keepdims=True).reshape(nrow, 1)).reshape(N_ROWBLK, T_PAD, LANES)

    scores, values = [], []
    for t in range(n_pages + 1):
        cols = slice(t * LANES, (t + 1) * LANES)
        allowed = ((t * LANES + lane) <= qpos)[None] & (selk[:, :, cols] > 0.5)
        madd = jnp.where(allowed, 0.0, NEG)[None]
        if t < n_pages:
            raw = _nn(qbd, pages[2][t][...].astype(BF16))
            values.append((pages[3][t][...].astype(BF16), True))
        else:
            raw = _nt(qbd, newt_ref[0].astype(BF16))
            values.append((newt_ref[1].astype(BF16), False))
        scores.append(by_head(raw) + by_head(bsel_ref[:, :, cols]) + madd)
    o_sel = softmax_pv(scores, values)

    scores, values = [], []
    kwt = wk_ref[...].astype(BF16)
    vwt = wv_ref[...].astype(BF16)
    for t in range(wbuf // LANES + 1):
        cols = slice(t * LANES, (t + 1) * LANES)
        rel_w = wbuf + tok - (t * LANES + lane)
        madd = jnp.where((rel_w >= 0) & (rel_w < WINDOW), 0.0, NEG)[None, None]
        if t < wbuf // LANES:
            raw = _nn(qbd, kwt[:, cols])
            values.append((vwt[:, cols], True))
        else:
            raw = _nt(qbd, newt_ref[2].astype(BF16))
            values.append((newt_ref[3].astype(BF16), False))
        scores.append(by_head(raw) + by_head(bwin_ref[:, :, cols]) + madd)
    o_win = softmax_pv(scores, values)

    _gate_and_store(gate_ref, za_ref, o_ref, o_cmp, o_sel, o_win, T_PAD)

    lane_full = lax.broadcasted_iota(jnp.int32, (LANES, LANES), 1)
    for idx, w_ref, out_ref in ((2, wk_ref, swk_ref), (3, wv_ref, swv_ref)):
        rolled = pltpu.roll(w_ref[...], wbuf - n_new, axis=1)
        tail = pltpu.roll(newt_ref[idx], LANES - n_new, axis=0).T
        out_ref[:, :wbuf - LANES] = rolled[:, :wbuf - LANES]
        out_ref[:, wbuf - LANES:] = jnp.where(lane_full >= LANES - n_new, tail, rolled[:, wbuf - LANES:])


def _round_up(x, m):
    return (x + m - 1) // m * m


def _nsa_sample(projs3, caches, wk, wv, page_table, cw_k, cw_v, rel_bias, n_new):
    nseq, n_pages = page_table.shape
    past = n_pages * PAGE_SIZE
    wbuf = wk.shape[2]
    length = past + n_new
    nc = (length - CMP_LEN) // CMP_STRIDE + 1
    nb = -(-length // SEL_BLOCK)
    n_keys = past + LANES
    n_win = wbuf + LANES
    pitched = lambda p: (p // CMP_STRIDE) * CMP_PITCH + p % CMP_STRIDE
    n_cmp_rows = _round_up(max(pitched(past) + T_PAD, pitched(CMP_SRC_ROWS - 1) + 1), SUBLANES)
    assert nc <= LANES and nb <= LANES and n_new == T_PAD // 2 and nb * SEL_BLOCK <= n_keys
    assert wbuf % LANES == 0 and PAGE_SIZE == LANES

    bias_cmp = _bias_table_sample(rel_bias, past - (CMP_LEN - 1), CMP_STRIDE, LANES)
    bias_sel = _bias_table_sample(rel_bias, past, 1, n_keys)
    bias_win = _bias_table_sample(rel_bias, wbuf, 1, n_win)
    cover = _cover_matrix(nc, nb)
    emat = _expand_matrix(n_keys)

    def new_spec(width, col):
        return pl.BlockSpec((None, T_PAD, width), lambda i, pt: (i, 0, col))

    kv_col = C_KV // LANES
    in_specs = [pl.BlockSpec(memory_space=pl.ANY)] * N_CACHES
    in_specs += [new_spec(A_WIDTH, C_Q // A_WIDTH)]
    in_specs += [new_spec(LANES, kv_col + c) for c in range(6)]
    in_specs += [new_spec(LANES, C_GATE // LANES), new_spec(A_WIDTH, C_ZA // A_WIDTH)]
    win_spec = pl.BlockSpec((None, LANES, wbuf), lambda i, pt: (i, 0, 0))
    in_specs += [win_spec] * 2
    consts = list(cw_k) + list(cw_v) + [bias_cmp, bias_sel, bias_win, cover, emat]
    in_specs += [_const_spec(a.shape) for a in consts]
    operands = list(caches) + [projs3] * 9 + [wk, wv] + consts

    grid_spec = pltpu.PrefetchScalarGridSpec(
        num_scalar_prefetch=1,
        grid=(nseq,),
        in_specs=in_specs,
        out_specs=[pl.BlockSpec((None, T_PAD, A_WIDTH), lambda i, pt: (i, 0, 0)), win_spec, win_spec],
        scratch_shapes=[pltpu.VMEM((n_cmp_rows, LANES), F32), pltpu.VMEM((n_cmp_rows, LANES), F32),
                        pltpu.VMEM((4, LANES, LANES), F32),
                        pltpu.VMEM((N_ROWBLK * T_PAD, LANES), BF16),
                        pltpu.VMEM((2, N_CACHES * n_pages, LANES, PAGE_SIZE), F32),
                        pltpu.SemaphoreType.DMA((2,))])
    return pl.pallas_call(
        functools.partial(_nsa_sample_body, n_pages=n_pages, past=past, wbuf=wbuf, nc=nc, nb=nb),
        grid_spec=grid_spec,
        out_shape=[jax.ShapeDtypeStruct((nseq, T_PAD, A_WIDTH), F32),
                   jax.ShapeDtypeStruct((nseq, LANES, wbuf), F32),
                   jax.ShapeDtypeStruct((nseq, LANES, wbuf), F32)],
        compiler_params=_cparams(1),
        name="nsa_sample",
    )(page_table, *operands)


def _hgrn_constants(rows, seg_stride, seg_len):
    idx = np.arange(rows)
    seg = idx // seg_stride
    pos = idx % seg_stride
    same = seg[:, None] == seg[None, :]
    pt, pu = pos[:, None], pos[None, :]
    mats = [same & (pu <= pt),
            same & (pu > pt) & (pu <= seg_len - 1)]
    levels = []
    m = seg_len // 2
    while m >= 2:
        levels.append(m)
        m //= 2
    for m in levels:
        mats.append(same & (pu >= (pt // m) * m) & (pu <= pt))
        mats.append(same & (pu > pt) & (pu <= (pt // m) * m + m - 1))
    wstack = np.concatenate(mats, axis=0).astype(np.float32)
    wstack = np.concatenate([wstack] * 3, axis=1)
    masks = []
    for m in levels + [1]:
        masks.append(same & (pt // (2 * m) == pu // (2 * m)) & ((pt // m) % 2 == 1) & ((pu // m) % 2 == 0))
    masks.append(idx[:, None] == idx[None, :])
    masks = np.stack(masks).astype(np.float32)
    return jnp.asarray(wstack, BF16), jnp.asarray(masks, F32), len(levels)


def _hgrn_chunk(q, f_pre, hl_ref, wst_ref, n_levels):
    rows = q.shape[0]
    hl = hl_ref[...]
    e = jnp.exp(hl - jnp.max(hl, axis=0, keepdims=True))
    lb = e[0:1] / jnp.sum(e, axis=0, keepdims=True)
    f = lb + (1.0 - lb) * _sigmoid(f_pre)
    g = jnp.log(f)
    k = 1.0 - f
    g_hi = g.astype(BF16)
    r1 = g - g_hi.astype(F32)
    g_mid = r1.astype(BF16)
    g_lo = (r1 - g_mid.astype(F32)).astype(BF16)
    sums = _nn(wst_ref[...], jnp.concatenate([g_hi, g_mid, g_lo], axis=0))
    big_g = sums[0:rows]
    qg = q * jnp.exp(big_g)
    kd = k * jnp.exp(sums[rows:2 * rows])
    q_lv, k_lv = [], []
    for li in range(n_levels):
        q_lv.append(q * jnp.exp(sums[(2 + 2 * li) * rows:(3 + 2 * li) * rows]))
        k_lv.append(k * jnp.exp(sums[(3 + 2 * li) * rows:(4 + 2 * li) * rows]))
    q_lv += [q * f, q]
    k_lv += [k, k]
    return big_g, qg, kd, q_lv, k_lv


def _hgrn_intra(q_lv, k_lv, masks_ref, hs):
    a = None
    for li in range(len(q_lv)):
        term = masks_ref[li] * _nt(q_lv[li][:, hs].astype(BF16), k_lv[li][:, hs].astype(BF16))
        a = term if a is None else a + term
    return a


def _hgrn_out(o, ng, z):
    o = o * lax.rsqrt(jnp.mean(o * o, axis=-1, keepdims=True) + EPS)
    return o * ng * _silu(z)


def _hgrn_prompt_body(q_ref, f_ref, i_ref, z_ref, hl_ref, ng_ref, wst_ref, masks_ref, o_ref, s_ref, st_ref,
                      *, n_levels, rows):
    c = pl.program_id(1)

    @pl.when(c == 0)
    def _():
        st_ref[...] = jnp.zeros(st_ref.shape, F32)

    chunks = []
    for ci in range(q_ref.shape[0] // rows):
        rs = slice(ci * rows, (ci + 1) * rows)
        v = i_ref[rs, :]
        big_g, qg, kd, q_lv, k_lv = _hgrn_chunk(q_ref[rs, :], f_ref[rs, :], hl_ref, wst_ref, n_levels)
        per_head = []
        for h in range(HG_HEADS):
            hs = slice(h * HG_DK, (h + 1) * HG_DK)
            vb = v[:, hs].astype(BF16)
            a = _hgrn_intra(q_lv, k_lv, masks_ref, hs)
            per_head.append((qg[:, hs].astype(BF16), _nn(a.astype(BF16), vb),
                             jnp.exp(big_g[rows - 1:rows, hs]), _tn(vb, kd[:, hs].astype(BF16))))
        chunks.append(per_head)

    for h in range(HG_HEADS):
        hs = slice(h * HG_DK, (h + 1) * HG_DK)
        st = st_ref[h]
        for ci, per_head in enumerate(chunks):
            rs = slice(ci * rows, (ci + 1) * rows)
            qgb, o_intra, decay, kv = per_head[h]
            o = _nt(qgb, st.astype(BF16)) + o_intra
            o_ref[rs, hs] = _hgrn_out(o, ng_ref[:, hs], z_ref[rs, hs])
            st = st * decay + kv
        st_ref[h] = st

        @pl.when(c == pl.num_programs(1) - 1)
        def _():
            s_ref[h] = st.T


HGRN_CHUNKS_PER_STEP = 8


def _hgrn_prompt(proj3, hg_lower, hg_norm_g):
    b, t, _ = proj3.shape
    chunk = math.gcd(t, HG_CHUNK)
    wst, masks, n_levels = _hgrn_constants(chunk, chunk, chunk)
    rows = chunk * math.gcd(t // chunk, HGRN_CHUNKS_PER_STEP)
    col = lambda c: pl.BlockSpec((None, rows, B_KEY), lambda i, j: (i, j, c // B_KEY))
    return pl.pallas_call(
        functools.partial(_hgrn_prompt_body, n_levels=n_levels, rows=chunk),
        grid=(b, t // rows),
        in_specs=[col(C_QB), col(C_FB), col(C_IB), col(C_ZB),
                  _const_spec(hg_lower.shape), _const_spec(hg_norm_g.shape),
                  _const_spec(wst.shape), _const_spec(masks.shape)],
        out_specs=[pl.BlockSpec((None, rows, B_WIDTH), lambda i, j: (i, j, 0)),
                   pl.BlockSpec((None, HG_HEADS, HG_DK, HG_DV), lambda i, j: (i, 0, 0, 0))],
        out_shape=[jax.ShapeDtypeStruct((b, t, B_WIDTH), F32),
                   jax.ShapeDtypeStruct((b, HG_HEADS, HG_DK, HG_DV), F32)],
        scratch_shapes=[pltpu.VMEM((HG_HEADS, HG_DV, HG_DK), F32)],
        compiler_params=_cparams(2),
        name="hgrn_prompt",
    )(proj3, proj3, proj3, proj3, hg_lower, hg_norm_g, wst, masks)


SEQ_PER_STEP = 8


def _hgrn_sample_body(q_ref, f_ref, i_ref, z_ref, s0_ref, hl_ref, ng_ref, wst_ref, masks_ref, o_ref, s_ref,
                      *, n_levels, n_new):
    v = i_ref[...]
    big_g, qg, kd, q_lv, k_lv = _hgrn_chunk(q_ref[...], f_ref[...], hl_ref, wst_ref, n_levels)
    for h in range(HG_HEADS):
        hs = slice(h * HG_DK, (h + 1) * HG_DK)
        vb = v[:, hs].astype(BF16)
        a = _hgrn_intra(q_lv, k_lv, masks_ref, hs)
        o_intra = _nn(a.astype(BF16), vb)
        for s in range(SEQ_PER_STEP):
            rs = slice(s * T_PAD, (s + 1) * T_PAD)
            s0 = s0_ref[s, h]
            o = _nn(qg[rs, hs].astype(BF16), s0.astype(BF16)) + o_intra[rs]
            o_ref[rs, hs] = _hgrn_out(o, ng_ref[:, hs], z_ref[rs, hs])
            decay = jnp.exp(big_g[rs, hs]).T[:, n_new - 1:n_new]
            s_ref[s, h] = s0 * decay + _tn(kd[rs, hs].astype(BF16), v[rs, hs].astype(BF16))


def _hgrn_sample(projs, state, hg_lower, hg_norm_g, n_new):
    nseq = state.shape[0]
    rows = SEQ_PER_STEP * T_PAD
    wst, masks, n_levels = _hgrn_constants(rows, T_PAD, n_new)
    col = lambda c: pl.BlockSpec((rows, B_KEY), lambda i: (i, c // B_KEY))
    st_spec = pl.BlockSpec((SEQ_PER_STEP, HG_HEADS, HG_DK, HG_DV), lambda i: (i, 0, 0, 0))
    return pl.pallas_call(
        functools.partial(_hgrn_sample_body, n_levels=n_levels, n_new=n_new),
        grid=(nseq // SEQ_PER_STEP,),
        in_specs=[col(C_QB), col(C_FB), col(C_IB), col(C_ZB), st_spec,
                  _const_spec(hg_lower.shape), _const_spec(hg_norm_g.shape),
                  _const_spec(wst.shape), _const_spec(masks.shape)],
        out_specs=[pl.BlockSpec((rows, B_WIDTH), lambda i: (i, 0)), st_spec],
        out_shape=[jax.ShapeDtypeStruct((nseq * T_PAD, B_WIDTH), F32),
                   jax.ShapeDtypeStruct(state.shape, F32)],
        compiler_params=_cparams(1),
        name="hgrn_sample",
    )(projs, projs, projs, projs, state, hg_lower, hg_norm_g, wst, masks)


def _merge_body(oa_ref, ob_ref, ma_ref, mb_ref, x_ref, wba_ref, wbb_ref, wout_ref, fg_ref, y_ref):
    a = _nn(oa_ref[...].astype(BF16), wba_ref[...])
    b = _nn(ob_ref[...].astype(BF16), wbb_ref[...])
    y = _sigmoid(ma_ref[...]) * a + _sigmoid(mb_ref[...]) * b
    h = x_ref[...] + _nn(y.astype(BF16), wout_ref[...])
    ms = jnp.mean(h * h, axis=-1, keepdims=True)
    y_ref[...] = h * lax.rsqrt(ms + EPS) * fg_ref[...]


def _merge(o_a, o_b, proj, x2d, w_ba, w_bb, w_out, final_g):
    m, d = x2d.shape
    tm = min(m, 512)
    row = lambda w, c: pl.BlockSpec((tm, w), lambda i: (i, c))
    return pl.pallas_call(
        _merge_body,
        grid=(m // tm,),
        in_specs=[row(A_WIDTH, 0), row(B_WIDTH, 0), row(d, C_MA // d), row(d, C_MA // d + 1), row(d, 0),
                  _const_spec(w_ba.shape), _const_spec(w_bb.shape), _const_spec(w_out.shape),
                  _const_spec((1, d))],
        out_specs=row(d, 0),
        out_shape=jax.ShapeDtypeStruct((m, d), F32),
        compiler_params=_cparams(1),
        name="merge_out",
    )(o_a, o_b, proj, proj, x2d, w_ba, w_bb, w_out, final_g.reshape(1, d))


def kernel(x_prompt, x_sample, cache_cmp_k, cache_cmp_v, cache_sel_k, cache_sel_v, state_win_k, state_win_v,
           state_hgrn, page_table, norm_g, w_in, cmp_pos_k, cmp_w1_k, cmp_w2_k, cmp_pos_v, cmp_w1_v, cmp_w2_v,
           rel_bias, hg_lower, hg_norm_g, w_branch_a, w_branch_b, w_out, final_g):
    depth = norm_g.shape[0]
    assert depth == 1, "single-layer trunk"
    b, t, d = x_prompt.shape
    nseq, n_new, _ = x_sample.shape
    assert C_MA % d == 0

    w = _arrange_w_in(w_in[0])
    n_cols = w.shape[1]
    w_ba = _permute_heads_cols(w_branch_a[0].T).T.astype(BF16)
    w_bb = w_branch_b[0].astype(BF16)
    w_o = w_out[0].astype(BF16)
    cw_k = _compress_weights(cmp_pos_k[0], cmp_w1_k[0], cmp_w2_k[0])
    cw_v = _compress_weights(cmp_pos_v[0], cmp_w1_v[0], cmp_w2_v[0])
    ng = hg_norm_g[0].reshape(1, B_WIDTH)
    kv_cols = lambda a, c: a[..., C_KV + c * A_KV:C_KV + (c + 1) * A_KV]

    xp2 = x_prompt.reshape(b * t, d)
    proj = _project(xp2, norm_g[0], w)
    proj3 = proj.reshape(b, t, n_cols)
    nc_p = (t - CMP_LEN) // CMP_STRIDE + 1
    kcmp, vcmp = _compress_prompt(proj3, cw_k, cw_v)
    bias_cmp = _bias_table(rel_bias, base=-(CMP_LEN - 1), step=TQ, row_stride=-CMP_STRIDE, col_stride=1,
                           rows=LANES, cols=TQ, steps=t // TQ)
    bias_tiles = _bias_table(rel_bias, base=0, step=TQ, row_stride=-1, col_stride=1, rows=TQ, cols=TQ,
                             steps=WINDOW // TQ + 2, lo=0, hi=WINDOW)
    o_a = _nsa_prompt(proj3, kcmp, vcmp, bias_cmp, bias_tiles)
    o_b, p_hgrn = _hgrn_prompt(proj3, hg_lower, ng)
    y_prompt = _merge(o_a.reshape(b * t, A_WIDTH), o_b.reshape(b * t, B_WIDTH), proj, xp2,
                      w_ba, w_bb, w_o, final_g).reshape(b, t, d)
    wb_p = min(WINDOW, t)
    p_kv = [kv_cols(proj3, c).reshape(1, b, t, NSA_KV, NSA_HD) for c in range(6)]
    p_state = p_kv[:4] + [p_kv[4][:, :, t - wb_p:], p_kv[5][:, :, t - wb_p:], p_hgrn[None]]

    xs_pad = jnp.pad(x_sample, ((0, 0), (0, T_PAD - n_new), (0, 0))).reshape(nseq * T_PAD, d)
    projs = _project(xs_pad, norm_g[0], w)
    projs3 = projs.reshape(nseq, T_PAD, n_cols)
    pool = cache_cmp_k.shape[1]
    chan_major = lambda a: jnp.transpose(a, (0, 2, 3, 1)).reshape(a.shape[0], A_KV, a.shape[1])
    row_major = lambda a: jnp.transpose(a.reshape(a.shape[0], NSA_KV, NSA_HD, a.shape[2]), (0, 3, 1, 2))
    caches = [chan_major(c[0]) for c in (cache_cmp_k, cache_cmp_v, cache_sel_k, cache_sel_v)]
    wbuf = state_win_k.shape[2]
    o_as, s_win_k, s_win_v = _nsa_sample(projs3, caches, chan_major(state_win_k[0]), chan_major(state_win_v[0]),
                                         page_table, cw_k, cw_v, rel_bias, n_new)
    s_win_k, s_win_v = row_major(s_win_k), row_major(s_win_v)
    o_bs, s_hgrn = _hgrn_sample(projs, state_hgrn[0], hg_lower, ng, n_new)
    y_s = _merge(o_as.reshape(nseq * T_PAD, A_WIDTH), o_bs, projs, xs_pad, w_ba, w_bb, w_o, final_g)
    y_sample = y_s.reshape(nseq, T_PAD, d)[:, :n_new]
    s_kv = [kv_cols(projs3, c)[:, :n_new].reshape(1, nseq, n_new, NSA_KV, NSA_HD) for c in range(4)]
    s_state = s_kv + [s_win_k.reshape(1, nseq, wbuf, NSA_KV, NSA_HD),
                      s_win_v.reshape(1, nseq, wbuf, NSA_KV, NSA_HD), s_hgrn[None]]

    return (y_prompt, y_sample, *p_state, *s_state)
```

```python
import functools
import math

import numpy as np
import jax
import jax.numpy as jnp
from jax import lax
from jax.experimental import pallas as pl
from jax.experimental.pallas import tpu as pltpu

F32 = jnp.float32
BF16 = jnp.bfloat16

NSA_HEADS = 8
NSA_KV = 2
NSA_HD = 64
NSA_REP = NSA_HEADS // NSA_KV
CMP_LEN = 32
CMP_STRIDE = 16
CMP_HID = 64
SEL_BLOCK = 64
SEL_TOPN = 8
WINDOW = 512
HG_HEADS = 4
HG_DK = 128
HG_DV = 128
HG_CHUNK = 64
N_BUCKETS = 32
MAX_DIST = 128
EPS = 1e-6
NEG = -1e30
FORCED = 1e9
PAGE_SIZE = 128

A_WIDTH = NSA_HEADS * NSA_HD
A_KV = NSA_KV * NSA_HD
B_WIDTH = HG_HEADS * HG_DV
B_KEY = HG_HEADS * HG_DK

LANES = 128
SUBLANES = 8
VMEM_LIMIT_BYTES = 56 * 1024 * 1024

TQ = 128
T_PAD = SUBLANES
N_ROWBLK = NSA_HEADS

C_Q = 0
C_KV = C_Q + A_WIDTH
C_GATE = C_KV + 6 * A_KV
C_ZA = C_GATE + 2 * LANES
C_QB = C_ZA + A_WIDTH
C_FB = C_QB + B_KEY
C_IB = C_FB + B_KEY
C_ZB = C_IB + B_WIDTH
C_MA = C_ZB + B_WIDTH

CMP_SRC_ROWS = (LANES - 1) * CMP_STRIDE + CMP_LEN
CMP_PITCH = CMP_STRIDE + 1


def _nn(a, b):
    return jnp.dot(a, b, preferred_element_type=F32)


def _nt(a, b):
    return lax.dot_general(a, b, (((1,), (1,)), ((), ())), preferred_element_type=F32)


def _tn(a, b):
    return lax.dot_general(a, b, (((0,), (0,)), ((), ())), preferred_element_type=F32)


def _sigmoid(x):
    return 1.0 / (1.0 + jnp.exp(-x))


def _silu(x):
    return x * _sigmoid(x)


def _split2(x):
    hi = x.astype(BF16)
    lo = (x - hi.astype(F32)).astype(BF16)
    return hi, lo


def _cparams(n_grid):
    return pltpu.CompilerParams(dimension_semantics=("arbitrary",) * n_grid,
                                vmem_limit_bytes=VMEM_LIMIT_BYTES)


def _t5_bucket(rel):
    n = jnp.maximum(rel, 0)
    exact = N_BUCKETS // 2
    nf = jnp.maximum(n, 1).astype(F32)
    scaled = jnp.log(nf / exact) / math.log(MAX_DIST / exact) * (N_BUCKETS - exact)
    large = exact + jnp.floor(scaled).astype(jnp.int32)
    large = jnp.minimum(large, N_BUCKETS - 1)
    return jnp.where(n < exact, n, large)


def _rowblk_head(rb):
    return NSA_REP * (rb % NSA_KV) + rb // NSA_KV


def _bias_body(tab_ref, o_ref, *, base, step, row_stride, col_stride, lo, hi, cols):
    sub_r = min(o_ref.shape[0], 4 * SUBLANES)
    sub_c = min(cols, LANES)
    shape = (sub_r, sub_c)
    row = lax.broadcasted_iota(jnp.int32, shape, 0)
    col = lax.broadcasted_iota(jnp.int32, shape, 1)
    for r0 in range(0, o_ref.shape[0], sub_r):
        for c0 in range(0, cols, sub_c):
            rel = base + step * pl.program_id(0) + row_stride * (row + r0) + col_stride * (col + c0)
            bucket = _t5_bucket(rel)
            accs = [jnp.zeros(shape, F32)] * N_ROWBLK
            for k in range(N_BUCKETS):
                hit = bucket == k
                accs = [jnp.where(hit, tab_ref[k, _rowblk_head(rb)], accs[rb]) for rb in range(N_ROWBLK)]
            for rb in range(N_ROWBLK):
                acc = accs[rb]
                if lo is not None:
                    acc = jnp.where((rel >= lo) & (rel < hi), acc, NEG)
                o_ref[r0:r0 + sub_r, rb * cols + c0:rb * cols + c0 + sub_c] = acc


def _bias_table(rel_bias, *, base, step, row_stride, col_stride, rows, cols, steps, lo=None, hi=None):
    return pl.pallas_call(
        functools.partial(_bias_body, base=base, step=step, row_stride=row_stride, col_stride=col_stride,
                          lo=lo, hi=hi, cols=cols),
        grid=(steps,),
        in_specs=[pl.BlockSpec(memory_space=pltpu.SMEM)],
        out_specs=pl.BlockSpec((None, rows, N_ROWBLK * cols), lambda i: (i, 0, 0)),
        out_shape=jax.ShapeDtypeStruct((steps, rows, N_ROWBLK * cols), F32),
        compiler_params=_cparams(1),
        name="bias_table",
    )(rel_bias)


def _bias_table_sample(rel_bias, base, col_stride, cols):
    t = _bias_table(rel_bias, base=base, step=0, row_stride=1, col_stride=-col_stride, rows=T_PAD, cols=cols, steps=1)
    return t.reshape(T_PAD, N_ROWBLK, cols).swapaxes(0, 1)


def _proj_body(x_ref, g_ref, w_ref, o_ref):
    x = x_ref[...]
    ms = jnp.mean(x * x, axis=-1, keepdims=True)
    xn = (x * lax.rsqrt(ms + EPS) * g_ref[...]).astype(BF16)
    o_ref[...] = _nn(xn, w_ref[...])


PROJ_ROWS = 512


def _project(x2d, norm_g, w):
    m, d = x2d.shape
    n = w.shape[1]
    tm = min(m, PROJ_ROWS)
    return pl.pallas_call(
        _proj_body,
        grid=(m // tm,),
        in_specs=[pl.BlockSpec((tm, d), lambda i: (i, 0)),
                  pl.BlockSpec((1, d), lambda i: (0, 0)),
                  pl.BlockSpec((d, n), lambda i: (0, 0), pipeline_mode=pl.Buffered(1))],
        out_specs=pl.BlockSpec((tm, n), lambda i: (i, 0)),
        out_shape=jax.ShapeDtypeStruct((m, n), F32),
        compiler_params=_cparams(1),
        name="in_proj",
    )(x2d, norm_g.reshape(1, d), w)


def _permute_heads_cols(w):
    lead = w.shape[:-1]
    return w.reshape(lead + (NSA_KV, NSA_REP, NSA_HD)).swapaxes(-3, -2).reshape(lead + (A_WIDTH,))


def _arrange_w_in(w_in):
    d = w_in.shape[0]
    sizes = (A_WIDTH, 6 * A_KV, 3 * NSA_HEADS, A_WIDTH, B_KEY, B_KEY, B_WIDTH, B_WIDTH, d, d)
    offs = np.concatenate([[0], np.cumsum(sizes)])
    q_a, kv_a, g_a, z_a, q_b, f_b, i_b, z_b, m_a, m_b = [w_in[:, offs[i]:offs[i + 1]] for i in range(10)]
    gate_pad = jnp.zeros((d, 2 * LANES - 3 * NSA_HEADS), w_in.dtype)
    w = jnp.concatenate([_permute_heads_cols(q_a), kv_a, g_a, gate_pad, _permute_heads_cols(z_a),
                         q_b, f_b, i_b, z_b, m_a, m_b], axis=1)
    return w.astype(BF16)


def _compress(streams, pitch):
    accs = [jnp.zeros((len(srcs) * LANES, LANES), F32) for srcs, _, _, _ in streams]
    for j in range(CMP_LEN):
        start = (j // CMP_STRIDE) * pitch + j % CMP_STRIDE
        for n, (srcs, pos_ref, w1_ref, _) in enumerate(streams):
            xj = jnp.concatenate([r[pl.ds(start, LANES, stride=pitch), :] for r in srcs], axis=0) + pos_ref[j:j + 1, :]
            accs[n] = accs[n] + _nn(xj.astype(BF16), w1_ref[j])
    return [_nn(_silu(acc).astype(BF16), w2_ref[...]) for acc, (_, _, _, w2_ref) in zip(accs, streams)]


def _compress_prompt_body(kc_ref, vc_ref, posk_ref, w1k_ref, w2k_ref, posv_ref, w1v_ref, w2v_ref,
                          ko_ref, vo_ref, kbuf_ref, vbuf_ref):
    t = kc_ref.shape[0]
    for buf_ref, src_ref in ((kbuf_ref, kc_ref), (vbuf_ref, vc_ref)):
        buf_ref[t:, :] = jnp.zeros((buf_ref.shape[0] - t, LANES), F32)
        buf_ref[:t, :] = src_ref[...]
    ko_ref[...], vo_ref[...] = _compress([([kbuf_ref], posk_ref, w1k_ref, w2k_ref),
                                          ([vbuf_ref], posv_ref, w1v_ref, w2v_ref)], CMP_STRIDE)


def _block_diag2(w):
    z = jnp.zeros_like(w)
    return jnp.concatenate([jnp.concatenate([w, z], axis=-1), jnp.concatenate([z, w], axis=-1)], axis=-2)


def _compress_weights(pos, w1, w2):
    pos2 = jnp.concatenate([pos, pos], axis=-1)
    w1bd = _block_diag2(w1.reshape(CMP_LEN, NSA_HD, CMP_HID)).astype(BF16)
    w2bd = _block_diag2(w2).astype(BF16)
    return pos2, w1bd, w2bd


def _const_spec(shape):
    nd = len(shape)
    return pl.BlockSpec(shape, lambda *_: (0,) * nd)


def _compress_prompt(proj3, cw_k, cw_v):
    b, t, _ = proj3.shape
    col = C_KV // LANES
    w_specs = [_const_spec(a.shape) for a in cw_k + cw_v]
    return pl.pallas_call(
        _compress_prompt_body,
        grid=(b,),
        in_specs=[pl.BlockSpec((None, t, LANES), lambda i: (i, 0, col)),
                  pl.BlockSpec((None, t, LANES), lambda i: (i, 0, col + 1))] + w_specs,
        out_specs=[pl.BlockSpec((None, LANES, LANES), lambda i: (i, 0, 0))] * 2,
        out_shape=[jax.ShapeDtypeStruct((b, LANES, LANES), F32)] * 2,
        scratch_shapes=[pltpu.VMEM((max(t + CMP_LEN, CMP_SRC_ROWS), LANES), F32)] * 2,
        compiler_params=_cparams(1),
        name="compress_prompt",
    )(proj3, proj3, *cw_k, *cw_v)


def _select_blocks(imp, qpos, nb, axis):
    j = lax.broadcasted_iota(jnp.int32, imp.shape, axis)
    cur = qpos // SEL_BLOCK
    valid = j * SEL_BLOCK <= qpos
    forced = (j == 0) | (j == cur) | (j == cur - 1)
    score = jnp.where(valid, jnp.where(forced, FORCED, imp), -1.0)
    score = jnp.where(j < nb, score, -2.0)
    rank = jnp.zeros(imp.shape, F32)
    for i in range(nb):
        s_i = lax.slice_in_dim(score, i, i + 1, axis=axis)
        tie = jnp.where(i < j, 1.0, 0.0)
        rank = rank + jnp.where(s_i > score, 1.0, jnp.where(s_i == score, tie, 0.0))
    return jnp.where((rank < min(SEL_TOPN, nb)) & (j < nb), 1.0, 0.0)


def _masked_softmax(s, mask):
    s = jnp.where(mask, s, NEG)
    m = jnp.max(s, axis=-1, keepdims=True)
    e = jnp.exp(s - m)
    p = e / jnp.sum(e, axis=-1, keepdims=True)
    return jnp.where(mask, p, 0.0)


def _stack_queries(q_ref, qbd_ref, rows):
    lane = lax.broadcasted_iota(jnp.int32, (rows, LANES), 1)
    for r in range(NSA_REP):
        blk = q_ref[:, r * LANES:(r + 1) * LANES] * (NSA_HD ** -0.5)
        qbd_ref[(2 * r) * rows:(2 * r + 1) * rows, :] = jnp.where(lane < NSA_HD, blk, 0.0).astype(BF16)
        qbd_ref[(2 * r + 1) * rows:(2 * r + 2) * rows, :] = jnp.where(lane >= NSA_HD, blk, 0.0).astype(BF16)


def _gate_and_store(gate_ref, za_ref, o_ref, o_cmp, o_sel, o_win, rows):
    gs = _sigmoid(gate_ref[...])
    lane = lax.broadcasted_iota(jnp.int32, (rows, LANES), 1)
    for r in range(NSA_REP):
        parts = []
        for g in range(NSA_KV):
            rb = 2 * r + g
            h = _rowblk_head(rb)
            o = (gs[:, h:h + 1] * o_cmp[rb] + gs[:, NSA_HEADS + h:NSA_HEADS + h + 1] * o_sel[rb]
                 + gs[:, 2 * NSA_HEADS + h:2 * NSA_HEADS + h + 1] * o_win[rb])
            parts.append(o)
        blk = jnp.where(lane < NSA_HD, parts[0], parts[1])
        o_ref[:, r * LANES:(r + 1) * LANES] = blk * _silu(za_ref[:, r * LANES:(r + 1) * LANES])


N_COL = N_ROWBLK * TQ
MASK_COLS = NSA_KV * TQ
COL_CHUNK = TQ
SEL_TILES_PER_ITER = 2


def _nsa_prompt_body(q_ref, gate_ref, za_ref, ks_ref, vs_ref, kw_ref, vw_ref, kcmp_ref, vcmp_ref,
                     bcmp_ref, btile_ref, covert_ref, o_ref,
                     qt_ref, ksb_ref, vst_ref, kwb_ref, vwt_ref, vcmpt_ref, mask_ref, s_ref, m_ref, l_ref, acc_ref, res_ref,
                     *, nc, nb):
    qb = pl.program_id(1)
    nkt = ksb_ref.shape[0]
    sub = lax.broadcasted_iota(jnp.int32, (LANES, TQ), 0)
    tok = lax.broadcasted_iota(jnp.int32, (LANES, TQ), 1)

    @pl.when(qb == 0)
    def _():
        for kt in range(nkt):
            rows = slice(kt * TQ, (kt + 1) * TQ)
            ksb_ref[kt] = ks_ref[rows, :].astype(BF16)
            kwb_ref[kt] = kw_ref[rows, :].astype(BF16)
            vst_ref[kt] = vs_ref[rows, :].T.astype(BF16)
            vwt_ref[kt] = vw_ref[rows, :].T.astype(BF16)
        vcmpt_ref[...] = vcmp_ref[...].T.astype(BF16)

    for r in range(NSA_REP):
        qt = (q_ref[:, r * LANES:(r + 1) * LANES] * (NSA_HD ** -0.5)).T
        qt_ref[:, (2 * r) * TQ:(2 * r + 1) * TQ] = jnp.where(sub < NSA_HD, qt, 0.0).astype(BF16)
        qt_ref[:, (2 * r + 1) * TQ:(2 * r + 2) * TQ] = jnp.where(sub >= NSA_HD, qt, 0.0).astype(BF16)

    valid = ((CMP_STRIDE * sub + CMP_LEN - 1) <= qb * TQ + tok) & (sub < nc)
    kcb = kcmp_ref[...].astype(BF16)
    psum = [None] * NSA_KV
    for rb in range(N_ROWBLK):
        cols = slice(rb * TQ, (rb + 1) * TQ)
        s = jnp.where(valid, _nn(kcb, qt_ref[:, cols]) + bcmp_ref[:, cols], NEG)
        e = jnp.exp(s - jnp.max(s, axis=0, keepdims=True))
        p = jnp.where(valid, e / jnp.sum(e, axis=0, keepdims=True), 0.0)
        res_ref[0, :, cols] = _nn(vcmpt_ref[...], p.astype(BF16))
        g = rb % NSA_KV
        psum[g] = p if psum[g] is None else psum[g] + p

    p_hi, p_lo = _split2(jnp.concatenate(psum, axis=1))
    imp = _nn(covert_ref[...], p_hi) + _nn(covert_ref[...], p_lo)
    nbp = _round_up(nb, SUBLANES)
    tok2 = lax.broadcasted_iota(jnp.int32, (1, MASK_COLS), 1) % TQ
    sel = _select_blocks(imp[:nbp], qb * TQ + tok2, nb, 0)
    upper = lax.broadcasted_iota(jnp.int32, (TQ, MASK_COLS), 0) < SEL_BLOCK
    for kt in range(nkt):
        picked = jnp.where(upper, sel[2 * kt:2 * kt + 1, :], sel[2 * kt + 1:2 * kt + 2, :]) > 0.5
        mask_ref[kt] = jnp.where(kt <= qb, jnp.where(picked, 0.0, NEG), NEG)

    m_ref[...] = jnp.full(m_ref.shape, NEG, F32)
    l_ref[...] = jnp.zeros(l_ref.shape, F32)
    acc_ref[...] = jnp.zeros(acc_ref.shape, F32)

    def sel_scores(pair, buf):
        for u in range(SEL_TILES_PER_ITER):
            kt = jnp.minimum(pair * SEL_TILES_PER_ITER + u, nkt - 1)
            k_t = ksb_ref[kt]
            bidx = jnp.clip(qb - kt, 0, 2)
            for c in range(N_COL // COL_CHUNK):
                cols = slice(c * COL_CHUNK, (c + 1) * COL_CHUNK)
                mcols = slice(c * COL_CHUNK % MASK_COLS, c * COL_CHUNK % MASK_COLS + COL_CHUNK)
                s_ref[buf, u, :, cols] = (_nn(k_t, qt_ref[:, cols]) + btile_ref[bidx, :, cols]
                                          + mask_ref[kt, :, mcols])

    def sel_update(pair, buf):
        v_ts = [vst_ref[pair * SEL_TILES_PER_ITER + u] for u in range(SEL_TILES_PER_ITER)]
        for c in range(N_COL // COL_CHUNK):
            cols = slice(c * COL_CHUNK, (c + 1) * COL_CHUNK)
            ss = [s_ref[buf, u, :, cols] for u in range(SEL_TILES_PER_ITER)]
            m_old = m_ref[:, cols]
            m_new = m_old
            for s in ss:
                m_new = jnp.maximum(m_new, jnp.max(s, axis=0, keepdims=True))
            alpha = jnp.exp(m_old - m_new)
            l_new = alpha * l_ref[:, cols]
            acc = alpha * acc_ref[:, cols]
            for s, v_t in zip(ss, v_ts):
                p = jnp.exp(s - m_new)
                l_new = l_new + jnp.sum(p, axis=0, keepdims=True)
                acc = acc + _nn(v_t, p.astype(BF16))
            l_ref[:, cols] = l_new
            acc_ref[:, cols] = acc
            m_ref[:, cols] = m_new

    def sel_body(it, carry):
        sel_scores(2 * it + 1, 1)
        sel_update(2 * it, 0)
        sel_scores(2 * it + 2, 0)
        sel_update(2 * it + 1, 1)
        return carry

    sel_scores(0, 0)
    lax.fori_loop(0, qb // (2 * SEL_TILES_PER_ITER) + 1, sel_body, 0)
    res_ref[1] = acc_ref[...] / l_ref[...]

    n_wt = WINDOW // TQ + 1
    for c in range(N_COL // COL_CHUNK):
        cols = slice(c * COL_CHUNK, (c + 1) * COL_CHUNK)
        qc = qt_ref[:, cols]
        ss = []
        for dist in range(n_wt):
            kt = jnp.maximum(qb - dist, 0)
            bidx = jnp.where(qb >= dist, dist, n_wt)
            ss.append(_nn(kwb_ref[kt], qc) + btile_ref[bidx, :, cols])
        m = jnp.max(ss[0], axis=0, keepdims=True)
        for s in ss[1:]:
            m = jnp.maximum(m, jnp.max(s, axis=0, keepdims=True))
        l_w = None
        acc = None
        for dist, s in enumerate(ss):
            p = jnp.exp(s - m)
            ps = jnp.sum(p, axis=0, keepdims=True)
            pv = _nn(vwt_ref[jnp.maximum(qb - dist, 0)], p.astype(BF16))
            l_w = ps if l_w is None else l_w + ps
            acc = pv if acc is None else acc + pv
        res_ref[2, :, cols] = acc / l_w

    gt = _sigmoid(gate_ref[...]).T
    for r in range(NSA_REP):
        parts = []
        for g in range(NSA_KV):
            rb = 2 * r + g
            h = _rowblk_head(rb)
            cols = slice(rb * TQ, (rb + 1) * TQ)
            parts.append(gt[h:h + 1, :] * res_ref[0, :, cols]
                         + gt[NSA_HEADS + h:NSA_HEADS + h + 1, :] * res_ref[1, :, cols]
                         + gt[2 * NSA_HEADS + h:2 * NSA_HEADS + h + 1, :] * res_ref[2, :, cols])
        blk = jnp.where(sub < NSA_HD, parts[0], parts[1]).T
        o_ref[:, r * LANES:(r + 1) * LANES] = blk * _silu(za_ref[:, r * LANES:(r + 1) * LANES])


def _cover_matrix(nc, nb):
    cs = np.arange(nc) * CMP_STRIDE
    ce = cs + CMP_LEN
    bs = np.arange(nb) * SEL_BLOCK
    be = bs + SEL_BLOCK
    ov = np.clip(np.minimum(ce[:, None], be[None, :]) - np.maximum(cs[:, None], bs[None, :]), 0, None) / CMP_STRIDE
    out = np.zeros((LANES, LANES), np.float32)
    out[:nc, :nb] = ov
    return jnp.asarray(out, BF16)


def _expand_matrix(nkeys):
    j = np.arange(LANES)[:, None]
    k = np.arange(nkeys)[None, :]
    return jnp.asarray((k // SEL_BLOCK == j).astype(np.float32), BF16)


def _nsa_prompt(proj3, kcmp, vcmp, bias_cmp, bias_tiles):
    b, t, _ = proj3.shape
    nq = t // TQ
    nc = (t - CMP_LEN) // CMP_STRIDE + 1
    nb = t // SEL_BLOCK
    assert nc <= LANES and nb <= LANES and t % (TQ * 2 * SEL_TILES_PER_ITER) == 0
    kv_col = C_KV // LANES
    kv_spec = lambda c: pl.BlockSpec((None, t, LANES), lambda i, j: (i, 0, kv_col + c))
    cover_t = _cover_matrix(nc, nb).T
    tile_bf16 = pltpu.VMEM((nq, TQ, LANES), BF16)
    return pl.pallas_call(
        functools.partial(_nsa_prompt_body, nc=nc, nb=nb),
        grid=(b, nq),
        in_specs=[pl.BlockSpec((None, TQ, A_WIDTH), lambda i, j: (i, j, C_Q // A_WIDTH)),
                  pl.BlockSpec((None, TQ, LANES), lambda i, j: (i, j, C_GATE // LANES)),
                  pl.BlockSpec((None, TQ, A_WIDTH), lambda i, j: (i, j, C_ZA // A_WIDTH)),
                  kv_spec(2), kv_spec(3), kv_spec(4), kv_spec(5),
                  pl.BlockSpec((None, LANES, LANES), lambda i, j: (i, 0, 0)),
                  pl.BlockSpec((None, LANES, LANES), lambda i, j: (i, 0, 0)),
                  pl.BlockSpec((None, LANES, N_COL), lambda i, j: (j, 0, 0)),
                  _const_spec(bias_tiles.shape), _const_spec(cover_t.shape)],
        out_specs=pl.BlockSpec((None, TQ, A_WIDTH), lambda i, j: (i, j, 0)),
        out_shape=jax.ShapeDtypeStruct((b, t, A_WIDTH), F32),
        scratch_shapes=[pltpu.VMEM((LANES, N_COL), BF16),
                        tile_bf16, tile_bf16, tile_bf16, tile_bf16,
                        pltpu.VMEM((LANES, LANES), BF16),
                        pltpu.VMEM((nq, TQ, MASK_COLS), F32),
                        pltpu.VMEM((2, SEL_TILES_PER_ITER, TQ, N_COL), F32),
                        pltpu.VMEM((1, N_COL), F32),
                        pltpu.VMEM((1, N_COL), F32),
                        pltpu.VMEM((LANES, N_COL), F32),
                        pltpu.VMEM((3, LANES, N_COL), F32)],
        compiler_params=_cparams(2),
        name="nsa_prompt",
    )(proj3, proj3, proj3, proj3, proj3, proj3, proj3, kcmp, vcmp, bias_cmp, bias_tiles, cover_t)


N_CACHES = 4


NSA_SEQ_PER_STEP = 2


def _group_page_copy(cache_refs, pt_ref, pbuf, sems, step, slot, s, c, pg, n_pages):
    seq = step * NSA_SEQ_PER_STEP + s
    return pltpu.make_async_copy(cache_refs[c].at[pt_ref[seq, pg]], pbuf.at[slot, s, c * n_pages + pg],
                                 sems.at[slot])


def _nsa_sample_group_body(pt_ref, *refs, n_pages, **static):
    cache_refs = refs[:N_CACHES]
    per_seq = refs[N_CACHES:N_CACHES + 11]
    consts = refs[N_CACHES + 11:N_CACHES + 22]
    outs = refs[N_CACHES + 22:N_CACHES + 25]
    scratch = refs[N_CACHES + 25:-2]
    pbuf, sems = refs[-2:]
    i = pl.program_id(0)
    slot = i % 2

    def all_pages(step, slot_):
        return [_group_page_copy(cache_refs, pt_ref, pbuf, sems, step, slot_, s, c, pg, n_pages)
                for s in range(NSA_SEQ_PER_STEP) for c in range(N_CACHES) for pg in range(n_pages)]

    @pl.when(i == 0)
    def _():
        for cp in all_pages(0, 0):
            cp.start()

    @pl.when(i + 1 < pl.num_programs(0))
    def _():
        for cp in all_pages(i + 1, 1 - slot):
            cp.start()

    for cp in all_pages(i, slot):
        cp.wait()
    pages = [[[pbuf.at[slot, s, c * n_pages + pg] for pg in range(n_pages)] for c in range(N_CACHES)]
             for s in range(NSA_SEQ_PER_STEP)]
    _nsa_sample_group(pages, per_seq, consts, outs, scratch, n_pages=n_pages, **static)


def _nsa_sample_group(pages, per_seq, consts, outs, scratch, *, n_pages, past, wbuf, nc, nb):
    q_ref, kc_ref, vc_ref, ks_ref, vs_ref, kw_ref, vw_ref, gate_ref, za_ref, wk_ref, wv_ref = per_seq
    (posk_ref, w1k_ref, w2k_ref, posv_ref, w1v_ref, w2v_ref, bcmp_ref, bsel_ref, bwin_ref,
     cover_ref, emat_ref) = consts
    o_ref, swk_ref, swv_ref = outs
    kc_all, vc_all, newt_ref, qbd_ref = scratch
    nseq = NSA_SEQ_PER_STEP
    seqs = range(nseq)
    n_keys = bsel_ref.shape[2]
    n_new = T_PAD // 2
    chunks_per_page = PAGE_SIZE // CMP_STRIDE
    nrow = N_ROWBLK * T_PAD

    for dst, cache, new_ref in ((kc_all, 0, kc_ref), (vc_all, 1, vc_ref)):
        for s in seqs:
            for pg in range(n_pages):
                rows = pages[s][cache][pg][...].T
                for c in range(chunks_per_page):
                    r0 = (pg * chunks_per_page + c) * CMP_PITCH
                    dst[s, r0:r0 + CMP_STRIDE, :] = rows[c * CMP_STRIDE:(c + 1) * CMP_STRIDE, :]
            r0 = (past // CMP_STRIDE) * CMP_PITCH
            dst[s, r0:r0 + T_PAD, :] = new_ref[s]
            dst[s, r0 + T_PAD:, :] = jnp.zeros((dst.shape[1] - r0 - T_PAD, LANES), F32)
    kcmp, vcmp = [c.astype(BF16) for c in _compress(
        [([kc_all.at[s] for s in seqs], posk_ref, w1k_ref, w2k_ref),
         ([vc_all.at[s] for s in seqs], posv_ref, w1v_ref, w2v_ref)], CMP_PITCH)]
    per_s = lambda x, s: x[s * LANES:(s + 1) * LANES]

    newt_ref[...] = jnp.zeros(newt_ref.shape, F32)
    for s in seqs:
        for idx, new_ref in enumerate((ks_ref, vs_ref, kw_ref, vw_ref)):
            newt_ref[s, idx, 0:T_PAD, :] = new_ref[s]
        _stack_queries(q_ref.at[s], qbd_ref.at[s], T_PAD)
    qbd = [qbd_ref[s] for s in seqs]
    tok = lax.broadcasted_iota(jnp.int32, (T_PAD, LANES), 0)
    lane = lax.broadcasted_iota(jnp.int32, (T_PAD, LANES), 1)
    qpos = past + tok
    stack = lambda xs: jnp.concatenate(xs, axis=0)
    by_head = lambda x: x.reshape(nseq, NSA_REP, NSA_KV, T_PAD, LANES)

    s_c = by_head(stack([_nt(qbd[s], per_s(kcmp, s)) for s in seqs])) + _by_head_bias(bcmp_ref[...])
    mask_c = ((CMP_STRIDE * lane + CMP_LEN - 1) <= qpos) & (lane < nc)
    p = _masked_softmax(s_c, mask_c[None, None, None]).reshape(nseq, nrow, LANES)
    o_cmp = [_nn(p[s].astype(BF16), per_s(vcmp, s)).reshape(N_ROWBLK, T_PAD, LANES) for s in seqs]

    psum = p.reshape(nseq, NSA_REP, NSA_KV * T_PAD, LANES).sum(axis=1).reshape(nseq * NSA_KV * T_PAD, LANES)
    p_hi, p_lo = _split2(psum)
    imp = _nn(p_hi, cover_ref[...]) + _nn(p_lo, cover_ref[...])
    sel = _select_blocks(imp.reshape(nseq, NSA_KV, T_PAD, LANES), qpos[None, None], nb, 3)
    selk = _nn(sel.reshape(nseq * NSA_KV * T_PAD, LANES).astype(BF16), emat_ref[...])
    selk = selk.reshape(nseq, NSA_KV, T_PAD, n_keys)

    def softmax_pv(scores, values):
        m = scores[0]
        for s_t in scores[1:]:
            m = jnp.maximum(m, s_t)
        m = jnp.max(m, axis=-1, keepdims=True)
        acc = [None] * nseq
        tot = None
        for s_t, (v_ts, channel_major) in zip(scores, values):
            p_t = jnp.exp(s_t - m)
            pb = p_t.reshape(nseq, nrow, LANES).astype(BF16)
            for s in seqs:
                pv = _nt(pb[s], v_ts[s]) if channel_major else _nn(pb[s], v_ts[s])
                acc[s] = pv if acc[s] is None else acc[s] + pv
            tot = p_t if tot is None else tot + p_t
        inv = 1.0 / jnp.sum(tot, axis=-1, keepdims=True).reshape(nseq, nrow, 1)
        return [(acc[s] * inv[s]).reshape(N_ROWBLK, T_PAD, LANES) for s in seqs]

    scores, values = [], []
    for t in range(n_pages + 1):
        cols = slice(t * LANES, (t + 1) * LANES)
        allowed = ((t * LANES + lane) <= qpos)[None, None] & (selk[:, :, :, cols] > 0.5)
        madd = jnp.where(allowed, 0.0, NEG)[:, None]
        if t < n_pages:
            raw = [_nn(qbd[s], pages[s][2][t][...].astype(BF16)) for s in seqs]
            values.append(([pages[s][3][t][...].astype(BF16) for s in seqs], True))
        else:
            raw = [_nt(qbd[s], newt_ref[s, 0].astype(BF16)) for s in seqs]
            values.append(([newt_ref[s, 1].astype(BF16) for s in seqs], False))
        scores.append(by_head(stack(raw)) + _by_head_bias(bsel_ref[:, :, cols]) + madd)
    o_sel = softmax_pv(scores, values)

    scores, values = [], []
    kwt = [wk_ref[s].astype(BF16) for s in seqs]
    vwt = [wv_ref[s].astype(BF16) for s in seqs]
    for t in range(wbuf // LANES + 1):
        cols = slice(t * LANES, (t + 1) * LANES)
        rel_w = wbuf + tok - (t * LANES + lane)
        madd = jnp.where((rel_w >= 0) & (rel_w < WINDOW), 0.0, NEG)[None, None, None]
        if t < wbuf // LANES:
            raw = [_nn(qbd[s], kwt[s][:, cols]) for s in seqs]
            values.append(([vwt[s][:, cols] for s in seqs], True))
        else:
            raw = [_nt(qbd[s], newt_ref[s, 2].astype(BF16)) for s in seqs]
            values.append(([newt_ref[s, 3].astype(BF16) for s in seqs], False))
        scores.append(by_head(stack(raw)) + _by_head_bias(bwin_ref[:, :, cols]) + madd)
    o_win = softmax_pv(scores, values)

    lane_full = lax.broadcasted_iota(jnp.int32, (LANES, LANES), 1)
    for s in seqs:
        _gate_and_store(gate_ref.at[s], za_ref.at[s], o_ref.at[s], o_cmp[s], o_sel[s], o_win[s], T_PAD)
        for idx, w_ref, out_ref in ((2, wk_ref, swk_ref), (3, wv_ref, swv_ref)):
            rolled = pltpu.roll(w_ref[s], wbuf - n_new, axis=1)
            tail = pltpu.roll(newt_ref[s, idx], LANES - n_new, axis=0).T
            out_ref[s, :, :wbuf - LANES] = rolled[:, :wbuf - LANES]
            out_ref[s, :, wbuf - LANES:] = jnp.where(lane_full >= LANES - n_new, tail, rolled[:, wbuf - LANES:])


def _by_head_bias(b):
    return b.reshape(1, NSA_REP, NSA_KV, T_PAD, b.shape[-1])


def _round_up(x, m):
    return (x + m - 1) // m * m


def _nsa_sample(projs3, caches, wk, wv, page_table, cw_k, cw_v, rel_bias, n_new):
    nseq, n_pages = page_table.shape
    past = n_pages * PAGE_SIZE
    wbuf = wk.shape[2]
    length = past + n_new
    nc = (length - CMP_LEN) // CMP_STRIDE + 1
    nb = -(-length // SEL_BLOCK)
    n_keys = past + LANES
    n_win = wbuf + LANES
    pitched = lambda p: (p // CMP_STRIDE) * CMP_PITCH + p % CMP_STRIDE
    n_cmp_rows = _round_up(max(pitched(past) + T_PAD, pitched(CMP_SRC_ROWS - 1) + 1), SUBLANES)
    assert nc <= LANES and nb <= LANES and n_new == T_PAD // 2 and nb * SEL_BLOCK <= n_keys
    assert wbuf % LANES == 0 and PAGE_SIZE == LANES

    bias_cmp = _bias_table_sample(rel_bias, past - (CMP_LEN - 1), CMP_STRIDE, LANES)
    bias_sel = _bias_table_sample(rel_bias, past, 1, n_keys)
    bias_win = _bias_table_sample(rel_bias, wbuf, 1, n_win)
    cover = _cover_matrix(nc, nb)
    emat = _expand_matrix(n_keys)

    sps = NSA_SEQ_PER_STEP
    assert nseq % sps == 0

    def new_spec(width, col):
        return pl.BlockSpec((sps, T_PAD, width), lambda i, pt: (i, 0, col))

    kv_col = C_KV // LANES
    in_specs = [pl.BlockSpec(memory_space=pl.ANY)] * N_CACHES
    in_specs += [new_spec(A_WIDTH, C_Q // A_WIDTH)]
    in_specs += [new_spec(LANES, kv_col + c) for c in range(6)]
    in_specs += [new_spec(LANES, C_GATE // LANES), new_spec(A_WIDTH, C_ZA // A_WIDTH)]
    win_spec = pl.BlockSpec((sps, LANES, wbuf), lambda i, pt: (i, 0, 0))
    in_specs += [win_spec] * 2
    consts = list(cw_k) + list(cw_v) + [bias_cmp, bias_sel, bias_win, cover, emat]
    in_specs += [_const_spec(a.shape) for a in consts]
    operands = list(caches) + [projs3] * 9 + [wk, wv] + consts

    grid_spec = pltpu.PrefetchScalarGridSpec(
        num_scalar_prefetch=1,
        grid=(nseq // sps,),
        in_specs=in_specs,
        out_specs=[pl.BlockSpec((sps, T_PAD, A_WIDTH), lambda i, pt: (i, 0, 0)), win_spec, win_spec],
        scratch_shapes=[pltpu.VMEM((sps, n_cmp_rows, LANES), F32), pltpu.VMEM((sps, n_cmp_rows, LANES), F32),
                        pltpu.VMEM((sps, 4, LANES, LANES), F32),
                        pltpu.VMEM((sps, N_ROWBLK * T_PAD, LANES), BF16),
                        pltpu.VMEM((2, sps, N_CACHES * n_pages, LANES, PAGE_SIZE), F32),
                        pltpu.SemaphoreType.DMA((2,))])
    return pl.pallas_call(
        functools.partial(_nsa_sample_group_body, n_pages=n_pages, past=past, wbuf=wbuf, nc=nc, nb=nb),
        grid_spec=grid_spec,
        out_shape=[jax.ShapeDtypeStruct((nseq, T_PAD, A_WIDTH), F32),
                   jax.ShapeDtypeStruct((nseq, LANES, wbuf), F32),
                   jax.ShapeDtypeStruct((nseq, LANES, wbuf), F32)],
        compiler_params=_cparams(1),
        name="nsa_sample",
    )(page_table, *operands)


def _hgrn_constants(rows, seg_stride, seg_len):
    idx = np.arange(rows)
    seg = idx // seg_stride
    pos = idx % seg_stride
    same = seg[:, None] == seg[None, :]
    pt, pu = pos[:, None], pos[None, :]
    mats = [same & (pu <= pt),
            same & (pu > pt) & (pu <= seg_len - 1)]
    levels = []
    m = seg_len // 2
    while m >= 2:
        levels.append(m)
        m //= 2
    for m in levels:
        mats.append(same & (pu >= (pt // m) * m) & (pu <= pt))
        mats.append(same & (pu > pt) & (pu <= (pt // m) * m + m - 1))
    wstack = np.concatenate(mats, axis=0).astype(np.float32)
    wstack = np.concatenate([wstack] * 3, axis=1)
    masks = []
    for m in levels + [1]:
        masks.append(same & (pt // (2 * m) == pu // (2 * m)) & ((pt // m) % 2 == 1) & ((pu // m) % 2 == 0))
    masks.append(idx[:, None] == idx[None, :])
    masks = np.stack(masks).astype(np.float32)
    return jnp.asarray(wstack, BF16), jnp.asarray(masks, F32), len(levels)


def _hgrn_chunk(q, f_pre, hl_ref, wst_ref, n_levels):
    rows = q.shape[0]
    hl = hl_ref[...]
    e = jnp.exp(hl - jnp.max(hl, axis=0, keepdims=True))
    lb = e[0:1] / jnp.sum(e, axis=0, keepdims=True)
    f = lb + (1.0 - lb) * _sigmoid(f_pre)
    g = jnp.log(f)
    k = 1.0 - f
    g_hi = g.astype(BF16)
    r1 = g - g_hi.astype(F32)
    g_mid = r1.astype(BF16)
    g_lo = (r1 - g_mid.astype(F32)).astype(BF16)
    sums = _nn(wst_ref[...], jnp.concatenate([g_hi, g_mid, g_lo], axis=0))
    big_g = sums[0:rows]
    qg = q * jnp.exp(big_g)
    kd = k * jnp.exp(sums[rows:2 * rows])
    q_lv, k_lv = [], []
    for li in range(n_levels):
        q_lv.append(q * jnp.exp(sums[(2 + 2 * li) * rows:(3 + 2 * li) * rows]))
        k_lv.append(k * jnp.exp(sums[(3 + 2 * li) * rows:(4 + 2 * li) * rows]))
    q_lv += [q * f, q]
    k_lv += [k, k]
    return big_g, qg, kd, q_lv, k_lv


def _hgrn_intra(q_lv, k_lv, masks_ref, hs):
    a = None
    for li in range(len(q_lv)):
        term = masks_ref[li] * _nt(q_lv[li][:, hs].astype(BF16), k_lv[li][:, hs].astype(BF16))
        a = term if a is None else a + term
    return a


def _hgrn_out(o, ng, z):
    o = o * lax.rsqrt(jnp.mean(o * o, axis=-1, keepdims=True) + EPS)
    return o * ng * _silu(z)


def _hgrn_prompt_body(q_ref, f_ref, i_ref, z_ref, hl_ref, ng_ref, wst_ref, masks_ref, o_ref, s_ref, st_ref,
                      *, n_levels, rows):
    c = pl.program_id(1)

    @pl.when(c == 0)
    def _():
        st_ref[...] = jnp.zeros(st_ref.shape, F32)

    chunks = []
    for ci in range(q_ref.shape[0] // rows):
        rs = slice(ci * rows, (ci + 1) * rows)
        v = i_ref[rs, :]
        big_g, qg, kd, q_lv, k_lv = _hgrn_chunk(q_ref[rs, :], f_ref[rs, :], hl_ref, wst_ref, n_levels)
        per_head = []
        for h in range(HG_HEADS):
            hs = slice(h * HG_DK, (h + 1) * HG_DK)
            vb = v[:, hs].astype(BF16)
            a = _hgrn_intra(q_lv, k_lv, masks_ref, hs)
            per_head.append((qg[:, hs].astype(BF16), _nn(a.astype(BF16), vb),
                             jnp.exp(big_g[rows - 1:rows, hs]), _tn(vb, kd[:, hs].astype(BF16))))
        chunks.append(per_head)

    for h in range(HG_HEADS):
        hs = slice(h * HG_DK, (h + 1) * HG_DK)
        st = st_ref[h]
        for ci, per_head in enumerate(chunks):
            rs = slice(ci * rows, (ci + 1) * rows)
            qgb, o_intra, decay, kv = per_head[h]
            o = _nt(qgb, st.astype(BF16)) + o_intra
            o_ref[rs, hs] = _hgrn_out(o, ng_ref[:, hs], z_ref[rs, hs])
            st = st * decay + kv
        st_ref[h] = st

        @pl.when(c == pl.num_programs(1) - 1)
        def _():
            s_ref[h] = st.T


HGRN_CHUNKS_PER_STEP = 8


def _hgrn_prompt(proj3, hg_lower, hg_norm_g):
    b, t, _ = proj3.shape
    chunk = math.gcd(t, HG_CHUNK)
    wst, masks, n_levels = _hgrn_constants(chunk, chunk, chunk)
    rows = chunk * math.gcd(t // chunk, HGRN_CHUNKS_PER_STEP)
    col = lambda c: pl.BlockSpec((None, rows, B_KEY), lambda i, j: (i, j, c // B_KEY))
    return pl.pallas_call(
        functools.partial(_hgrn_prompt_body, n_levels=n_levels, rows=chunk),
        grid=(b, t // rows),
        in_specs=[col(C_QB), col(C_FB), col(C_IB), col(C_ZB),
                  _const_spec(hg_lower.shape), _const_spec(hg_norm_g.shape),
                  _const_spec(wst.shape), _const_spec(masks.shape)],
        out_specs=[pl.BlockSpec((None, rows, B_WIDTH), lambda i, j: (i, j, 0)),
                   pl.BlockSpec((None, HG_HEADS, HG_DK, HG_DV), lambda i, j: (i, 0, 0, 0))],
        out_shape=[jax.ShapeDtypeStruct((b, t, B_WIDTH), F32),
                   jax.ShapeDtypeStruct((b, HG_HEADS, HG_DK, HG_DV), F32)],
        scratch_shapes=[pltpu.VMEM((HG_HEADS, HG_DV, HG_DK), F32)],
        compiler_params=_cparams(2),
        name="hgrn_prompt",
    )(proj3, proj3, proj3, proj3, hg_lower, hg_norm_g, wst, masks)


SEQ_PER_STEP = 8


def _hgrn_sample_body(q_ref, f_ref, i_ref, z_ref, s0_ref, hl_ref, ng_ref, wst_ref, masks_ref, o_ref, s_ref,
                      *, n_levels, n_new):
    v = i_ref[...]
    big_g, qg, kd, q_lv, k_lv = _hgrn_chunk(q_ref[...], f_ref[...], hl_ref, wst_ref, n_levels)
    for h in range(HG_HEADS):
        hs = slice(h * HG_DK, (h + 1) * HG_DK)
        vb = v[:, hs].astype(BF16)
        a = _hgrn_intra(q_lv, k_lv, masks_ref, hs)
        o_intra = _nn(a.astype(BF16), vb)
        for s in range(SEQ_PER_STEP):
            rs = slice(s * T_PAD, (s + 1) * T_PAD)
            s0 = s0_ref[s, h]
            o = _nn(qg[rs, hs].astype(BF16), s0.astype(BF16)) + o_intra[rs]
            o_ref[rs, hs] = _hgrn_out(o, ng_ref[:, hs], z_ref[rs, hs])
            decay = jnp.exp(big_g[rs, hs]).T[:, n_new - 1:n_new]
            s_ref[s, h] = s0 * decay + _tn(kd[rs, hs].astype(BF16), v[rs, hs].astype(BF16))


def _hgrn_sample(projs, state, hg_lower, hg_norm_g, n_new):
    nseq = state.shape[0]
    rows = SEQ_PER_STEP * T_PAD
    wst, masks, n_levels = _hgrn_constants(rows, T_PAD, n_new)
    col = lambda c: pl.BlockSpec((rows, B_KEY), lambda i: (i, c // B_KEY))
    st_spec = pl.BlockSpec((SEQ_PER_STEP, HG_HEADS, HG_DK, HG_DV), lambda i: (i, 0, 0, 0))
    return pl.pallas_call(
        functools.partial(_hgrn_sample_body, n_levels=n_levels, n_new=n_new),
        grid=(nseq // SEQ_PER_STEP,),
        in_specs=[col(C_QB), col(C_FB), col(C_IB), col(C_ZB), st_spec,
                  _const_spec(hg_lower.shape), _const_spec(hg_norm_g.shape),
                  _const_spec(wst.shape), _const_spec(masks.shape)],
        out_specs=[pl.BlockSpec((rows, B_WIDTH), lambda i: (i, 0)), st_spec],
        out_shape=[jax.ShapeDtypeStruct((nseq * T_PAD, B_WIDTH), F32),
                   jax.ShapeDtypeStruct(state.shape, F32)],
        compiler_params=_cparams(1),
        name="hgrn_sample",
    )(projs, projs, projs, projs, state, hg_lower, hg_norm_g, wst, masks)


def _merge_body(oa_ref, ob_ref, ma_ref, mb_ref, x_ref, wba_ref, wbb_ref, wout_ref, fg_ref, y_ref):
    a = _nn(oa_ref[...].astype(BF16), wba_ref[...])
    b = _nn(ob_ref[...].astype(BF16), wbb_ref[...])
    y = _sigmoid(ma_ref[...]) * a + _sigmoid(mb_ref[...]) * b
    h = x_ref[...] + _nn(y.astype(BF16), wout_ref[...])
    ms = jnp.mean(h * h, axis=-1, keepdims=True)
    y_ref[...] = h * lax.rsqrt(ms + EPS) * fg_ref[...]


def _merge(o_a, o_b, proj, x2d, w_ba, w_bb, w_out, final_g):
    m, d = x2d.shape
    tm = min(m, 512)
    row = lambda w, c: pl.BlockSpec((tm, w), lambda i: (i, c))
    return pl.pallas_call(
        _merge_body,
        grid=(m // tm,),
        in_specs=[row(A_WIDTH, 0), row(B_WIDTH, 0), row(d, C_MA // d), row(d, C_MA // d + 1), row(d, 0),
                  _const_spec(w_ba.shape), _const_spec(w_bb.shape), _const_spec(w_out.shape),
                  _const_spec((1, d))],
        out_specs=row(d, 0),
        out_shape=jax.ShapeDtypeStruct((m, d), F32),
        compiler_params=_cparams(1),
        name="merge_out",
    )(o_a, o_b, proj, proj, x2d, w_ba, w_bb, w_out, final_g.reshape(1, d))


def kernel(x_prompt, x_sample, cache_cmp_k, cache_cmp_v, cache_sel_k, cache_sel_v, state_win_k, state_win_v,
           state_hgrn, page_table, norm_g, w_in, cmp_pos_k, cmp_w1_k, cmp_w2_k, cmp_pos_v, cmp_w1_v, cmp_w2_v,
           rel_bias, hg_lower, hg_norm_g, w_branch_a, w_branch_b, w_out, final_g):
    depth = norm_g.shape[0]
    assert depth == 1, "single-layer trunk"
    b, t, d = x_prompt.shape
    nseq, n_new, _ = x_sample.shape
    assert C_MA % d == 0

    w = _arrange_w_in(w_in[0])
    n_cols = w.shape[1]
    w_ba = _permute_heads_cols(w_branch_a[0].T).T.astype(BF16)
    w_bb = w_branch_b[0].astype(BF16)
    w_o = w_out[0].astype(BF16)
    cw_k = _compress_weights(cmp_pos_k[0], cmp_w1_k[0], cmp_w2_k[0])
    cw_v = _compress_weights(cmp_pos_v[0], cmp_w1_v[0], cmp_w2_v[0])
    ng = hg_norm_g[0].reshape(1, B_WIDTH)
    kv_cols = lambda a, c: a[..., C_KV + c * A_KV:C_KV + (c + 1) * A_KV]

    xp2 = x_prompt.reshape(b * t, d)
    proj = _project(xp2, norm_g[0], w)
    proj3 = proj.reshape(b, t, n_cols)
    nc_p = (t - CMP_LEN) // CMP_STRIDE + 1
    kcmp, vcmp = _compress_prompt(proj3, cw_k, cw_v)
    bias_cmp = _bias_table(rel_bias, base=-(CMP_LEN - 1), step=TQ, row_stride=-CMP_STRIDE, col_stride=1,
                           rows=LANES, cols=TQ, steps=t // TQ)
    bias_tiles = _bias_table(rel_bias, base=0, step=TQ, row_stride=-1, col_stride=1, rows=TQ, cols=TQ,
                             steps=WINDOW // TQ + 2, lo=0, hi=WINDOW)
    o_a = _nsa_prompt(proj3, kcmp, vcmp, bias_cmp, bias_tiles)
    o_b, p_hgrn = _hgrn_prompt(proj3, hg_lower, ng)
    y_prompt = _merge(o_a.reshape(b * t, A_WIDTH), o_b.reshape(b * t, B_WIDTH), proj, xp2,
                      w_ba, w_bb, w_o, final_g).reshape(b, t, d)
    wb_p = min(WINDOW, t)
    p_kv = [kv_cols(proj3, c).reshape(1, b, t, NSA_KV, NSA_HD) for c in range(6)]
    p_state = p_kv[:4] + [p_kv[4][:, :, t - wb_p:], p_kv[5][:, :, t - wb_p:], p_hgrn[None]]

    xs_pad = jnp.pad(x_sample, ((0, 0), (0, T_PAD - n_new), (0, 0))).reshape(nseq * T_PAD, d)
    projs = _project(xs_pad, norm_g[0], w)
    projs3 = projs.reshape(nseq, T_PAD, n_cols)
    pool = cache_cmp_k.shape[1]
    chan_major = lambda a: jnp.transpose(a, (0, 2, 3, 1)).reshape(a.shape[0], A_KV, a.shape[1])
    row_major = lambda a: jnp.transpose(a.reshape(a.shape[0], NSA_KV, NSA_HD, a.shape[2]), (0, 3, 1, 2))
    caches = [chan_major(c[0]) for c in (cache_cmp_k, cache_cmp_v, cache_sel_k, cache_sel_v)]
    wbuf = state_win_k.shape[2]
    o_as, s_win_k, s_win_v = _nsa_sample(projs3, caches, chan_major(state_win_k[0]), chan_major(state_win_v[0]),
                                         page_table, cw_k, cw_v, rel_bias, n_new)
    s_win_k, s_win_v = row_major(s_win_k), row_major(s_win_v)
    o_bs, s_hgrn = _hgrn_sample(projs, state_hgrn[0], hg_lower, ng, n_new)
    y_s = _merge(o_as.reshape(nseq * T_PAD, A_WIDTH), o_bs, projs, xs_pad, w_ba, w_bb, w_o, final_g)
    y_sample = y_s.reshape(nseq, T_PAD, d)[:, :n_new]
    s_kv = [kv_cols(projs3, c)[:, :n_new].reshape(1, nseq, n_new, NSA_KV, NSA_HD) for c in range(4)]
    s_state = s_kv + [s_win_k.reshape(1, nseq, wbuf, NSA_KV, NSA_HD),
                      s_win_v.reshape(1, nseq, wbuf, NSA_KV, NSA_HD), s_hgrn[None]]

    return (y_prompt, y_sample, *p_state, *s_state)
```

```python
import functools
import math

import numpy as np
import jax
import jax.numpy as jnp
from jax import lax
from jax.experimental import pallas as pl
from jax.experimental.pallas import tpu as pltpu

F32 = jnp.float32
BF16 = jnp.bfloat16

NSA_HEADS = 8
NSA_KV = 2
NSA_HD = 64
NSA_REP = NSA_HEADS // NSA_KV
CMP_LEN = 32
CMP_STRIDE = 16
CMP_HID = 64
SEL_BLOCK = 64
SEL_TOPN = 8
WINDOW = 512
HG_HEADS = 4
HG_DK = 128
HG_DV = 128
HG_CHUNK = 64
N_BUCKETS = 32
MAX_DIST = 128
EPS = 1e-6
NEG = -1e30
FORCED = 1e9
PAGE_SIZE = 128

A_WIDTH = NSA_HEADS * NSA_HD
A_KV = NSA_KV * NSA_HD
B_WIDTH = HG_HEADS * HG_DV
B_KEY = HG_HEADS * HG_DK

LANES = 128
SUBLANES = 8
VMEM_LIMIT_BYTES = 56 * 1024 * 1024

TQ = 128
T_PAD = SUBLANES
N_ROWBLK = NSA_HEADS

C_Q = 0
C_KV = C_Q + A_WIDTH
C_GATE = C_KV + 6 * A_KV
C_ZA = C_GATE + 2 * LANES
C_QB = C_ZA + A_WIDTH
C_FB = C_QB + B_KEY
C_IB = C_FB + B_KEY
C_ZB = C_IB + B_WIDTH
C_MA = C_ZB + B_WIDTH

CMP_SRC_ROWS = (LANES - 1) * CMP_STRIDE + CMP_LEN
CMP_PITCH = CMP_STRIDE + 1


def _nn(a, b):
    return jnp.dot(a, b, preferred_element_type=F32)


def _nt(a, b):
    return lax.dot_general(a, b, (((1,), (1,)), ((), ())), preferred_element_type=F32)


def _tn(a, b):
    return lax.dot_general(a, b, (((0,), (0,)), ((), ())), preferred_element_type=F32)


def _sigmoid(x):
    return 1.0 / (1.0 + jnp.exp(-x))


def _silu(x):
    return x * _sigmoid(x)


def _split2(x):
    hi = x.astype(BF16)
    lo = (x - hi.astype(F32)).astype(BF16)
    return hi, lo


def _cparams(n_grid):
    return pltpu.CompilerParams(dimension_semantics=("arbitrary",) * n_grid,
                                vmem_limit_bytes=VMEM_LIMIT_BYTES)


def _t5_bucket(rel):
    n = jnp.maximum(rel, 0)
    exact = N_BUCKETS // 2
    nf = jnp.maximum(n, 1).astype(F32)
    scaled = jnp.log(nf / exact) / math.log(MAX_DIST / exact) * (N_BUCKETS - exact)
    large = exact + jnp.floor(scaled).astype(jnp.int32)
    large = jnp.minimum(large, N_BUCKETS - 1)
    return jnp.where(n < exact, n, large)


def _rowblk_head(rb):
    return NSA_REP * (rb % NSA_KV) + rb // NSA_KV


def _bias_body(tab_ref, o_ref, *, base, step, row_stride, col_stride, lo, hi, cols):
    sub_r = min(o_ref.shape[0], 4 * SUBLANES)
    sub_c = min(cols, LANES)
    shape = (sub_r, sub_c)
    row = lax.broadcasted_iota(jnp.int32, shape, 0)
    col = lax.broadcasted_iota(jnp.int32, shape, 1)
    for r0 in range(0, o_ref.shape[0], sub_r):
        for c0 in range(0, cols, sub_c):
            rel = base + step * pl.program_id(0) + row_stride * (row + r0) + col_stride * (col + c0)
            bucket = _t5_bucket(rel)
            accs = [jnp.zeros(shape, F32)] * N_ROWBLK
            for k in range(N_BUCKETS):
                hit = bucket == k
                accs = [jnp.where(hit, tab_ref[k, _rowblk_head(rb)], accs[rb]) for rb in range(N_ROWBLK)]
            for rb in range(N_ROWBLK):
                acc = accs[rb]
                if lo is not None:
                    acc = jnp.where((rel >= lo) & (rel < hi), acc, NEG)
                o_ref[r0:r0 + sub_r, rb * cols + c0:rb * cols + c0 + sub_c] = acc


def _bias_table(rel_bias, *, base, step, row_stride, col_stride, rows, cols, steps, lo=None, hi=None):
    return pl.pallas_call(
        functools.partial(_bias_body, base=base, step=step, row_stride=row_stride, col_stride=col_stride,
                          lo=lo, hi=hi, cols=cols),
        grid=(steps,),
        in_specs=[pl.BlockSpec(memory_space=pltpu.SMEM)],
        out_specs=pl.BlockSpec((None, rows, N_ROWBLK * cols), lambda i: (i, 0, 0)),
        out_shape=jax.ShapeDtypeStruct((steps, rows, N_ROWBLK * cols), F32),
        compiler_params=_cparams(1),
        name="bias_table",
    )(rel_bias)


def _bias_table_sample(rel_bias, base, col_stride, cols):
    t = _bias_table(rel_bias, base=base, step=0, row_stride=1, col_stride=-col_stride, rows=T_PAD, cols=cols, steps=1)
    return t.reshape(T_PAD, N_ROWBLK, cols).swapaxes(0, 1)


N_KV_STREAMS = 6


def _proj_body(x_ref, g_ref, w_ref, o_ref, *kvt_ref):
    x = x_ref[...]
    ms = jnp.mean(x * x, axis=-1, keepdims=True)
    xn = (x * lax.rsqrt(ms + EPS) * g_ref[...]).astype(BF16)
    o = _nn(xn, w_ref[...])
    o_ref[...] = o
    if kvt_ref:
        for c in range(N_KV_STREAMS):
            kvt_ref[0][c] = o[:, C_KV + c * A_KV:C_KV + (c + 1) * A_KV].T


PROJ_ROWS = 512


def _project(x2d, norm_g, w, seq_len=None):
    m, d = x2d.shape
    n = w.shape[1]
    tm = min(m, PROJ_ROWS)
    out_specs = [pl.BlockSpec((tm, n), lambda i: (i, 0))]
    out_shape = [jax.ShapeDtypeStruct((m, n), F32)]
    if seq_len is not None:
        per_seq = seq_len // tm
        assert seq_len % tm == 0
        out_specs.append(pl.BlockSpec((None, N_KV_STREAMS, A_KV, tm), lambda i: (i // per_seq, 0, 0, i % per_seq)))
        out_shape.append(jax.ShapeDtypeStruct((m // seq_len, N_KV_STREAMS, A_KV, seq_len), F32))
    outs = pl.pallas_call(
        _proj_body,
        grid=(m // tm,),
        in_specs=[pl.BlockSpec((tm, d), lambda i: (i, 0)),
                  pl.BlockSpec((1, d), lambda i: (0, 0)),
                  pl.BlockSpec((d, n), lambda i: (0, 0), pipeline_mode=pl.Buffered(1))],
        out_specs=out_specs,
        out_shape=out_shape,
        compiler_params=_cparams(1),
        name="in_proj",
    )(x2d, norm_g.reshape(1, d), w)
    return outs if seq_len is not None else outs[0]


def _permute_heads_cols(w):
    lead = w.shape[:-1]
    return w.reshape(lead + (NSA_KV, NSA_REP, NSA_HD)).swapaxes(-3, -2).reshape(lead + (A_WIDTH,))


def _arrange_w_in(w_in):
    d = w_in.shape[0]
    sizes = (A_WIDTH, 6 * A_KV, 3 * NSA_HEADS, A_WIDTH, B_KEY, B_KEY, B_WIDTH, B_WIDTH, d, d)
    offs = np.concatenate([[0], np.cumsum(sizes)])
    q_a, kv_a, g_a, z_a, q_b, f_b, i_b, z_b, m_a, m_b = [w_in[:, offs[i]:offs[i + 1]] for i in range(10)]
    gate_pad = jnp.zeros((d, 2 * LANES - 3 * NSA_HEADS), w_in.dtype)
    w = jnp.concatenate([_permute_heads_cols(q_a), kv_a, g_a, gate_pad, _permute_heads_cols(z_a),
                         q_b, f_b, i_b, z_b, m_a, m_b], axis=1)
    return w.astype(BF16)


def _compress(streams, pitch):
    accs = [jnp.zeros((len(srcs) * LANES, LANES), F32) for srcs, _, _, _ in streams]
    for j in range(CMP_LEN):
        start = (j // CMP_STRIDE) * pitch + j % CMP_STRIDE
        for n, (srcs, pos_ref, w1_ref, _) in enumerate(streams):
            xj = jnp.concatenate([r[pl.ds(start, LANES, stride=pitch), :] for r in srcs], axis=0) + pos_ref[j:j + 1, :]
            accs[n] = accs[n] + _nn(xj.astype(BF16), w1_ref[j])
    return [_nn(_silu(acc).astype(BF16), w2_ref[...]) for acc, (_, _, _, w2_ref) in zip(accs, streams)]


def _compress_prompt_body(kc_ref, vc_ref, posk_ref, w1k_ref, w2k_ref, posv_ref, w1v_ref, w2v_ref,
                          ko_ref, vo_ref, kbuf_ref, vbuf_ref):
    t = kc_ref.shape[0]
    for buf_ref, src_ref in ((kbuf_ref, kc_ref), (vbuf_ref, vc_ref)):
        buf_ref[t:, :] = jnp.zeros((buf_ref.shape[0] - t, LANES), F32)
        buf_ref[:t, :] = src_ref[...]
    ko_ref[...], vo_ref[...] = _compress([([kbuf_ref], posk_ref, w1k_ref, w2k_ref),
                                          ([vbuf_ref], posv_ref, w1v_ref, w2v_ref)], CMP_STRIDE)


def _block_diag2(w):
    z = jnp.zeros_like(w)
    return jnp.concatenate([jnp.concatenate([w, z], axis=-1), jnp.concatenate([z, w], axis=-1)], axis=-2)


def _compress_weights(pos, w1, w2):
    pos2 = jnp.concatenate([pos, pos], axis=-1)
    w1bd = _block_diag2(w1.reshape(CMP_LEN, NSA_HD, CMP_HID)).astype(BF16)
    w2bd = _block_diag2(w2).astype(BF16)
    return pos2, w1bd, w2bd


def _const_spec(shape, single_buffer=False):
    nd = len(shape)
    if single_buffer:
        return pl.BlockSpec(shape, lambda *_: (0,) * nd, pipeline_mode=pl.Buffered(1))
    return pl.BlockSpec(shape, lambda *_: (0,) * nd)


def _compress_prompt(proj3, cw_k, cw_v):
    b, t, _ = proj3.shape
    col = C_KV // LANES
    w_specs = [_const_spec(a.shape) for a in cw_k + cw_v]
    return pl.pallas_call(
        _compress_prompt_body,
        grid=(b,),
        in_specs=[pl.BlockSpec((None, t, LANES), lambda i: (i, 0, col)),
                  pl.BlockSpec((None, t, LANES), lambda i: (i, 0, col + 1))] + w_specs,
        out_specs=[pl.BlockSpec((None, LANES, LANES), lambda i: (i, 0, 0))] * 2,
        out_shape=[jax.ShapeDtypeStruct((b, LANES, LANES), F32)] * 2,
        scratch_shapes=[pltpu.VMEM((max(t + CMP_LEN, CMP_SRC_ROWS), LANES), F32)] * 2,
        compiler_params=_cparams(1),
        name="compress_prompt",
    )(proj3, proj3, *cw_k, *cw_v)


def _select_blocks(imp, qpos, nb, axis):
    j = lax.broadcasted_iota(jnp.int32, imp.shape, axis)
    cur = qpos // SEL_BLOCK
    valid = j * SEL_BLOCK <= qpos
    forced = (j == 0) | (j == cur) | (j == cur - 1)
    score = jnp.where(valid, jnp.where(forced, FORCED, imp), -1.0)
    score = jnp.where(j < nb, score, -2.0)
    rank = jnp.zeros(imp.shape, F32)
    for i in range(nb):
        s_i = lax.slice_in_dim(score, i, i + 1, axis=axis)
        tie = jnp.where(i < j, 1.0, 0.0)
        rank = rank + jnp.where(s_i > score, 1.0, jnp.where(s_i == score, tie, 0.0))
    return jnp.where((rank < min(SEL_TOPN, nb)) & (j < nb), 1.0, 0.0)


def _masked_softmax(s, mask):
    s = jnp.where(mask, s, NEG)
    m = jnp.max(s, axis=-1, keepdims=True)
    e = jnp.exp(s - m)
    p = e / jnp.sum(e, axis=-1, keepdims=True)
    return jnp.where(mask, p, 0.0)


def _stack_queries(q_ref, qbd_ref, rows):
    lane = lax.broadcasted_iota(jnp.int32, (rows, LANES), 1)
    for r in range(NSA_REP):
        blk = q_ref[:, r * LANES:(r + 1) * LANES] * (NSA_HD ** -0.5)
        qbd_ref[(2 * r) * rows:(2 * r + 1) * rows, :] = jnp.where(lane < NSA_HD, blk, 0.0).astype(BF16)
        qbd_ref[(2 * r + 1) * rows:(2 * r + 2) * rows, :] = jnp.where(lane >= NSA_HD, blk, 0.0).astype(BF16)


def _gate_and_store(gate_ref, za_ref, o_ref, o_cmp, o_sel, o_win, rows):
    gs = _sigmoid(gate_ref[...])
    lane = lax.broadcasted_iota(jnp.int32, (rows, LANES), 1)
    for r in range(NSA_REP):
        parts = []
        for g in range(NSA_KV):
            rb = 2 * r + g
            h = _rowblk_head(rb)
            o = (gs[:, h:h + 1] * o_cmp[rb] + gs[:, NSA_HEADS + h:NSA_HEADS + h + 1] * o_sel[rb]
                 + gs[:, 2 * NSA_HEADS + h:2 * NSA_HEADS + h + 1] * o_win[rb])
            parts.append(o)
        blk = jnp.where(lane < NSA_HD, parts[0], parts[1])
        o_ref[:, r * LANES:(r + 1) * LANES] = blk * _silu(za_ref[:, r * LANES:(r + 1) * LANES])


N_COL = N_ROWBLK * TQ
MASK_COLS = NSA_KV * TQ
COL_CHUNK = TQ
SEL_TILES_PER_ITER = 2


def _nsa_prompt_body(q_ref, gate_ref, za_ref, ks_ref, vs_ref, kw_ref, vw_ref, kcmp_ref, vcmp_ref,
                     bcmp_ref, btile_ref, covert_ref, o_ref,
                     qt_ref, ksb_ref, vst_ref, kwb_ref, vwt_ref, vcmpt_ref, mask_ref, s_ref, m_ref, l_ref, acc_ref, res_ref,
                     *, nc, nb):
    qb = pl.program_id(1)
    nkt = ksb_ref.shape[0]
    sub = lax.broadcasted_iota(jnp.int32, (LANES, TQ), 0)
    tok = lax.broadcasted_iota(jnp.int32, (LANES, TQ), 1)

    @pl.when(qb == 0)
    def _():
        for kt in range(nkt):
            rows = slice(kt * TQ, (kt + 1) * TQ)
            ksb_ref[kt] = ks_ref[rows, :].astype(BF16)
            kwb_ref[kt] = kw_ref[rows, :].astype(BF16)
            vst_ref[kt] = vs_ref[rows, :].T.astype(BF16)
            vwt_ref[kt] = vw_ref[rows, :].T.astype(BF16)
        vcmpt_ref[...] = vcmp_ref[...].T.astype(BF16)

    for r in range(NSA_REP):
        qt = (q_ref[:, r * LANES:(r + 1) * LANES] * (NSA_HD ** -0.5)).T
        qt_ref[:, (2 * r) * TQ:(2 * r + 1) * TQ] = jnp.where(sub < NSA_HD, qt, 0.0).astype(BF16)
        qt_ref[:, (2 * r + 1) * TQ:(2 * r + 2) * TQ] = jnp.where(sub >= NSA_HD, qt, 0.0).astype(BF16)

    valid = ((CMP_STRIDE * sub + CMP_LEN - 1) <= qb * TQ + tok) & (sub < nc)
    kcb = kcmp_ref[...].astype(BF16)
    psum = [None] * NSA_KV
    for rb in range(N_ROWBLK):
        cols = slice(rb * TQ, (rb + 1) * TQ)
        s = jnp.where(valid, _nn(kcb, qt_ref[:, cols]) + bcmp_ref[:, cols], NEG)
        e = jnp.exp(s - jnp.max(s, axis=0, keepdims=True))
        p = jnp.where(valid, e / jnp.sum(e, axis=0, keepdims=True), 0.0)
        res_ref[0, :, cols] = _nn(vcmpt_ref[...], p.astype(BF16))
        g = rb % NSA_KV
        psum[g] = p if psum[g] is None else psum[g] + p

    p_hi, p_lo = _split2(jnp.concatenate(psum, axis=1))
    imp = _nn(covert_ref[...], p_hi) + _nn(covert_ref[...], p_lo)
    nbp = _round_up(nb, SUBLANES)
    tok2 = lax.broadcasted_iota(jnp.int32, (1, MASK_COLS), 1) % TQ
    sel = _select_blocks(imp[:nbp], qb * TQ + tok2, nb, 0)
    upper = lax.broadcasted_iota(jnp.int32, (TQ, MASK_COLS), 0) < SEL_BLOCK
    for kt in range(nkt):
        picked = jnp.where(upper, sel[2 * kt:2 * kt + 1, :], sel[2 * kt + 1:2 * kt + 2, :]) > 0.5
        mask_ref[kt] = jnp.where(kt <= qb, jnp.where(picked, 0.0, NEG), NEG)

    m_ref[...] = jnp.full(m_ref.shape, NEG, F32)
    l_ref[...] = jnp.zeros(l_ref.shape, F32)
    acc_ref[...] = jnp.zeros(acc_ref.shape, F32)

    def sel_scores(pair, buf):
        for u in range(SEL_TILES_PER_ITER):
            kt = jnp.minimum(pair * SEL_TILES_PER_ITER + u, nkt - 1)
            k_t = ksb_ref[kt]
            bidx = jnp.clip(qb - kt, 0, 2)
            for c in range(N_COL // COL_CHUNK):
                cols = slice(c * COL_CHUNK, (c + 1) * COL_CHUNK)
                mcols = slice(c * COL_CHUNK % MASK_COLS, c * COL_CHUNK % MASK_COLS + COL_CHUNK)
                s_ref[buf, u, :, cols] = (_nn(k_t, qt_ref[:, cols]) + btile_ref[bidx, :, cols]
                                          + mask_ref[kt, :, mcols])

    def sel_update(pair, buf):
        v_ts = [vst_ref[pair * SEL_TILES_PER_ITER + u] for u in range(SEL_TILES_PER_ITER)]
        for c in range(N_COL // COL_CHUNK):
            cols = slice(c * COL_CHUNK, (c + 1) * COL_CHUNK)
            ss = [s_ref[buf, u, :, cols] for u in range(SEL_TILES_PER_ITER)]
            m_old = m_ref[:, cols]
            m_new = m_old
            for s in ss:
                m_new = jnp.maximum(m_new, jnp.max(s, axis=0, keepdims=True))
            alpha = jnp.exp(m_old - m_new)
            l_new = alpha * l_ref[:, cols]
            acc = alpha * acc_ref[:, cols]
            for s, v_t in zip(ss, v_ts):
                p = jnp.exp(s - m_new)
                l_new = l_new + jnp.sum(p, axis=0, keepdims=True)
                acc = acc + _nn(v_t, p.astype(BF16))
            l_ref[:, cols] = l_new
            acc_ref[:, cols] = acc
            m_ref[:, cols] = m_new

    def sel_body(it, carry):
        sel_scores(2 * it + 1, 1)
        sel_update(2 * it, 0)
        sel_scores(2 * it + 2, 0)
        sel_update(2 * it + 1, 1)
        return carry

    sel_scores(0, 0)
    lax.fori_loop(0, qb // (2 * SEL_TILES_PER_ITER) + 1, sel_body, 0)
    res_ref[1] = acc_ref[...] / l_ref[...]

    n_wt = WINDOW // TQ + 1
    for c in range(N_COL // COL_CHUNK):
        cols = slice(c * COL_CHUNK, (c + 1) * COL_CHUNK)
        qc = qt_ref[:, cols]
        ss = []
        for dist in range(n_wt):
            kt = jnp.maximum(qb - dist, 0)
            bidx = jnp.where(qb >= dist, dist, n_wt)
            ss.append(_nn(kwb_ref[kt], qc) + btile_ref[bidx, :, cols])
        m = jnp.max(ss[0], axis=0, keepdims=True)
        for s in ss[1:]:
            m = jnp.maximum(m, jnp.max(s, axis=0, keepdims=True))
        l_w = None
        acc = None
        for dist, s in enumerate(ss):
            p = jnp.exp(s - m)
            ps = jnp.sum(p, axis=0, keepdims=True)
            pv = _nn(vwt_ref[jnp.maximum(qb - dist, 0)], p.astype(BF16))
            l_w = ps if l_w is None else l_w + ps
            acc = pv if acc is None else acc + pv
        res_ref[2, :, cols] = acc / l_w

    gt = _sigmoid(gate_ref[...]).T
    for r in range(NSA_REP):
        parts = []
        for g in range(NSA_KV):
            rb = 2 * r + g
            h = _rowblk_head(rb)
            cols = slice(rb * TQ, (rb + 1) * TQ)
            parts.append(gt[h:h + 1, :] * res_ref[0, :, cols]
                         + gt[NSA_HEADS + h:NSA_HEADS + h + 1, :] * res_ref[1, :, cols]
                         + gt[2 * NSA_HEADS + h:2 * NSA_HEADS + h + 1, :] * res_ref[2, :, cols])
        blk = jnp.where(sub < NSA_HD, parts[0], parts[1]).T
        o_ref[:, r * LANES:(r + 1) * LANES] = (blk * _silu(za_ref[:, r * LANES:(r + 1) * LANES])).astype(o_ref.dtype)


def _cover_matrix(nc, nb):
    cs = np.arange(nc) * CMP_STRIDE
    ce = cs + CMP_LEN
    bs = np.arange(nb) * SEL_BLOCK
    be = bs + SEL_BLOCK
    ov = np.clip(np.minimum(ce[:, None], be[None, :]) - np.maximum(cs[:, None], bs[None, :]), 0, None) / CMP_STRIDE
    out = np.zeros((LANES, LANES), np.float32)
    out[:nc, :nb] = ov
    return jnp.asarray(out, BF16)


def _expand_matrix(nkeys):
    j = np.arange(LANES)[:, None]
    k = np.arange(nkeys)[None, :]
    return jnp.asarray((k // SEL_BLOCK == j).astype(np.float32), BF16)


def _nsa_prompt(proj3, kcmp, vcmp, bias_cmp, bias_tiles):
    b, t, _ = proj3.shape
    nq = t // TQ
    nc = (t - CMP_LEN) // CMP_STRIDE + 1
    nb = t // SEL_BLOCK
    assert nc <= LANES and nb <= LANES and t % (TQ * 2 * SEL_TILES_PER_ITER) == 0
    kv_col = C_KV // LANES
    kv_spec = lambda c: pl.BlockSpec((None, t, LANES), lambda i, j: (i, 0, kv_col + c))
    cover_t = _cover_matrix(nc, nb).T
    tile_bf16 = pltpu.VMEM((nq, TQ, LANES), BF16)
    return pl.pallas_call(
        functools.partial(_nsa_prompt_body, nc=nc, nb=nb),
        grid=(b, nq),
        in_specs=[pl.BlockSpec((None, TQ, A_WIDTH), lambda i, j: (i, j, C_Q // A_WIDTH)),
                  pl.BlockSpec((None, TQ, LANES), lambda i, j: (i, j, C_GATE // LANES)),
                  pl.BlockSpec((None, TQ, A_WIDTH), lambda i, j: (i, j, C_ZA // A_WIDTH)),
                  kv_spec(2), kv_spec(3), kv_spec(4), kv_spec(5),
                  pl.BlockSpec((None, LANES, LANES), lambda i, j: (i, 0, 0)),
                  pl.BlockSpec((None, LANES, LANES), lambda i, j: (i, 0, 0)),
                  pl.BlockSpec((None, LANES, N_COL), lambda i, j: (j, 0, 0)),
                  _const_spec(bias_tiles.shape), _const_spec(cover_t.shape)],
        out_specs=pl.BlockSpec((None, TQ, A_WIDTH), lambda i, j: (i, j, 0)),
        out_shape=jax.ShapeDtypeStruct((b, t, A_WIDTH), BF16),
        scratch_shapes=[pltpu.VMEM((LANES, N_COL), BF16),
                        tile_bf16, tile_bf16, tile_bf16, tile_bf16,
                        pltpu.VMEM((LANES, LANES), BF16),
                        pltpu.VMEM((nq, TQ, MASK_COLS), F32),
                        pltpu.VMEM((2, SEL_TILES_PER_ITER, TQ, N_COL), F32),
                        pltpu.VMEM((1, N_COL), F32),
                        pltpu.VMEM((1, N_COL), F32),
                        pltpu.VMEM((LANES, N_COL), F32),
                        pltpu.VMEM((3, LANES, N_COL), F32)],
        compiler_params=_cparams(2),
        name="nsa_prompt",
    )(proj3, proj3, proj3, proj3, proj3, proj3, proj3, kcmp, vcmp, bias_cmp, bias_tiles, cover_t)


N_CACHES = 4


NSA_SEQ_PER_STEP = 4


def _group_page_copy(cache_refs, pt_ref, pbuf, sems, step, slot, s, c, pg, n_pages):
    seq = step * NSA_SEQ_PER_STEP + s
    return pltpu.make_async_copy(cache_refs[c].at[pt_ref[seq, pg]], pbuf.at[slot, s, c * n_pages + pg],
                                 sems.at[slot])


def _nsa_sample_group_body(pt_ref, *refs, n_pages, **static):
    cache_refs = refs[:N_CACHES]
    per_seq = refs[N_CACHES:N_CACHES + 11]
    consts = refs[N_CACHES + 11:N_CACHES + 22]
    outs = refs[N_CACHES + 22:N_CACHES + 25]
    scratch = refs[N_CACHES + 25:-2]
    pbuf, sems = refs[-2:]
    i = pl.program_id(0)
    slot = i % 2

    def all_pages(step, slot_):
        return [_group_page_copy(cache_refs, pt_ref, pbuf, sems, step, slot_, s, c, pg, n_pages)
                for s in range(NSA_SEQ_PER_STEP) for c in range(N_CACHES) for pg in range(n_pages)]

    @pl.when(i == 0)
    def _():
        for cp in all_pages(0, 0):
            cp.start()

    @pl.when(i + 1 < pl.num_programs(0))
    def _():
        for cp in all_pages(i + 1, 1 - slot):
            cp.start()

    for cp in all_pages(i, slot):
        cp.wait()
    pages = [[[pbuf.at[slot, s, c * n_pages + pg] for pg in range(n_pages)] for c in range(N_CACHES)]
             for s in range(NSA_SEQ_PER_STEP)]
    _nsa_sample_group(pages, per_seq, consts, outs, scratch, n_pages=n_pages, **static)


def _nsa_sample_group(pages, per_seq, consts, outs, scratch, *, n_pages, past, wbuf, nc, nb):
    q_ref, kc_ref, vc_ref, ks_ref, vs_ref, kw_ref, vw_ref, gate_ref, za_ref, wk_ref, wv_ref = per_seq
    (posk_ref, w1k_ref, w2k_ref, posv_ref, w1v_ref, w2v_ref, bcmp_ref, bsel_ref, bwin_ref,
     cover_ref, emat_ref) = consts
    o_ref, swk_ref, swv_ref = outs
    kc_all, vc_all, newt_ref, qbd_ref = scratch
    nseq = NSA_SEQ_PER_STEP
    seqs = range(nseq)
    n_keys = bsel_ref.shape[2]
    n_new = T_PAD // 2
    chunks_per_page = PAGE_SIZE // CMP_STRIDE
    nrow = N_ROWBLK * T_PAD

    for dst, cache, new_ref in ((kc_all, 0, kc_ref), (vc_all, 1, vc_ref)):
        for s in seqs:
            for pg in range(n_pages):
                rows = pages[s][cache][pg][...].T
                for c in range(chunks_per_page):
                    r0 = (pg * chunks_per_page + c) * CMP_PITCH
                    dst[s, r0:r0 + CMP_STRIDE, :] = rows[c * CMP_STRIDE:(c + 1) * CMP_STRIDE, :]
            r0 = (past // CMP_STRIDE) * CMP_PITCH
            dst[s, r0:r0 + T_PAD, :] = new_ref[s]
            dst[s, r0 + T_PAD:, :] = jnp.zeros((dst.shape[1] - r0 - T_PAD, LANES), F32)
    kcmp, vcmp = [c.astype(BF16) for c in _compress(
        [([kc_all.at[s] for s in seqs], posk_ref, w1k_ref, w2k_ref),
         ([vc_all.at[s] for s in seqs], posv_ref, w1v_ref, w2v_ref)], CMP_PITCH)]
    per_s = lambda x, s: x[s * LANES:(s + 1) * LANES]

    newt_ref[...] = jnp.zeros(newt_ref.shape, F32)
    for s in seqs:
        for idx, new_ref in enumerate((ks_ref, vs_ref, kw_ref, vw_ref)):
            newt_ref[s, idx, 0:T_PAD, :] = new_ref[s]
        _stack_queries(q_ref.at[s], qbd_ref.at[s], T_PAD)
    qbd = [qbd_ref[s] for s in seqs]
    tok = lax.broadcasted_iota(jnp.int32, (T_PAD, LANES), 0)
    lane = lax.broadcasted_iota(jnp.int32, (T_PAD, LANES), 1)
    qpos = past + tok
    stack = lambda xs: jnp.concatenate(xs, axis=0)
    by_head = lambda x: x.reshape(nseq, NSA_REP, NSA_KV, T_PAD, LANES)

    s_c = by_head(stack([_nt(qbd[s], per_s(kcmp, s)) for s in seqs])) + _by_head_bias(bcmp_ref[...])
    mask_c = ((CMP_STRIDE * lane + CMP_LEN - 1) <= qpos) & (lane < nc)
    p = _masked_softmax(s_c, mask_c[None, None, None]).reshape(nseq, nrow, LANES)
    o_cmp = [_nn(p[s].astype(BF16), per_s(vcmp, s)).reshape(N_ROWBLK, T_PAD, LANES) for s in seqs]

    psum = p.reshape(nseq, NSA_REP, NSA_KV * T_PAD, LANES).sum(axis=1).reshape(nseq * NSA_KV * T_PAD, LANES)
    p_hi, p_lo = _split2(psum)
    imp = _nn(p_hi, cover_ref[...]) + _nn(p_lo, cover_ref[...])
    sel = _select_blocks(imp.reshape(nseq, NSA_KV, T_PAD, LANES), qpos[None, None], nb, 3)
    selk = _nn(sel.reshape(nseq * NSA_KV * T_PAD, LANES).astype(BF16), emat_ref[...])
    selk = selk.reshape(nseq, NSA_KV, T_PAD, n_keys)

    def softmax_pv(scores, values):
        m = scores[0]
        for s_t in scores[1:]:
            m = jnp.maximum(m, s_t)
        m = jnp.max(m, axis=-1, keepdims=True)
        acc = [None] * nseq
        tot = None
        for s_t, (v_ts, channel_major) in zip(scores, values):
            p_t = jnp.exp(s_t - m)
            pb = p_t.reshape(nseq, nrow, LANES).astype(BF16)
            for s in seqs:
                pv = _nt(pb[s], v_ts[s]) if channel_major else _nn(pb[s], v_ts[s])
                acc[s] = pv if acc[s] is None else acc[s] + pv
            tot = p_t if tot is None else tot + p_t
        inv = 1.0 / jnp.sum(tot, axis=-1, keepdims=True).reshape(nseq, nrow, 1)
        return [(acc[s] * inv[s]).reshape(N_ROWBLK, T_PAD, LANES) for s in seqs]

    scores, values = [], []
    for t in range(n_pages + 1):
        cols = slice(t * LANES, (t + 1) * LANES)
        allowed = ((t * LANES + lane) <= qpos)[None, None] & (selk[:, :, :, cols] > 0.5)
        madd = jnp.where(allowed, 0.0, NEG)[:, None]
        if t < n_pages:
            raw = [_nn(qbd[s], pages[s][2][t][...].astype(BF16)) for s in seqs]
            values.append(([pages[s][3][t][...].astype(BF16) for s in seqs], True))
        else:
            raw = [_nt(qbd[s], newt_ref[s, 0].astype(BF16)) for s in seqs]
            values.append(([newt_ref[s, 1].astype(BF16) for s in seqs], False))
        scores.append(by_head(stack(raw)) + _by_head_bias(bsel_ref[:, :, cols]) + madd)
    o_sel = softmax_pv(scores, values)

    scores, values = [], []
    kwt = [wk_ref[s].astype(BF16) for s in seqs]
    vwt = [wv_ref[s].astype(BF16) for s in seqs]
    for t in range(wbuf // LANES + 1):
        cols = slice(t * LANES, (t + 1) * LANES)
        rel_w = wbuf + tok - (t * LANES + lane)
        madd = jnp.where((rel_w >= 0) & (rel_w < WINDOW), 0.0, NEG)[None, None, None]
        if t < wbuf // LANES:
            raw = [_nn(qbd[s], kwt[s][:, cols]) for s in seqs]
            values.append(([vwt[s][:, cols] for s in seqs], True))
        else:
            raw = [_nt(qbd[s], newt_ref[s, 2].astype(BF16)) for s in seqs]
            values.append(([newt_ref[s, 3].astype(BF16) for s in seqs], False))
        scores.append(by_head(stack(raw)) + _by_head_bias(bwin_ref[:, :, cols]) + madd)
    o_win = softmax_pv(scores, values)

    lane_full = lax.broadcasted_iota(jnp.int32, (LANES, LANES), 1)
    for s in seqs:
        _gate_and_store(gate_ref.at[s], za_ref.at[s], o_ref.at[s], o_cmp[s], o_sel[s], o_win[s], T_PAD)
        for idx, w_ref, out_ref in ((2, wk_ref, swk_ref), (3, wv_ref, swv_ref)):
            rolled = pltpu.roll(w_ref[s], wbuf - n_new, axis=1)
            tail = pltpu.roll(newt_ref[s, idx], LANES - n_new, axis=0).T
            out_ref[s, :, :wbuf - LANES] = rolled[:, :wbuf - LANES]
            out_ref[s, :, wbuf - LANES:] = jnp.where(lane_full >= LANES - n_new, tail, rolled[:, wbuf - LANES:])


def _by_head_bias(b):
    return b.reshape(1, NSA_REP, NSA_KV, T_PAD, b.shape[-1])


def _round_up(x, m):
    return (x + m - 1) // m * m


def _nsa_sample(projs3, caches, wk, wv, page_table, cw_k, cw_v, rel_bias, n_new):
    nseq, n_pages = page_table.shape
    past = n_pages * PAGE_SIZE
    wbuf = wk.shape[2]
    length = past + n_new
    nc = (length - CMP_LEN) // CMP_STRIDE + 1
    nb = -(-length // SEL_BLOCK)
    n_keys = past + LANES
    n_win = wbuf + LANES
    pitched = lambda p: (p // CMP_STRIDE) * CMP_PITCH + p % CMP_STRIDE
    n_cmp_rows = _round_up(max(pitched(past) + T_PAD, pitched(CMP_SRC_ROWS - 1) + 1), SUBLANES)
    assert nc <= LANES and nb <= LANES and n_new == T_PAD // 2 and nb * SEL_BLOCK <= n_keys
    assert wbuf % LANES == 0 and PAGE_SIZE == LANES

    bias_cmp = _bias_table_sample(rel_bias, past - (CMP_LEN - 1), CMP_STRIDE, LANES)
    bias_sel = _bias_table_sample(rel_bias, past, 1, n_keys)
    bias_win = _bias_table_sample(rel_bias, wbuf, 1, n_win)
    cover = _cover_matrix(nc, nb)
    emat = _expand_matrix(n_keys)

    sps = NSA_SEQ_PER_STEP
    assert nseq % sps == 0

    def new_spec(width, col):
        return pl.BlockSpec((sps, T_PAD, width), lambda i, pt: (i, 0, col))

    kv_col = C_KV // LANES
    in_specs = [pl.BlockSpec(memory_space=pl.ANY)] * N_CACHES
    in_specs += [new_spec(A_WIDTH, C_Q // A_WIDTH)]
    in_specs += [new_spec(LANES, kv_col + c) for c in range(6)]
    in_specs += [new_spec(LANES, C_GATE // LANES), new_spec(A_WIDTH, C_ZA // A_WIDTH)]
    win_spec = pl.BlockSpec((sps, LANES, wbuf), lambda i, pt: (i, 0, 0))
    in_specs += [win_spec] * 2
    consts = list(cw_k) + list(cw_v) + [bias_cmp, bias_sel, bias_win, cover, emat]
    in_specs += [_const_spec(a.shape, single_buffer=True) for a in consts]
    operands = list(caches) + [projs3] * 9 + [wk, wv] + consts

    grid_spec = pltpu.PrefetchScalarGridSpec(
        num_scalar_prefetch=1,
        grid=(nseq // sps,),
        in_specs=in_specs,
        out_specs=[pl.BlockSpec((sps, T_PAD, A_WIDTH), lambda i, pt: (i, 0, 0)), win_spec, win_spec],
        scratch_shapes=[pltpu.VMEM((sps, n_cmp_rows, LANES), F32), pltpu.VMEM((sps, n_cmp_rows, LANES), F32),
                        pltpu.VMEM((sps, 4, LANES, LANES), F32),
                        pltpu.VMEM((sps, N_ROWBLK * T_PAD, LANES), BF16),
                        pltpu.VMEM((2, sps, N_CACHES * n_pages, LANES, PAGE_SIZE), F32),
                        pltpu.SemaphoreType.DMA((2,))])
    return pl.pallas_call(
        functools.partial(_nsa_sample_group_body, n_pages=n_pages, past=past, wbuf=wbuf, nc=nc, nb=nb),
        grid_spec=grid_spec,
        out_shape=[jax.ShapeDtypeStruct((nseq, T_PAD, A_WIDTH), F32),
                   jax.ShapeDtypeStruct((nseq, LANES, wbuf), F32),
                   jax.ShapeDtypeStruct((nseq, LANES, wbuf), F32)],
        compiler_params=_cparams(1),
        name="nsa_sample",
    )(page_table, *operands)


def _hgrn_constants(rows, seg_stride, seg_len):
    idx = np.arange(rows)
    seg = idx // seg_stride
    pos = idx % seg_stride
    same = seg[:, None] == seg[None, :]
    pt, pu = pos[:, None], pos[None, :]
    mats = [same & (pu <= pt),
            same & (pu > pt) & (pu <= seg_len - 1)]
    levels = []
    m = seg_len // 2
    while m >= 2:
        levels.append(m)
        m //= 2
    for m in levels:
        mats.append(same & (pu >= (pt // m) * m) & (pu <= pt))
        mats.append(same & (pu > pt) & (pu <= (pt // m) * m + m - 1))
    wstack = np.concatenate(mats, axis=0).astype(np.float32)
    wstack = np.concatenate([wstack] * 3, axis=1)
    masks = []
    for m in levels + [1]:
        masks.append(same & (pt // (2 * m) == pu // (2 * m)) & ((pt // m) % 2 == 1) & ((pu // m) % 2 == 0))
    masks.append(idx[:, None] == idx[None, :])
    masks = np.stack(masks).astype(np.float32)
    return jnp.asarray(wstack, BF16), jnp.asarray(masks, F32), len(levels)


def _hgrn_chunk(q, f_pre, hl_ref, wst_ref, n_levels):
    rows = q.shape[0]
    hl = hl_ref[...]
    e = jnp.exp(hl - jnp.max(hl, axis=0, keepdims=True))
    lb = e[0:1] / jnp.sum(e, axis=0, keepdims=True)
    f = lb + (1.0 - lb) * _sigmoid(f_pre)
    g = jnp.log(f)
    k = 1.0 - f
    g_hi = g.astype(BF16)
    r1 = g - g_hi.astype(F32)
    g_mid = r1.astype(BF16)
    g_lo = (r1 - g_mid.astype(F32)).astype(BF16)
    sums = _nn(wst_ref[...], jnp.concatenate([g_hi, g_mid, g_lo], axis=0))
    big_g = sums[0:rows]
    qg = q * jnp.exp(big_g)
    kd = k * jnp.exp(sums[rows:2 * rows])
    q_lv, k_lv = [], []
    for li in range(n_levels):
        q_lv.append(q * jnp.exp(sums[(2 + 2 * li) * rows:(3 + 2 * li) * rows]))
        k_lv.append(k * jnp.exp(sums[(3 + 2 * li) * rows:(4 + 2 * li) * rows]))
    q_lv += [q * f, q]
    k_lv += [k, k]
    return big_g, qg, kd, q_lv, k_lv


def _hgrn_intra(q_lv, k_lv, masks_ref, hs):
    a = None
    for li in range(len(q_lv)):
        term = masks_ref[li] * _nt(q_lv[li][:, hs].astype(BF16), k_lv[li][:, hs].astype(BF16))
        a = term if a is None else a + term
    return a


def _hgrn_out(o, ng, z):
    o = o * lax.rsqrt(jnp.mean(o * o, axis=-1, keepdims=True) + EPS)
    return o * ng * _silu(z)


def _hgrn_prompt_body(q_ref, f_ref, i_ref, z_ref, hl_ref, ng_ref, wst_ref, masks_ref, o_ref, s_ref, st_ref,
                      *, n_levels, rows):
    c = pl.program_id(1)

    @pl.when(c == 0)
    def _():
        st_ref[...] = jnp.zeros(st_ref.shape, F32)

    chunks = []
    for ci in range(q_ref.shape[0] // rows):
        rs = slice(ci * rows, (ci + 1) * rows)
        v = i_ref[rs, :]
        big_g, qg, kd, q_lv, k_lv = _hgrn_chunk(q_ref[rs, :], f_ref[rs, :], hl_ref, wst_ref, n_levels)
        per_head = []
        for h in range(HG_HEADS):
            hs = slice(h * HG_DK, (h + 1) * HG_DK)
            vb = v[:, hs].astype(BF16)
            a = _hgrn_intra(q_lv, k_lv, masks_ref, hs)
            per_head.append((qg[:, hs].astype(BF16), _nn(a.astype(BF16), vb),
                             jnp.exp(big_g[rows - 1:rows, hs]), _tn(vb, kd[:, hs].astype(BF16))))
        chunks.append(per_head)

    for h in range(HG_HEADS):
        hs = slice(h * HG_DK, (h + 1) * HG_DK)
        st = st_ref[h]
        for ci, per_head in enumerate(chunks):
            rs = slice(ci * rows, (ci + 1) * rows)
            qgb, o_intra, decay, kv = per_head[h]
            o = _nt(qgb, st.astype(BF16)) + o_intra
            o_ref[rs, hs] = _hgrn_out(o, ng_ref[:, hs], z_ref[rs, hs]).astype(o_ref.dtype)
            st = st * decay + kv
        st_ref[h] = st

        @pl.when(c == pl.num_programs(1) - 1)
        def _():
            s_ref[h] = st.T


HGRN_CHUNKS_PER_STEP = 8


def _hgrn_prompt(proj3, hg_lower, hg_norm_g):
    b, t, _ = proj3.shape
    chunk = math.gcd(t, HG_CHUNK)
    wst, masks, n_levels = _hgrn_constants(chunk, chunk, chunk)
    rows = chunk * math.gcd(t // chunk, HGRN_CHUNKS_PER_STEP)
    col = lambda c: pl.BlockSpec((None, rows, B_KEY), lambda i, j: (i, j, c // B_KEY))
    return pl.pallas_call(
        functools.partial(_hgrn_prompt_body, n_levels=n_levels, rows=chunk),
        grid=(b, t // rows),
        in_specs=[col(C_QB), col(C_FB), col(C_IB), col(C_ZB),
                  _const_spec(hg_lower.shape), _const_spec(hg_norm_g.shape),
                  _const_spec(wst.shape), _const_spec(masks.shape)],
        out_specs=[pl.BlockSpec((None, rows, B_WIDTH), lambda i, j: (i, j, 0)),
                   pl.BlockSpec((None, HG_HEADS, HG_DK, HG_DV), lambda i, j: (i, 0, 0, 0))],
        out_shape=[jax.ShapeDtypeStruct((b, t, B_WIDTH), BF16),
                   jax.ShapeDtypeStruct((b, HG_HEADS, HG_DK, HG_DV), F32)],
        scratch_shapes=[pltpu.VMEM((HG_HEADS, HG_DV, HG_DK), F32)],
        compiler_params=_cparams(2),
        name="hgrn_prompt",
    )(proj3, proj3, proj3, proj3, hg_lower, hg_norm_g, wst, masks)


SEQ_PER_STEP = 8


def _hgrn_sample_body(q_ref, f_ref, i_ref, z_ref, s0_ref, hl_ref, ng_ref, wst_ref, masks_ref, o_ref, s_ref,
                      *, n_levels, n_new):
    v = i_ref[...]
    big_g, qg, kd, q_lv, k_lv = _hgrn_chunk(q_ref[...], f_ref[...], hl_ref, wst_ref, n_levels)
    for h in range(HG_HEADS):
        hs = slice(h * HG_DK, (h + 1) * HG_DK)
        vb = v[:, hs].astype(BF16)
        a = _hgrn_intra(q_lv, k_lv, masks_ref, hs)
        o_intra = _nn(a.astype(BF16), vb)
        for s in range(SEQ_PER_STEP):
            rs = slice(s * T_PAD, (s + 1) * T_PAD)
            s0 = s0_ref[s, h]
            o = _nn(qg[rs, hs].astype(BF16), s0.astype(BF16)) + o_intra[rs]
            o_ref[rs, hs] = _hgrn_out(o, ng_ref[:, hs], z_ref[rs, hs])
            decay = jnp.exp(big_g[rs, hs]).T[:, n_new - 1:n_new]
            s_ref[s, h] = s0 * decay + _tn(kd[rs, hs].astype(BF16), v[rs, hs].astype(BF16))


def _hgrn_sample(projs, state, hg_lower, hg_norm_g, n_new):
    nseq = state.shape[0]
    rows = SEQ_PER_STEP * T_PAD
    wst, masks, n_levels = _hgrn_constants(rows, T_PAD, n_new)
    col = lambda c: pl.BlockSpec((rows, B_KEY), lambda i: (i, c // B_KEY))
    st_spec = pl.BlockSpec((SEQ_PER_STEP, HG_HEADS, HG_DK, HG_DV), lambda i: (i, 0, 0, 0))
    return pl.pallas_call(
        functools.partial(_hgrn_sample_body, n_levels=n_levels, n_new=n_new),
        grid=(nseq // SEQ_PER_STEP,),
        in_specs=[col(C_QB), col(C_FB), col(C_IB), col(C_ZB), st_spec,
                  _const_spec(hg_lower.shape), _const_spec(hg_norm_g.shape),
                  _const_spec(wst.shape), _const_spec(masks.shape)],
        out_specs=[pl.BlockSpec((rows, B_WIDTH), lambda i: (i, 0)), st_spec],
        out_shape=[jax.ShapeDtypeStruct((nseq * T_PAD, B_WIDTH), F32),
                   jax.ShapeDtypeStruct(state.shape, F32)],
        compiler_params=_cparams(1),
        name="hgrn_sample",
    )(projs, projs, projs, projs, state, hg_lower, hg_norm_g, wst, masks)


def _merge_body(oa_ref, ob_ref, ma_ref, mb_ref, x_ref, wba_ref, wbb_ref, wout_ref, fg_ref, y_ref):
    a = _nn(oa_ref[...].astype(BF16), wba_ref[...])
    b = _nn(ob_ref[...].astype(BF16), wbb_ref[...])
    y = _sigmoid(ma_ref[...]) * a + _sigmoid(mb_ref[...]) * b
    h = x_ref[...] + _nn(y.astype(BF16), wout_ref[...])
    ms = jnp.mean(h * h, axis=-1, keepdims=True)
    y_ref[...] = h * lax.rsqrt(ms + EPS) * fg_ref[...]


def _merge(o_a, o_b, proj, x2d, w_ba, w_bb, w_out, final_g):
    m, d = x2d.shape
    tm = min(m, 512)
    row = lambda w, c: pl.BlockSpec((tm, w), lambda i: (i, c))
    return pl.pallas_call(
        _merge_body,
        grid=(m // tm,),
        in_specs=[row(A_WIDTH, 0), row(B_WIDTH, 0), row(d, C_MA // d), row(d, C_MA // d + 1), row(d, 0),
                  _const_spec(w_ba.shape), _const_spec(w_bb.shape), _const_spec(w_out.shape),
                  _const_spec((1, d))],
        out_specs=row(d, 0),
        out_shape=jax.ShapeDtypeStruct((m, d), F32),
        compiler_params=_cparams(1),
        name="merge_out",
    )(o_a, o_b, proj, proj, x2d, w_ba, w_bb, w_out, final_g.reshape(1, d))


def kernel(x_prompt, x_sample, cache_cmp_k, cache_cmp_v, cache_sel_k, cache_sel_v, state_win_k, state_win_v,
           state_hgrn, page_table, norm_g, w_in, cmp_pos_k, cmp_w1_k, cmp_w2_k, cmp_pos_v, cmp_w1_v, cmp_w2_v,
           rel_bias, hg_lower, hg_norm_g, w_branch_a, w_branch_b, w_out, final_g):
    depth = norm_g.shape[0]
    assert depth == 1, "single-layer trunk"
    b, t, d = x_prompt.shape
    nseq, n_new, _ = x_sample.shape
    assert C_MA % d == 0

    w = _arrange_w_in(w_in[0])
    n_cols = w.shape[1]
    w_ba = _permute_heads_cols(w_branch_a[0].T).T.astype(BF16)
    w_bb = w_branch_b[0].astype(BF16)
    w_o = w_out[0].astype(BF16)
    cw_k = _compress_weights(cmp_pos_k[0], cmp_w1_k[0], cmp_w2_k[0])
    cw_v = _compress_weights(cmp_pos_v[0], cmp_w1_v[0], cmp_w2_v[0])
    ng = hg_norm_g[0].reshape(1, B_WIDTH)
    kv_cols = lambda a, c: a[..., C_KV + c * A_KV:C_KV + (c + 1) * A_KV]

    xp2 = x_prompt.reshape(b * t, d)
    proj, kv_t = _project(xp2, norm_g[0], w, seq_len=t)
    proj3 = proj.reshape(b, t, n_cols)
    nc_p = (t - CMP_LEN) // CMP_STRIDE + 1
    kcmp, vcmp = _compress_prompt(proj3, cw_k, cw_v)
    bias_cmp = _bias_table(rel_bias, base=-(CMP_LEN - 1), step=TQ, row_stride=-CMP_STRIDE, col_stride=1,
                           rows=LANES, cols=TQ, steps=t // TQ)
    bias_tiles = _bias_table(rel_bias, base=0, step=TQ, row_stride=-1, col_stride=1, rows=TQ, cols=TQ,
                             steps=WINDOW // TQ + 2, lo=0, hi=WINDOW)
    o_a = _nsa_prompt(proj3, kcmp, vcmp, bias_cmp, bias_tiles)
    o_b, p_hgrn = _hgrn_prompt(proj3, hg_lower, ng)
    y_prompt = _merge(o_a.reshape(b * t, A_WIDTH), o_b.reshape(b * t, B_WIDTH), proj, xp2,
                      w_ba, w_bb, w_o, final_g).reshape(b, t, d)
    wb_p = min(WINDOW, t)
    chan_major = lambda a: jnp.transpose(a, (0, 2, 3, 1)).reshape(a.shape[0], A_KV, a.shape[1])
    row_major = lambda a: jnp.transpose(a.reshape(a.shape[0], NSA_KV, NSA_HD, a.shape[2]), (0, 3, 1, 2))
    p_kv = ([row_major(kv_t[:, c])[None] for c in range(4)]
            + [row_major(kv_t[:, c, :, t - wb_p:])[None] for c in (4, 5)])
    p_state = p_kv + [p_hgrn[None]]

    xs_pad = jnp.pad(x_sample, ((0, 0), (0, T_PAD - n_new), (0, 0))).reshape(nseq * T_PAD, d)
    projs = _project(xs_pad, norm_g[0], w)
    projs3 = projs.reshape(nseq, T_PAD, n_cols)
    pool = cache_cmp_k.shape[1]
    caches =[chan_major(c[0]) for c in (cache_cmp_k, cache_cmp_v, cache_sel_k, cache_sel_v)]
    wbuf = state_win_k.shape[2]
    o_as, s_win_k, s_win_v = _nsa_sample(projs3, caches, chan_major(state_win_k[0]), chan_major(state_win_v[0]),
                                         page_table, cw_k, cw_v, rel_bias, n_new)
    s_win_k, s_win_v = row_major(s_win_k), row_major(s_win_v)
    o_bs, s_hgrn = _hgrn_sample(projs, state_hgrn[0], hg_lower, ng, n_new)
    y_s = _merge(o_as.reshape(nseq * T_PAD, A_WIDTH), o_bs, projs, xs_pad, w_ba, w_bb, w_o, final_g)
    y_sample = y_s.reshape(nseq, T_PAD, d)[:, :n_new]
    s_kv = [kv_cols(projs3, c)[:, :n_new].reshape(1, nseq, n_new, NSA_KV, NSA_HD) for c in range(4)]
    s_state = s_kv + [s_win_k.reshape(1, nseq, wbuf, NSA_KV, NSA_HD),
                      s_win_v.reshape(1, nseq, wbuf, NSA_KV, NSA_HD), s_hgrn[None]]

    return (y_prompt, y_sample, *p_state, *s_state)
```

```python
import functools
import math

import numpy as np
import jax
import jax.numpy as jnp
from jax import lax
from jax.experimental import pallas as pl
from jax.experimental.pallas import tpu as pltpu

F32 = jnp.float32
BF16 = jnp.bfloat16

NSA_HEADS = 8
NSA_KV = 2
NSA_HD = 64
NSA_REP = NSA_HEADS // NSA_KV
CMP_LEN = 32
CMP_STRIDE = 16
CMP_HID = 64
SEL_BLOCK = 64
SEL_TOPN = 8
WINDOW = 512
HG_HEADS = 4
HG_DK = 128
HG_DV = 128
HG_CHUNK = 64
N_BUCKETS = 32
MAX_DIST = 128
EPS = 1e-6
NEG = -1e30
FORCED = 1e9
PAGE_SIZE = 128

A_WIDTH = NSA_HEADS * NSA_HD
A_KV = NSA_KV * NSA_HD
B_WIDTH = HG_HEADS * HG_DV
B_KEY = HG_HEADS * HG_DK

LANES = 128
SUBLANES = 8
VMEM_LIMIT_BYTES = 56 * 1024 * 1024

TQ = 128
T_PAD = SUBLANES
N_ROWBLK = NSA_HEADS

C_Q = 0
C_KV = C_Q + A_WIDTH
C_GATE = C_KV + 6 * A_KV
C_ZA = C_GATE + 2 * LANES
C_QB = C_ZA + A_WIDTH
C_FB = C_QB + B_KEY
C_IB = C_FB + B_KEY
C_ZB = C_IB + B_WIDTH
C_MA = C_ZB + B_WIDTH

CMP_SRC_ROWS = (LANES - 1) * CMP_STRIDE + CMP_LEN
CMP_PITCH = CMP_STRIDE + 1


def _nn(a, b):
    return jnp.dot(a, b, preferred_element_type=F32)


def _nt(a, b):
    return lax.dot_general(a, b, (((1,), (1,)), ((), ())), preferred_element_type=F32)


def _tn(a, b):
    return lax.dot_general(a, b, (((0,), (0,)), ((), ())), preferred_element_type=F32)


def _sigmoid(x):
    return 1.0 / (1.0 + jnp.exp(-x))


def _silu(x):
    return x * _sigmoid(x)


def _split2(x):
    hi = x.astype(BF16)
    lo = (x - hi.astype(F32)).astype(BF16)
    return hi, lo


def _cparams(n_grid):
    return pltpu.CompilerParams(dimension_semantics=("arbitrary",) * n_grid,
                                vmem_limit_bytes=VMEM_LIMIT_BYTES)


def _t5_bucket(rel):
    n = jnp.maximum(rel, 0)
    exact = N_BUCKETS // 2
    nf = jnp.maximum(n, 1).astype(F32)
    scaled = jnp.log(nf / exact) / math.log(MAX_DIST / exact) * (N_BUCKETS - exact)
    large = exact + jnp.floor(scaled).astype(jnp.int32)
    large = jnp.minimum(large, N_BUCKETS - 1)
    return jnp.where(n < exact, n, large)


def _rowblk_head(rb):
    return NSA_REP * (rb % NSA_KV) + rb // NSA_KV


def _bias_body(tab_ref, o_ref, *, base, step, row_stride, col_stride, lo, hi, cols):
    sub_r = min(o_ref.shape[1], 4 * SUBLANES)
    sub_c = min(cols, LANES)
    shape = (sub_r, sub_c)
    row = lax.broadcasted_iota(jnp.int32, shape, 0)
    col = lax.broadcasted_iota(jnp.int32, shape, 1)
    for r0 in range(0, o_ref.shape[1], sub_r):
        for c0 in range(0, cols, sub_c):
            rel = base + step * pl.program_id(0) + row_stride * (row + r0) + col_stride * (col + c0)
            bucket = _t5_bucket(rel)
            accs = [jnp.zeros(shape, F32)] * N_ROWBLK
            for k in range(N_BUCKETS):
                hit = bucket == k
                accs = [jnp.where(hit, tab_ref[k, _rowblk_head(rb)], accs[rb]) for rb in range(N_ROWBLK)]
            for rb in range(N_ROWBLK):
                acc = accs[rb]
                if lo is not None:
                    acc = jnp.where((rel >= lo) & (rel < hi), acc, NEG)
                o_ref[rb, r0:r0 + sub_r, c0:c0 + sub_c] = acc


def _bias_table(rel_bias, *, base, step, row_stride, col_stride, rows, cols, steps, lo=None, hi=None):
    return pl.pallas_call(
        functools.partial(_bias_body, base=base, step=step, row_stride=row_stride, col_stride=col_stride,
                          lo=lo, hi=hi, cols=cols),
        grid=(steps,),
        in_specs=[pl.BlockSpec(memory_space=pltpu.SMEM)],
        out_specs=pl.BlockSpec((None, N_ROWBLK, rows, cols), lambda i: (i, 0, 0, 0)),
        out_shape=jax.ShapeDtypeStruct((steps, N_ROWBLK, rows, cols), F32),
        compiler_params=_cparams(1),
        name="bias_table",
    )(rel_bias)


def _bias_table_sample(rel_bias, base, col_stride, cols):
    return _bias_table(rel_bias, base=base, step=0, row_stride=1, col_stride=-col_stride, rows=T_PAD, cols=cols,
                       steps=1)[0]


N_KV_STREAMS = 6


def _proj_body(x_ref, g_ref, w_ref, o_ref, *kvt_ref):
    x = x_ref[...]
    ms = jnp.mean(x * x, axis=-1, keepdims=True)
    xn = (x * lax.rsqrt(ms + EPS) * g_ref[...]).astype(BF16)
    o = _nn(xn, w_ref[...])
    o_ref[...] = o
    if kvt_ref:
        for c in range(N_KV_STREAMS):
            kvt_ref[0][c] = o[:, C_KV + c * A_KV:C_KV + (c + 1) * A_KV].T


PROJ_ROWS = 512


def _project(x2d, norm_g, w, seq_len=None):
    m, d = x2d.shape
    n = w.shape[1]
    tm = min(m, PROJ_ROWS)
    out_specs = [pl.BlockSpec((tm, n), lambda i: (i, 0))]
    out_shape = [jax.ShapeDtypeStruct((m, n), F32)]
    if seq_len is not None:
        per_seq = seq_len // tm
        assert seq_len % tm == 0
        out_specs.append(pl.BlockSpec((N_KV_STREAMS, None, A_KV, tm), lambda i: (0, i // per_seq, 0, i % per_seq)))
        out_shape.append(jax.ShapeDtypeStruct((N_KV_STREAMS, m // seq_len, A_KV, seq_len), F32))
    outs = pl.pallas_call(
        _proj_body,
        grid=(m // tm,),
        in_specs=[pl.BlockSpec((tm, d), lambda i: (i, 0)),
                  pl.BlockSpec((1, d), lambda i: (0, 0)),
                  pl.BlockSpec((d, n), lambda i: (0, 0), pipeline_mode=pl.Buffered(1))],
        out_specs=out_specs,
        out_shape=out_shape,
        compiler_params=_cparams(1),
        name="in_proj",
    )(x2d, norm_g.reshape(1, d), w)
    return outs if seq_len is not None else outs[0]


def _permute_heads_cols(w):
    lead = w.shape[:-1]
    return w.reshape(lead + (NSA_KV, NSA_REP, NSA_HD)).swapaxes(-3, -2).reshape(lead + (A_WIDTH,))


def _arrange_w_in(w_in):
    d = w_in.shape[0]
    sizes = (A_WIDTH, 6 * A_KV, 3 * NSA_HEADS, A_WIDTH, B_KEY, B_KEY, B_WIDTH, B_WIDTH, d, d)
    offs = np.concatenate([[0], np.cumsum(sizes)])
    q_a, kv_a, g_a, z_a, q_b, f_b, i_b, z_b, m_a, m_b = [w_in[:, offs[i]:offs[i + 1]] for i in range(10)]
    gate_pad = jnp.zeros((d, 2 * LANES - 3 * NSA_HEADS), w_in.dtype)
    w = jnp.concatenate([_permute_heads_cols(q_a), kv_a, g_a, gate_pad, _permute_heads_cols(z_a),
                         q_b, f_b, i_b, z_b, m_a, m_b], axis=1)
    return w.astype(BF16)


def _compress(streams, pitch):
    accs = [jnp.zeros((len(srcs) * LANES, LANES), F32) for srcs, _, _, _ in streams]
    for j in range(CMP_LEN):
        start = (j // CMP_STRIDE) * pitch + j % CMP_STRIDE
        for n, (srcs, pos_ref, w1_ref, _) in enumerate(streams):
            xj = jnp.concatenate([r[pl.ds(start, LANES, stride=pitch), :] for r in srcs], axis=0) + pos_ref[j:j + 1, :]
            accs[n] = accs[n] + _nn(xj.astype(BF16), w1_ref[j])
    return [_nn(_silu(acc).astype(BF16), w2_ref[...]) for acc, (_, _, _, w2_ref) in zip(accs, streams)]


def _compress_prompt_body(kc_ref, vc_ref, posk_ref, w1k_ref, w2k_ref, posv_ref, w1v_ref, w2v_ref,
                          ko_ref, vo_ref, kbuf_ref, vbuf_ref):
    t = kc_ref.shape[0]
    for buf_ref, src_ref in ((kbuf_ref, kc_ref), (vbuf_ref, vc_ref)):
        buf_ref[t:, :] = jnp.zeros((buf_ref.shape[0] - t, LANES), F32)
        buf_ref[:t, :] = src_ref[...]
    ko_ref[...], vo_ref[...] = _compress([([kbuf_ref], posk_ref, w1k_ref, w2k_ref),
                                          ([vbuf_ref], posv_ref, w1v_ref, w2v_ref)], CMP_STRIDE)


def _block_diag2(w):
    z = jnp.zeros_like(w)
    return jnp.concatenate([jnp.concatenate([w, z], axis=-1), jnp.concatenate([z, w], axis=-1)], axis=-2)


def _compress_weights(pos, w1, w2):
    pos2 = jnp.concatenate([pos, pos], axis=-1)
    w1bd = _block_diag2(w1.reshape(CMP_LEN, NSA_HD, CMP_HID)).astype(BF16)
    w2bd = _block_diag2(w2).astype(BF16)
    return pos2, w1bd, w2bd


def _const_spec(shape, single_buffer=False):
    nd = len(shape)
    if single_buffer:
        return pl.BlockSpec(shape, lambda *_: (0,) * nd, pipeline_mode=pl.Buffered(1))
    return pl.BlockSpec(shape, lambda *_: (0,) * nd)


def _compress_prompt(proj3, cw_k, cw_v):
    b, t, _ = proj3.shape
    col = C_KV // LANES
    w_specs = [_const_spec(a.shape) for a in cw_k + cw_v]
    return pl.pallas_call(
        _compress_prompt_body,
        grid=(b,),
        in_specs=[pl.BlockSpec((None, t, LANES), lambda i: (i, 0, col)),
                  pl.BlockSpec((None, t, LANES), lambda i: (i, 0, col + 1))] + w_specs,
        out_specs=[pl.BlockSpec((None, LANES, LANES), lambda i: (i, 0, 0))] * 2,
        out_shape=[jax.ShapeDtypeStruct((b, LANES, LANES), F32)] * 2,
        scratch_shapes=[pltpu.VMEM((max(t + CMP_LEN, CMP_SRC_ROWS), LANES), F32)] * 2,
        compiler_params=_cparams(1),
        name="compress_prompt",
    )(proj3, proj3, *cw_k, *cw_v)


def _select_blocks(imp, qpos, nb, axis):
    j = lax.broadcasted_iota(jnp.int32, imp.shape, axis)
    cur = qpos // SEL_BLOCK
    valid = j * SEL_BLOCK <= qpos
    forced = (j == 0) | (j == cur) | (j == cur - 1)
    score = jnp.where(valid, jnp.where(forced, FORCED, imp), -1.0)
    score = jnp.where(j < nb, score, -2.0)
    rank = jnp.zeros(imp.shape, F32)
    for i in range(nb):
        s_i = lax.slice_in_dim(score, i, i + 1, axis=axis)
        tie = jnp.where(i < j, 1.0, 0.0)
        rank = rank + jnp.where(s_i > score, 1.0, jnp.where(s_i == score, tie, 0.0))
    return jnp.where((rank < min(SEL_TOPN, nb)) & (j < nb), 1.0, 0.0)


def _masked_softmax(s, mask):
    s = jnp.where(mask, s, NEG)
    m = jnp.max(s, axis=-1, keepdims=True)
    e = jnp.exp(s - m)
    p = e / jnp.sum(e, axis=-1, keepdims=True)
    return jnp.where(mask, p, 0.0)


def _stack_queries(q_ref, qbd_ref, rows):
    lane = lax.broadcasted_iota(jnp.int32, (rows, LANES), 1)
    for r in range(NSA_REP):
        blk = q_ref[:, r * LANES:(r + 1) * LANES] * (NSA_HD ** -0.5)
        qbd_ref[(2 * r) * rows:(2 * r + 1) * rows, :] = jnp.where(lane < NSA_HD, blk, 0.0).astype(BF16)
        qbd_ref[(2 * r + 1) * rows:(2 * r + 2) * rows, :] = jnp.where(lane >= NSA_HD, blk, 0.0).astype(BF16)


def _gate_and_store(gate_ref, za_ref, o_ref, o_cmp, o_sel, o_win, rows):
    gs = _sigmoid(gate_ref[...])
    lane = lax.broadcasted_iota(jnp.int32, (rows, LANES), 1)
    for r in range(NSA_REP):
        parts = []
        for g in range(NSA_KV):
            rb = 2 * r + g
            h = _rowblk_head(rb)
            o = (gs[:, h:h + 1] * o_cmp[rb] + gs[:, NSA_HEADS + h:NSA_HEADS + h + 1] * o_sel[rb]
                 + gs[:, 2 * NSA_HEADS + h:2 * NSA_HEADS + h + 1] * o_win[rb])
            parts.append(o)
        blk = jnp.where(lane < NSA_HD, parts[0], parts[1])
        o_ref[:, r * LANES:(r + 1) * LANES] = blk * _silu(za_ref[:, r * LANES:(r + 1) * LANES])


SEL_TILES_PER_ITER = 2


def _nsa_prompt_body(q_ref, gate_ref, za_ref, ks_ref, vs_ref, kw_ref, vw_ref, kcmp_ref, vcmp_ref,
                     bcmp_ref, btile_ref, covert_ref, o_ref,
                     qt_ref, ksb_ref, vst_ref, kwb_ref, vwt_ref, vcmpt_ref, mask_ref, s_ref, m_ref, l_ref, acc_ref, res_ref,
                     *, nc, nb):
    qb = pl.program_id(1)
    nkt = ksb_ref.shape[0]
    sub = lax.broadcasted_iota(jnp.int32, (LANES, TQ), 0)
    tok = lax.broadcasted_iota(jnp.int32, (LANES, TQ), 1)

    @pl.when(qb == 0)
    def _():
        for kt in range(nkt):
            rows = slice(kt * TQ, (kt + 1) * TQ)
            ksb_ref[kt] = ks_ref[rows, :].astype(BF16)
            kwb_ref[kt] = kw_ref[rows, :].astype(BF16)
            vst_ref[kt] = vs_ref[rows, :].T.astype(BF16)
            vwt_ref[kt] = vw_ref[rows, :].T.astype(BF16)
        vcmpt_ref[...] = vcmp_ref[...].T.astype(BF16)

    for r in range(NSA_REP):
        qt = (q_ref[:, r * LANES:(r + 1) * LANES] * (NSA_HD ** -0.5)).T
        qt_ref[2 * r] = jnp.where(sub < NSA_HD, qt, 0.0).astype(BF16)
        qt_ref[2 * r + 1] = jnp.where(sub >= NSA_HD, qt, 0.0).astype(BF16)

    valid = ((CMP_STRIDE * sub + CMP_LEN - 1) <= qb * TQ + tok) & (sub < nc)
    kcb = kcmp_ref[...].astype(BF16)
    psum = [None] * NSA_KV
    for rb in range(N_ROWBLK):
        s = jnp.where(valid, _nn(kcb, qt_ref[rb]) + bcmp_ref[rb], NEG)
        e = jnp.exp(s - jnp.max(s, axis=0, keepdims=True))
        p = jnp.where(valid, e / jnp.sum(e, axis=0, keepdims=True), 0.0)
        res_ref[0, rb] = _nn(vcmpt_ref[...], p.astype(BF16))
        g = rb % NSA_KV
        psum[g] = p if psum[g] is None else psum[g] + p

    nbp = _round_up(nb, SUBLANES)
    upper = sub < SEL_BLOCK
    for g in range(NSA_KV):
        p_hi, p_lo = _split2(psum[g])
        imp = _nn(covert_ref[...], p_hi) + _nn(covert_ref[...], p_lo)
        sel = _select_blocks(imp[:nbp], qb * TQ + tok[:1], nb, 0)
        for kt in range(nkt):
            picked = jnp.where(upper, sel[2 * kt:2 * kt + 1, :], sel[2 * kt + 1:2 * kt + 2, :]) > 0.5
            mask_ref[kt, g] = jnp.where(kt <= qb, jnp.where(picked, 0.0, NEG), NEG)

    m_ref[...] = jnp.full(m_ref.shape, NEG, F32)
    l_ref[...] = jnp.zeros(l_ref.shape, F32)
    acc_ref[...] = jnp.zeros(acc_ref.shape, F32)

    def sel_scores(pair, buf):
        for u in range(SEL_TILES_PER_ITER):
            kt = jnp.minimum(pair * SEL_TILES_PER_ITER + u, nkt - 1)
            k_t = ksb_ref[kt]
            bidx = jnp.clip(qb - kt, 0, 2)
            for rb in range(N_ROWBLK):
                s_ref[buf, u, rb] = _nn(k_t, qt_ref[rb]) + btile_ref[bidx, rb] + mask_ref[kt, rb % NSA_KV]

    def sel_update(pair, buf):
        v_ts = [vst_ref[pair * SEL_TILES_PER_ITER + u] for u in range(SEL_TILES_PER_ITER)]
        for rb in range(N_ROWBLK):
            ss = [s_ref[buf, u, rb] for u in range(SEL_TILES_PER_ITER)]
            m_old = m_ref[rb]
            m_new = m_old
            for s in ss:
                m_new = jnp.maximum(m_new, jnp.max(s, axis=0, keepdims=True))
            alpha = jnp.exp(m_old - m_new)
            l_new = alpha * l_ref[rb]
            acc = alpha * acc_ref[rb]
            for s, v_t in zip(ss, v_ts):
                p = jnp.exp(s - m_new)
                l_new = l_new + jnp.sum(p, axis=0, keepdims=True)
                acc = acc + _nn(v_t, p.astype(BF16))
            l_ref[rb] = l_new
            acc_ref[rb] = acc
            m_ref[rb] = m_new

    def sel_body(it, carry):
        sel_scores(2 * it + 1, 1)
        sel_update(2 * it, 0)
        sel_scores(2 * it + 2, 0)
        sel_update(2 * it + 1, 1)
        return carry

    sel_scores(0, 0)
    lax.fori_loop(0, qb // (2 * SEL_TILES_PER_ITER) + 1, sel_body, 0)
    res_ref[1] = acc_ref[...] / l_ref[...]

    n_wt = WINDOW // TQ + 1
    for rb in range(N_ROWBLK):
        qc = qt_ref[rb]
        ss = []
        for dist in range(n_wt):
            kt = jnp.maximum(qb - dist, 0)
            bidx = jnp.where(qb >= dist, dist, n_wt)
            ss.append(_nn(kwb_ref[kt], qc) + btile_ref[bidx, rb])
        m = jnp.max(ss[0], axis=0, keepdims=True)
        for s in ss[1:]:
            m = jnp.maximum(m, jnp.max(s, axis=0, keepdims=True))
        l_w = None
        acc = None
        for dist, s in enumerate(ss):
            p = jnp.exp(s - m)
            ps = jnp.sum(p, axis=0, keepdims=True)
            pv = _nn(vwt_ref[jnp.maximum(qb - dist, 0)], p.astype(BF16))
            l_w = ps if l_w is None else l_w + ps
            acc = pv if acc is None else acc + pv
        res_ref[2, rb] = acc / l_w

    gt = _sigmoid(gate_ref[...]).T
    for r in range(NSA_REP):
        parts = []
        for g in range(NSA_KV):
            rb = 2 * r + g
            h = _rowblk_head(rb)
            parts.append(gt[h:h + 1, :] * res_ref[0, rb]
                         + gt[NSA_HEADS + h:NSA_HEADS + h + 1, :] * res_ref[1, rb]
                         + gt[2 * NSA_HEADS + h:2 * NSA_HEADS + h + 1, :] * res_ref[2, rb])
        blk = jnp.where(sub < NSA_HD, parts[0], parts[1]).T
        o_ref[:, r * LANES:(r + 1) * LANES] = (blk * _silu(za_ref[:, r * LANES:(r + 1) * LANES])).astype(o_ref.dtype)


def _cover_matrix(nc, nb):
    cs = np.arange(nc) * CMP_STRIDE
    ce = cs + CMP_LEN
    bs = np.arange(nb) * SEL_BLOCK
    be = bs + SEL_BLOCK
    ov = np.clip(np.minimum(ce[:, None], be[None, :]) - np.maximum(cs[:, None], bs[None, :]), 0, None) / CMP_STRIDE
    out = np.zeros((LANES, LANES), np.float32)
    out[:nc, :nb] = ov
    return jnp.asarray(out, BF16)


def _expand_matrix(nkeys):
    j = np.arange(LANES)[:, None]
    k = np.arange(nkeys)[None, :]
    return jnp.asarray((k // SEL_BLOCK == j).astype(np.float32), BF16)


def _nsa_prompt(proj3, kcmp, vcmp, bias_cmp, bias_tiles):
    b, t, _ = proj3.shape
    nq = t // TQ
    nc = (t - CMP_LEN) // CMP_STRIDE + 1
    nb = t // SEL_BLOCK
    assert nc <= LANES and nb <= LANES and t % (TQ * 2 * SEL_TILES_PER_ITER) == 0
    kv_col = C_KV // LANES
    kv_spec = lambda c: pl.BlockSpec((None, t, LANES), lambda i, j: (i, 0, kv_col + c))
    cover_t = _cover_matrix(nc, nb).T
    tile_bf16 = pltpu.VMEM((nq, TQ, LANES), BF16)
    return pl.pallas_call(
        functools.partial(_nsa_prompt_body, nc=nc, nb=nb),
        grid=(b, nq),
        in_specs=[pl.BlockSpec((None, TQ, A_WIDTH), lambda i, j: (i, j, C_Q // A_WIDTH)),
                  pl.BlockSpec((None, TQ, LANES), lambda i, j: (i, j, C_GATE // LANES)),
                  pl.BlockSpec((None, TQ, A_WIDTH), lambda i, j: (i, j, C_ZA // A_WIDTH)),
                  kv_spec(2), kv_spec(3), kv_spec(4), kv_spec(5),
                  pl.BlockSpec((None, LANES, LANES), lambda i, j: (i, 0, 0)),
                  pl.BlockSpec((None, LANES, LANES), lambda i, j: (i, 0, 0)),
                  pl.BlockSpec((None, N_ROWBLK, LANES, TQ), lambda i, j: (j, 0, 0, 0)),
                  _const_spec(bias_tiles.shape), _const_spec(cover_t.shape)],
        out_specs=pl.BlockSpec((None, TQ, A_WIDTH), lambda i, j: (i, j, 0)),
        out_shape=jax.ShapeDtypeStruct((b, t, A_WIDTH), BF16),
        scratch_shapes=[pltpu.VMEM((N_ROWBLK, LANES, TQ), BF16),
                        tile_bf16, tile_bf16, tile_bf16, tile_bf16,
                        pltpu.VMEM((LANES, LANES), BF16),
                        pltpu.VMEM((nq, NSA_KV, TQ, TQ), F32),
                        pltpu.VMEM((2, SEL_TILES_PER_ITER, N_ROWBLK, TQ, TQ), F32),
                        pltpu.VMEM((N_ROWBLK, 1, TQ), F32),
                        pltpu.VMEM((N_ROWBLK, 1, TQ), F32),
                        pltpu.VMEM((N_ROWBLK, LANES, TQ), F32),
                        pltpu.VMEM((3, N_ROWBLK, LANES, TQ), F32)],
        compiler_params=_cparams(2),
        name="nsa_prompt",
    )(proj3, proj3, proj3, proj3, proj3, proj3, proj3, kcmp, vcmp, bias_cmp, bias_tiles, cover_t)


N_CACHES = 4


NSA_SEQ_PER_STEP = 4


def _group_page_copy(cache_refs, pt_ref, pbuf, sems, step, slot, s, c, pg, n_pages):
    seq = step * NSA_SEQ_PER_STEP + s
    return pltpu.make_async_copy(cache_refs[c].at[pt_ref[seq, pg]], pbuf.at[slot, s, c * n_pages + pg],
                                 sems.at[slot])


def _nsa_sample_group_body(pt_ref, *refs, n_pages, **static):
    cache_refs = refs[:N_CACHES]
    per_seq = refs[N_CACHES:N_CACHES + 11]
    consts = refs[N_CACHES + 11:N_CACHES + 22]
    outs = refs[N_CACHES + 22:N_CACHES + 25]
    scratch = refs[N_CACHES + 25:-2]
    pbuf, sems = refs[-2:]
    i = pl.program_id(0)
    slot = i % 2

    def all_pages(step, slot_):
        return [_group_page_copy(cache_refs, pt_ref, pbuf, sems, step, slot_, s, c, pg, n_pages)
                for s in range(NSA_SEQ_PER_STEP) for c in range(N_CACHES) for pg in range(n_pages)]

    @pl.when(i == 0)
    def _():
        for cp in all_pages(0, 0):
            cp.start()

    @pl.when(i + 1 < pl.num_programs(0))
    def _():
        for cp in all_pages(i + 1, 1 - slot):
            cp.start()

    for cp in all_pages(i, slot):
        cp.wait()
    pages = [[[pbuf.at[slot, s, c * n_pages + pg] for pg in range(n_pages)] for c in range(N_CACHES)]
             for s in range(NSA_SEQ_PER_STEP)]
    _nsa_sample_group(pages, per_seq, consts, outs, scratch, n_pages=n_pages, **static)


def _nsa_sample_group(pages, per_seq, consts, outs, scratch, *, n_pages, past, wbuf, nc, nb):
    q_ref, kc_ref, vc_ref, ks_ref, vs_ref, kw_ref, vw_ref, gate_ref, za_ref, wk_ref, wv_ref = per_seq
    (posk_ref, w1k_ref, w2k_ref, posv_ref, w1v_ref, w2v_ref, bcmp_ref, bsel_ref, bwin_ref,
     cover_ref, emat_ref) = consts
    o_ref, swk_ref, swv_ref = outs
    kc_all, vc_all, newt_ref, qbd_ref = scratch
    nseq = NSA_SEQ_PER_STEP
    seqs = range(nseq)
    n_keys = bsel_ref.shape[2]
    n_new = T_PAD // 2
    chunks_per_page = PAGE_SIZE // CMP_STRIDE
    nrow = N_ROWBLK * T_PAD

    for dst, cache, new_ref in ((kc_all, 0, kc_ref), (vc_all, 1, vc_ref)):
        for s in seqs:
            for pg in range(n_pages):
                rows = pages[s][cache][pg][...].T
                for c in range(chunks_per_page):
                    r0 = (pg * chunks_per_page + c) * CMP_PITCH
                    dst[s, r0:r0 + CMP_STRIDE, :] = rows[c * CMP_STRIDE:(c + 1) * CMP_STRIDE, :]
            r0 = (past // CMP_STRIDE) * CMP_PITCH
            dst[s, r0:r0 + T_PAD, :] = new_ref[s]
            dst[s, r0 + T_PAD:, :] = jnp.zeros((dst.shape[1] - r0 - T_PAD, LANES), F32)
    kcmp, vcmp = [c.astype(BF16) for c in _compress(
        [([kc_all.at[s] for s in seqs], posk_ref, w1k_ref, w2k_ref),
         ([vc_all.at[s] for s in seqs], posv_ref, w1v_ref, w2v_ref)], CMP_PITCH)]
    per_s = lambda x, s: x[s * LANES:(s + 1) * LANES]

    newt_ref[...] = jnp.zeros(newt_ref.shape, F32)
    for s in seqs:
        for idx, new_ref in enumerate((ks_ref, vs_ref, kw_ref, vw_ref)):
            newt_ref[s, idx, 0:T_PAD, :] = new_ref[s]
        _stack_queries(q_ref.at[s], qbd_ref.at[s], T_PAD)
    qbd = [qbd_ref[s] for s in seqs]
    tok = lax.broadcasted_iota(jnp.int32, (T_PAD, LANES), 0)
    lane = lax.broadcasted_iota(jnp.int32, (T_PAD, LANES), 1)
    qpos = past + tok
    stack = lambda xs: jnp.concatenate(xs, axis=0)
    by_head = lambda x: x.reshape(nseq, NSA_REP, NSA_KV, T_PAD, LANES)

    s_c = by_head(stack([_nt(qbd[s], per_s(kcmp, s)) for s in seqs])) + _by_head_bias(bcmp_ref[...])
    mask_c = ((CMP_STRIDE * lane + CMP_LEN - 1) <= qpos) & (lane < nc)
    p = _masked_softmax(s_c, mask_c[None, None, None]).reshape(nseq, nrow, LANES)
    o_cmp = [_nn(p[s].astype(BF16), per_s(vcmp, s)).reshape(N_ROWBLK, T_PAD, LANES) for s in seqs]

    psum = p.reshape(nseq, NSA_REP, NSA_KV * T_PAD, LANES).sum(axis=1).reshape(nseq * NSA_KV * T_PAD, LANES)
    p_hi, p_lo = _split2(psum)
    imp = _nn(p_hi, cover_ref[...]) + _nn(p_lo, cover_ref[...])
    sel = _select_blocks(imp.reshape(nseq, NSA_KV, T_PAD, LANES), qpos[None, None], nb, 3)
    selk = _nn(sel.reshape(nseq * NSA_KV * T_PAD, LANES).astype(BF16), emat_ref[...])
    selk = selk.reshape(nseq, NSA_KV, T_PAD, n_keys)

    def softmax_pv(scores, values):
        m = scores[0]
        for s_t in scores[1:]:
            m = jnp.maximum(m, s_t)
        m = jnp.max(m, axis=-1, keepdims=True)
        acc = [None] * nseq
        tot = None
        for s_t, (v_ts, channel_major) in zip(scores, values):
            p_t = jnp.exp(s_t - m)
            pb = p_t.reshape(nseq, nrow, LANES).astype(BF16)
            for s in seqs:
                pv = _nt(pb[s], v_ts[s]) if channel_major else _nn(pb[s], v_ts[s])
                acc[s] = pv if acc[s] is None else acc[s] + pv
            tot = p_t if tot is None else tot + p_t
        inv = 1.0 / jnp.sum(tot, axis=-1, keepdims=True).reshape(nseq, nrow, 1)
        return [(acc[s] * inv[s]).reshape(N_ROWBLK, T_PAD, LANES) for s in seqs]

    scores, values = [], []
    for t in range(n_pages + 1):
        cols = slice(t * LANES, (t + 1) * LANES)
        allowed = ((t * LANES + lane) <= qpos)[None, None] & (selk[:, :, :, cols] > 0.5)
        madd = jnp.where(allowed, 0.0, NEG)[:, None]
        if t < n_pages:
            raw = [_nn(qbd[s], pages[s][2][t][...].astype(BF16)) for s in seqs]
            values.append(([pages[s][3][t][...].astype(BF16) for s in seqs], True))
        else:
            raw = [_nt(qbd[s], newt_ref[s, 0].astype(BF16)) for s in seqs]
            values.append(([newt_ref[s, 1].astype(BF16) for s in seqs], False))
        scores.append(by_head(stack(raw)) + _by_head_bias(bsel_ref[:, :, cols]) + madd)
    o_sel = softmax_pv(scores, values)

    scores, values = [], []
    kwt = [wk_ref[s].astype(BF16) for s in seqs]
    vwt = [wv_ref[s].astype(BF16) for s in seqs]
    for t in range(wbuf // LANES + 1):
        cols = slice(t * LANES, (t + 1) * LANES)
        rel_w = wbuf + tok - (t * LANES + lane)
        madd = jnp.where((rel_w >= 0) & (rel_w < WINDOW), 0.0, NEG)[None, None, None]
        if t < wbuf // LANES:
            raw = [_nn(qbd[s], kwt[s][:, cols]) for s in seqs]
            values.append(([vwt[s][:, cols] for s in seqs], True))
        else:
            raw = [_nt(qbd[s], newt_ref[s, 2].astype(BF16)) for s in seqs]
            values.append(([newt_ref[s, 3].astype(BF16) for s in seqs], False))
        scores.append(by_head(stack(raw)) + _by_head_bias(bwin_ref[:, :, cols]) + madd)
    o_win = softmax_pv(scores, values)

    lane_full = lax.broadcasted_iota(jnp.int32, (LANES, LANES), 1)
    for s in seqs:
        _gate_and_store(gate_ref.at[s], za_ref.at[s], o_ref.at[s], o_cmp[s], o_sel[s], o_win[s], T_PAD)
        for idx, w_ref, out_ref in ((2, wk_ref, swk_ref), (3, wv_ref, swv_ref)):
            rolled = pltpu.roll(w_ref[s], wbuf - n_new, axis=1)
            tail = pltpu.roll(newt_ref[s, idx], LANES - n_new, axis=0).T
            out_ref[s, :, :wbuf - LANES] = rolled[:, :wbuf - LANES]
            out_ref[s, :, wbuf - LANES:] = jnp.where(lane_full >= LANES - n_new, tail, rolled[:, wbuf - LANES:])


def _by_head_bias(b):
    return b.reshape(1, NSA_REP, NSA_KV, T_PAD, b.shape[-1])


def _round_up(x, m):
    return (x + m - 1) // m * m


def _nsa_sample(projs3, caches, wk, wv, page_table, cw_k, cw_v, rel_bias, n_new):
    nseq, n_pages = page_table.shape
    past = n_pages * PAGE_SIZE
    wbuf = wk.shape[2]
    length = past + n_new
    nc = (length - CMP_LEN) // CMP_STRIDE + 1
    nb = -(-length // SEL_BLOCK)
    n_keys = past + LANES
    n_win = wbuf + LANES
    pitched = lambda p: (p // CMP_STRIDE) * CMP_PITCH + p % CMP_STRIDE
    n_cmp_rows = _round_up(max(pitched(past) + T_PAD, pitched(CMP_SRC_ROWS - 1) + 1), SUBLANES)
    assert nc <= LANES and nb <= LANES and n_new == T_PAD // 2 and nb * SEL_BLOCK <= n_keys
    assert wbuf % LANES == 0 and PAGE_SIZE == LANES

    bias_cmp = _bias_table_sample(rel_bias, past - (CMP_LEN - 1), CMP_STRIDE, LANES)
    bias_sel = _bias_table_sample(rel_bias, past, 1, n_keys)
    bias_win = _bias_table_sample(rel_bias, wbuf, 1, n_win)
    cover = _cover_matrix(nc, nb)
    emat = _expand_matrix(n_keys)

    sps = NSA_SEQ_PER_STEP
    assert nseq % sps == 0

    def new_spec(width, col):
        return pl.BlockSpec((sps, T_PAD, width), lambda i, pt: (i, 0, col))

    kv_col = C_KV // LANES
    in_specs = [pl.BlockSpec(memory_space=pl.ANY)] * N_CACHES
    in_specs += [new_spec(A_WIDTH, C_Q // A_WIDTH)]
    in_specs += [new_spec(LANES, kv_col + c) for c in range(6)]
    in_specs += [new_spec(LANES, C_GATE // LANES), new_spec(A_WIDTH, C_ZA // A_WIDTH)]
    win_spec = pl.BlockSpec((sps, LANES, wbuf), lambda i, pt: (i, 0, 0))
    in_specs += [win_spec] * 2
    consts = list(cw_k) + list(cw_v) + [bias_cmp, bias_sel, bias_win, cover, emat]
    in_specs += [_const_spec(a.shape, single_buffer=True) for a in consts]
    operands = list(caches) + [projs3] * 9 + [wk, wv] + consts

    grid_spec = pltpu.PrefetchScalarGridSpec(
        num_scalar_prefetch=1,
        grid=(nseq // sps,),
        in_specs=in_specs,
        out_specs=[pl.BlockSpec((sps, T_PAD, A_WIDTH), lambda i, pt: (i, 0, 0)), win_spec, win_spec],
        scratch_shapes=[pltpu.VMEM((sps, n_cmp_rows, LANES), F32), pltpu.VMEM((sps, n_cmp_rows, LANES), F32),
                        pltpu.VMEM((sps, 4, LANES, LANES), F32),
                        pltpu.VMEM((sps, N_ROWBLK * T_PAD, LANES), BF16),
                        pltpu.VMEM((2, sps, N_CACHES * n_pages, LANES, PAGE_SIZE), F32),
                        pltpu.SemaphoreType.DMA((2,))])
    return pl.pallas_call(
        functools.partial(_nsa_sample_group_body, n_pages=n_pages, past=past, wbuf=wbuf, nc=nc, nb=nb),
        grid_spec=grid_spec,
        out_shape=[jax.ShapeDtypeStruct((nseq, T_PAD, A_WIDTH), F32),
                   jax.ShapeDtypeStruct((nseq, LANES, wbuf), F32),
                   jax.ShapeDtypeStruct((nseq, LANES, wbuf), F32)],
        compiler_params=_cparams(1),
        name="nsa_sample",
    )(page_table, *operands)


def _hgrn_constants(rows, seg_stride, seg_len):
    idx = np.arange(rows)
    seg = idx // seg_stride
    pos = idx % seg_stride
    same = seg[:, None] == seg[None, :]
    pt, pu = pos[:, None], pos[None, :]
    mats = [same & (pu <= pt),
            same & (pu > pt) & (pu <= seg_len - 1)]
    levels = []
    m = seg_len // 2
    while m >= 2:
        levels.append(m)
        m //= 2
    for m in levels:
        mats.append(same & (pu >= (pt // m) * m) & (pu <= pt))
        mats.append(same & (pu > pt) & (pu <= (pt // m) * m + m - 1))
    wstack = np.concatenate(mats, axis=0).astype(np.float32)
    wstack = np.concatenate([wstack] * 3, axis=1)
    masks = []
    for m in levels + [1]:
        masks.append(same & (pt // (2 * m) == pu // (2 * m)) & ((pt // m) % 2 == 1) & ((pu // m) % 2 == 0))
    masks.append(idx[:, None] == idx[None, :])
    masks = np.stack(masks).astype(np.float32)
    return jnp.asarray(wstack, BF16), jnp.asarray(masks, F32), len(levels)


def _hgrn_chunk(q, f_pre, hl_ref, wst_ref, n_levels):
    rows = q.shape[0]
    hl = hl_ref[...]
    e = jnp.exp(hl - jnp.max(hl, axis=0, keepdims=True))
    lb = e[0:1] / jnp.sum(e, axis=0, keepdims=True)
    f = lb + (1.0 - lb) * _sigmoid(f_pre)
    g = jnp.log(f)
    k = 1.0 - f
    g_hi = g.astype(BF16)
    r1 = g - g_hi.astype(F32)
    g_mid = r1.astype(BF16)
    g_lo = (r1 - g_mid.astype(F32)).astype(BF16)
    sums = _nn(wst_ref[...], jnp.concatenate([g_hi, g_mid, g_lo], axis=0))
    big_g = sums[0:rows]
    qg = q * jnp.exp(big_g)
    kd = k * jnp.exp(sums[rows:2 * rows])
    q_lv, k_lv = [], []
    for li in range(n_levels):
        q_lv.append(q * jnp.exp(sums[(2 + 2 * li) * rows:(3 + 2 * li) * rows]))
        k_lv.append(k * jnp.exp(sums[(3 + 2 * li) * rows:(4 + 2 * li) * rows]))
    q_lv += [q * f, q]
    k_lv += [k, k]
    return big_g, qg, kd, q_lv, k_lv


def _hgrn_intra(q_lv, k_lv, masks_ref, hs):
    a = None
    for li in range(len(q_lv)):
        term = masks_ref[li] * _nt(q_lv[li][:, hs].astype(BF16), k_lv[li][:, hs].astype(BF16))
        a = term if a is None else a + term
    return a


def _hgrn_out(o, ng, z):
    o = o * lax.rsqrt(jnp.mean(o * o, axis=-1, keepdims=True) + EPS)
    return o * ng * _silu(z)


def _hgrn_prompt_body(q_ref, f_ref, i_ref, z_ref, hl_ref, ng_ref, wst_ref, masks_ref, o_ref, s_ref, st_ref,
                      *, n_levels, rows):
    c = pl.program_id(1)

    @pl.when(c == 0)
    def _():
        st_ref[...] = jnp.zeros(st_ref.shape, F32)

    chunks = []
    for ci in range(q_ref.shape[0] // rows):
        rs = slice(ci * rows, (ci + 1) * rows)
        v = i_ref[rs, :]
        big_g, qg, kd, q_lv, k_lv = _hgrn_chunk(q_ref[rs, :], f_ref[rs, :], hl_ref, wst_ref, n_levels)
        per_head = []
        for h in range(HG_HEADS):
            hs = slice(h * HG_DK, (h + 1) * HG_DK)
            vb = v[:, hs].astype(BF16)
            a = _hgrn_intra(q_lv, k_lv, masks_ref, hs)
            per_head.append((qg[:, hs].astype(BF16), _nn(a.astype(BF16), vb),
                             jnp.exp(big_g[rows - 1:rows, hs]), _tn(vb, kd[:, hs].astype(BF16))))
        chunks.append(per_head)

    for h in range(HG_HEADS):
        hs = slice(h * HG_DK, (h + 1) * HG_DK)
        st = st_ref[h]
        for ci, per_head in enumerate(chunks):
            rs = slice(ci * rows, (ci + 1) * rows)
            qgb, o_intra, decay, kv = per_head[h]
            o = _nt(qgb, st.astype(BF16)) + o_intra
            o_ref[rs, hs] = _hgrn_out(o, ng_ref[:, hs], z_ref[rs, hs]).astype(o_ref.dtype)
            st = st * decay + kv
        st_ref[h] = st

        @pl.when(c == pl.num_programs(1) - 1)
        def _():
            s_ref[h] = st.T


HGRN_CHUNKS_PER_STEP = 16


def _hgrn_prompt(proj3, hg_lower, hg_norm_g):
    b, t, _ = proj3.shape
    chunk = math.gcd(t, HG_CHUNK)
    wst, masks, n_levels = _hgrn_constants(chunk, chunk, chunk)
    rows = chunk * math.gcd(t // chunk, HGRN_CHUNKS_PER_STEP)
    col = lambda c: pl.BlockSpec((None, rows, B_KEY), lambda i, j: (i, j, c // B_KEY))
    return pl.pallas_call(
        functools.partial(_hgrn_prompt_body, n_levels=n_levels, rows=chunk),
        grid=(b, t // rows),
        in_specs=[col(C_QB), col(C_FB), col(C_IB), col(C_ZB),
                  _const_spec(hg_lower.shape), _const_spec(hg_norm_g.shape),
                  _const_spec(wst.shape), _const_spec(masks.shape)],
        out_specs=[pl.BlockSpec((None, rows, B_WIDTH), lambda i, j: (i, j, 0)),
                   pl.BlockSpec((None, HG_HEADS, HG_DK, HG_DV), lambda i, j: (i, 0, 0, 0))],
        out_shape=[jax.ShapeDtypeStruct((b, t, B_WIDTH), BF16),
                   jax.ShapeDtypeStruct((b, HG_HEADS, HG_DK, HG_DV), F32)],
        scratch_shapes=[pltpu.VMEM((HG_HEADS, HG_DV, HG_DK), F32)],
        compiler_params=_cparams(2),
        name="hgrn_prompt",
    )(proj3, proj3, proj3, proj3, hg_lower, hg_norm_g, wst, masks)


SEQ_PER_STEP = 8


def _hgrn_sample_body(q_ref, f_ref, i_ref, z_ref, s0_ref, hl_ref, ng_ref, wst_ref, masks_ref, o_ref, s_ref,
                      *, n_levels, n_new):
    v = i_ref[...]
    big_g, qg, kd, q_lv, k_lv = _hgrn_chunk(q_ref[...], f_ref[...], hl_ref, wst_ref, n_levels)
    for h in range(HG_HEADS):
        hs = slice(h * HG_DK, (h + 1) * HG_DK)
        vb = v[:, hs].astype(BF16)
        a = _hgrn_intra(q_lv, k_lv, masks_ref, hs)
        o_intra = _nn(a.astype(BF16), vb)
        for s in range(SEQ_PER_STEP):
            rs = slice(s * T_PAD, (s + 1) * T_PAD)
            s0 = s0_ref[s, h]
            o = _nn(qg[rs, hs].astype(BF16), s0.astype(BF16)) + o_intra[rs]
            o_ref[rs, hs] = _hgrn_out(o, ng_ref[:, hs], z_ref[rs, hs])
            decay = jnp.exp(big_g[rs, hs]).T[:, n_new - 1:n_new]
            s_ref[s, h] = s0 * decay + _tn(kd[rs, hs].astype(BF16), v[rs, hs].astype(BF16))


def _hgrn_sample(projs, state, hg_lower, hg_norm_g, n_new):
    nseq = state.shape[0]
    rows = SEQ_PER_STEP * T_PAD
    wst, masks, n_levels = _hgrn_constants(rows, T_PAD, n_new)
    col = lambda c: pl.BlockSpec((rows, B_KEY), lambda i: (i, c // B_KEY))
    st_spec = pl.BlockSpec((SEQ_PER_STEP, HG_HEADS, HG_DK, HG_DV), lambda i: (i, 0, 0, 0))
    return pl.pallas_call(
        functools.partial(_hgrn_sample_body, n_levels=n_levels, n_new=n_new),
        grid=(nseq // SEQ_PER_STEP,),
        in_specs=[col(C_QB), col(C_FB), col(C_IB), col(C_ZB), st_spec,
                  _const_spec(hg_lower.shape), _const_spec(hg_norm_g.shape),
                  _const_spec(wst.shape), _const_spec(masks.shape)],
        out_specs=[pl.BlockSpec((rows, B_WIDTH), lambda i: (i, 0)), st_spec],
        out_shape=[jax.ShapeDtypeStruct((nseq * T_PAD, B_WIDTH), F32),
                   jax.ShapeDtypeStruct(state.shape, F32)],
        compiler_params=_cparams(1),
        name="hgrn_sample",
    )(projs, projs, projs, projs, state, hg_lower, hg_norm_g, wst, masks)


def _merge_body(oa_ref, ob_ref, ma_ref, mb_ref, x_ref, wba_ref, wbb_ref, wout_ref, fg_ref, y_ref):
    a = _nn(oa_ref[...].astype(BF16), wba_ref[...])
    b = _nn(ob_ref[...].astype(BF16), wbb_ref[...])
    y = _sigmoid(ma_ref[...]) * a + _sigmoid(mb_ref[...]) * b
    h = x_ref[...] + _nn(y.astype(BF16), wout_ref[...])
    ms = jnp.mean(h * h, axis=-1, keepdims=True)
    y_ref[...] = h * lax.rsqrt(ms + EPS) * fg_ref[...]


def _merge(o_a, o_b, proj, x2d, w_ba, w_bb, w_out, final_g):
    m, d = x2d.shape
    tm = min(m, 512)
    row = lambda w, c: pl.BlockSpec((tm, w), lambda i: (i, c))
    return pl.pallas_call(
        _merge_body,
        grid=(m // tm,),
        in_specs=[row(A_WIDTH, 0), row(B_WIDTH, 0), row(d, C_MA // d), row(d, C_MA // d + 1), row(d, 0),
                  _const_spec(w_ba.shape), _const_spec(w_bb.shape), _const_spec(w_out.shape),
                  _const_spec((1, d))],
        out_specs=row(d, 0),
        out_shape=jax.ShapeDtypeStruct((m, d), F32),
        compiler_params=_cparams(1),
        name="merge_out",
    )(o_a, o_b, proj, proj, x2d, w_ba, w_bb, w_out, final_g.reshape(1, d))


def kernel(x_prompt, x_sample, cache_cmp_k, cache_cmp_v, cache_sel_k, cache_sel_v, state_win_k, state_win_v,
           state_hgrn, page_table, norm_g, w_in, cmp_pos_k, cmp_w1_k, cmp_w2_k, cmp_pos_v, cmp_w1_v, cmp_w2_v,
           rel_bias, hg_lower, hg_norm_g, w_branch_a, w_branch_b, w_out, final_g):
    depth = norm_g.shape[0]
    assert depth == 1, "single-layer trunk"
    b, t, d = x_prompt.shape
    nseq, n_new, _ = x_sample.shape
    assert C_MA % d == 0

    w = _arrange_w_in(w_in[0])
    n_cols = w.shape[1]
    w_ba = _permute_heads_cols(w_branch_a[0].T).T.astype(BF16)
    w_bb = w_branch_b[0].astype(BF16)
    w_o = w_out[0].astype(BF16)
    cw_k = _compress_weights(cmp_pos_k[0], cmp_w1_k[0], cmp_w2_k[0])
    cw_v = _compress_weights(cmp_pos_v[0], cmp_w1_v[0], cmp_w2_v[0])
    ng = hg_norm_g[0].reshape(1, B_WIDTH)
    kv_cols = lambda a, c: a[..., C_KV + c * A_KV:C_KV + (c + 1) * A_KV]

    xp2 = x_prompt.reshape(b * t, d)
    proj, kv_t = _project(xp2, norm_g[0], w, seq_len=t)
    proj3 = proj.reshape(b, t, n_cols)
    nc_p = (t - CMP_LEN) // CMP_STRIDE + 1
    kcmp, vcmp = _compress_prompt(proj3, cw_k, cw_v)
    bias_cmp = _bias_table(rel_bias, base=-(CMP_LEN - 1), step=TQ, row_stride=-CMP_STRIDE, col_stride=1,
                           rows=LANES, cols=TQ, steps=t // TQ)
    bias_tiles = _bias_table(rel_bias, base=0, step=TQ, row_stride=-1, col_stride=1, rows=TQ, cols=TQ,
                             steps=WINDOW // TQ + 2, lo=0, hi=WINDOW)
    o_a = _nsa_prompt(proj3, kcmp, vcmp, bias_cmp, bias_tiles)
    o_b, p_hgrn = _hgrn_prompt(proj3, hg_lower, ng)
    y_prompt = _merge(o_a.reshape(b * t, A_WIDTH), o_b.reshape(b * t, B_WIDTH), proj, xp2,
                      w_ba, w_bb, w_o, final_g).reshape(b, t, d)
    wb_p = min(WINDOW, t)
    chan_major = lambda a: jnp.transpose(a, (0, 2, 3, 1)).reshape(a.shape[0], A_KV, a.shape[1])
    row_major = lambda a: jnp.transpose(a.reshape(a.shape[0], NSA_KV, NSA_HD, a.shape[2]), (0, 3, 1, 2))
    p_kv = ([row_major(kv_t[c])[None] for c in range(4)]
            + [row_major(kv_t[c, :, :, t - wb_p:])[None] for c in (4, 5)])
    p_state = p_kv + [p_hgrn[None]]

    xs_pad = jnp.pad(x_sample, ((0, 0), (0, T_PAD - n_new), (0, 0))).reshape(nseq * T_PAD, d)
    projs = _project(xs_pad, norm_g[0], w)
    projs3 = projs.reshape(nseq, T_PAD, n_cols)
    pool = cache_cmp_k.shape[1]
    caches =[chan_major(c[0]) for c in (cache_cmp_k, cache_cmp_v, cache_sel_k, cache_sel_v)]
    wbuf = state_win_k.shape[2]
    o_as, s_win_k, s_win_v = _nsa_sample(projs3, caches, chan_major(state_win_k[0]), chan_major(state_win_v[0]),
                                         page_table, cw_k, cw_v, rel_bias, n_new)
    s_win_k, s_win_v = row_major(s_win_k), row_major(s_win_v)
    o_bs, s_hgrn = _hgrn_sample(projs, state_hgrn[0], hg_lower, ng, n_new)
    y_s = _merge(o_as.reshape(nseq * T_PAD, A_WIDTH), o_bs, projs, xs_pad, w_ba, w_bb, w_o, final_g)
    y_sample = y_s.reshape(nseq, T_PAD, d)[:, :n_new]
    s_kv = [kv_cols(projs3, c)[:, :n_new].reshape(1, nseq, n_new, NSA_KV, NSA_HD) for c in range(4)]
    s_state = s_kv + [s_win_k.reshape(1, nseq, wbuf, NSA_KV, NSA_HD),
                      s_win_v.reshape(1, nseq, wbuf, NSA_KV, NSA_HD), s_hgrn[None]]

    return (y_prompt, y_sample, *p_state, *s_state)
```

```python
import functools
import math

import numpy as np
import jax
import jax.numpy as jnp
from jax import lax
from jax.experimental import pallas as pl
from jax.experimental.pallas import tpu as pltpu

F32 = jnp.float32
BF16 = jnp.bfloat16

NSA_HEADS = 8
NSA_KV = 2
NSA_HD = 64
NSA_REP = NSA_HEADS // NSA_KV
CMP_LEN = 32
CMP_STRIDE = 16
CMP_HID = 64
SEL_BLOCK = 64
SEL_TOPN = 8
WINDOW = 512
HG_HEADS = 4
HG_DK = 128
HG_DV = 128
HG_CHUNK = 64
N_BUCKETS = 32
MAX_DIST = 128
EPS = 1e-6
NEG = -1e30
FORCED = 1e9
PAGE_SIZE = 128

A_WIDTH = NSA_HEADS * NSA_HD
A_KV = NSA_KV * NSA_HD
B_WIDTH = HG_HEADS * HG_DV
B_KEY = HG_HEADS * HG_DK

LANES = 128
SUBLANES = 8
VMEM_LIMIT_BYTES = 56 * 1024 * 1024

TQ = 128
T_PAD = SUBLANES
N_ROWBLK = NSA_HEADS

C_Q = 0
C_KV = C_Q + A_WIDTH
C_GATE = C_KV + 6 * A_KV
C_ZA = C_GATE + 2 * LANES
C_QB = C_ZA + A_WIDTH
C_FB = C_QB + B_KEY
C_IB = C_FB + B_KEY
C_ZB = C_IB + B_WIDTH
C_MA = C_ZB + B_WIDTH

CMP_SRC_ROWS = (LANES - 1) * CMP_STRIDE + CMP_LEN
CMP_PITCH = CMP_STRIDE + 1


def _nn(a, b):
    return jnp.dot(a, b, preferred_element_type=F32)


def _nt(a, b):
    return lax.dot_general(a, b, (((1,), (1,)), ((), ())), preferred_element_type=F32)


def _tn(a, b):
    return lax.dot_general(a, b, (((0,), (0,)), ((), ())), preferred_element_type=F32)


def _sigmoid(x):
    return 1.0 / (1.0 + jnp.exp(-x))


def _silu(x):
    return x * _sigmoid(x)


def _split2(x):
    hi = x.astype(BF16)
    lo = (x - hi.astype(F32)).astype(BF16)
    return hi, lo


def _cparams(n_grid):
    return pltpu.CompilerParams(dimension_semantics=("arbitrary",) * n_grid,
                                vmem_limit_bytes=VMEM_LIMIT_BYTES)


def _t5_bucket(rel):
    n = jnp.maximum(rel, 0)
    exact = N_BUCKETS // 2
    nf = jnp.maximum(n, 1).astype(F32)
    scaled = jnp.log(nf / exact) / math.log(MAX_DIST / exact) * (N_BUCKETS - exact)
    large = exact + jnp.floor(scaled).astype(jnp.int32)
    large = jnp.minimum(large, N_BUCKETS - 1)
    return jnp.where(n < exact, n, large)


def _rowblk_head(rb):
    return NSA_REP * (rb % NSA_KV) + rb // NSA_KV


def _bias_body(tab_ref, o_ref, *, base, step, row_stride, col_stride, lo, hi, cols):
    sub_r = min(o_ref.shape[1], 4 * SUBLANES)
    sub_c = min(cols, LANES)
    shape = (sub_r, sub_c)
    row = lax.broadcasted_iota(jnp.int32, shape, 0)
    col = lax.broadcasted_iota(jnp.int32, shape, 1)
    for r0 in range(0, o_ref.shape[1], sub_r):
        for c0 in range(0, cols, sub_c):
            rel = base + step * pl.program_id(0) + row_stride * (row + r0) + col_stride * (col + c0)
            bucket = _t5_bucket(rel)
            accs = [jnp.zeros(shape, F32)] * N_ROWBLK
            for k in range(N_BUCKETS):
                hit = bucket == k
                accs = [jnp.where(hit, tab_ref[k, _rowblk_head(rb)], accs[rb]) for rb in range(N_ROWBLK)]
            for rb in range(N_ROWBLK):
                acc = accs[rb]
                if lo is not None:
                    acc = jnp.where((rel >= lo) & (rel < hi), acc, NEG)
                o_ref[rb, r0:r0 + sub_r, c0:c0 + sub_c] = acc


def _bias_table(rel_bias, *, base, step, row_stride, col_stride, rows, cols, steps, lo=None, hi=None):
    return pl.pallas_call(
        functools.partial(_bias_body, base=base, step=step, row_stride=row_stride, col_stride=col_stride,
                          lo=lo, hi=hi, cols=cols),
        grid=(steps,),
        in_specs=[pl.BlockSpec(memory_space=pltpu.SMEM)],
        out_specs=pl.BlockSpec((None, N_ROWBLK, rows, cols), lambda i: (i, 0, 0, 0)),
        out_shape=jax.ShapeDtypeStruct((steps, N_ROWBLK, rows, cols), F32),
        compiler_params=_cparams(1),
        name="bias_table",
    )(rel_bias)


def _bias_table_sample(rel_bias, base, col_stride, cols):
    return _bias_table(rel_bias, base=base, step=0, row_stride=1, col_stride=-col_stride, rows=T_PAD, cols=cols,
                       steps=1)[0]


N_KV_STREAMS = 6


def _proj_body(x_ref, g_ref, w_ref, o_ref, *kvt_refs):
    x = x_ref[...]
    ms = jnp.mean(x * x, axis=-1, keepdims=True)
    xn = (x * lax.rsqrt(ms + EPS) * g_ref[...]).astype(BF16)
    o = _nt(xn, w_ref[...])
    o_ref[...] = o
    for c, kvt_ref in enumerate(kvt_refs):
        kvt_ref[...] = o[:, C_KV + c * A_KV:C_KV + (c + 1) * A_KV].T


PROJ_ROWS = 512


def _project(x2d, norm_g, w, seq_len=None):
    m, d = x2d.shape
    n = w.shape[0]
    tm = min(m, PROJ_ROWS)
    out_specs = [pl.BlockSpec((tm, n), lambda i: (i, 0))]
    out_shape = [jax.ShapeDtypeStruct((m, n), F32)]
    if seq_len is not None:
        per_seq = seq_len // tm
        assert seq_len % tm == 0
        out_specs += [pl.BlockSpec((None, A_KV, tm), lambda i: (i // per_seq, 0, i % per_seq))] * N_KV_STREAMS
        out_shape += [jax.ShapeDtypeStruct((m // seq_len, A_KV, seq_len), F32)] * N_KV_STREAMS
    outs = pl.pallas_call(
        _proj_body,
        grid=(m // tm,),
        in_specs=[pl.BlockSpec((tm, d), lambda i: (i, 0)),
                  pl.BlockSpec((1, d), lambda i: (0, 0)),
                  pl.BlockSpec((n, d), lambda i: (0, 0), pipeline_mode=pl.Buffered(1))],
        out_specs=out_specs,
        out_shape=out_shape,
        compiler_params=_cparams(1),
        name="in_proj",
    )(x2d, norm_g.reshape(1, d), w)
    return outs if seq_len is not None else outs[0]


def _permute_heads_rows(w):
    return w.reshape(NSA_KV, NSA_REP, NSA_HD, w.shape[-1]).swapaxes(0, 1).reshape(A_WIDTH, w.shape[-1])


def _arrange_w_in(w_in):
    d = w_in.shape[0]
    wt = w_in.T
    sizes = (A_WIDTH, 6 * A_KV, 3 * NSA_HEADS, A_WIDTH, 2 * B_KEY + 2 * B_WIDTH + 2 * d)
    offs = np.concatenate([[0], np.cumsum(sizes)])
    q_a, kv_a, g_a, z_a, rest = [wt[offs[i]:offs[i + 1]] for i in range(5)]
    gate_pad = jnp.zeros((2 * LANES - 3 * NSA_HEADS, d), w_in.dtype)
    return jnp.concatenate([_permute_heads_rows(q_a), kv_a, g_a, gate_pad, _permute_heads_rows(z_a), rest],
                           axis=0).astype(BF16)


def _compress(streams, pitch):
    accs = [jnp.zeros((len(srcs) * LANES, LANES), F32) for srcs, _, _, _ in streams]
    for j in range(CMP_LEN):
        start = (j // CMP_STRIDE) * pitch + j % CMP_STRIDE
        for n, (srcs, pos_ref, w1_ref, _) in enumerate(streams):
            xj = jnp.concatenate([r[pl.ds(start, LANES, stride=pitch), :] for r in srcs], axis=0) + pos_ref[j:j + 1, :]
            accs[n] = accs[n] + _nn(xj.astype(BF16), w1_ref[j])
    return [_nn(_silu(acc).astype(BF16), w2_ref[...]) for acc, (_, _, _, w2_ref) in zip(accs, streams)]


def _compress_prompt_body(kc_ref, vc_ref, posk_ref, w1k_ref, w2k_ref, posv_ref, w1v_ref, w2v_ref,
                          ko_ref, vo_ref, kbuf_ref, vbuf_ref):
    t = kc_ref.shape[0]
    for buf_ref, src_ref in ((kbuf_ref, kc_ref), (vbuf_ref, vc_ref)):
        buf_ref[t:, :] = jnp.zeros((buf_ref.shape[0] - t, LANES), F32)
        buf_ref[:t, :] = src_ref[...]
    ko_ref[...], vo_ref[...] = _compress([([kbuf_ref], posk_ref, w1k_ref, w2k_ref),
                                          ([vbuf_ref], posv_ref, w1v_ref, w2v_ref)], CMP_STRIDE)


def _block_diag2(w):
    z = jnp.zeros_like(w)
    return jnp.concatenate([jnp.concatenate([w, z], axis=-1), jnp.concatenate([z, w], axis=-1)], axis=-2)


def _compress_weights(pos, w1, w2):
    pos2 = jnp.concatenate([pos, pos], axis=-1)
    w1bd = _block_diag2(w1.reshape(CMP_LEN, NSA_HD, CMP_HID)).astype(BF16)
    w2bd = _block_diag2(w2).astype(BF16)
    return pos2, w1bd, w2bd


def _const_spec(shape, single_buffer=False):
    nd = len(shape)
    if single_buffer:
        return pl.BlockSpec(shape, lambda *_: (0,) * nd, pipeline_mode=pl.Buffered(1))
    return pl.BlockSpec(shape, lambda *_: (0,) * nd)


def _compress_prompt(proj3, cw_k, cw_v):
    b, t, _ = proj3.shape
    col = C_KV // LANES
    w_specs = [_const_spec(a.shape) for a in cw_k + cw_v]
    return pl.pallas_call(
        _compress_prompt_body,
        grid=(b,),
        in_specs=[pl.BlockSpec((None, t, LANES), lambda i: (i, 0, col)),
                  pl.BlockSpec((None, t, LANES), lambda i: (i, 0, col + 1))] + w_specs,
        out_specs=[pl.BlockSpec((None, LANES, LANES), lambda i: (i, 0, 0))] * 2,
        out_shape=[jax.ShapeDtypeStruct((b, LANES, LANES), F32)] * 2,
        scratch_shapes=[pltpu.VMEM((max(t + CMP_LEN, CMP_SRC_ROWS), LANES), F32)] * 2,
        compiler_params=_cparams(1),
        name="compress_prompt",
    )(proj3, proj3, *cw_k, *cw_v)


def _select_blocks(imp, qpos, nb, axis):
    j = lax.broadcasted_iota(jnp.int32, imp.shape, axis)
    cur = qpos // SEL_BLOCK
    valid = j * SEL_BLOCK <= qpos
    forced = (j == 0) | (j == cur) | (j == cur - 1)
    score = jnp.where(valid, jnp.where(forced, FORCED, imp), -1.0)
    score = jnp.where(j < nb, score, -2.0)
    rank = jnp.zeros(imp.shape, F32)
    for i in range(nb):
        s_i = lax.slice_in_dim(score, i, i + 1, axis=axis)
        tie = jnp.where(i < j, 1.0, 0.0)
        rank = rank + jnp.where(s_i > score, 1.0, jnp.where(s_i == score, tie, 0.0))
    return jnp.where((rank < min(SEL_TOPN, nb)) & (j < nb), 1.0, 0.0)


def _masked_softmax(s, mask):
    s = jnp.where(mask, s, NEG)
    m = jnp.max(s, axis=-1, keepdims=True)
    e = jnp.exp(s - m)
    p = e / jnp.sum(e, axis=-1, keepdims=True)
    return jnp.where(mask, p, 0.0)


def _stack_queries(q_ref, qbd_ref, rows):
    lane = lax.broadcasted_iota(jnp.int32, (rows, LANES), 1)
    for r in range(NSA_REP):
        blk = q_ref[:, r * LANES:(r + 1) * LANES] * (NSA_HD ** -0.5)
        qbd_ref[(2 * r) * rows:(2 * r + 1) * rows, :] = jnp.where(lane < NSA_HD, blk, 0.0).astype(BF16)
        qbd_ref[(2 * r + 1) * rows:(2 * r + 2) * rows, :] = jnp.where(lane >= NSA_HD, blk, 0.0).astype(BF16)


def _gate_and_store(gate_ref, za_ref, o_ref, o_cmp, o_sel, o_win, rows):
    gs = _sigmoid(gate_ref[...])
    lane = lax.broadcasted_iota(jnp.int32, (rows, LANES), 1)
    for r in range(NSA_REP):
        parts = []
        for g in range(NSA_KV):
            rb = 2 * r + g
            h = _rowblk_head(rb)
            o = (gs[:, h:h + 1] * o_cmp[rb] + gs[:, NSA_HEADS + h:NSA_HEADS + h + 1] * o_sel[rb]
                 + gs[:, 2 * NSA_HEADS + h:2 * NSA_HEADS + h + 1] * o_win[rb])
            parts.append(o)
        blk = jnp.where(lane < NSA_HD, parts[0], parts[1])
        o_ref[:, r * LANES:(r + 1) * LANES] = blk * _silu(za_ref[:, r * LANES:(r + 1) * LANES])


SEL_TILES_PER_ITER = 2


def _nsa_prompt_body(q_ref, gate_ref, za_ref, ks_ref, vs_ref, kw_ref, vw_ref, kcmp_ref, vcmp_ref,
                     bcmp_ref, btile_ref, covert_ref, o_ref,
                     qt_ref, ksb_ref, vst_ref, kwb_ref, vwt_ref, vcmpt_ref, mask_ref, s_ref, m_ref, l_ref, acc_ref, res_ref,
                     *, nc, nb):
    qb = pl.program_id(1)
    nkt = ksb_ref.shape[0]
    sub = lax.broadcasted_iota(jnp.int32, (LANES, TQ), 0)
    tok = lax.broadcasted_iota(jnp.int32, (LANES, TQ), 1)

    @pl.when(qb == 0)
    def _():
        for kt in range(nkt):
            rows = slice(kt * TQ, (kt + 1) * TQ)
            ksb_ref[kt] = ks_ref[rows, :].astype(BF16)
            kwb_ref[kt] = kw_ref[rows, :].astype(BF16)
            vst_ref[kt] = vs_ref[rows, :].T.astype(BF16)
            vwt_ref[kt] = vw_ref[rows, :].T.astype(BF16)
        vcmpt_ref[...] = vcmp_ref[...].T.astype(BF16)

    for r in range(NSA_REP):
        qt = (q_ref[:, r * LANES:(r + 1) * LANES] * (NSA_HD ** -0.5)).T
        qt_ref[2 * r] = jnp.where(sub < NSA_HD, qt, 0.0).astype(BF16)
        qt_ref[2 * r + 1] = jnp.where(sub >= NSA_HD, qt, 0.0).astype(BF16)

    valid = ((CMP_STRIDE * sub + CMP_LEN - 1) <= qb * TQ + tok) & (sub < nc)
    kcb = kcmp_ref[...].astype(BF16)
    psum = [None] * NSA_KV
    for rb in range(N_ROWBLK):
        s = jnp.where(valid, _nn(kcb, qt_ref[rb]) + bcmp_ref[rb], NEG)
        e = jnp.exp(s - jnp.max(s, axis=0, keepdims=True))
        p = jnp.where(valid, e / jnp.sum(e, axis=0, keepdims=True), 0.0)
        res_ref[0, rb] = _nn(vcmpt_ref[...], p.astype(BF16))
        g = rb % NSA_KV
        psum[g] = p if psum[g] is None else psum[g] + p

    nbp = _round_up(nb, SUBLANES)
    upper = sub < SEL_BLOCK
    for g in range(NSA_KV):
        p_hi, p_lo = _split2(psum[g])
        imp = _nn(covert_ref[...], p_hi) + _nn(covert_ref[...], p_lo)
        sel = _select_blocks(imp[:nbp], qb * TQ + tok[:1], nb, 0)
        for kt in range(nkt):
            picked = jnp.where(upper, sel[2 * kt:2 * kt + 1, :], sel[2 * kt + 1:2 * kt + 2, :]) > 0.5
            mask_ref[kt, g] = jnp.where(kt <= qb, jnp.where(picked, 0.0, NEG), NEG)

    m_ref[...] = jnp.full(m_ref.shape, NEG, F32)
    l_ref[...] = jnp.zeros(l_ref.shape, F32)
    acc_ref[...] = jnp.zeros(acc_ref.shape, F32)

    def sel_scores(pair, buf):
        for u in range(SEL_TILES_PER_ITER):
            kt = jnp.minimum(pair * SEL_TILES_PER_ITER + u, nkt - 1)
            k_t = ksb_ref[kt]
            bidx = jnp.clip(qb - kt, 0, 2)
            for rb in range(N_ROWBLK):
                s_ref[buf, u, rb] = _nn(k_t, qt_ref[rb]) + btile_ref[bidx, rb] + mask_ref[kt, rb % NSA_KV]

    def sel_update(pair, buf):
        v_ts = [vst_ref[pair * SEL_TILES_PER_ITER + u] for u in range(SEL_TILES_PER_ITER)]
        for rb in range(N_ROWBLK):
            ss = [s_ref[buf, u, rb] for u in range(SEL_TILES_PER_ITER)]
            m_old = m_ref[rb]
            m_new = m_old
            for s in ss:
                m_new = jnp.maximum(m_new, jnp.max(s, axis=0, keepdims=True))
            alpha = jnp.exp(m_old - m_new)
            l_new = alpha * l_ref[rb]
            acc = alpha * acc_ref[rb]
            for s, v_t in zip(ss, v_ts):
                p = jnp.exp(s - m_new)
                l_new = l_new + jnp.sum(p, axis=0, keepdims=True)
                acc = acc + _nn(v_t, p.astype(BF16))
            l_ref[rb] = l_new
            acc_ref[rb] = acc
            m_ref[rb] = m_new

    def sel_body(it, carry):
        sel_scores(2 * it + 1, 1)
        sel_update(2 * it, 0)
        sel_scores(2 * it + 2, 0)
        sel_update(2 * it + 1, 1)
        return carry

    sel_scores(0, 0)
    lax.fori_loop(0, qb // (2 * SEL_TILES_PER_ITER) + 1, sel_body, 0)
    res_ref[1] = acc_ref[...] / l_ref[...]

    n_wt = WINDOW // TQ + 1
    for rb in range(N_ROWBLK):
        qc = qt_ref[rb]
        ss = []
        for dist in range(n_wt):
            kt = jnp.maximum(qb - dist, 0)
            bidx = jnp.where(qb >= dist, dist, n_wt)
            ss.append(_nn(kwb_ref[kt], qc) + btile_ref[bidx, rb])
        m = jnp.max(ss[0], axis=0, keepdims=True)
        for s in ss[1:]:
            m = jnp.maximum(m, jnp.max(s, axis=0, keepdims=True))
        l_w = None
        acc = None
        for dist, s in enumerate(ss):
            p = jnp.exp(s - m)
            ps = jnp.sum(p, axis=0, keepdims=True)
            pv = _nn(vwt_ref[jnp.maximum(qb - dist, 0)], p.astype(BF16))
            l_w = ps if l_w is None else l_w + ps
            acc = pv if acc is None else acc + pv
        res_ref[2, rb] = acc / l_w

    gt = _sigmoid(gate_ref[...]).T
    for r in range(NSA_REP):
        parts = []
        for g in range(NSA_KV):
            rb = 2 * r + g
            h = _rowblk_head(rb)
            parts.append(gt[h:h + 1, :] * res_ref[0, rb]
                         + gt[NSA_HEADS + h:NSA_HEADS + h + 1, :] * res_ref[1, rb]
                         + gt[2 * NSA_HEADS + h:2 * NSA_HEADS + h + 1, :] * res_ref[2, rb])
        blk = jnp.where(sub < NSA_HD, parts[0], parts[1]).T
        o_ref[:, r * LANES:(r + 1) * LANES] = (blk * _silu(za_ref[:, r * LANES:(r + 1) * LANES])).astype(o_ref.dtype)


def _cover_matrix(nc, nb):
    cs = np.arange(nc) * CMP_STRIDE
    ce = cs + CMP_LEN
    bs = np.arange(nb) * SEL_BLOCK
    be = bs + SEL_BLOCK
    ov = np.clip(np.minimum(ce[:, None], be[None, :]) - np.maximum(cs[:, None], bs[None, :]), 0, None) / CMP_STRIDE
    out = np.zeros((LANES, LANES), np.float32)
    out[:nc, :nb] = ov
    return jnp.asarray(out, BF16)


def _expand_matrix(nkeys):
    j = np.arange(LANES)[:, None]
    k = np.arange(nkeys)[None, :]
    return jnp.asarray((k // SEL_BLOCK == j).astype(np.float32), BF16)


def _nsa_prompt(proj3, kcmp, vcmp, bias_cmp, bias_tiles):
    b, t, _ = proj3.shape
    nq = t // TQ
    nc = (t - CMP_LEN) // CMP_STRIDE + 1
    nb = t // SEL_BLOCK
    assert nc <= LANES and nb <= LANES and t % (TQ * 2 * SEL_TILES_PER_ITER) == 0
    kv_col = C_KV // LANES
    kv_spec = lambda c: pl.BlockSpec((None, t, LANES), lambda i, j: (i, 0, kv_col + c))
    cover_t = _cover_matrix(nc, nb).T
    tile_bf16 = pltpu.VMEM((nq, TQ, LANES), BF16)
    return pl.pallas_call(
        functools.partial(_nsa_prompt_body, nc=nc, nb=nb),
        grid=(b, nq),
        in_specs=[pl.BlockSpec((None, TQ, A_WIDTH), lambda i, j: (i, j, C_Q // A_WIDTH)),
                  pl.BlockSpec((None, TQ, LANES), lambda i, j: (i, j, C_GATE // LANES)),
                  pl.BlockSpec((None, TQ, A_WIDTH), lambda i, j: (i, j, C_ZA // A_WIDTH)),
                  kv_spec(2), kv_spec(3), kv_spec(4), kv_spec(5),
                  pl.BlockSpec((None, LANES, LANES), lambda i, j: (i, 0, 0)),
                  pl.BlockSpec((None, LANES, LANES), lambda i, j: (i, 0, 0)),
                  pl.BlockSpec((None, N_ROWBLK, LANES, TQ), lambda i, j: (j, 0, 0, 0)),
                  _const_spec(bias_tiles.shape), _const_spec(cover_t.shape)],
        out_specs=pl.BlockSpec((None, TQ, A_WIDTH), lambda i, j: (i, j, 0)),
        out_shape=jax.ShapeDtypeStruct((b, t, A_WIDTH), BF16),
        scratch_shapes=[pltpu.VMEM((N_ROWBLK, LANES, TQ), BF16),
                        tile_bf16, tile_bf16, tile_bf16, tile_bf16,
                        pltpu.VMEM((LANES, LANES), BF16),
                        pltpu.VMEM((nq, NSA_KV, TQ, TQ), F32),
                        pltpu.VMEM((2, SEL_TILES_PER_ITER, N_ROWBLK, TQ, TQ), F32),
                        pltpu.VMEM((N_ROWBLK, 1, TQ), F32),
                        pltpu.VMEM((N_ROWBLK, 1, TQ), F32),
                        pltpu.VMEM((N_ROWBLK, LANES, TQ), F32),
                        pltpu.VMEM((3, N_ROWBLK, LANES, TQ), F32)],
        compiler_params=_cparams(2),
        name="nsa_prompt",
    )(proj3, proj3, proj3, proj3, proj3, proj3, proj3, kcmp, vcmp, bias_cmp, bias_tiles, cover_t)


N_CACHES = 4


NSA_SEQ_PER_STEP = 4


def _group_page_copy(cache_refs, pt_ref, pbuf, sems, step, slot, s, c, pg, n_pages):
    seq = step * NSA_SEQ_PER_STEP + s
    return pltpu.make_async_copy(cache_refs[c].at[pt_ref[seq, pg]], pbuf.at[slot, s, c * n_pages + pg],
                                 sems.at[slot])


def _nsa_sample_group_body(pt_ref, *refs, n_pages, **static):
    cache_refs = refs[:N_CACHES]
    per_seq = refs[N_CACHES:N_CACHES + 11]
    consts = refs[N_CACHES + 11:N_CACHES + 22]
    outs = refs[N_CACHES + 22:N_CACHES + 25]
    scratch = refs[N_CACHES + 25:-2]
    pbuf, sems = refs[-2:]
    i = pl.program_id(0)
    slot = i % 2

    def all_pages(step, slot_):
        return [_group_page_copy(cache_refs, pt_ref, pbuf, sems, step, slot_, s, c, pg, n_pages)
                for s in range(NSA_SEQ_PER_STEP) for c in range(N_CACHES) for pg in range(n_pages)]

    @pl.when(i == 0)
    def _():
        for cp in all_pages(0, 0):
            cp.start()

    @pl.when(i + 1 < pl.num_programs(0))
    def _():
        for cp in all_pages(i + 1, 1 - slot):
            cp.start()

    for cp in all_pages(i, slot):
        cp.wait()
    pages = [[[pbuf.at[slot, s, c * n_pages + pg] for pg in range(n_pages)] for c in range(N_CACHES)]
             for s in range(NSA_SEQ_PER_STEP)]
    _nsa_sample_group(pages, per_seq, consts, outs, scratch, n_pages=n_pages, **static)


def _nsa_sample_group(pages, per_seq, consts, outs, scratch, *, n_pages, past, wbuf, nc, nb):
    q_ref, kc_ref, vc_ref, ks_ref, vs_ref, kw_ref, vw_ref, gate_ref, za_ref, wk_ref, wv_ref = per_seq
    (posk_ref, w1k_ref, w2k_ref, posv_ref, w1v_ref, w2v_ref, bcmp_ref, bsel_ref, bwin_ref,
     cover_ref, emat_ref) = consts
    o_ref, swk_ref, swv_ref = outs
    kc_all, vc_all, newt_ref, qbd_ref = scratch
    nseq = NSA_SEQ_PER_STEP
    seqs = range(nseq)
    n_keys = bsel_ref.shape[2]
    n_new = T_PAD // 2
    chunks_per_page = PAGE_SIZE // CMP_STRIDE
    nrow = N_ROWBLK * T_PAD

    for dst, cache, new_ref in ((kc_all, 0, kc_ref), (vc_all, 1, vc_ref)):
        for s in seqs:
            for pg in range(n_pages):
                rows = pages[s][cache][pg][...].T
                for c in range(chunks_per_page):
                    r0 = (pg * chunks_per_page + c) * CMP_PITCH
                    dst[s, r0:r0 + CMP_STRIDE, :] = rows[c * CMP_STRIDE:(c + 1) * CMP_STRIDE, :]
            r0 = (past // CMP_STRIDE) * CMP_PITCH
            dst[s, r0:r0 + T_PAD, :] = new_ref[s]
            dst[s, r0 + T_PAD:, :] = jnp.zeros((dst.shape[1] - r0 - T_PAD, LANES), F32)
    kcmp, vcmp = [c.astype(BF16) for c in _compress(
        [([kc_all.at[s] for s in seqs], posk_ref, w1k_ref, w2k_ref),
         ([vc_all.at[s] for s in seqs], posv_ref, w1v_ref, w2v_ref)], CMP_PITCH)]
    per_s = lambda x, s: x[s * LANES:(s + 1) * LANES]

    newt_ref[...] = jnp.zeros(newt_ref.shape, F32)
    for s in seqs:
        for idx, new_ref in enumerate((ks_ref, vs_ref, kw_ref, vw_ref)):
            newt_ref[s, idx, 0:T_PAD, :] = new_ref[s]
        _stack_queries(q_ref.at[s], qbd_ref.at[s], T_PAD)
    qbd = [qbd_ref[s] for s in seqs]
    tok = lax.broadcasted_iota(jnp.int32, (T_PAD, LANES), 0)
    lane = lax.broadcasted_iota(jnp.int32, (T_PAD, LANES), 1)
    qpos = past + tok
    stack = lambda xs: jnp.concatenate(xs, axis=0)
    by_head = lambda x: x.reshape(nseq, NSA_REP, NSA_KV, T_PAD, LANES)

    s_c = by_head(stack([_nt(qbd[s], per_s(kcmp, s)) for s in seqs])) + _by_head_bias(bcmp_ref[...])
    mask_c = ((CMP_STRIDE * lane + CMP_LEN - 1) <= qpos) & (lane < nc)
    p = _masked_softmax(s_c, mask_c[None, None, None]).reshape(nseq, nrow, LANES)
    o_cmp = [_nn(p[s].astype(BF16), per_s(vcmp, s)).reshape(N_ROWBLK, T_PAD, LANES) for s in seqs]

    psum = p.reshape(nseq, NSA_REP, NSA_KV * T_PAD, LANES).sum(axis=1).reshape(nseq * NSA_KV * T_PAD, LANES)
    p_hi, p_lo = _split2(psum)
    imp = _nn(p_hi, cover_ref[...]) + _nn(p_lo, cover_ref[...])
    sel = _select_blocks(imp.reshape(nseq, NSA_KV, T_PAD, LANES), qpos[None, None], nb, 3)
    selk = _nn(sel.reshape(nseq * NSA_KV * T_PAD, LANES).astype(BF16), emat_ref[...])
    selk = selk.reshape(nseq, NSA_KV, T_PAD, n_keys)

    def softmax_pv(scores, values):
        m = scores[0]
        for s_t in scores[1:]:
            m = jnp.maximum(m, s_t)
        m = jnp.max(m, axis=-1, keepdims=True)
        acc = [None] * nseq
        tot = None
        for s_t, (v_ts, channel_major) in zip(scores, values):
            p_t = jnp.exp(s_t - m)
            pb = p_t.reshape(nseq, nrow, LANES).astype(BF16)
            for s in seqs:
                pv = _nt(pb[s], v_ts[s]) if channel_major else _nn(pb[s], v_ts[s])
                acc[s] = pv if acc[s] is None else acc[s] + pv
            tot = p_t if tot is None else tot + p_t
        inv = 1.0 / jnp.sum(tot, axis=-1, keepdims=True).reshape(nseq, nrow, 1)
        return [(acc[s] * inv[s]).reshape(N_ROWBLK, T_PAD, LANES) for s in seqs]

    scores, values = [], []
    for t in range(n_pages + 1):
        cols = slice(t * LANES, (t + 1) * LANES)
        allowed = ((t * LANES + lane) <= qpos)[None, None] & (selk[:, :, :, cols] > 0.5)
        madd = jnp.where(allowed, 0.0, NEG)[:, None]
        if t < n_pages:
            raw = [_nn(qbd[s], pages[s][2][t][...].astype(BF16)) for s in seqs]
            values.append(([pages[s][3][t][...].astype(BF16) for s in seqs], True))
        else:
            raw = [_nt(qbd[s], newt_ref[s, 0].astype(BF16)) for s in seqs]
            values.append(([newt_ref[s, 1].astype(BF16) for s in seqs], False))
        scores.append(by_head(stack(raw)) + _by_head_bias(bsel_ref[:, :, cols]) + madd)
    o_sel = softmax_pv(scores, values)

    scores, values = [], []
    kwt = [wk_ref[s].astype(BF16) for s in seqs]
    vwt = [wv_ref[s].astype(BF16) for s in seqs]
    for t in range(wbuf // LANES + 1):
        cols = slice(t * LANES, (t + 1) * LANES)
        rel_w = wbuf + tok - (t * LANES + lane)
        madd = jnp.where((rel_w >= 0) & (rel_w < WINDOW), 0.0, NEG)[None, None, None]
        if t < wbuf // LANES:
            raw = [_nn(qbd[s], kwt[s][:, cols]) for s in seqs]
            values.append(([vwt[s][:, cols] for s in seqs], True))
        else:
            raw = [_nt(qbd[s], newt_ref[s, 2].astype(BF16)) for s in seqs]
            values.append(([newt_ref[s, 3].astype(BF16) for s in seqs], False))
        scores.append(by_head(stack(raw)) + _by_head_bias(bwin_ref[:, :, cols]) + madd)
    o_win = softmax_pv(scores, values)

    lane_full = lax.broadcasted_iota(jnp.int32, (LANES, LANES), 1)
    for s in seqs:
        _gate_and_store(gate_ref.at[s], za_ref.at[s], o_ref.at[s], o_cmp[s], o_sel[s], o_win[s], T_PAD)
        for idx, w_ref, out_ref in ((2, wk_ref, swk_ref), (3, wv_ref, swv_ref)):
            rolled = pltpu.roll(w_ref[s], wbuf - n_new, axis=1)
            tail = pltpu.roll(newt_ref[s, idx], LANES - n_new, axis=0).T
            out_ref[s, :, :wbuf - LANES] = rolled[:, :wbuf - LANES]
            out_ref[s, :, wbuf - LANES:] = jnp.where(lane_full >= LANES - n_new, tail, rolled[:, wbuf - LANES:])


def _by_head_bias(b):
    return b.reshape(1, NSA_REP, NSA_KV, T_PAD, b.shape[-1])


def _round_up(x, m):
    return (x + m - 1) // m * m


def _nsa_sample(projs3, caches, wk, wv, page_table, cw_k, cw_v, rel_bias, n_new):
    nseq, n_pages = page_table.shape
    past = n_pages * PAGE_SIZE
    wbuf = wk.shape[2]
    length = past + n_new
    nc = (length - CMP_LEN) // CMP_STRIDE + 1
    nb = -(-length // SEL_BLOCK)
    n_keys = past + LANES
    n_win = wbuf + LANES
    pitched = lambda p: (p // CMP_STRIDE) * CMP_PITCH + p % CMP_STRIDE
    n_cmp_rows = _round_up(max(pitched(past) + T_PAD, pitched(CMP_SRC_ROWS - 1) + 1), SUBLANES)
    assert nc <= LANES and nb <= LANES and n_new == T_PAD // 2 and nb * SEL_BLOCK <= n_keys
    assert wbuf % LANES == 0 and PAGE_SIZE == LANES

    bias_cmp = _bias_table_sample(rel_bias, past - (CMP_LEN - 1), CMP_STRIDE, LANES)
    bias_sel = _bias_table_sample(rel_bias, past, 1, n_keys)
    bias_win = _bias_table_sample(rel_bias, wbuf, 1, n_win)
    cover = _cover_matrix(nc, nb)
    emat = _expand_matrix(n_keys)

    sps = NSA_SEQ_PER_STEP
    assert nseq % sps == 0

    def new_spec(width, col):
        return pl.BlockSpec((sps, T_PAD, width), lambda i, pt: (i, 0, col))

    kv_col = C_KV // LANES
    in_specs = [pl.BlockSpec(memory_space=pl.ANY)] * N_CACHES
    in_specs += [new_spec(A_WIDTH, C_Q // A_WIDTH)]
    in_specs += [new_spec(LANES, kv_col + c) for c in range(6)]
    in_specs += [new_spec(LANES, C_GATE // LANES), new_spec(A_WIDTH, C_ZA // A_WIDTH)]
    win_spec = pl.BlockSpec((sps, LANES, wbuf), lambda i, pt: (i, 0, 0))
    in_specs += [win_spec] * 2
    consts = list(cw_k) + list(cw_v) + [bias_cmp, bias_sel, bias_win, cover, emat]
    in_specs += [_const_spec(a.shape, single_buffer=True) for a in consts]
    operands = list(caches) + [projs3] * 9 + [wk, wv] + consts

    grid_spec = pltpu.PrefetchScalarGridSpec(
        num_scalar_prefetch=1,
        grid=(nseq // sps,),
        in_specs=in_specs,
        out_specs=[pl.BlockSpec((sps, T_PAD, A_WIDTH), lambda i, pt: (i, 0, 0)), win_spec, win_spec],
        scratch_shapes=[pltpu.VMEM((sps, n_cmp_rows, LANES), F32), pltpu.VMEM((sps, n_cmp_rows, LANES), F32),
                        pltpu.VMEM((sps, 4, LANES, LANES), F32),
                        pltpu.VMEM((sps, N_ROWBLK * T_PAD, LANES), BF16),
                        pltpu.VMEM((2, sps, N_CACHES * n_pages, LANES, PAGE_SIZE), F32),
                        pltpu.SemaphoreType.DMA((2,))])
    return pl.pallas_call(
        functools.partial(_nsa_sample_group_body, n_pages=n_pages, past=past, wbuf=wbuf, nc=nc, nb=nb),
        grid_spec=grid_spec,
        out_shape=[jax.ShapeDtypeStruct((nseq, T_PAD, A_WIDTH), F32),
                   jax.ShapeDtypeStruct((nseq, LANES, wbuf), F32),
                   jax.ShapeDtypeStruct((nseq, LANES, wbuf), F32)],
        compiler_params=_cparams(1),
        name="nsa_sample",
    )(page_table, *operands)


def _hgrn_constants(rows, seg_stride, seg_len):
    idx = np.arange(rows)
    seg = idx // seg_stride
    pos = idx % seg_stride
    same = seg[:, None] == seg[None, :]
    pt, pu = pos[:, None], pos[None, :]
    mats = [same & (pu <= pt),
            same & (pu > pt) & (pu <= seg_len - 1)]
    levels = []
    m = seg_len // 2
    while m >= 2:
        levels.append(m)
        m //= 2
    for m in levels:
        mats.append(same & (pu >= (pt // m) * m) & (pu <= pt))
        mats.append(same & (pu > pt) & (pu <= (pt // m) * m + m - 1))
    wstack = np.concatenate(mats, axis=0).astype(np.float32)
    wstack = np.concatenate([wstack] * 3, axis=1)
    masks = []
    for m in levels + [1]:
        masks.append(same & (pt // (2 * m) == pu // (2 * m)) & ((pt // m) % 2 == 1) & ((pu // m) % 2 == 0))
    masks.append(idx[:, None] == idx[None, :])
    masks = np.stack(masks).astype(np.float32)
    return jnp.asarray(wstack, BF16), jnp.asarray(masks, F32), len(levels)


def _hgrn_chunk(q, f_pre, hl_ref, wst_ref, n_levels):
    rows = q.shape[0]
    hl = hl_ref[...]
    e = jnp.exp(hl - jnp.max(hl, axis=0, keepdims=True))
    lb = e[0:1] / jnp.sum(e, axis=0, keepdims=True)
    f = lb + (1.0 - lb) * _sigmoid(f_pre)
    g = jnp.log(f)
    k = 1.0 - f
    g_hi = g.astype(BF16)
    r1 = g - g_hi.astype(F32)
    g_mid = r1.astype(BF16)
    g_lo = (r1 - g_mid.astype(F32)).astype(BF16)
    sums = _nn(wst_ref[...], jnp.concatenate([g_hi, g_mid, g_lo], axis=0))
    big_g = sums[0:rows]
    qg = q * jnp.exp(big_g)
    kd = k * jnp.exp(sums[rows:2 * rows])
    q_lv, k_lv = [], []
    for li in range(n_levels):
        q_lv.append(q * jnp.exp(sums[(2 + 2 * li) * rows:(3 + 2 * li) * rows]))
        k_lv.append(k * jnp.exp(sums[(3 + 2 * li) * rows:(4 + 2 * li) * rows]))
    q_lv += [q * f, q]
    k_lv += [k, k]
    return big_g, qg, kd, q_lv, k_lv


def _hgrn_intra(q_lv, k_lv, masks_ref, hs):
    a = None
    for li in range(len(q_lv)):
        term = masks_ref[li] * _nt(q_lv[li][:, hs].astype(BF16), k_lv[li][:, hs].astype(BF16))
        a = term if a is None else a + term
    return a


def _hgrn_out(o, ng, z):
    o = o * lax.rsqrt(jnp.mean(o * o, axis=-1, keepdims=True) + EPS)
    return o * ng * _silu(z)


def _hgrn_prompt_body(q_ref, f_ref, i_ref, z_ref, hl_ref, ng_ref, wst_ref, masks_ref, o_ref, s_ref, st_ref,
                      *, n_levels, rows):
    c = pl.program_id(1)

    @pl.when(c == 0)
    def _():
        st_ref[...] = jnp.zeros(st_ref.shape, F32)

    chunks = []
    for ci in range(q_ref.shape[0] // rows):
        rs = slice(ci * rows, (ci + 1) * rows)
        v = i_ref[rs, :]
        big_g, qg, kd, q_lv, k_lv = _hgrn_chunk(q_ref[rs, :], f_ref[rs, :], hl_ref, wst_ref, n_levels)
        per_head = []
        for h in range(HG_HEADS):
            hs = slice(h * HG_DK, (h + 1) * HG_DK)
            vb = v[:, hs].astype(BF16)
            a = _hgrn_intra(q_lv, k_lv, masks_ref, hs)
            per_head.append((qg[:, hs].astype(BF16), _nn(a.astype(BF16), vb),
                             jnp.exp(big_g[rows - 1:rows, hs]), _tn(vb, kd[:, hs].astype(BF16))))
        chunks.append(per_head)

    for h in range(HG_HEADS):
        hs = slice(h * HG_DK, (h + 1) * HG_DK)
        st = st_ref[h]
        for ci, per_head in enumerate(chunks):
            rs = slice(ci * rows, (ci + 1) * rows)
            qgb, o_intra, decay, kv = per_head[h]
            o = _nt(qgb, st.astype(BF16)) + o_intra
            o_ref[rs, hs] = _hgrn_out(o, ng_ref[:, hs], z_ref[rs, hs]).astype(o_ref.dtype)
            st = st * decay + kv
        st_ref[h] = st

        @pl.when(c == pl.num_programs(1) - 1)
        def _():
            s_ref[h] = st.T


HGRN_CHUNKS_PER_STEP = 16


def _hgrn_prompt(proj3, hg_lower, hg_norm_g):
    b, t, _ = proj3.shape
    chunk = math.gcd(t, HG_CHUNK)
    wst, masks, n_levels = _hgrn_constants(chunk, chunk, chunk)
    rows = chunk * math.gcd(t // chunk, HGRN_CHUNKS_PER_STEP)
    col = lambda c: pl.BlockSpec((None, rows, B_KEY), lambda i, j: (i, j, c // B_KEY))
    return pl.pallas_call(
        functools.partial(_hgrn_prompt_body, n_levels=n_levels, rows=chunk),
        grid=(b, t // rows),
        in_specs=[col(C_QB), col(C_FB), col(C_IB), col(C_ZB),
                  _const_spec(hg_lower.shape), _const_spec(hg_norm_g.shape),
                  _const_spec(wst.shape), _const_spec(masks.shape)],
        out_specs=[pl.BlockSpec((None, rows, B_WIDTH), lambda i, j: (i, j, 0)),
                   pl.BlockSpec((None, HG_HEADS, HG_DK, HG_DV), lambda i, j: (i, 0, 0, 0))],
        out_shape=[jax.ShapeDtypeStruct((b, t, B_WIDTH), BF16),
                   jax.ShapeDtypeStruct((b, HG_HEADS, HG_DK, HG_DV), F32)],
        scratch_shapes=[pltpu.VMEM((HG_HEADS, HG_DV, HG_DK), F32)],
        compiler_params=_cparams(2),
        name="hgrn_prompt",
    )(proj3, proj3, proj3, proj3, hg_lower, hg_norm_g, wst, masks)


SEQ_PER_STEP = 8


def _hgrn_sample_body(q_ref, f_ref, i_ref, z_ref, s0_ref, hl_ref, ng_ref, wst_ref, masks_ref, o_ref, s_ref,
                      *, n_levels, n_new):
    v = i_ref[...]
    big_g, qg, kd, q_lv, k_lv = _hgrn_chunk(q_ref[...], f_ref[...], hl_ref, wst_ref, n_levels)
    for h in range(HG_HEADS):
        hs = slice(h * HG_DK, (h + 1) * HG_DK)
        vb = v[:, hs].astype(BF16)
        a = _hgrn_intra(q_lv, k_lv, masks_ref, hs)
        o_intra = _nn(a.astype(BF16), vb)
        for s in range(SEQ_PER_STEP):
            rs = slice(s * T_PAD, (s + 1) * T_PAD)
            s0 = s0_ref[s, h]
            o = _nn(qg[rs, hs].astype(BF16), s0.astype(BF16)) + o_intra[rs]
            o_ref[rs, hs] = _hgrn_out(o, ng_ref[:, hs], z_ref[rs, hs])
            decay = jnp.exp(big_g[rs, hs]).T[:, n_new - 1:n_new]
            s_ref[s, h] = s0 * decay + _tn(kd[rs, hs].astype(BF16), v[rs, hs].astype(BF16))


def _hgrn_sample(projs, state, hg_lower, hg_norm_g, n_new):
    nseq = state.shape[0]
    rows = SEQ_PER_STEP * T_PAD
    wst, masks, n_levels = _hgrn_constants(rows, T_PAD, n_new)
    col = lambda c: pl.BlockSpec((rows, B_KEY), lambda i: (i, c // B_KEY))
    st_spec = pl.BlockSpec((SEQ_PER_STEP, HG_HEADS, HG_DK, HG_DV), lambda i: (i, 0, 0, 0))
    return pl.pallas_call(
        functools.partial(_hgrn_sample_body, n_levels=n_levels, n_new=n_new),
        grid=(nseq // SEQ_PER_STEP,),
        in_specs=[col(C_QB), col(C_FB), col(C_IB), col(C_ZB), st_spec,
                  _const_spec(hg_lower.shape), _const_spec(hg_norm_g.shape),
                  _const_spec(wst.shape), _const_spec(masks.shape)],
        out_specs=[pl.BlockSpec((rows, B_WIDTH), lambda i: (i, 0)), st_spec],
        out_shape=[jax.ShapeDtypeStruct((nseq * T_PAD, B_WIDTH), F32),
                   jax.ShapeDtypeStruct(state.shape, F32)],
        compiler_params=_cparams(1),
        name="hgrn_sample",
    )(projs, projs, projs, projs, state, hg_lower, hg_norm_g, wst, masks)


def _merge_body(oa_ref, ob_ref, ma_ref, mb_ref, x_ref, wba_ref, wbb_ref, wout_ref, fg_ref, y_ref):
    a = _nn(oa_ref[...].astype(BF16), wba_ref[...])
    b = _nn(ob_ref[...].astype(BF16), wbb_ref[...])
    y = _sigmoid(ma_ref[...]) * a + _sigmoid(mb_ref[...]) * b
    h = x_ref[...] + _nn(y.astype(BF16), wout_ref[...])
    ms = jnp.mean(h * h, axis=-1, keepdims=True)
    y_ref[...] = h * lax.rsqrt(ms + EPS) * fg_ref[...]


def _merge(o_a, o_b, proj, x2d, w_ba, w_bb, w_out, final_g):
    m, d = x2d.shape
    tm = min(m, 512)
    row = lambda w, c: pl.BlockSpec((tm, w), lambda i: (i, c))
    return pl.pallas_call(
        _merge_body,
        grid=(m // tm,),
        in_specs=[row(A_WIDTH, 0), row(B_WIDTH, 0), row(d, C_MA // d), row(d, C_MA // d + 1), row(d, 0),
                  _const_spec(w_ba.shape), _const_spec(w_bb.shape), _const_spec(w_out.shape),
                  _const_spec((1, d))],
        out_specs=row(d, 0),
        out_shape=jax.ShapeDtypeStruct((m, d), F32),
        compiler_params=_cparams(1),
        name="merge_out",
    )(o_a, o_b, proj, proj, x2d, w_ba, w_bb, w_out, final_g.reshape(1, d))


def kernel(x_prompt, x_sample, cache_cmp_k, cache_cmp_v, cache_sel_k, cache_sel_v, state_win_k, state_win_v,
           state_hgrn, page_table, norm_g, w_in, cmp_pos_k, cmp_w1_k, cmp_w2_k, cmp_pos_v, cmp_w1_v, cmp_w2_v,
           rel_bias, hg_lower, hg_norm_g, w_branch_a, w_branch_b, w_out, final_g):
    depth = norm_g.shape[0]
    assert depth == 1, "single-layer trunk"
    b, t, d = x_prompt.shape
    nseq, n_new, _ = x_sample.shape
    assert C_MA % d == 0

    w = _arrange_w_in(w_in[0])
    n_cols = w.shape[0]
    w_ba = _permute_heads_rows(w_branch_a[0]).astype(BF16)
    w_bb = w_branch_b[0].astype(BF16)
    w_o = w_out[0].astype(BF16)
    cw_k = _compress_weights(cmp_pos_k[0], cmp_w1_k[0], cmp_w2_k[0])
    cw_v = _compress_weights(cmp_pos_v[0], cmp_w1_v[0], cmp_w2_v[0])
    ng = hg_norm_g[0].reshape(1, B_WIDTH)
    kv_cols = lambda a, c: a[..., C_KV + c * A_KV:C_KV + (c + 1) * A_KV]

    xp2 = x_prompt.reshape(b * t, d)
    proj, *kv_t = _project(xp2, norm_g[0], w, seq_len=t)
    proj3 = proj.reshape(b, t, n_cols)
    nc_p = (t - CMP_LEN) // CMP_STRIDE + 1
    kcmp, vcmp = _compress_prompt(proj3, cw_k, cw_v)
    bias_cmp = _bias_table(rel_bias, base=-(CMP_LEN - 1), step=TQ, row_stride=-CMP_STRIDE, col_stride=1,
                           rows=LANES, cols=TQ, steps=t // TQ)
    bias_tiles = _bias_table(rel_bias, base=0, step=TQ, row_stride=-1, col_stride=1, rows=TQ, cols=TQ,
                             steps=WINDOW // TQ + 2, lo=0, hi=WINDOW)
    o_a = _nsa_prompt(proj3, kcmp, vcmp, bias_cmp, bias_tiles)
    o_b, p_hgrn = _hgrn_prompt(proj3, hg_lower, ng)
    y_prompt = _merge(o_a.reshape(b * t, A_WIDTH), o_b.reshape(b * t, B_WIDTH), proj, xp2,
                      w_ba, w_bb, w_o, final_g).reshape(b, t, d)
    wb_p = min(WINDOW, t)
    chan_major = lambda a: jnp.transpose(a, (0, 2, 3, 1)).reshape(a.shape[0], A_KV, a.shape[1])
    row_major = lambda a: jnp.transpose(a.reshape(a.shape[0], NSA_KV, NSA_HD, a.shape[2]), (0, 3, 1, 2))
    p_kv = ([row_major(kv_t[c])[None] for c in range(4)]
            + [row_major(kv_t[c][:, :, t - wb_p:])[None] for c in (4, 5)])
    p_state = p_kv + [p_hgrn[None]]

    xs_pad = jnp.pad(x_sample, ((0, 0), (0, T_PAD - n_new), (0, 0))).reshape(nseq * T_PAD, d)
    projs = _project(xs_pad, norm_g[0], w)
    projs3 = projs.reshape(nseq, T_PAD, n_cols)
    pool = cache_cmp_k.shape[1]
    caches =[chan_major(c[0]) for c in (cache_cmp_k, cache_cmp_v, cache_sel_k, cache_sel_v)]
    wbuf = state_win_k.shape[2]
    o_as, s_win_k, s_win_v = _nsa_sample(projs3, caches, chan_major(state_win_k[0]), chan_major(state_win_v[0]),
                                         page_table, cw_k, cw_v, rel_bias, n_new)
    s_win_k, s_win_v = row_major(s_win_k), row_major(s_win_v)
    o_bs, s_hgrn = _hgrn_sample(projs, state_hgrn[0], hg_lower, ng, n_new)
    y_s = _merge(o_as.reshape(nseq * T_PAD, A_WIDTH), o_bs, projs, xs_pad, w_ba, w_bb, w_o, final_g)
    y_sample = y_s.reshape(nseq, T_PAD, d)[:, :n_new]
    s_kv = [kv_cols(projs3, c)[:, :n_new].reshape(1, nseq, n_new, NSA_KV, NSA_HD) for c in range(4)]
    s_state = s_kv + [s_win_k.reshape(1, nseq, wbuf, NSA_KV, NSA_HD),
                      s_win_v.reshape(1, nseq, wbuf, NSA_KV, NSA_HD), s_hgrn[None]]

    return (y_prompt, y_sample, *p_state, *s_state)
```

```python
import functools
import math

import numpy as np
import jax
import jax.numpy as jnp
from jax import lax
from jax.experimental import pallas as pl
from jax.experimental.pallas import tpu as pltpu

F32 = jnp.float32
BF16 = jnp.bfloat16

NSA_HEADS = 8
NSA_KV = 2
NSA_HD = 64
NSA_REP = NSA_HEADS // NSA_KV
CMP_LEN = 32
CMP_STRIDE = 16
CMP_HID = 64
SEL_BLOCK = 64
SEL_TOPN = 8
WINDOW = 512
HG_HEADS = 4
HG_DK = 128
HG_DV = 128
HG_CHUNK = 64
N_BUCKETS = 32
MAX_DIST = 128
EPS = 1e-6
NEG = -1e30
FORCED = 1e9
PAGE_SIZE = 128

A_WIDTH = NSA_HEADS * NSA_HD
A_KV = NSA_KV * NSA_HD
B_WIDTH = HG_HEADS * HG_DV
B_KEY = HG_HEADS * HG_DK

LANES = 128
SUBLANES = 8
VMEM_LIMIT_BYTES = 56 * 1024 * 1024

TQ = 128
T_PAD = SUBLANES
N_ROWBLK = NSA_HEADS

C_Q = 0
C_KV = C_Q + A_WIDTH
C_GATE = C_KV + 6 * A_KV
C_ZA = C_GATE + 2 * LANES
C_QB = C_ZA + A_WIDTH
C_FB = C_QB + B_KEY
C_IB = C_FB + B_KEY
C_ZB = C_IB + B_WIDTH
C_MA = C_ZB + B_WIDTH

CMP_SRC_ROWS = (LANES - 1) * CMP_STRIDE + CMP_LEN
CMP_PITCH = CMP_STRIDE + 1


def _nn(a, b):
    return jnp.dot(a, b, preferred_element_type=F32)


def _nt(a, b):
    return lax.dot_general(a, b, (((1,), (1,)), ((), ())), preferred_element_type=F32)


def _tn(a, b):
    return lax.dot_general(a, b, (((0,), (0,)), ((), ())), preferred_element_type=F32)


def _sigmoid(x):
    return 1.0 / (1.0 + jnp.exp(-x))


def _silu(x):
    return x * _sigmoid(x)


def _split2(x):
    hi = x.astype(BF16)
    lo = (x - hi.astype(F32)).astype(BF16)
    return hi, lo


def _cparams(n_grid):
    return pltpu.CompilerParams(dimension_semantics=("arbitrary",) * n_grid,
                                vmem_limit_bytes=VMEM_LIMIT_BYTES)


def _t5_bucket(rel):
    n = jnp.maximum(rel, 0)
    exact = N_BUCKETS // 2
    nf = jnp.maximum(n, 1).astype(F32)
    scaled = jnp.log(nf / exact) / math.log(MAX_DIST / exact) * (N_BUCKETS - exact)
    large = exact + jnp.floor(scaled).astype(jnp.int32)
    large = jnp.minimum(large, N_BUCKETS - 1)
    return jnp.where(n < exact, n, large)


def _rowblk_head(rb):
    return NSA_REP * (rb % NSA_KV) + rb // NSA_KV


def _bias_body(tab_ref, o_ref, *, base, step, row_stride, col_stride, lo, hi, cols):
    sub_r = min(o_ref.shape[1], 4 * SUBLANES)
    sub_c = min(cols, LANES)
    shape = (sub_r, sub_c)
    row = lax.broadcasted_iota(jnp.int32, shape, 0)
    col = lax.broadcasted_iota(jnp.int32, shape, 1)
    for r0 in range(0, o_ref.shape[1], sub_r):
        for c0 in range(0, cols, sub_c):
            rel = base + step * pl.program_id(0) + row_stride * (row + r0) + col_stride * (col + c0)
            bucket = _t5_bucket(rel)
            accs = [jnp.zeros(shape, F32)] * N_ROWBLK
            for k in range(N_BUCKETS):
                hit = bucket == k
                accs = [jnp.where(hit, tab_ref[k, _rowblk_head(rb)], accs[rb]) for rb in range(N_ROWBLK)]
            for rb in range(N_ROWBLK):
                acc = accs[rb]
                if lo is not None:
                    acc = jnp.where((rel >= lo) & (rel < hi), acc, NEG)
                o_ref[rb, r0:r0 + sub_r, c0:c0 + sub_c] = acc


def _bias_table(rel_bias, *, base, step, row_stride, col_stride, rows, cols, steps, lo=None, hi=None):
    return pl.pallas_call(
        functools.partial(_bias_body, base=base, step=step, row_stride=row_stride, col_stride=col_stride,
                          lo=lo, hi=hi, cols=cols),
        grid=(steps,),
        in_specs=[pl.BlockSpec(memory_space=pltpu.SMEM)],
        out_specs=pl.BlockSpec((None, N_ROWBLK, rows, cols), lambda i: (i, 0, 0, 0)),
        out_shape=jax.ShapeDtypeStruct((steps, N_ROWBLK, rows, cols), F32),
        compiler_params=_cparams(1),
        name="bias_table",
    )(rel_bias)


def _bias_table_sample(rel_bias, base, col_stride, cols):
    return _bias_table(rel_bias, base=base, step=0, row_stride=1, col_stride=-col_stride, rows=T_PAD, cols=cols,
                       steps=1)[0]


N_KV_STREAMS = 6


def _proj_body(x_ref, g_ref, w_ref, o_ref, *kvt_refs):
    x = x_ref[...]
    ms = jnp.mean(x * x, axis=-1, keepdims=True)
    xn = (x * lax.rsqrt(ms + EPS) * g_ref[...]).astype(BF16)
    o = _nt(xn, w_ref[...])
    o_ref[...] = o
    for c, kvt_ref in enumerate(kvt_refs):
        kvt_ref[...] = o[:, C_KV + c * A_KV:C_KV + (c + 1) * A_KV].T


PROJ_ROWS = 512


def _project(x2d, norm_g, w, seq_len=None):
    m, d = x2d.shape
    n = w.shape[0]
    tm = min(m, PROJ_ROWS)
    out_specs = [pl.BlockSpec((tm, n), lambda i: (i, 0))]
    out_shape = [jax.ShapeDtypeStruct((m, n), F32)]
    if seq_len is not None:
        per_seq = seq_len // tm
        assert seq_len % tm == 0
        out_specs += [pl.BlockSpec((None, A_KV, tm), lambda i: (i // per_seq, 0, i % per_seq))] * N_KV_STREAMS
        out_shape += [jax.ShapeDtypeStruct((m // seq_len, A_KV, seq_len), F32)] * N_KV_STREAMS
    outs = pl.pallas_call(
        _proj_body,
        grid=(m // tm,),
        in_specs=[pl.BlockSpec((tm, d), lambda i: (i, 0)),
                  pl.BlockSpec((1, d), lambda i: (0, 0)),
                  pl.BlockSpec((n, d), lambda i: (0, 0), pipeline_mode=pl.Buffered(1))],
        out_specs=out_specs,
        out_shape=out_shape,
        compiler_params=_cparams(1),
        name="in_proj",
    )(x2d, norm_g.reshape(1, d), w)
    return outs if seq_len is not None else outs[0]


def _permute_heads_rows(w):
    return w.reshape(NSA_KV, NSA_REP, NSA_HD, w.shape[-1]).swapaxes(0, 1).reshape(A_WIDTH, w.shape[-1])


def _arrange_w_in(w_in):
    d = w_in.shape[0]
    wt = w_in.T
    sizes = (A_WIDTH, 6 * A_KV, 3 * NSA_HEADS, A_WIDTH, 2 * B_KEY + 2 * B_WIDTH + 2 * d)
    offs = np.concatenate([[0], np.cumsum(sizes)])
    q_a, kv_a, g_a, z_a, rest = [wt[offs[i]:offs[i + 1]] for i in range(5)]
    gate_pad = jnp.zeros((2 * LANES - 3 * NSA_HEADS, d), w_in.dtype)
    return jnp.concatenate([_permute_heads_rows(q_a), kv_a, g_a, gate_pad, _permute_heads_rows(z_a), rest],
                           axis=0).astype(BF16)


def _compress(streams, pitch):
    accs = [jnp.zeros((len(srcs) * LANES, LANES), F32) for srcs, _, _, _ in streams]
    for j in range(CMP_LEN):
        start = (j // CMP_STRIDE) * pitch + j % CMP_STRIDE
        for n, (srcs, pos_ref, w1_ref, _) in enumerate(streams):
            xj = jnp.concatenate([r[pl.ds(start, LANES, stride=pitch), :] for r in srcs], axis=0) + pos_ref[j:j + 1, :]
            accs[n] = accs[n] + _nn(xj.astype(BF16), w1_ref[j])
    return [_nn(_silu(acc).astype(BF16), w2_ref[...]) for acc, (_, _, _, w2_ref) in zip(accs, streams)]


def _compress_prompt_body(kc_ref, vc_ref, posk_ref, w1k_ref, w2k_ref, posv_ref, w1v_ref, w2v_ref,
                          ko_ref, vo_ref, kbuf_ref, vbuf_ref):
    t = kc_ref.shape[0]
    for buf_ref, src_ref in ((kbuf_ref, kc_ref), (vbuf_ref, vc_ref)):
        buf_ref[t:, :] = jnp.zeros((buf_ref.shape[0] - t, LANES), F32)
        buf_ref[:t, :] = src_ref[...]
    ko_ref[...], vo_ref[...] = _compress([([kbuf_ref], posk_ref, w1k_ref, w2k_ref),
                                          ([vbuf_ref], posv_ref, w1v_ref, w2v_ref)], CMP_STRIDE)


def _block_diag2(w):
    z = jnp.zeros_like(w)
    return jnp.concatenate([jnp.concatenate([w, z], axis=-1), jnp.concatenate([z, w], axis=-1)], axis=-2)


def _compress_weights(pos, w1, w2):
    pos2 = jnp.concatenate([pos, pos], axis=-1)
    w1bd = _block_diag2(w1.reshape(CMP_LEN, NSA_HD, CMP_HID)).astype(BF16)
    w2bd = _block_diag2(w2).astype(BF16)
    return pos2, w1bd, w2bd


def _const_spec(shape, single_buffer=False):
    nd = len(shape)
    if single_buffer:
        return pl.BlockSpec(shape, lambda *_: (0,) * nd, pipeline_mode=pl.Buffered(1))
    return pl.BlockSpec(shape, lambda *_: (0,) * nd)


def _compress_prompt(proj3, cw_k, cw_v):
    b, t, _ = proj3.shape
    col = C_KV // LANES
    w_specs = [_const_spec(a.shape) for a in cw_k + cw_v]
    return pl.pallas_call(
        _compress_prompt_body,
        grid=(b,),
        in_specs=[pl.BlockSpec((None, t, LANES), lambda i: (i, 0, col)),
                  pl.BlockSpec((None, t, LANES), lambda i: (i, 0, col + 1))] + w_specs,
        out_specs=[pl.BlockSpec((None, LANES, LANES), lambda i: (i, 0, 0))] * 2,
        out_shape=[jax.ShapeDtypeStruct((b, LANES, LANES), F32)] * 2,
        scratch_shapes=[pltpu.VMEM((max(t + CMP_LEN, CMP_SRC_ROWS), LANES), F32)] * 2,
        compiler_params=_cparams(1),
        name="compress_prompt",
    )(proj3, proj3, *cw_k, *cw_v)


def _select_blocks(imp, qpos, nb, axis):
    j = lax.broadcasted_iota(jnp.int32, imp.shape, axis)
    cur = qpos // SEL_BLOCK
    valid = j * SEL_BLOCK <= qpos
    forced = (j == 0) | (j == cur) | (j == cur - 1)
    score = jnp.where(valid, jnp.where(forced, FORCED, imp), -1.0)
    score = jnp.where(j < nb, score, -2.0)
    rank = jnp.zeros(imp.shape, F32)
    for i in range(nb):
        s_i = lax.slice_in_dim(score, i, i + 1, axis=axis)
        tie = jnp.where(i < j, 1.0, 0.0)
        rank = rank + jnp.where(s_i > score, 1.0, jnp.where(s_i == score, tie, 0.0))
    return jnp.where((rank < min(SEL_TOPN, nb)) & (j < nb), 1.0, 0.0)


def _masked_softmax(s, mask):
    s = jnp.where(mask, s, NEG)
    m = jnp.max(s, axis=-1, keepdims=True)
    e = jnp.exp(s - m)
    p = e / jnp.sum(e, axis=-1, keepdims=True)
    return jnp.where(mask, p, 0.0)


def _stack_queries(q_ref, qbd_ref, rows):
    lane = lax.broadcasted_iota(jnp.int32, (rows, LANES), 1)
    for r in range(NSA_REP):
        blk = q_ref[:, r * LANES:(r + 1) * LANES] * (NSA_HD ** -0.5)
        qbd_ref[(2 * r) * rows:(2 * r + 1) * rows, :] = jnp.where(lane < NSA_HD, blk, 0.0).astype(BF16)
        qbd_ref[(2 * r + 1) * rows:(2 * r + 2) * rows, :] = jnp.where(lane >= NSA_HD, blk, 0.0).astype(BF16)


def _gate_and_store(gate_ref, za_ref, o_ref, o_cmp, o_sel, o_win, rows):
    gs = _sigmoid(gate_ref[...])
    lane = lax.broadcasted_iota(jnp.int32, (rows, LANES), 1)
    for r in range(NSA_REP):
        parts = []
        for g in range(NSA_KV):
            rb = 2 * r + g
            h = _rowblk_head(rb)
            o = (gs[:, h:h + 1] * o_cmp[rb] + gs[:, NSA_HEADS + h:NSA_HEADS + h + 1] * o_sel[rb]
                 + gs[:, 2 * NSA_HEADS + h:2 * NSA_HEADS + h + 1] * o_win[rb])
            parts.append(o)
        blk = jnp.where(lane < NSA_HD, parts[0], parts[1])
        o_ref[:, r * LANES:(r + 1) * LANES] = blk * _silu(za_ref[:, r * LANES:(r + 1) * LANES])


SEL_TILES_PER_ITER = 2


def _nsa_prompt_body(q_ref, gate_ref, za_ref, ks_ref, vs_ref, kw_ref, vw_ref, kcmp_ref, vcmp_ref,
                     bcmp_ref, btile_ref, covert_ref, o_ref,
                     qt_ref, ksb_ref, vst_ref, kwb_ref, vwt_ref, vcmpt_ref, mask_ref, s_ref, m_ref, l_ref, acc_ref, res_ref,
                     *, nc, nb):
    qb = pl.program_id(1)
    nkt = ksb_ref.shape[0]
    sub = lax.broadcasted_iota(jnp.int32, (LANES, TQ), 0)
    tok = lax.broadcasted_iota(jnp.int32, (LANES, TQ), 1)

    @pl.when(qb == 0)
    def _():
        for kt in range(nkt):
            rows = slice(kt * TQ, (kt + 1) * TQ)
            ksb_ref[kt] = ks_ref[rows, :].astype(BF16)
            kwb_ref[kt] = kw_ref[rows, :].astype(BF16)
            vst_ref[kt] = vs_ref[rows, :].T.astype(BF16)
            vwt_ref[kt] = vw_ref[rows, :].T.astype(BF16)
        vcmpt_ref[...] = vcmp_ref[...].T.astype(BF16)

    for r in range(NSA_REP):
        qt = (q_ref[:, r * LANES:(r + 1) * LANES] * (NSA_HD ** -0.5)).T
        qt_ref[2 * r] = jnp.where(sub < NSA_HD, qt, 0.0).astype(BF16)
        qt_ref[2 * r + 1] = jnp.where(sub >= NSA_HD, qt, 0.0).astype(BF16)

    valid = ((CMP_STRIDE * sub + CMP_LEN - 1) <= qb * TQ + tok) & (sub < nc)
    kcb = kcmp_ref[...].astype(BF16)
    psum = [None] * NSA_KV
    for rb in range(N_ROWBLK):
        s = jnp.where(valid, _nn(kcb, qt_ref[rb]) + bcmp_ref[rb], NEG)
        e = jnp.exp(s - jnp.max(s, axis=0, keepdims=True))
        p = jnp.where(valid, e / jnp.sum(e, axis=0, keepdims=True), 0.0)
        res_ref[0, rb] = _nn(vcmpt_ref[...], p.astype(BF16))
        g = rb % NSA_KV
        psum[g] = p if psum[g] is None else psum[g] + p

    nbp = _round_up(nb, SUBLANES)
    upper = sub < SEL_BLOCK
    for g in range(NSA_KV):
        p_hi, p_lo = _split2(psum[g])
        imp = _nn(covert_ref[...], p_hi) + _nn(covert_ref[...], p_lo)
        sel = _select_blocks(imp[:nbp], qb * TQ + tok[:1], nb, 0)
        for kt in range(nkt):
            picked = jnp.where(upper, sel[2 * kt:2 * kt + 1, :], sel[2 * kt + 1:2 * kt + 2, :]) > 0.5
            mask_ref[kt, g] = jnp.where(kt <= qb, jnp.where(picked, 0.0, NEG), NEG)

    m_ref[...] = jnp.full(m_ref.shape, NEG, F32)
    l_ref[...] = jnp.zeros(l_ref.shape, F32)
    acc_ref[...] = jnp.zeros(acc_ref.shape, F32)

    def sel_scores(pair, buf):
        for u in range(SEL_TILES_PER_ITER):
            kt = jnp.minimum(pair * SEL_TILES_PER_ITER + u, nkt - 1)
            k_t = ksb_ref[kt]
            bidx = jnp.clip(qb - kt, 0, 2)
            for g in range(NSA_KV):
                madd = mask_ref[kt, g]
                for rb in range(g, N_ROWBLK, NSA_KV):
                    s_ref[buf, u, rb] = _nn(k_t, qt_ref[rb]) + btile_ref[bidx, rb] + madd

    def sel_update(pair, buf):
        v_ts = [vst_ref[pair * SEL_TILES_PER_ITER + u] for u in range(SEL_TILES_PER_ITER)]
        for rb in range(N_ROWBLK):
            ss = [s_ref[buf, u, rb] for u in range(SEL_TILES_PER_ITER)]
            m_old = m_ref[rb]
            m_new = m_old
            for s in ss:
                m_new = jnp.maximum(m_new, jnp.max(s, axis=0, keepdims=True))
            alpha = jnp.exp(m_old - m_new)
            l_new = alpha * l_ref[rb]
            acc = alpha * acc_ref[rb]
            for s, v_t in zip(ss, v_ts):
                p = jnp.exp(s - m_new)
                l_new = l_new + jnp.sum(p, axis=0, keepdims=True)
                acc = acc + _nn(v_t, p.astype(BF16))
            l_ref[rb] = l_new
            acc_ref[rb] = acc
            m_ref[rb] = m_new

    def sel_body(it, carry):
        sel_scores(2 * it + 1, 1)
        sel_update(2 * it, 0)
        sel_scores(2 * it + 2, 0)
        sel_update(2 * it + 1, 1)
        return carry

    sel_scores(0, 0)
    lax.fori_loop(0, qb // (2 * SEL_TILES_PER_ITER) + 1, sel_body, 0)
    res_ref[1] = acc_ref[...] / l_ref[...]

    n_wt = WINDOW // TQ + 1
    for rb in range(N_ROWBLK):
        qc = qt_ref[rb]
        ss = []
        for dist in range(n_wt):
            kt = jnp.maximum(qb - dist, 0)
            bidx = jnp.where(qb >= dist, dist, n_wt)
            ss.append(_nn(kwb_ref[kt], qc) + btile_ref[bidx, rb])
        m = jnp.max(ss[0], axis=0, keepdims=True)
        for s in ss[1:]:
            m = jnp.maximum(m, jnp.max(s, axis=0, keepdims=True))
        l_w = None
        acc = None
        for dist, s in enumerate(ss):
            p = jnp.exp(s - m)
            ps = jnp.sum(p, axis=0, keepdims=True)
            pv = _nn(vwt_ref[jnp.maximum(qb - dist, 0)], p.astype(BF16))
            l_w = ps if l_w is None else l_w + ps
            acc = pv if acc is None else acc + pv
        res_ref[2, rb] = acc / l_w

    gt = _sigmoid(gate_ref[...]).T
    for r in range(NSA_REP):
        parts = []
        for g in range(NSA_KV):
            rb = 2 * r + g
            h = _rowblk_head(rb)
            parts.append(gt[h:h + 1, :] * res_ref[0, rb]
                         + gt[NSA_HEADS + h:NSA_HEADS + h + 1, :] * res_ref[1, rb]
                         + gt[2 * NSA_HEADS + h:2 * NSA_HEADS + h + 1, :] * res_ref[2, rb])
        blk = jnp.where(sub < NSA_HD, parts[0], parts[1]).T
        o_ref[:, r * LANES:(r + 1) * LANES] = (blk * _silu(za_ref[:, r * LANES:(r + 1) * LANES])).astype(o_ref.dtype)


def _cover_matrix(nc, nb):
    cs = np.arange(nc) * CMP_STRIDE
    ce = cs + CMP_LEN
    bs = np.arange(nb) * SEL_BLOCK
    be = bs + SEL_BLOCK
    ov = np.clip(np.minimum(ce[:, None], be[None, :]) - np.maximum(cs[:, None], bs[None, :]), 0, None) / CMP_STRIDE
    out = np.zeros((LANES, LANES), np.float32)
    out[:nc, :nb] = ov
    return jnp.asarray(out, BF16)


def _expand_matrix(nkeys):
    j = np.arange(LANES)[:, None]
    k = np.arange(nkeys)[None, :]
    return jnp.asarray((k // SEL_BLOCK == j).astype(np.float32), BF16)


def _nsa_prompt(proj3, kcmp, vcmp, bias_cmp, bias_tiles):
    b, t, _ = proj3.shape
    nq = t // TQ
    nc = (t - CMP_LEN) // CMP_STRIDE + 1
    nb = t // SEL_BLOCK
    assert nc <= LANES and nb <= LANES and t % (TQ * 2 * SEL_TILES_PER_ITER) == 0
    kv_col = C_KV // LANES
    kv_spec = lambda c: pl.BlockSpec((None, t, LANES), lambda i, j: (i, 0, kv_col + c))
    cover_t = _cover_matrix(nc, nb).T
    tile_bf16 = pltpu.VMEM((nq, TQ, LANES), BF16)
    return pl.pallas_call(
        functools.partial(_nsa_prompt_body, nc=nc, nb=nb),
        grid=(b, nq),
        in_specs=[pl.BlockSpec((None, TQ, A_WIDTH), lambda i, j: (i, j, C_Q // A_WIDTH)),
                  pl.BlockSpec((None, TQ, LANES), lambda i, j: (i, j, C_GATE // LANES)),
                  pl.BlockSpec((None, TQ, A_WIDTH), lambda i, j: (i, j, C_ZA // A_WIDTH)),
                  kv_spec(2), kv_spec(3), kv_spec(4), kv_spec(5),
                  pl.BlockSpec((None, LANES, LANES), lambda i, j: (i, 0, 0)),
                  pl.BlockSpec((None, LANES, LANES), lambda i, j: (i, 0, 0)),
                  pl.BlockSpec((None, N_ROWBLK, LANES, TQ), lambda i, j: (j, 0, 0, 0)),
                  _const_spec(bias_tiles.shape), _const_spec(cover_t.shape)],
        out_specs=pl.BlockSpec((None, TQ, A_WIDTH), lambda i, j: (i, j, 0)),
        out_shape=jax.ShapeDtypeStruct((b, t, A_WIDTH), BF16),
        scratch_shapes=[pltpu.VMEM((N_ROWBLK, LANES, TQ), BF16),
                        tile_bf16, tile_bf16, tile_bf16, tile_bf16,
                        pltpu.VMEM((LANES, LANES), BF16),
                        pltpu.VMEM((nq, NSA_KV, TQ, TQ), F32),
                        pltpu.VMEM((2, SEL_TILES_PER_ITER, N_ROWBLK, TQ, TQ), F32),
                        pltpu.VMEM((N_ROWBLK, 1, TQ), F32),
                        pltpu.VMEM((N_ROWBLK, 1, TQ), F32),
                        pltpu.VMEM((N_ROWBLK, LANES, TQ), F32),
                        pltpu.VMEM((3, N_ROWBLK, LANES, TQ), F32)],
        compiler_params=_cparams(2),
        name="nsa_prompt",
    )(proj3, proj3, proj3, proj3, proj3, proj3, proj3, kcmp, vcmp, bias_cmp, bias_tiles, cover_t)


N_CACHES = 4


NSA_SEQ_PER_STEP = 4


def _group_page_copy(cache_refs, pt_ref, pbuf, sems, step, slot, s, c, pg, n_pages):
    seq = step * NSA_SEQ_PER_STEP + s
    return pltpu.make_async_copy(cache_refs[c].at[pt_ref[seq, pg]], pbuf.at[slot, s, c * n_pages + pg],
                                 sems.at[slot])


def _nsa_sample_group_body(pt_ref, *refs, n_pages, **static):
    cache_refs = refs[:N_CACHES]
    per_seq = refs[N_CACHES:N_CACHES + 11]
    consts = refs[N_CACHES + 11:N_CACHES + 22]
    outs = refs[N_CACHES + 22:N_CACHES + 25]
    scratch = refs[N_CACHES + 25:-2]
    pbuf, sems = refs[-2:]
    i = pl.program_id(0)
    slot = i % 2

    def all_pages(step, slot_):
        return [_group_page_copy(cache_refs, pt_ref, pbuf, sems, step, slot_, s, c, pg, n_pages)
                for s in range(NSA_SEQ_PER_STEP) for c in range(N_CACHES) for pg in range(n_pages)]

    @pl.when(i == 0)
    def _():
        for cp in all_pages(0, 0):
            cp.start()

    @pl.when(i + 1 < pl.num_programs(0))
    def _():
        for cp in all_pages(i + 1, 1 - slot):
            cp.start()

    for cp in all_pages(i, slot):
        cp.wait()
    pages = [[[pbuf.at[slot, s, c * n_pages + pg] for pg in range(n_pages)] for c in range(N_CACHES)]
             for s in range(NSA_SEQ_PER_STEP)]
    _nsa_sample_group(pages, per_seq, consts, outs, scratch, n_pages=n_pages, **static)


def _nsa_sample_group(pages, per_seq, consts, outs, scratch, *, n_pages, past, wbuf, nc, nb):
    q_ref, kc_ref, vc_ref, ks_ref, vs_ref, kw_ref, vw_ref, gate_ref, za_ref, wk_ref, wv_ref = per_seq
    (posk_ref, w1k_ref, w2k_ref, posv_ref, w1v_ref, w2v_ref, bcmp_ref, bsel_ref, bwin_ref,
     cover_ref, emat_ref) = consts
    o_ref, swk_ref, swv_ref = outs
    kc_all, vc_all, newt_ref, qbd_ref = scratch
    nseq = NSA_SEQ_PER_STEP
    seqs = range(nseq)
    n_keys = bsel_ref.shape[2]
    n_new = T_PAD // 2
    chunks_per_page = PAGE_SIZE // CMP_STRIDE
    nrow = N_ROWBLK * T_PAD

    for dst, cache, new_ref in ((kc_all, 0, kc_ref), (vc_all, 1, vc_ref)):
        for s in seqs:
            for pg in range(n_pages):
                rows = pages[s][cache][pg][...].T
                for c in range(chunks_per_page):
                    r0 = (pg * chunks_per_page + c) * CMP_PITCH
                    dst[s, r0:r0 + CMP_STRIDE, :] = rows[c * CMP_STRIDE:(c + 1) * CMP_STRIDE, :]
            r0 = (past // CMP_STRIDE) * CMP_PITCH
            dst[s, r0:r0 + T_PAD, :] = new_ref[s]
            dst[s, r0 + T_PAD:, :] = jnp.zeros((dst.shape[1] - r0 - T_PAD, LANES), F32)
    kcmp, vcmp = [c.astype(BF16) for c in _compress(
        [([kc_all.at[s] for s in seqs], posk_ref, w1k_ref, w2k_ref),
         ([vc_all.at[s] for s in seqs], posv_ref, w1v_ref, w2v_ref)], CMP_PITCH)]
    per_s = lambda x, s: x[s * LANES:(s + 1) * LANES]

    newt_ref[...] = jnp.zeros(newt_ref.shape, F32)
    for s in seqs:
        for idx, new_ref in enumerate((ks_ref, vs_ref, kw_ref, vw_ref)):
            newt_ref[s, idx, 0:T_PAD, :] = new_ref[s]
        _stack_queries(q_ref.at[s], qbd_ref.at[s], T_PAD)
    qbd = [qbd_ref[s] for s in seqs]
    tok = lax.broadcasted_iota(jnp.int32, (T_PAD, LANES), 0)
    lane = lax.broadcasted_iota(jnp.int32, (T_PAD, LANES), 1)
    qpos = past + tok
    stack = lambda xs: jnp.concatenate(xs, axis=0)
    by_head = lambda x: x.reshape(nseq, NSA_REP, NSA_KV, T_PAD, LANES)

    s_c = by_head(stack([_nt(qbd[s], per_s(kcmp, s)) for s in seqs])) + _by_head_bias(bcmp_ref[...])
    mask_c = ((CMP_STRIDE * lane + CMP_LEN - 1) <= qpos) & (lane < nc)
    p = _masked_softmax(s_c, mask_c[None, None, None]).reshape(nseq, nrow, LANES)
    o_cmp = [_nn(p[s].astype(BF16), per_s(vcmp, s)).reshape(N_ROWBLK, T_PAD, LANES) for s in seqs]

    psum = p.reshape(nseq, NSA_REP, NSA_KV * T_PAD, LANES).sum(axis=1).reshape(nseq * NSA_KV * T_PAD, LANES)
    p_hi, p_lo = _split2(psum)
    imp = _nn(p_hi, cover_ref[...]) + _nn(p_lo, cover_ref[...])
    sel = _select_blocks(imp.reshape(nseq, NSA_KV, T_PAD, LANES), qpos[None, None], nb, 3)
    selk = _nn(sel.reshape(nseq * NSA_KV * T_PAD, LANES).astype(BF16), emat_ref[...])
    selk = selk.reshape(nseq, NSA_KV, T_PAD, n_keys)

    def softmax_pv(scores, values):
        m = scores[0]
        for s_t in scores[1:]:
            m = jnp.maximum(m, s_t)
        m = jnp.max(m, axis=-1, keepdims=True)
        acc = [None] * nseq
        tot = None
        for s_t, (v_ts, channel_major) in zip(scores, values):
            p_t = jnp.exp(s_t - m)
            pb = p_t.reshape(nseq, nrow, LANES).astype(BF16)
            for s in seqs:
                pv = _nt(pb[s], v_ts[s]) if channel_major else _nn(pb[s], v_ts[s])
                acc[s] = pv if acc[s] is None else acc[s] + pv
            tot = p_t if tot is None else tot + p_t
        inv = 1.0 / jnp.sum(tot, axis=-1, keepdims=True).reshape(nseq, nrow, 1)
        return [(acc[s] * inv[s]).reshape(N_ROWBLK, T_PAD, LANES) for s in seqs]

    scores, values = [], []
    for t in range(n_pages + 1):
        cols = slice(t * LANES, (t + 1) * LANES)
        allowed = ((t * LANES + lane) <= qpos)[None, None] & (selk[:, :, :, cols] > 0.5)
        madd = jnp.where(allowed, 0.0, NEG)[:, None]
        if t < n_pages:
            raw = [_nn(qbd[s], pages[s][2][t][...].astype(BF16)) for s in seqs]
            values.append(([pages[s][3][t][...].astype(BF16) for s in seqs], True))
        else:
            raw = [_nt(qbd[s], newt_ref[s, 0].astype(BF16)) for s in seqs]
            values.append(([newt_ref[s, 1].astype(BF16) for s in seqs], False))
        scores.append(by_head(stack(raw)) + _by_head_bias(bsel_ref[:, :, cols]) + madd)
    o_sel = softmax_pv(scores, values)

    scores, values = [], []
    kwt = [wk_ref[s].astype(BF16) for s in seqs]
    vwt = [wv_ref[s].astype(BF16) for s in seqs]
    for t in range(wbuf // LANES + 1):
        cols = slice(t * LANES, (t + 1) * LANES)
        rel_w = wbuf + tok - (t * LANES + lane)
        madd = jnp.where((rel_w >= 0) & (rel_w < WINDOW), 0.0, NEG)[None, None, None]
        if t < wbuf // LANES:
            raw = [_nn(qbd[s], kwt[s][:, cols]) for s in seqs]
            values.append(([vwt[s][:, cols] for s in seqs], True))
        else:
            raw = [_nt(qbd[s], newt_ref[s, 2].astype(BF16)) for s in seqs]
            values.append(([newt_ref[s, 3].astype(BF16) for s in seqs], False))
        scores.append(by_head(stack(raw)) + _by_head_bias(bwin_ref[:, :, cols]) + madd)
    o_win = softmax_pv(scores, values)

    lane_full = lax.broadcasted_iota(jnp.int32, (LANES, LANES), 1)
    for s in seqs:
        _gate_and_store(gate_ref.at[s], za_ref.at[s], o_ref.at[s], o_cmp[s], o_sel[s], o_win[s], T_PAD)
        for idx, w_ref, out_ref in ((2, wk_ref, swk_ref), (3, wv_ref, swv_ref)):
            rolled = pltpu.roll(w_ref[s], wbuf - n_new, axis=1)
            tail = pltpu.roll(newt_ref[s, idx], LANES - n_new, axis=0).T
            out_ref[s, :, :wbuf - LANES] = rolled[:, :wbuf - LANES]
            out_ref[s, :, wbuf - LANES:] = jnp.where(lane_full >= LANES - n_new, tail, rolled[:, wbuf - LANES:])


def _by_head_bias(b):
    return b.reshape(1, NSA_REP, NSA_KV, T_PAD, b.shape[-1])


def _round_up(x, m):
    return (x + m - 1) // m * m


def _nsa_sample(projs3, caches, wk, wv, page_table, cw_k, cw_v, rel_bias, n_new):
    nseq, n_pages = page_table.shape
    past = n_pages * PAGE_SIZE
    wbuf = wk.shape[2]
    length = past + n_new
    nc = (length - CMP_LEN) // CMP_STRIDE + 1
    nb = -(-length // SEL_BLOCK)
    n_keys = past + LANES
    n_win = wbuf + LANES
    pitched = lambda p: (p // CMP_STRIDE) * CMP_PITCH + p % CMP_STRIDE
    n_cmp_rows = _round_up(max(pitched(past) + T_PAD, pitched(CMP_SRC_ROWS - 1) + 1), SUBLANES)
    assert nc <= LANES and nb <= LANES and n_new == T_PAD // 2 and nb * SEL_BLOCK <= n_keys
    assert wbuf % LANES == 0 and PAGE_SIZE == LANES

    bias_cmp = _bias_table_sample(rel_bias, past - (CMP_LEN - 1), CMP_STRIDE, LANES)
    bias_sel = _bias_table_sample(rel_bias, past, 1, n_keys)
    bias_win = _bias_table_sample(rel_bias, wbuf, 1, n_win)
    cover = _cover_matrix(nc, nb)
    emat = _expand_matrix(n_keys)

    sps = NSA_SEQ_PER_STEP
    assert nseq % sps == 0

    def new_spec(width, col):
        return pl.BlockSpec((sps, T_PAD, width), lambda i, pt: (i, 0, col))

    kv_col = C_KV // LANES
    in_specs = [pl.BlockSpec(memory_space=pl.ANY)] * N_CACHES
    in_specs += [new_spec(A_WIDTH, C_Q // A_WIDTH)]
    in_specs += [new_spec(LANES, kv_col + c) for c in range(6)]
    in_specs += [new_spec(LANES, C_GATE // LANES), new_spec(A_WIDTH, C_ZA // A_WIDTH)]
    win_spec = pl.BlockSpec((sps, LANES, wbuf), lambda i, pt: (i, 0, 0))
    in_specs += [win_spec] * 2
    consts = list(cw_k) + list(cw_v) + [bias_cmp, bias_sel, bias_win, cover, emat]
    in_specs += [_const_spec(a.shape, single_buffer=True) for a in consts]
    operands = list(caches) + [projs3] * 9 + [wk, wv] + consts

    grid_spec = pltpu.PrefetchScalarGridSpec(
        num_scalar_prefetch=1,
        grid=(nseq // sps,),
        in_specs=in_specs,
        out_specs=[pl.BlockSpec((sps, T_PAD, A_WIDTH), lambda i, pt: (i, 0, 0)), win_spec, win_spec],
        scratch_shapes=[pltpu.VMEM((sps, n_cmp_rows, LANES), F32), pltpu.VMEM((sps, n_cmp_rows, LANES), F32),
                        pltpu.VMEM((sps, 4, LANES, LANES), F32),
                        pltpu.VMEM((sps, N_ROWBLK * T_PAD, LANES), BF16),
                        pltpu.VMEM((2, sps, N_CACHES * n_pages, LANES, PAGE_SIZE), F32),
                        pltpu.SemaphoreType.DMA((2,))])
    return pl.pallas_call(
        functools.partial(_nsa_sample_group_body, n_pages=n_pages, past=past, wbuf=wbuf, nc=nc, nb=nb),
        grid_spec=grid_spec,
        out_shape=[jax.ShapeDtypeStruct((nseq, T_PAD, A_WIDTH), F32),
                   jax.ShapeDtypeStruct((nseq, LANES, wbuf), F32),
                   jax.ShapeDtypeStruct((nseq, LANES, wbuf), F32)],
        compiler_params=_cparams(1),
        name="nsa_sample",
    )(page_table, *operands)


def _hgrn_constants(rows, seg_stride, seg_len):
    idx = np.arange(rows)
    seg = idx // seg_stride
    pos = idx % seg_stride
    same = seg[:, None] == seg[None, :]
    pt, pu = pos[:, None], pos[None, :]
    mats = [same & (pu <= pt),
            same & (pu > pt) & (pu <= seg_len - 1)]
    levels = []
    m = seg_len // 2
    while m >= 2:
        levels.append(m)
        m //= 2
    for m in levels:
        mats.append(same & (pu >= (pt // m) * m) & (pu <= pt))
        mats.append(same & (pu > pt) & (pu <= (pt // m) * m + m - 1))
    wstack = np.concatenate(mats, axis=0).astype(np.float32)
    wstack = np.concatenate([wstack] * 3, axis=1)
    masks = []
    for m in levels + [1]:
        masks.append(same & (pt // (2 * m) == pu // (2 * m)) & ((pt // m) % 2 == 1) & ((pu // m) % 2 == 0))
    masks.append(idx[:, None] == idx[None, :])
    masks = np.stack(masks).astype(np.float32)
    return jnp.asarray(wstack, BF16), jnp.asarray(masks, F32), len(levels)


def _hgrn_gates(f_pre, hl_ref, wst_ref):
    hl = hl_ref[...]
    e = jnp.exp(hl - jnp.max(hl, axis=0, keepdims=True))
    lb = e[0:1] / jnp.sum(e, axis=0, keepdims=True)
    f = lb + (1.0 - lb) * _sigmoid(f_pre)
    g = jnp.log(f)
    k = 1.0 - f
    g_hi = g.astype(BF16)
    r1 = g - g_hi.astype(F32)
    g_mid = r1.astype(BF16)
    g_lo = (r1 - g_mid.astype(F32)).astype(BF16)
    sums = _nn(wst_ref[...], jnp.concatenate([g_hi, g_mid, g_lo], axis=0))
    return f, k, sums


def _hgrn_head(q, f, k, sums, masks_ref, hs, n_levels):
    rows = q.shape[0]
    qh, fh, kh = q[:, hs], f[:, hs], k[:, hs]
    part = lambda i: sums[i * rows:(i + 1) * rows, hs]
    big_g = part(0)
    qg = qh * jnp.exp(big_g)
    kd = kh * jnp.exp(part(1))
    a = None
    for li in range(n_levels + 2):
        if li < n_levels:
            ql, kl = qh * jnp.exp(part(2 + 2 * li)), kh * jnp.exp(part(3 + 2 * li))
        elif li == n_levels:
            ql, kl = qh * fh, kh
        else:
            ql, kl = qh, kh
        term = masks_ref[li] * _nt(ql.astype(BF16), kl.astype(BF16))
        a = term if a is None else a + term
    return big_g, qg, kd, a


def _hgrn_out(o, ng, z):
    o = o * lax.rsqrt(jnp.mean(o * o, axis=-1, keepdims=True) + EPS)
    return o * ng * _silu(z)


def _hgrn_prompt_body(q_ref, f_ref, i_ref, z_ref, hl_ref, ng_ref, wst_ref, masks_ref, o_ref, s_ref, st_ref,
                      *, n_levels, rows):
    c = pl.program_id(1)

    @pl.when(c == 0)
    def _():
        st_ref[...] = jnp.zeros(st_ref.shape, F32)

    chunks = []
    for ci in range(q_ref.shape[0] // rows):
        rs = slice(ci * rows, (ci + 1) * rows)
        v = i_ref[rs, :]
        q = q_ref[rs, :]
        f, k, sums = _hgrn_gates(f_ref[rs, :], hl_ref, wst_ref)
        per_head = []
        for h in range(HG_HEADS):
            hs = slice(h * HG_DK, (h + 1) * HG_DK)
            vb = v[:, hs].astype(BF16)
            big_g, qg, kd, a = _hgrn_head(q, f, k, sums, masks_ref, hs, n_levels)
            per_head.append((qg.astype(BF16), _nn(a.astype(BF16), vb),
                             jnp.exp(big_g[rows - 1:rows]), _tn(vb, kd.astype(BF16))))
        chunks.append(per_head)

    states = [st_ref[h] for h in range(HG_HEADS)]
    for ci, per_head in enumerate(chunks):
        rs = slice(ci * rows, (ci + 1) * rows)
        for h in range(HG_HEADS):
            hs = slice(h * HG_DK, (h + 1) * HG_DK)
            qgb, o_intra, decay, kv = per_head[h]
            o = _nt(qgb, states[h].astype(BF16)) + o_intra
            o_ref[rs, hs] = _hgrn_out(o, ng_ref[:, hs], z_ref[rs, hs]).astype(o_ref.dtype)
            states[h] = states[h] * decay + kv
    for h in range(HG_HEADS):
        st_ref[h] = states[h]

    @pl.when(c == pl.num_programs(1) - 1)
    def _():
        for h in range(HG_HEADS):
            s_ref[h] = states[h].T


HGRN_CHUNKS_PER_STEP = 16


def _hgrn_prompt(proj3, hg_lower, hg_norm_g):
    b, t, _ = proj3.shape
    chunk = math.gcd(t, HG_CHUNK)
    wst, masks, n_levels = _hgrn_constants(chunk, chunk, chunk)
    rows = chunk * math.gcd(t // chunk, HGRN_CHUNKS_PER_STEP)
    col = lambda c: pl.BlockSpec((None, rows, B_KEY), lambda i, j: (i, j, c // B_KEY))
    return pl.pallas_call(
        functools.partial(_hgrn_prompt_body, n_levels=n_levels, rows=chunk),
        grid=(b, t // rows),
        in_specs=[col(C_QB), col(C_FB), col(C_IB), col(C_ZB),
                  _const_spec(hg_lower.shape), _const_spec(hg_norm_g.shape),
                  _const_spec(wst.shape), _const_spec(masks.shape)],
        out_specs=[pl.BlockSpec((None, rows, B_WIDTH), lambda i, j: (i, j, 0)),
                   pl.BlockSpec((None, HG_HEADS, HG_DK, HG_DV), lambda i, j: (i, 0, 0, 0))],
        out_shape=[jax.ShapeDtypeStruct((b, t, B_WIDTH), BF16),
                   jax.ShapeDtypeStruct((b, HG_HEADS, HG_DK, HG_DV), F32)],
        scratch_shapes=[pltpu.VMEM((HG_HEADS, HG_DV, HG_DK), F32)],
        compiler_params=_cparams(2),
        name="hgrn_prompt",
    )(proj3, proj3, proj3, proj3, hg_lower, hg_norm_g, wst, masks)


SEQ_PER_STEP = 8


def _hgrn_sample_body(q_ref, f_ref, i_ref, z_ref, s0_ref, hl_ref, ng_ref, wst_ref, masks_ref, o_ref, s_ref,
                      *, n_levels, n_new):
    v = i_ref[...]
    q = q_ref[...]
    f, k, sums = _hgrn_gates(f_ref[...], hl_ref, wst_ref)
    for h in range(HG_HEADS):
        hs = slice(h * HG_DK, (h + 1) * HG_DK)
        vb = v[:, hs].astype(BF16)
        big_g, qg, kd, a = _hgrn_head(q, f, k, sums, masks_ref, hs, n_levels)
        o_intra = _nn(a.astype(BF16), vb)
        for s in range(SEQ_PER_STEP):
            rs = slice(s * T_PAD, (s + 1) * T_PAD)
            s0 = s0_ref[s, h]
            o = _nn(qg[rs].astype(BF16), s0.astype(BF16)) + o_intra[rs]
            o_ref[rs, hs] = _hgrn_out(o, ng_ref[:, hs], z_ref[rs, hs])
            decay = jnp.exp(big_g[rs]).T[:, n_new - 1:n_new]
            s_ref[s, h] = s0 * decay + _tn(kd[rs].astype(BF16), v[rs, hs].astype(BF16))


def _hgrn_sample(projs, state, hg_lower, hg_norm_g, n_new):
    nseq = state.shape[0]
    rows = SEQ_PER_STEP * T_PAD
    wst, masks, n_levels = _hgrn_constants(rows, T_PAD, n_new)
    col = lambda c: pl.BlockSpec((rows, B_KEY), lambda i: (i, c // B_KEY))
    st_spec = pl.BlockSpec((SEQ_PER_STEP, HG_HEADS, HG_DK, HG_DV), lambda i: (i, 0, 0, 0))
    return pl.pallas_call(
        functools.partial(_hgrn_sample_body, n_levels=n_levels, n_new=n_new),
        grid=(nseq // SEQ_PER_STEP,),
        in_specs=[col(C_QB), col(C_FB), col(C_IB), col(C_ZB), st_spec,
                  _const_spec(hg_lower.shape), _const_spec(hg_norm_g.shape),
                  _const_spec(wst.shape), _const_spec(masks.shape)],
        out_specs=[pl.BlockSpec((rows, B_WIDTH), lambda i: (i, 0)), st_spec],
        out_shape=[jax.ShapeDtypeStruct((nseq * T_PAD, B_WIDTH), F32),
                   jax.ShapeDtypeStruct(state.shape, F32)],
        compiler_params=_cparams(1),
        name="hgrn_sample",
    )(projs, projs, projs, projs, state, hg_lower, hg_norm_g, wst, masks)


def _merge_body(oa_ref, ob_ref, ma_ref, mb_ref, x_ref, wba_ref, wbb_ref, wout_ref, fg_ref, y_ref):
    a = _nn(oa_ref[...].astype(BF16), wba_ref[...])
    b = _nn(ob_ref[...].astype(BF16), wbb_ref[...])
    y = _sigmoid(ma_ref[...]) * a + _sigmoid(mb_ref[...]) * b
    h = x_ref[...] + _nn(y.astype(BF16), wout_ref[...])
    ms = jnp.mean(h * h, axis=-1, keepdims=True)
    y_ref[...] = h * lax.rsqrt(ms + EPS) * fg_ref[...]


def _merge(o_a, o_b, proj, x2d, w_ba, w_bb, w_out, final_g):
    m, d = x2d.shape
    tm = min(m, 512)
    row = lambda w, c: pl.BlockSpec((tm, w), lambda i: (i, c))
    return pl.pallas_call(
        _merge_body,
        grid=(m // tm,),
        in_specs=[row(A_WIDTH, 0), row(B_WIDTH, 0), row(d, C_MA // d), row(d, C_MA // d + 1), row(d, 0),
                  _const_spec(w_ba.shape), _const_spec(w_bb.shape), _const_spec(w_out.shape),
                  _const_spec((1, d))],
        out_specs=row(d, 0),
        out_shape=jax.ShapeDtypeStruct((m, d), F32),
        compiler_params=_cparams(1),
        name="merge_out",
    )(o_a, o_b, proj, proj, x2d, w_ba, w_bb, w_out, final_g.reshape(1, d))


def kernel(x_prompt, x_sample, cache_cmp_k, cache_cmp_v, cache_sel_k, cache_sel_v, state_win_k, state_win_v,
           state_hgrn, page_table, norm_g, w_in, cmp_pos_k, cmp_w1_k, cmp_w2_k, cmp_pos_v, cmp_w1_v, cmp_w2_v,
           rel_bias, hg_lower, hg_norm_g, w_branch_a, w_branch_b, w_out, final_g):
    depth = norm_g.shape[0]
    assert depth == 1, "single-layer trunk"
    b, t, d = x_prompt.shape
    nseq, n_new, _ = x_sample.shape
    assert C_MA % d == 0

    w = _arrange_w_in(w_in[0])
    n_cols = w.shape[0]
    w_ba = _permute_heads_rows(w_branch_a[0]).astype(BF16)
    w_bb = w_branch_b[0].astype(BF16)
    w_o = w_out[0].astype(BF16)
    cw_k = _compress_weights(cmp_pos_k[0], cmp_w1_k[0], cmp_w2_k[0])
    cw_v = _compress_weights(cmp_pos_v[0], cmp_w1_v[0], cmp_w2_v[0])
    ng = hg_norm_g[0].reshape(1, B_WIDTH)
    kv_cols = lambda a, c: a[..., C_KV + c * A_KV:C_KV + (c + 1) * A_KV]

    xp2 = x_prompt.reshape(b * t, d)
    proj, *kv_t = _project(xp2, norm_g[0], w, seq_len=t)
    proj3 = proj.reshape(b, t, n_cols)
    nc_p = (t - CMP_LEN) // CMP_STRIDE + 1
    kcmp, vcmp = _compress_prompt(proj3, cw_k, cw_v)
    bias_cmp = _bias_table(rel_bias, base=-(CMP_LEN - 1), step=TQ, row_stride=-CMP_STRIDE, col_stride=1,
                           rows=LANES, cols=TQ, steps=t // TQ)
    bias_tiles = _bias_table(rel_bias, base=0, step=TQ, row_stride=-1, col_stride=1, rows=TQ, cols=TQ,
                             steps=WINDOW // TQ + 2, lo=0, hi=WINDOW)
    o_a = _nsa_prompt(proj3, kcmp, vcmp, bias_cmp, bias_tiles)
    o_b, p_hgrn = _hgrn_prompt(proj3, hg_lower, ng)
    y_prompt = _merge(o_a.reshape(b * t, A_WIDTH), o_b.reshape(b * t, B_WIDTH), proj, xp2,
                      w_ba, w_bb, w_o, final_g).reshape(b, t, d)
    wb_p = min(WINDOW, t)
    chan_major = lambda a: jnp.transpose(a, (0, 2, 3, 1)).reshape(a.shape[0], A_KV, a.shape[1])
    row_major = lambda a: jnp.transpose(a.reshape(a.shape[0], NSA_KV, NSA_HD, a.shape[2]), (0, 3, 1, 2))
    p_kv = ([row_major(kv_t[c])[None] for c in range(4)]
            + [row_major(kv_t[c][:, :, t - wb_p:])[None] for c in (4, 5)])
    p_state = p_kv + [p_hgrn[None]]

    xs_pad = jnp.pad(x_sample, ((0, 0), (0, T_PAD - n_new), (0, 0))).reshape(nseq * T_PAD, d)
    projs = _project(xs_pad, norm_g[0], w)
    projs3 = projs.reshape(nseq, T_PAD, n_cols)
    pool = cache_cmp_k.shape[1]
    caches =[chan_major(c[0]) for c in (cache_cmp_k, cache_cmp_v, cache_sel_k, cache_sel_v)]
    wbuf = state_win_k.shape[2]
    o_as, s_win_k, s_win_v = _nsa_sample(projs3, caches, chan_major(state_win_k[0]), chan_major(state_win_v[0]),
                                         page_table, cw_k, cw_v, rel_bias, n_new)
    s_win_k, s_win_v = row_major(s_win_k), row_major(s_win_v)
    o_bs, s_hgrn = _hgrn_sample(projs, state_hgrn[0], hg_lower, ng, n_new)
    y_s = _merge(o_as.reshape(nseq * T_PAD, A_WIDTH), o_bs, projs, xs_pad, w_ba, w_bb, w_o, final_g)
    y_sample = y_s.reshape(nseq, T_PAD, d)[:, :n_new]
    s_kv = [kv_cols(projs3, c)[:, :n_new].reshape(1, nseq, n_new, NSA_KV, NSA_HD) for c in range(4)]
    s_state = s_kv + [s_win_k.reshape(1, nseq, wbuf, NSA_KV, NSA_HD),
                      s_win_v.reshape(1, nseq, wbuf, NSA_KV, NSA_HD), s_hgrn[None]]

    return (y_prompt, y_sample, *p_state, *s_state)
```

```python
import functools
import math

import numpy as np
import jax
import jax.numpy as jnp
from jax import lax
from jax.experimental import pallas as pl
from jax.experimental.pallas import tpu as pltpu

F32 = jnp.float32
BF16 = jnp.bfloat16

NSA_HEADS = 8
NSA_KV = 2
NSA_HD = 64
NSA_REP = NSA_HEADS // NSA_KV
CMP_LEN = 32
CMP_STRIDE = 16
CMP_HID = 64
SEL_BLOCK = 64
SEL_TOPN = 8
WINDOW = 512
HG_HEADS = 4
HG_DK = 128
HG_DV = 128
HG_CHUNK = 64
N_BUCKETS = 32
MAX_DIST = 128
EPS = 1e-6
NEG = -1e30
FORCED = 1e9
PAGE_SIZE = 128

A_WIDTH = NSA_HEADS * NSA_HD
A_KV = NSA_KV * NSA_HD
B_WIDTH = HG_HEADS * HG_DV
B_KEY = HG_HEADS * HG_DK
N_KV_STREAMS = 6
N_CACHED_STREAMS = 4

LANES = 128
SUBLANES = 8
VMEM_LIMIT_BYTES = 56 * 1024 * 1024

TQ = 128
T_PAD = SUBLANES
N_ROWBLK = NSA_HEADS

C_Q = 0
C_KV = C_Q + A_WIDTH
C_GATE = C_KV + N_KV_STREAMS * A_KV
C_ZA = C_GATE + 2 * LANES
C_QB = C_ZA + A_WIDTH
C_FB = C_QB + B_KEY
C_IB = C_FB + B_KEY
C_ZB = C_IB + B_WIDTH
C_MA = C_ZB + B_WIDTH

CMP_SRC_ROWS = (LANES - 1) * CMP_STRIDE + CMP_LEN
CMP_PITCH = CMP_STRIDE + 1


def _nn(a, b):
    return jnp.dot(a, b, preferred_element_type=F32)


def _nt(a, b):
    return lax.dot_general(a, b, (((1,), (1,)), ((), ())), preferred_element_type=F32)


def _tn(a, b):
    return lax.dot_general(a, b, (((0,), (0,)), ((), ())), preferred_element_type=F32)


def _sigmoid(x):
    return 1.0 / (1.0 + jnp.exp(-x))


def _silu(x):
    return x * _sigmoid(x)


def _split2(x):
    hi = x.astype(BF16)
    lo = (x - hi.astype(F32)).astype(BF16)
    return hi, lo


def _cparams(n_grid):
    return pltpu.CompilerParams(dimension_semantics=("arbitrary",) * n_grid,
                                vmem_limit_bytes=VMEM_LIMIT_BYTES)


def _t5_bucket(rel):
    n = jnp.maximum(rel, 0)
    exact = N_BUCKETS // 2
    nf = jnp.maximum(n, 1).astype(F32)
    scaled = jnp.log(nf / exact) / math.log(MAX_DIST / exact) * (N_BUCKETS - exact)
    large = exact + jnp.floor(scaled).astype(jnp.int32)
    large = jnp.minimum(large, N_BUCKETS - 1)
    return jnp.where(n < exact, n, large)


def _rowblk_head(rb):
    return NSA_REP * (rb % NSA_KV) + rb // NSA_KV


def _bias_body(tab_ref, o_ref, *, base, step, row_stride, col_stride, lo, hi, cols):
    sub_r = min(o_ref.shape[1], 4 * SUBLANES)
    sub_c = min(cols, LANES)
    shape = (sub_r, sub_c)
    row = lax.broadcasted_iota(jnp.int32, shape, 0)
    col = lax.broadcasted_iota(jnp.int32, shape, 1)
    for r0 in range(0, o_ref.shape[1], sub_r):
        for c0 in range(0, cols, sub_c):
            rel = base + step * pl.program_id(0) + row_stride * (row + r0) + col_stride * (col + c0)
            bucket = _t5_bucket(rel)
            accs = [jnp.zeros(shape, F32)] * N_ROWBLK
            for k in range(N_BUCKETS):
                hit = bucket == k
                accs = [jnp.where(hit, tab_ref[k, _rowblk_head(rb)], accs[rb]) for rb in range(N_ROWBLK)]
            for rb in range(N_ROWBLK):
                acc = accs[rb]
                if lo is not None:
                    acc = jnp.where((rel >= lo) & (rel < hi), acc, NEG)
                o_ref[rb, r0:r0 + sub_r, c0:c0 + sub_c] = acc


def _bias_table(rel_bias, *, base, step, row_stride, col_stride, rows, cols, steps, lo=None, hi=None):
    return pl.pallas_call(
        functools.partial(_bias_body, base=base, step=step, row_stride=row_stride, col_stride=col_stride,
                          lo=lo, hi=hi, cols=cols),
        grid=(steps,),
        in_specs=[pl.BlockSpec(memory_space=pltpu.SMEM)],
        out_specs=pl.BlockSpec((None, N_ROWBLK, rows, cols), lambda i: (i, 0, 0, 0)),
        out_shape=jax.ShapeDtypeStruct((steps, N_ROWBLK, rows, cols), F32),
        compiler_params=_cparams(1),
        name="bias_table",
    )(rel_bias)


def _bias_table_sample(rel_bias, base, col_stride, cols):
    return _bias_table(rel_bias, base=base, step=0, row_stride=1, col_stride=-col_stride, rows=T_PAD, cols=cols,
                       steps=1)[0]


def _proj_body(x_ref, g_ref, w_ref, o_ref, *kvt_refs):
    x = x_ref[...]
    ms = jnp.mean(x * x, axis=-1, keepdims=True)
    xn = (x * lax.rsqrt(ms + EPS) * g_ref[...]).astype(BF16)
    o = _nt(xn, w_ref[...])
    o_ref[...] = o
    for c, kvt_ref in enumerate(kvt_refs):
        kvt_ref[...] = o[:, C_KV + c * A_KV:C_KV + (c + 1) * A_KV].T


PROJ_ROWS = 512


def _project(x2d, norm_g, w, seq_len=None):
    m, d = x2d.shape
    n = w.shape[0]
    tm = min(m, PROJ_ROWS)
    out_specs = [pl.BlockSpec((tm, n), lambda i: (i, 0))]
    out_shape = [jax.ShapeDtypeStruct((m, n), F32)]
    if seq_len is not None:
        per_seq = seq_len // tm
        assert seq_len % tm == 0
        out_specs += [pl.BlockSpec((None, A_KV, tm), lambda i: (i // per_seq, 0, i % per_seq))] * N_KV_STREAMS
        out_shape += [jax.ShapeDtypeStruct((m // seq_len, A_KV, seq_len), F32)] * N_KV_STREAMS
    outs = pl.pallas_call(
        _proj_body,
        grid=(m // tm,),
        in_specs=[pl.BlockSpec((tm, d), lambda i: (i, 0)),
                  pl.BlockSpec((1, d), lambda i: (0, 0)),
                  pl.BlockSpec((n, d), lambda i: (0, 0), pipeline_mode=pl.Buffered(1))],
        out_specs=out_specs,
        out_shape=out_shape,
        compiler_params=_cparams(1),
        name="in_proj",
    )(x2d, norm_g.reshape(1, d), w)
    return outs if seq_len is not None else outs[0]


def _permute_heads_rows(w):
    return w.reshape(NSA_KV, NSA_REP, NSA_HD, w.shape[-1]).swapaxes(0, 1).reshape(A_WIDTH, w.shape[-1])


def _arrange_w_in(w_in):
    d = w_in.shape[0]
    wt = w_in.T
    sizes = (A_WIDTH, N_KV_STREAMS * A_KV, 3 * NSA_HEADS, A_WIDTH, 2 * B_KEY + 2 * B_WIDTH + 2 * d)
    offs = np.concatenate([[0], np.cumsum(sizes)])
    q_a, kv_a, g_a, z_a, rest = [wt[offs[i]:offs[i + 1]] for i in range(5)]
    gate_pad = jnp.zeros((2 * LANES - 3 * NSA_HEADS, d), w_in.dtype)
    return jnp.concatenate([_permute_heads_rows(q_a), kv_a, g_a, gate_pad, _permute_heads_rows(z_a), rest],
                           axis=0).astype(BF16)


def _compress(streams, pitch):
    accs = [jnp.zeros((len(srcs) * LANES, LANES), F32) for srcs, _, _, _ in streams]
    for j in range(CMP_LEN):
        start = (j // CMP_STRIDE) * pitch + j % CMP_STRIDE
        for n, (srcs, pos_ref, w1_ref, _) in enumerate(streams):
            xj = jnp.concatenate([r[pl.ds(start, LANES, stride=pitch), :] for r in srcs], axis=0) + pos_ref[j:j + 1, :]
            accs[n] = accs[n] + _nn(xj.astype(BF16), w1_ref[j])
    return [_nn(_silu(acc).astype(BF16), w2_ref[...]) for acc, (_, _, _, w2_ref) in zip(accs, streams)]


def _compress_prompt_body(kc_ref, vc_ref, posk_ref, w1k_ref, w2k_ref, posv_ref, w1v_ref, w2v_ref,
                          ko_ref, vo_ref, kbuf_ref, vbuf_ref):
    t = kc_ref.shape[0]
    for buf_ref, src_ref in ((kbuf_ref, kc_ref), (vbuf_ref, vc_ref)):
        buf_ref[t:, :] = jnp.zeros((buf_ref.shape[0] - t, LANES), F32)
        buf_ref[:t, :] = src_ref[...]
    ko_ref[...], vo_ref[...] = _compress([([kbuf_ref], posk_ref, w1k_ref, w2k_ref),
                                          ([vbuf_ref], posv_ref, w1v_ref, w2v_ref)], CMP_STRIDE)


def _block_diag2(w):
    z = jnp.zeros_like(w)
    return jnp.concatenate([jnp.concatenate([w, z], axis=-1), jnp.concatenate([z, w], axis=-1)], axis=-2)


def _compress_weights(pos, w1, w2):
    pos2 = jnp.concatenate([pos, pos], axis=-1)
    w1bd = _block_diag2(w1.reshape(CMP_LEN, NSA_HD, CMP_HID)).astype(BF16)
    w2bd = _block_diag2(w2).astype(BF16)
    return pos2, w1bd, w2bd


def _const_spec(shape, single_buffer=False):
    nd = len(shape)
    if single_buffer:
        return pl.BlockSpec(shape, lambda *_: (0,) * nd, pipeline_mode=pl.Buffered(1))
    return pl.BlockSpec(shape, lambda *_: (0,) * nd)


def _compress_prompt(proj3, cw_k, cw_v):
    b, t, _ = proj3.shape
    col = C_KV // LANES
    w_specs = [_const_spec(a.shape) for a in cw_k + cw_v]
    return pl.pallas_call(
        _compress_prompt_body,
        grid=(b,),
        in_specs=[pl.BlockSpec((None, t, LANES), lambda i: (i, 0, col)),
                  pl.BlockSpec((None, t, LANES), lambda i: (i, 0, col + 1))] + w_specs,
        out_specs=[pl.BlockSpec((None, LANES, LANES), lambda i: (i, 0, 0))] * 2,
        out_shape=[jax.ShapeDtypeStruct((b, LANES, LANES), F32)] * 2,
        scratch_shapes=[pltpu.VMEM((max(t + CMP_LEN, CMP_SRC_ROWS), LANES), F32)] * 2,
        compiler_params=_cparams(1),
        name="compress_prompt",
    )(proj3, proj3, *cw_k, *cw_v)


def _select_blocks(imp, qpos, nb, axis):
    j = lax.broadcasted_iota(jnp.int32, imp.shape, axis)
    cur = qpos // SEL_BLOCK
    valid = j * SEL_BLOCK <= qpos
    forced = (j == 0) | (j == cur) | (j == cur - 1)
    score = jnp.where(valid, jnp.where(forced, FORCED, imp), -1.0)
    score = jnp.where(j < nb, score, -2.0)
    rank = jnp.zeros(imp.shape, F32)
    for i in range(nb):
        s_i = lax.slice_in_dim(score, i, i + 1, axis=axis)
        tie = jnp.where(i < j, 1.0, 0.0)
        rank = rank + jnp.where(s_i > score, 1.0, jnp.where(s_i == score, tie, 0.0))
    return jnp.where((rank < min(SEL_TOPN, nb)) & (j < nb), 1.0, 0.0)


def _masked_softmax(s, mask):
    s = jnp.where(mask, s, NEG)
    m = jnp.max(s, axis=-1, keepdims=True)
    e = jnp.exp(s - m)
    p = e / jnp.sum(e, axis=-1, keepdims=True)
    return jnp.where(mask, p, 0.0)


def _stack_queries(q_ref, qbd_ref, rows):
    lane = lax.broadcasted_iota(jnp.int32, (rows, LANES), 1)
    for r in range(NSA_REP):
        blk = q_ref[:, r * LANES:(r + 1) * LANES] * (NSA_HD ** -0.5)
        qbd_ref[(2 * r) * rows:(2 * r + 1) * rows, :] = jnp.where(lane < NSA_HD, blk, 0.0).astype(BF16)
        qbd_ref[(2 * r + 1) * rows:(2 * r + 2) * rows, :] = jnp.where(lane >= NSA_HD, blk, 0.0).astype(BF16)


def _gate_and_store(gate_ref, za_ref, o_ref, o_cmp, o_sel, o_win, rows):
    gs = _sigmoid(gate_ref[...])
    lane = lax.broadcasted_iota(jnp.int32, (rows, LANES), 1)
    for r in range(NSA_REP):
        parts = []
        for g in range(NSA_KV):
            rb = 2 * r + g
            h = _rowblk_head(rb)
            o = (gs[:, h:h + 1] * o_cmp[rb] + gs[:, NSA_HEADS + h:NSA_HEADS + h + 1] * o_sel[rb]
                 + gs[:, 2 * NSA_HEADS + h:2 * NSA_HEADS + h + 1] * o_win[rb])
            parts.append(o)
        blk = jnp.where(lane < NSA_HD, parts[0], parts[1])
        o_ref[:, r * LANES:(r + 1) * LANES] = blk * _silu(za_ref[:, r * LANES:(r + 1) * LANES])


SEL_TILES_PER_ITER = 2


def _nsa_prompt_body(q_ref, gate_ref, za_ref, ks_ref, vs_ref, kw_ref, vw_ref, kcmp_ref, vcmp_ref,
                     bcmp_ref, btile_ref, covert_ref, o_ref,
                     qt_ref, ksb_ref, vst_ref, kwb_ref, vwt_ref, vcmpt_ref, mask_ref, s_ref, m_ref, l_ref, acc_ref, res_ref,
                     *, nc, nb):
    qb = pl.program_id(1)
    nkt = ksb_ref.shape[0]
    sub = lax.broadcasted_iota(jnp.int32, (LANES, TQ), 0)
    tok = lax.broadcasted_iota(jnp.int32, (LANES, TQ), 1)

    @pl.when(qb == 0)
    def _():
        for kt in range(nkt):
            rows = slice(kt * TQ, (kt + 1) * TQ)
            ksb_ref[kt] = ks_ref[rows, :].astype(BF16)
            kwb_ref[kt] = kw_ref[rows, :].astype(BF16)
            vst_ref[kt] = vs_ref[rows, :].T.astype(BF16)
            vwt_ref[kt] = vw_ref[rows, :].T.astype(BF16)
        vcmpt_ref[...] = vcmp_ref[...].T.astype(BF16)

    for r in range(NSA_REP):
        qt = (q_ref[:, r * LANES:(r + 1) * LANES] * (NSA_HD ** -0.5)).T
        qt_ref[2 * r] = jnp.where(sub < NSA_HD, qt, 0.0).astype(BF16)
        qt_ref[2 * r + 1] = jnp.where(sub >= NSA_HD, qt, 0.0).astype(BF16)

    valid = ((CMP_STRIDE * sub + CMP_LEN - 1) <= qb * TQ + tok) & (sub < nc)
    kcb = kcmp_ref[...].astype(BF16)
    psum = [None] * NSA_KV
    for rb in range(N_ROWBLK):
        s = jnp.where(valid, _nn(kcb, qt_ref[rb]) + bcmp_ref[rb], NEG)
        e = jnp.exp(s - jnp.max(s, axis=0, keepdims=True))
        p = jnp.where(valid, e / jnp.sum(e, axis=0, keepdims=True), 0.0)
        res_ref[0, rb] = _nn(vcmpt_ref[...], p.astype(BF16))
        g = rb % NSA_KV
        psum[g] = p if psum[g] is None else psum[g] + p

    nbp = _round_up(nb, SUBLANES)
    upper = sub < SEL_BLOCK
    for g in range(NSA_KV):
        p_hi, p_lo = _split2(psum[g])
        imp = _nn(covert_ref[...], p_hi) + _nn(covert_ref[...], p_lo)
        sel = _select_blocks(imp[:nbp], qb * TQ + tok[:1], nb, 0)
        for kt in range(nkt):
            picked = jnp.where(upper, sel[2 * kt:2 * kt + 1, :], sel[2 * kt + 1:2 * kt + 2, :]) > 0.5
            mask_ref[kt, g] = jnp.where(kt <= qb, jnp.where(picked, 0.0, NEG), NEG)

    m_ref[...] = jnp.full(m_ref.shape, NEG, F32)
    l_ref[...] = jnp.zeros(l_ref.shape, F32)
    acc_ref[...] = jnp.zeros(acc_ref.shape, F32)

    def sel_scores(pair, buf):
        for u in range(SEL_TILES_PER_ITER):
            kt = jnp.minimum(pair * SEL_TILES_PER_ITER + u, nkt - 1)
            k_t = ksb_ref[kt]
            bidx = jnp.clip(qb - kt, 0, 2)
            for g in range(NSA_KV):
                madd = mask_ref[kt, g]
                for rb in range(g, N_ROWBLK, NSA_KV):
                    s_ref[buf, u, rb] = _nn(k_t, qt_ref[rb]) + btile_ref[bidx, rb] + madd

    def sel_update(pair, buf):
        v_ts = [vst_ref[pair * SEL_TILES_PER_ITER + u] for u in range(SEL_TILES_PER_ITER)]
        for rb in range(N_ROWBLK):
            ss = [s_ref[buf, u, rb] for u in range(SEL_TILES_PER_ITER)]
            m_old = m_ref[rb]
            m_new = m_old
            for s in ss:
                m_new = jnp.maximum(m_new, jnp.max(s, axis=0, keepdims=True))
            alpha = jnp.exp(m_old - m_new)
            l_new = alpha * l_ref[rb]
            acc = alpha * acc_ref[rb]
            for s, v_t in zip(ss, v_ts):
                p = jnp.exp(s - m_new)
                l_new = l_new + jnp.sum(p, axis=0, keepdims=True)
                acc = acc + _nn(v_t, p.astype(BF16))
            l_ref[rb] = l_new
            acc_ref[rb] = acc
            m_ref[rb] = m_new

    def sel_body(it, carry):
        sel_scores(2 * it + 1, 1)
        sel_update(2 * it, 0)
        sel_scores(2 * it + 2, 0)
        sel_update(2 * it + 1, 1)
        return carry

    sel_scores(0, 0)
    lax.fori_loop(0, qb // (2 * SEL_TILES_PER_ITER) + 1, sel_body, 0)
    res_ref[1] = acc_ref[...] / l_ref[...]

    n_wt = WINDOW // TQ + 1
    for rb in range(N_ROWBLK):
        qc = qt_ref[rb]
        ss = []
        for dist in range(n_wt):
            kt = jnp.maximum(qb - dist, 0)
            bidx = jnp.where(qb >= dist, dist, n_wt)
            ss.append(_nn(kwb_ref[kt], qc) + btile_ref[bidx, rb])
        m = jnp.max(ss[0], axis=0, keepdims=True)
        for s in ss[1:]:
            m = jnp.maximum(m, jnp.max(s, axis=0, keepdims=True))
        l_w = None
        acc = None
        for dist, s in enumerate(ss):
            p = jnp.exp(s - m)
            ps = jnp.sum(p, axis=0, keepdims=True)
            pv = _nn(vwt_ref[jnp.maximum(qb - dist, 0)], p.astype(BF16))
            l_w = ps if l_w is None else l_w + ps
            acc = pv if acc is None else acc + pv
        res_ref[2, rb] = acc / l_w

    gt = _sigmoid(gate_ref[...]).T
    for r in range(NSA_REP):
        parts = []
        for g in range(NSA_KV):
            rb = 2 * r + g
            h = _rowblk_head(rb)
            parts.append(gt[h:h + 1, :] * res_ref[0, rb]
                         + gt[NSA_HEADS + h:NSA_HEADS + h + 1, :] * res_ref[1, rb]
                         + gt[2 * NSA_HEADS + h:2 * NSA_HEADS + h + 1, :] * res_ref[2, rb])
        blk = jnp.where(sub < NSA_HD, parts[0], parts[1]).T
        o_ref[:, r * LANES:(r + 1) * LANES] = (blk * _silu(za_ref[:, r * LANES:(r + 1) * LANES])).astype(o_ref.dtype)


def _cover_matrix(nc, nb):
    cs = np.arange(nc) * CMP_STRIDE
    ce = cs + CMP_LEN
    bs = np.arange(nb) * SEL_BLOCK
    be = bs + SEL_BLOCK
    ov = np.clip(np.minimum(ce[:, None], be[None, :]) - np.maximum(cs[:, None], bs[None, :]), 0, None) / CMP_STRIDE
    out = np.zeros((LANES, LANES), np.float32)
    out[:nc, :nb] = ov
    return jnp.asarray(out, BF16)


def _expand_matrix(nkeys):
    j = np.arange(LANES)[:, None]
    k = np.arange(nkeys)[None, :]
    return jnp.asarray((k // SEL_BLOCK == j).astype(np.float32), BF16)


def _nsa_prompt(proj3, kcmp, vcmp, bias_cmp, bias_tiles):
    b, t, _ = proj3.shape
    nq = t // TQ
    nc = (t - CMP_LEN) // CMP_STRIDE + 1
    nb = t // SEL_BLOCK
    assert nc <= LANES and nb <= LANES and t % (TQ * 2 * SEL_TILES_PER_ITER) == 0
    kv_col = C_KV // LANES
    kv_spec = lambda c: pl.BlockSpec((None, t, LANES), lambda i, j: (i, 0, kv_col + c))
    cover_t = _cover_matrix(nc, nb).T
    tile_bf16 = pltpu.VMEM((nq, TQ, LANES), BF16)
    return pl.pallas_call(
        functools.partial(_nsa_prompt_body, nc=nc, nb=nb),
        grid=(b, nq),
        in_specs=[pl.BlockSpec((None, TQ, A_WIDTH), lambda i, j: (i, j, C_Q // A_WIDTH)),
                  pl.BlockSpec((None, TQ, LANES), lambda i, j: (i, j, C_GATE // LANES)),
                  pl.BlockSpec((None, TQ, A_WIDTH), lambda i, j: (i, j, C_ZA // A_WIDTH)),
                  kv_spec(2), kv_spec(3), kv_spec(4), kv_spec(5),
                  pl.BlockSpec((None, LANES, LANES), lambda i, j: (i, 0, 0)),
                  pl.BlockSpec((None, LANES, LANES), lambda i, j: (i, 0, 0)),
                  pl.BlockSpec((None, N_ROWBLK, LANES, TQ), lambda i, j: (j, 0, 0, 0)),
                  _const_spec(bias_tiles.shape), _const_spec(cover_t.shape)],
        out_specs=pl.BlockSpec((None, TQ, A_WIDTH), lambda i, j: (i, j, 0)),
        out_shape=jax.ShapeDtypeStruct((b, t, A_WIDTH), BF16),
        scratch_shapes=[pltpu.VMEM((N_ROWBLK, LANES, TQ), BF16),
                        tile_bf16, tile_bf16, tile_bf16, tile_bf16,
                        pltpu.VMEM((LANES, LANES), BF16),
                        pltpu.VMEM((nq, NSA_KV, TQ, TQ), F32),
                        pltpu.VMEM((2, SEL_TILES_PER_ITER, N_ROWBLK, TQ, TQ), F32),
                        pltpu.VMEM((N_ROWBLK, 1, TQ), F32),
                        pltpu.VMEM((N_ROWBLK, 1, TQ), F32),
                        pltpu.VMEM((N_ROWBLK, LANES, TQ), F32),
                        pltpu.VMEM((3, N_ROWBLK, LANES, TQ), F32)],
        compiler_params=_cparams(2),
        name="nsa_prompt",
    )(proj3, proj3, proj3, proj3, proj3, proj3, proj3, kcmp, vcmp, bias_cmp, bias_tiles, cover_t)


NSA_SEQ_PER_STEP = 4


def _group_page_copy(cache_refs, pt_ref, pbuf, sems, step, slot, s, c, pg, n_pages):
    seq = step * NSA_SEQ_PER_STEP + s
    return pltpu.make_async_copy(cache_refs[c].at[pt_ref[seq, pg]], pbuf.at[slot, s, c * n_pages + pg],
                                 sems.at[slot])


def _nsa_sample_group_body(pt_ref, *refs, n_pages, **static):
    cache_refs = refs[:N_CACHED_STREAMS]
    per_seq = refs[N_CACHED_STREAMS:N_CACHED_STREAMS + 11]
    consts = refs[N_CACHED_STREAMS + 11:N_CACHED_STREAMS + 22]
    outs = refs[N_CACHED_STREAMS + 22:N_CACHED_STREAMS + 25]
    scratch = refs[N_CACHED_STREAMS + 25:-2]
    pbuf, sems = refs[-2:]
    i = pl.program_id(0)
    slot = i % 2

    def all_pages(step, slot_):
        return [_group_page_copy(cache_refs, pt_ref, pbuf, sems, step, slot_, s, c, pg, n_pages)
                for s in range(NSA_SEQ_PER_STEP) for c in range(N_CACHED_STREAMS) for pg in range(n_pages)]

    @pl.when(i == 0)
    def _():
        for cp in all_pages(0, 0):
            cp.start()

    @pl.when(i + 1 < pl.num_programs(0))
    def _():
        for cp in all_pages(i + 1, 1 - slot):
            cp.start()

    for cp in all_pages(i, slot):
        cp.wait()
    pages = [[[pbuf.at[slot, s, c * n_pages + pg] for pg in range(n_pages)] for c in range(N_CACHED_STREAMS)]
             for s in range(NSA_SEQ_PER_STEP)]
    _nsa_sample_group(pages, per_seq, consts, outs, scratch, n_pages=n_pages, **static)


def _nsa_sample_group(pages, per_seq, consts, outs, scratch, *, n_pages, past, wbuf, nc, nb):
    q_ref, kc_ref, vc_ref, ks_ref, vs_ref, kw_ref, vw_ref, gate_ref, za_ref, wk_ref, wv_ref = per_seq
    (posk_ref, w1k_ref, w2k_ref, posv_ref, w1v_ref, w2v_ref, bcmp_ref, bsel_ref, bwin_ref,
     cover_ref, emat_ref) = consts
    o_ref, swk_ref, swv_ref = outs
    kc_all, vc_all, newt_ref, qbd_ref = scratch
    nseq = NSA_SEQ_PER_STEP
    seqs = range(nseq)
    n_keys = bsel_ref.shape[2]
    n_new = T_PAD // 2
    chunks_per_page = PAGE_SIZE // CMP_STRIDE
    nrow = N_ROWBLK * T_PAD

    for dst, cache, new_ref in ((kc_all, 0, kc_ref), (vc_all, 1, vc_ref)):
        for s in seqs:
            for pg in range(n_pages):
                rows = pages[s][cache][pg][...].T
                for c in range(chunks_per_page):
                    r0 = (pg * chunks_per_page + c) * CMP_PITCH
                    dst[s, r0:r0 + CMP_STRIDE, :] = rows[c * CMP_STRIDE:(c + 1) * CMP_STRIDE, :]
            r0 = (past // CMP_STRIDE) * CMP_PITCH
            dst[s, r0:r0 + T_PAD, :] = new_ref[s]
            dst[s, r0 + T_PAD:, :] = jnp.zeros((dst.shape[1] - r0 - T_PAD, LANES), F32)
    kcmp, vcmp = [c.astype(BF16) for c in _compress(
        [([kc_all.at[s] for s in seqs], posk_ref, w1k_ref, w2k_ref),
         ([vc_all.at[s] for s in seqs], posv_ref, w1v_ref, w2v_ref)], CMP_PITCH)]
    per_s = lambda x, s: x[s * LANES:(s + 1) * LANES]

    newt_ref[...] = jnp.zeros(newt_ref.shape, F32)
    for s in seqs:
        for idx, new_ref in enumerate((ks_ref, vs_ref, kw_ref, vw_ref)):
            newt_ref[s, idx, 0:T_PAD, :] = new_ref[s]
        _stack_queries(q_ref.at[s], qbd_ref.at[s], T_PAD)
    qbd = [qbd_ref[s] for s in seqs]
    tok = lax.broadcasted_iota(jnp.int32, (T_PAD, LANES), 0)
    lane = lax.broadcasted_iota(jnp.int32, (T_PAD, LANES), 1)
    qpos = past + tok
    stack = lambda xs: jnp.concatenate(xs, axis=0)
    by_head = lambda x: x.reshape(nseq, NSA_REP, NSA_KV, T_PAD, LANES)

    s_c = by_head(stack([_nt(qbd[s], per_s(kcmp, s)) for s in seqs])) + _by_head_bias(bcmp_ref[...])
    mask_c = ((CMP_STRIDE * lane + CMP_LEN - 1) <= qpos) & (lane < nc)
    p = _masked_softmax(s_c, mask_c[None, None, None]).reshape(nseq, nrow, LANES)
    o_cmp = [_nn(p[s].astype(BF16), per_s(vcmp, s)).reshape(N_ROWBLK, T_PAD, LANES) for s in seqs]

    psum = p.reshape(nseq, NSA_REP, NSA_KV * T_PAD, LANES).sum(axis=1).reshape(nseq * NSA_KV * T_PAD, LANES)
    p_hi, p_lo = _split2(psum)
    imp = _nn(p_hi, cover_ref[...]) + _nn(p_lo, cover_ref[...])
    sel = _select_blocks(imp.reshape(nseq, NSA_KV, T_PAD, LANES), qpos[None, None], nb, 3)
    selk = _nn(sel.reshape(nseq * NSA_KV * T_PAD, LANES).astype(BF16), emat_ref[...])
    selk = selk.reshape(nseq, NSA_KV, T_PAD, n_keys)

    def softmax_pv(scores, values):
        m = scores[0]
        for s_t in scores[1:]:
            m = jnp.maximum(m, s_t)
        m = jnp.max(m, axis=-1, keepdims=True)
        acc = [None] * nseq
        tot = None
        for s_t, (v_ts, channel_major) in zip(scores, values):
            p_t = jnp.exp(s_t - m)
            pb = p_t.reshape(nseq, nrow, LANES).astype(BF16)
            for s in seqs:
                pv = _nt(pb[s], v_ts[s]) if channel_major else _nn(pb[s], v_ts[s])
                acc[s] = pv if acc[s] is None else acc[s] + pv
            tot = p_t if tot is None else tot + p_t
        inv = 1.0 / jnp.sum(tot, axis=-1, keepdims=True).reshape(nseq, nrow, 1)
        return [(acc[s] * inv[s]).reshape(N_ROWBLK, T_PAD, LANES) for s in seqs]

    scores, values = [], []
    for t in range(n_pages + 1):
        cols = slice(t * LANES, (t + 1) * LANES)
        allowed = ((t * LANES + lane) <= qpos)[None, None] & (selk[:, :, :, cols] > 0.5)
        madd = jnp.where(allowed, 0.0, NEG)[:, None]
        if t < n_pages:
            raw = [_nn(qbd[s], pages[s][2][t][...].astype(BF16)) for s in seqs]
            values.append(([pages[s][3][t][...].astype(BF16) for s in seqs], True))
        else:
            raw = [_nt(qbd[s], newt_ref[s, 0].astype(BF16)) for s in seqs]
            values.append(([newt_ref[s, 1].astype(BF16) for s in seqs], False))
        scores.append(by_head(stack(raw)) + _by_head_bias(bsel_ref[:, :, cols]) + madd)
    o_sel = softmax_pv(scores, values)

    scores, values = [], []
    kwt = [wk_ref[s].astype(BF16) for s in seqs]
    vwt = [wv_ref[s].astype(BF16) for s in seqs]
    for t in range(wbuf // LANES + 1):
        cols = slice(t * LANES, (t + 1) * LANES)
        rel_w = wbuf + tok - (t * LANES + lane)
        madd = jnp.where((rel_w >= 0) & (rel_w < WINDOW), 0.0, NEG)[None, None, None]
        if t < wbuf // LANES:
            raw = [_nn(qbd[s], kwt[s][:, cols]) for s in seqs]
            values.append(([vwt[s][:, cols] for s in seqs], True))
        else:
            raw = [_nt(qbd[s], newt_ref[s, 2].astype(BF16)) for s in seqs]
            values.append(([newt_ref[s, 3].astype(BF16) for s in seqs], False))
        scores.append(by_head(stack(raw)) + _by_head_bias(bwin_ref[:, :, cols]) + madd)
    o_win = softmax_pv(scores, values)

    lane_full = lax.broadcasted_iota(jnp.int32, (LANES, LANES), 1)
    for s in seqs:
        _gate_and_store(gate_ref.at[s], za_ref.at[s], o_ref.at[s], o_cmp[s], o_sel[s], o_win[s], T_PAD)
        for idx, w_ref, out_ref in ((2, wk_ref, swk_ref), (3, wv_ref, swv_ref)):
            rolled = pltpu.roll(w_ref[s], wbuf - n_new, axis=1)
            tail = pltpu.roll(newt_ref[s, idx], LANES - n_new, axis=0).T
            out_ref[s, :, :wbuf - LANES] = rolled[:, :wbuf - LANES]
            out_ref[s, :, wbuf - LANES:] = jnp.where(lane_full >= LANES - n_new, tail, rolled[:, wbuf - LANES:])


def _by_head_bias(b):
    return b.reshape(1, NSA_REP, NSA_KV, T_PAD, b.shape[-1])


def _round_up(x, m):
    return (x + m - 1) // m * m


def _nsa_sample(projs3, caches, wk, wv, page_table, cw_k, cw_v, rel_bias, n_new):
    nseq, n_pages = page_table.shape
    past = n_pages * PAGE_SIZE
    wbuf = wk.shape[2]
    length = past + n_new
    nc = (length - CMP_LEN) // CMP_STRIDE + 1
    nb = -(-length // SEL_BLOCK)
    n_keys = past + LANES
    n_win = wbuf + LANES
    pitched = lambda p: (p // CMP_STRIDE) * CMP_PITCH + p % CMP_STRIDE
    n_cmp_rows = _round_up(max(pitched(past) + T_PAD, pitched(CMP_SRC_ROWS - 1) + 1), SUBLANES)
    assert nc <= LANES and nb <= LANES and n_new == T_PAD // 2 and nb * SEL_BLOCK <= n_keys
    assert wbuf % LANES == 0 and PAGE_SIZE == LANES

    bias_cmp = _bias_table_sample(rel_bias, past - (CMP_LEN - 1), CMP_STRIDE, LANES)
    bias_sel = _bias_table_sample(rel_bias, past, 1, n_keys)
    bias_win = _bias_table_sample(rel_bias, wbuf, 1, n_win)
    cover = _cover_matrix(nc, nb)
    emat = _expand_matrix(n_keys)

    sps = NSA_SEQ_PER_STEP
    assert nseq % sps == 0

    def new_spec(width, col):
        return pl.BlockSpec((sps, T_PAD, width), lambda i, pt: (i, 0, col))

    kv_col = C_KV // LANES
    in_specs = [pl.BlockSpec(memory_space=pl.ANY)] * N_CACHED_STREAMS
    in_specs += [new_spec(A_WIDTH, C_Q // A_WIDTH)]
    in_specs += [new_spec(LANES, kv_col + c) for c in range(N_KV_STREAMS)]
    in_specs += [new_spec(LANES, C_GATE // LANES), new_spec(A_WIDTH, C_ZA // A_WIDTH)]
    win_spec = pl.BlockSpec((sps, LANES, wbuf), lambda i, pt: (i, 0, 0))
    in_specs += [win_spec] * 2
    consts = list(cw_k) + list(cw_v) + [bias_cmp, bias_sel, bias_win, cover, emat]
    in_specs += [_const_spec(a.shape, single_buffer=True) for a in consts]
    operands = list(caches) + [projs3] * 9 + [wk, wv] + consts

    grid_spec = pltpu.PrefetchScalarGridSpec(
        num_scalar_prefetch=1,
        grid=(nseq // sps,),
        in_specs=in_specs,
        out_specs=[pl.BlockSpec((sps, T_PAD, A_WIDTH), lambda i, pt: (i, 0, 0)), win_spec, win_spec],
        scratch_shapes=[pltpu.VMEM((sps, n_cmp_rows, LANES), F32), pltpu.VMEM((sps, n_cmp_rows, LANES), F32),
                        pltpu.VMEM((sps, 4, LANES, LANES), F32),
                        pltpu.VMEM((sps, N_ROWBLK * T_PAD, LANES), BF16),
                        pltpu.VMEM((2, sps, N_CACHED_STREAMS * n_pages, LANES, PAGE_SIZE), F32),
                        pltpu.SemaphoreType.DMA((2,))])
    return pl.pallas_call(
        functools.partial(_nsa_sample_group_body, n_pages=n_pages, past=past, wbuf=wbuf, nc=nc, nb=nb),
        grid_spec=grid_spec,
        out_shape=[jax.ShapeDtypeStruct((nseq, T_PAD, A_WIDTH), F32),
                   jax.ShapeDtypeStruct((nseq, LANES, wbuf), F32),
                   jax.ShapeDtypeStruct((nseq, LANES, wbuf), F32)],
        compiler_params=_cparams(1),
        name="nsa_sample",
    )(page_table, *operands)


def _hgrn_constants(rows, seg_stride, seg_len):
    idx = np.arange(rows)
    seg = idx // seg_stride
    pos = idx % seg_stride
    same = seg[:, None] == seg[None, :]
    pt, pu = pos[:, None], pos[None, :]
    mats = [same & (pu <= pt),
            same & (pu > pt) & (pu <= seg_len - 1)]
    levels = []
    m = seg_len // 2
    while m >= 2:
        levels.append(m)
        m //= 2
    for m in levels:
        mats.append(same & (pu >= (pt // m) * m) & (pu <= pt))
        mats.append(same & (pu > pt) & (pu <= (pt // m) * m + m - 1))
    wstack = np.concatenate(mats, axis=0).astype(np.float32)
    wstack = np.concatenate([wstack] * 3, axis=1)
    masks = []
    for m in levels + [1]:
        masks.append(same & (pt // (2 * m) == pu // (2 * m)) & ((pt // m) % 2 == 1) & ((pu // m) % 2 == 0))
    masks.append(idx[:, None] == idx[None, :])
    masks = np.stack(masks).astype(np.float32)
    return jnp.asarray(wstack, BF16), jnp.asarray(masks, F32), len(levels)


def _hgrn_gates(f_pre, hl_ref, wst_ref):
    hl = hl_ref[...]
    e = jnp.exp(hl - jnp.max(hl, axis=0, keepdims=True))
    lb = e[0:1] / jnp.sum(e, axis=0, keepdims=True)
    f = lb + (1.0 - lb) * _sigmoid(f_pre)
    g = jnp.log(f)
    k = 1.0 - f
    g_hi = g.astype(BF16)
    r1 = g - g_hi.astype(F32)
    g_mid = r1.astype(BF16)
    g_lo = (r1 - g_mid.astype(F32)).astype(BF16)
    sums = _nn(wst_ref[...], jnp.concatenate([g_hi, g_mid, g_lo], axis=0))
    return f, k, sums


def _hgrn_head(q, f, k, sums, masks_ref, hs, n_levels):
    rows = q.shape[0]
    qh, fh, kh = q[:, hs], f[:, hs], k[:, hs]
    part = lambda i: sums[i * rows:(i + 1) * rows, hs]
    big_g = part(0)
    qg = qh * jnp.exp(big_g)
    kd = kh * jnp.exp(part(1))
    a = None
    for li in range(n_levels + 2):
        if li < n_levels:
            ql, kl = qh * jnp.exp(part(2 + 2 * li)), kh * jnp.exp(part(3 + 2 * li))
        elif li == n_levels:
            ql, kl = qh * fh, kh
        else:
            ql, kl = qh, kh
        term = masks_ref[li] * _nt(ql.astype(BF16), kl.astype(BF16))
        a = term if a is None else a + term
    return big_g, qg, kd, a


def _hgrn_out(o, ng, z):
    o = o * lax.rsqrt(jnp.mean(o * o, axis=-1, keepdims=True) + EPS)
    return o * ng * _silu(z)


def _hgrn_prompt_body(q_ref, f_ref, i_ref, z_ref, hl_ref, ng_ref, wst_ref, masks_ref, o_ref, s_ref, st_ref,
                      *, n_levels, rows):
    c = pl.program_id(1)

    @pl.when(c == 0)
    def _():
        st_ref[...] = jnp.zeros(st_ref.shape, F32)

    chunks = []
    for ci in range(q_ref.shape[0] // rows):
        rs = slice(ci * rows, (ci + 1) * rows)
        v = i_ref[rs, :]
        q = q_ref[rs, :]
        f, k, sums = _hgrn_gates(f_ref[rs, :], hl_ref, wst_ref)
        per_head = []
        for h in range(HG_HEADS):
            hs = slice(h * HG_DK, (h + 1) * HG_DK)
            vb = v[:, hs].astype(BF16)
            big_g, qg, kd, a = _hgrn_head(q, f, k, sums, masks_ref, hs, n_levels)
            per_head.append((qg.astype(BF16), _nn(a.astype(BF16), vb),
                             jnp.exp(big_g[rows - 1:rows]), _tn(vb, kd.astype(BF16))))
        chunks.append(per_head)

    states = [st_ref[h] for h in range(HG_HEADS)]
    for ci, per_head in enumerate(chunks):
        rs = slice(ci * rows, (ci + 1) * rows)
        for h in range(HG_HEADS):
            hs = slice(h * HG_DK, (h + 1) * HG_DK)
            qgb, o_intra, decay, kv = per_head[h]
            o = _nt(qgb, states[h].astype(BF16)) + o_intra
            o_ref[rs, hs] = _hgrn_out(o, ng_ref[:, hs], z_ref[rs, hs]).astype(o_ref.dtype)
            states[h] = states[h] * decay + kv
    for h in range(HG_HEADS):
        st_ref[h] = states[h]

    @pl.when(c == pl.num_programs(1) - 1)
    def _():
        for h in range(HG_HEADS):
            s_ref[h] = states[h].T


HGRN_CHUNKS_PER_STEP = 16


def _hgrn_prompt(proj3, hg_lower, hg_norm_g):
    b, t, _ = proj3.shape
    chunk = math.gcd(t, HG_CHUNK)
    wst, masks, n_levels = _hgrn_constants(chunk, chunk, chunk)
    rows = chunk * math.gcd(t // chunk, HGRN_CHUNKS_PER_STEP)
    col = lambda c: pl.BlockSpec((None, rows, B_KEY), lambda i, j: (i, j, c // B_KEY))
    return pl.pallas_call(
        functools.partial(_hgrn_prompt_body, n_levels=n_levels, rows=chunk),
        grid=(b, t // rows),
        in_specs=[col(C_QB), col(C_FB), col(C_IB), col(C_ZB),
                  _const_spec(hg_lower.shape), _const_spec(hg_norm_g.shape),
                  _const_spec(wst.shape), _const_spec(masks.shape)],
        out_specs=[pl.BlockSpec((None, rows, B_WIDTH), lambda i, j: (i, j, 0)),
                   pl.BlockSpec((None, HG_HEADS, HG_DK, HG_DV), lambda i, j: (i, 0, 0, 0))],
        out_shape=[jax.ShapeDtypeStruct((b, t, B_WIDTH), BF16),
                   jax.ShapeDtypeStruct((b, HG_HEADS, HG_DK, HG_DV), F32)],
        scratch_shapes=[pltpu.VMEM((HG_HEADS, HG_DV, HG_DK), F32)],
        compiler_params=_cparams(2),
        name="hgrn_prompt",
    )(proj3, proj3, proj3, proj3, hg_lower, hg_norm_g, wst, masks)


SEQ_PER_STEP = 16


def _hgrn_sample_body(q_ref, f_ref, i_ref, z_ref, s0_ref, hl_ref, ng_ref, wst_ref, masks_ref, o_ref, s_ref,
                      *, n_levels, n_new):
    v = i_ref[...]
    q = q_ref[...]
    f, k, sums = _hgrn_gates(f_ref[...], hl_ref, wst_ref)
    for h in range(HG_HEADS):
        hs = slice(h * HG_DK, (h + 1) * HG_DK)
        vb = v[:, hs].astype(BF16)
        big_g, qg, kd, a = _hgrn_head(q, f, k, sums, masks_ref, hs, n_levels)
        o_intra = _nn(a.astype(BF16), vb)
        for s in range(SEQ_PER_STEP):
            rs = slice(s * T_PAD, (s + 1) * T_PAD)
            s0 = s0_ref[s, h]
            o = _nn(qg[rs].astype(BF16), s0.astype(BF16)) + o_intra[rs]
            o_ref[rs, hs] = _hgrn_out(o, ng_ref[:, hs], z_ref[rs, hs])
            decay = jnp.exp(big_g[rs]).T[:, n_new - 1:n_new]
            s_ref[s, h] = s0 * decay + _tn(kd[rs].astype(BF16), v[rs, hs].astype(BF16))


def _hgrn_sample(projs, state, hg_lower, hg_norm_g, n_new):
    nseq = state.shape[0]
    rows = SEQ_PER_STEP * T_PAD
    wst, masks, n_levels = _hgrn_constants(rows, T_PAD, n_new)
    col = lambda c: pl.BlockSpec((rows, B_KEY), lambda i: (i, c // B_KEY))
    st_spec = pl.BlockSpec((SEQ_PER_STEP, HG_HEADS, HG_DK, HG_DV), lambda i: (i, 0, 0, 0))
    return pl.pallas_call(
        functools.partial(_hgrn_sample_body, n_levels=n_levels, n_new=n_new),
        grid=(nseq // SEQ_PER_STEP,),
        in_specs=[col(C_QB), col(C_FB), col(C_IB), col(C_ZB), st_spec,
                  _const_spec(hg_lower.shape), _const_spec(hg_norm_g.shape),
                  _const_spec(wst.shape), _const_spec(masks.shape)],
        out_specs=[pl.BlockSpec((rows, B_WIDTH), lambda i: (i, 0)), st_spec],
        out_shape=[jax.ShapeDtypeStruct((nseq * T_PAD, B_WIDTH), F32),
                   jax.ShapeDtypeStruct(state.shape, F32)],
        compiler_params=_cparams(1),
        name="hgrn_sample",
    )(projs, projs, projs, projs, state, hg_lower, hg_norm_g, wst, masks)


def _merge_body(oa_ref, ob_ref, ma_ref, mb_ref, x_ref, wba_ref, wbb_ref, wout_ref, fg_ref, y_ref):
    a = _nn(oa_ref[...].astype(BF16), wba_ref[...])
    b = _nn(ob_ref[...].astype(BF16), wbb_ref[...])
    y = _sigmoid(ma_ref[...]) * a + _sigmoid(mb_ref[...]) * b
    h = x_ref[...] + _nn(y.astype(BF16), wout_ref[...])
    ms = jnp.mean(h * h, axis=-1, keepdims=True)
    y_ref[...] = h * lax.rsqrt(ms + EPS) * fg_ref[...]


MERGE_ROWS = 1024


def _merge(o_a, o_b, proj, x2d, w_ba, w_bb, w_out, final_g):
    m, d = x2d.shape
    tm = min(m, MERGE_ROWS)
    row = lambda w, c: pl.BlockSpec((tm, w), lambda i: (i, c))
    return pl.pallas_call(
        _merge_body,
        grid=(m // tm,),
        in_specs=[row(A_WIDTH, 0), row(B_WIDTH, 0), row(d, C_MA // d), row(d, C_MA // d + 1), row(d, 0),
                  _const_spec(w_ba.shape), _const_spec(w_bb.shape), _const_spec(w_out.shape),
                  _const_spec((1, d))],
        out_specs=row(d, 0),
        out_shape=jax.ShapeDtypeStruct((m, d), F32),
        compiler_params=_cparams(1),
        name="merge_out",
    )(o_a, o_b, proj, proj, x2d, w_ba, w_bb, w_out, final_g.reshape(1, d))


def kernel(x_prompt, x_sample, cache_cmp_k, cache_cmp_v, cache_sel_k, cache_sel_v, state_win_k, state_win_v,
           state_hgrn, page_table, norm_g, w_in, cmp_pos_k, cmp_w1_k, cmp_w2_k, cmp_pos_v, cmp_w1_v, cmp_w2_v,
           rel_bias, hg_lower, hg_norm_g, w_branch_a, w_branch_b, w_out, final_g):
    depth = norm_g.shape[0]
    assert depth == 1, "single-layer trunk"
    b, t, d = x_prompt.shape
    nseq, n_new, _ = x_sample.shape
    assert C_MA % d == 0

    w = _arrange_w_in(w_in[0])
    n_cols = w.shape[0]
    w_ba = _permute_heads_rows(w_branch_a[0]).astype(BF16)
    w_bb = w_branch_b[0].astype(BF16)
    w_o = w_out[0].astype(BF16)
    cw_k = _compress_weights(cmp_pos_k[0], cmp_w1_k[0], cmp_w2_k[0])
    cw_v = _compress_weights(cmp_pos_v[0], cmp_w1_v[0], cmp_w2_v[0])
    ng = hg_norm_g[0].reshape(1, B_WIDTH)
    kv_cols = lambda a, c: a[..., C_KV + c * A_KV:C_KV + (c + 1) * A_KV]

    xp2 = x_prompt.reshape(b * t, d)
    proj, *kv_t = _project(xp2, norm_g[0], w, seq_len=t)
    proj3 = proj.reshape(b, t, n_cols)
    nc_p = (t - CMP_LEN) // CMP_STRIDE + 1
    kcmp, vcmp = _compress_prompt(proj3, cw_k, cw_v)
    bias_cmp = _bias_table(rel_bias, base=-(CMP_LEN - 1), step=TQ, row_stride=-CMP_STRIDE, col_stride=1,
                           rows=LANES, cols=TQ, steps=t // TQ)
    bias_tiles = _bias_table(rel_bias, base=0, step=TQ, row_stride=-1, col_stride=1, rows=TQ, cols=TQ,
                             steps=WINDOW // TQ + 2, lo=0, hi=WINDOW)
    o_a = _nsa_prompt(proj3, kcmp, vcmp, bias_cmp, bias_tiles)
    o_b, p_hgrn = _hgrn_prompt(proj3, hg_lower, ng)
    y_prompt = _merge(o_a.reshape(b * t, A_WIDTH), o_b.reshape(b * t, B_WIDTH), proj, xp2,
                      w_ba, w_bb, w_o, final_g).reshape(b, t, d)
    wb_p = min(WINDOW, t)
    chan_major = lambda a: jnp.transpose(a, (0, 2, 3, 1)).reshape(a.shape[0], A_KV, a.shape[1])
    row_major = lambda a: jnp.transpose(a.reshape(a.shape[0], NSA_KV, NSA_HD, a.shape[2]), (0, 3, 1, 2))
    p_kv = ([row_major(kv_t[c])[None] for c in range(N_CACHED_STREAMS)]
            + [row_major(kv_t[c][:, :, t - wb_p:])[None] for c in range(N_CACHED_STREAMS, N_KV_STREAMS)])
    p_state = p_kv + [p_hgrn[None]]

    xs_pad = jnp.pad(x_sample, ((0, 0), (0, T_PAD - n_new), (0, 0))).reshape(nseq * T_PAD, d)
    projs = _project(xs_pad, norm_g[0], w)
    projs3 = projs.reshape(nseq, T_PAD, n_cols)
    pool = cache_cmp_k.shape[1]
    caches =[chan_major(c[0]) for c in (cache_cmp_k, cache_cmp_v, cache_sel_k, cache_sel_v)]
    wbuf = state_win_k.shape[2]
    o_as, s_win_k, s_win_v = _nsa_sample(projs3, caches, chan_major(state_win_k[0]), chan_major(state_win_v[0]),
                                         page_table, cw_k, cw_v, rel_bias, n_new)
    s_win_k, s_win_v = row_major(s_win_k), row_major(s_win_v)
    o_bs, s_hgrn = _hgrn_sample(projs, state_hgrn[0], hg_lower, ng, n_new)
    y_s = _merge(o_as.reshape(nseq * T_PAD, A_WIDTH), o_bs, projs, xs_pad, w_ba, w_bb, w_o, final_g)
    y_sample = y_s.reshape(nseq, T_PAD, d)[:, :n_new]
    s_kv = [kv_cols(projs3, c)[:, :n_new].reshape(1, nseq, n_new, NSA_KV, NSA_HD) for c in range(N_CACHED_STREAMS)]
    s_state = s_kv + [s_win_k.reshape(1, nseq, wbuf, NSA_KV, NSA_HD),
                      s_win_v.reshape(1, nseq, wbuf, NSA_KV, NSA_HD), s_hgrn[None]]

    return (y_prompt, y_sample, *p_state, *s_state)
```

```python
import functools
import math

import numpy as np
import jax
import jax.numpy as jnp
from jax import lax
from jax.experimental import pallas as pl
from jax.experimental.pallas import tpu as pltpu

F32 = jnp.float32
BF16 = jnp.bfloat16

NSA_HEADS = 8
NSA_KV = 2
NSA_HD = 64
NSA_REP = NSA_HEADS // NSA_KV
CMP_LEN = 32
CMP_STRIDE = 16
CMP_HID = 64
SEL_BLOCK = 64
SEL_TOPN = 8
WINDOW = 512
HG_HEADS = 4
HG_DK = 128
HG_DV = 128
HG_CHUNK = 64
N_BUCKETS = 32
MAX_DIST = 128
EPS = 1e-6
NEG = -1e30
FORCED = 1e9
PAGE_SIZE = 128

A_WIDTH = NSA_HEADS * NSA_HD
A_KV = NSA_KV * NSA_HD
B_WIDTH = HG_HEADS * HG_DV
B_KEY = HG_HEADS * HG_DK
N_KV_STREAMS = 6
N_CACHED_STREAMS = 4

LANES = 128
SUBLANES = 8
VMEM_LIMIT_BYTES = 56 * 1024 * 1024

TQ = 128
T_PAD = SUBLANES
N_ROWBLK = NSA_HEADS

C_Q = 0
C_KV = C_Q + A_WIDTH
C_GATE = C_KV + N_KV_STREAMS * A_KV
C_ZA = C_GATE + 2 * LANES
C_QB = C_ZA + A_WIDTH
C_FB = C_QB + B_KEY
C_IB = C_FB + B_KEY
C_ZB = C_IB + B_WIDTH
C_MA = C_ZB + B_WIDTH

CMP_SRC_ROWS = (LANES - 1) * CMP_STRIDE + CMP_LEN
CMP_PITCH = CMP_STRIDE + 1


def _nn(a, b):
    return jnp.dot(a, b, preferred_element_type=F32)


def _nt(a, b):
    return lax.dot_general(a, b, (((1,), (1,)), ((), ())), preferred_element_type=F32)


def _tn(a, b):
    return lax.dot_general(a, b, (((0,), (0,)), ((), ())), preferred_element_type=F32)


def _sigmoid(x):
    return 1.0 / (1.0 + jnp.exp(-x))


def _silu(x):
    return x * _sigmoid(x)


def _split2(x):
    hi = x.astype(BF16)
    lo = (x - hi.astype(F32)).astype(BF16)
    return hi, lo


def _cparams(n_grid):
    return pltpu.CompilerParams(dimension_semantics=("arbitrary",) * n_grid,
                                vmem_limit_bytes=VMEM_LIMIT_BYTES)


def _t5_bucket(rel):
    n = jnp.maximum(rel, 0)
    exact = N_BUCKETS // 2
    nf = jnp.maximum(n, 1).astype(F32)
    scaled = jnp.log(nf / exact) / math.log(MAX_DIST / exact) * (N_BUCKETS - exact)
    large = exact + jnp.floor(scaled).astype(jnp.int32)
    large = jnp.minimum(large, N_BUCKETS - 1)
    return jnp.where(n < exact, n, large)


def _rowblk_head(rb):
    return NSA_REP * (rb % NSA_KV) + rb // NSA_KV


def _bias_body(tab_ref, o_ref, *, base, step, row_stride, col_stride, lo, hi, cols):
    sub_r = min(o_ref.shape[1], 4 * SUBLANES)
    sub_c = min(cols, LANES)
    shape = (sub_r, sub_c)
    row = lax.broadcasted_iota(jnp.int32, shape, 0)
    col = lax.broadcasted_iota(jnp.int32, shape, 1)
    for r0 in range(0, o_ref.shape[1], sub_r):
        for c0 in range(0, cols, sub_c):
            rel = base + step * pl.program_id(0) + row_stride * (row + r0) + col_stride * (col + c0)
            bucket = _t5_bucket(rel)
            accs = [jnp.zeros(shape, F32)] * N_ROWBLK
            for k in range(N_BUCKETS):
                hit = bucket == k
                accs = [jnp.where(hit, tab_ref[k, _rowblk_head(rb)], accs[rb]) for rb in range(N_ROWBLK)]
            for rb in range(N_ROWBLK):
                acc = accs[rb]
                if lo is not None:
                    acc = jnp.where((rel >= lo) & (rel < hi), acc, NEG)
                o_ref[rb, r0:r0 + sub_r, c0:c0 + sub_c] = acc


def _bias_table(rel_bias, *, base, step, row_stride, col_stride, rows, cols, steps, lo=None, hi=None):
    return pl.pallas_call(
        functools.partial(_bias_body, base=base, step=step, row_stride=row_stride, col_stride=col_stride,
                          lo=lo, hi=hi, cols=cols),
        grid=(steps,),
        in_specs=[pl.BlockSpec(memory_space=pltpu.SMEM)],
        out_specs=pl.BlockSpec((None, N_ROWBLK, rows, cols), lambda i: (i, 0, 0, 0)),
        out_shape=jax.ShapeDtypeStruct((steps, N_ROWBLK, rows, cols), F32),
        compiler_params=_cparams(1),
        name="bias_table",
    )(rel_bias)


def _bias_table_sample(rel_bias, base, col_stride, cols):
    return _bias_table(rel_bias, base=base, step=0, row_stride=1, col_stride=-col_stride, rows=T_PAD, cols=cols,
                       steps=1)[0]


def _proj_body(x_ref, g_ref, w_ref, o_ref, *kvt_refs):
    x = x_ref[...]
    ms = jnp.mean(x * x, axis=-1, keepdims=True)
    xn = (x * lax.rsqrt(ms + EPS) * g_ref[...]).astype(BF16)
    o = _nt(xn, w_ref[...])
    o_ref[...] = o
    for c, kvt_ref in enumerate(kvt_refs):
        kvt_ref[...] = o[:, C_KV + c * A_KV:C_KV + (c + 1) * A_KV].T


PROJ_ROWS = 512


def _project(x2d, norm_g, w, seq_len=None):
    m, d = x2d.shape
    n = w.shape[0]
    tm = min(m, PROJ_ROWS)
    out_specs = [pl.BlockSpec((tm, n), lambda i: (i, 0))]
    out_shape = [jax.ShapeDtypeStruct((m, n), F32)]
    if seq_len is not None:
        per_seq = seq_len // tm
        assert seq_len % tm == 0
        out_specs += [pl.BlockSpec((None, A_KV, tm), lambda i: (i // per_seq, 0, i % per_seq))] * N_KV_STREAMS
        out_shape += [jax.ShapeDtypeStruct((m // seq_len, A_KV, seq_len), F32)] * N_KV_STREAMS
    outs = pl.pallas_call(
        _proj_body,
        grid=(m // tm,),
        in_specs=[pl.BlockSpec((tm, d), lambda i: (i, 0)),
                  pl.BlockSpec((1, d), lambda i: (0, 0)),
                  pl.BlockSpec((n, d), lambda i: (0, 0), pipeline_mode=pl.Buffered(1))],
        out_specs=out_specs,
        out_shape=out_shape,
        compiler_params=_cparams(1),
        name="in_proj",
    )(x2d, norm_g.reshape(1, d), w)
    return outs if seq_len is not None else outs[0]


def _permute_heads_rows(w):
    return w.reshape(NSA_KV, NSA_REP, NSA_HD, w.shape[-1]).swapaxes(0, 1).reshape(A_WIDTH, w.shape[-1])


def _arrange_w_in(w_in):
    d = w_in.shape[0]
    wt = w_in.T
    sizes = (A_WIDTH, N_KV_STREAMS * A_KV, 3 * NSA_HEADS, A_WIDTH, 2 * B_KEY + 2 * B_WIDTH + 2 * d)
    offs = np.concatenate([[0], np.cumsum(sizes)])
    q_a, kv_a, g_a, z_a, rest = [wt[offs[i]:offs[i + 1]] for i in range(5)]
    gate_pad = jnp.zeros((2 * LANES - 3 * NSA_HEADS, d), w_in.dtype)
    return jnp.concatenate([_permute_heads_rows(q_a), kv_a, g_a, gate_pad, _permute_heads_rows(z_a), rest],
                           axis=0).astype(BF16)


def _compress(streams, pitch):
    accs = [jnp.zeros((len(srcs) * LANES, LANES), F32) for srcs, _, _, _ in streams]
    for j in range(CMP_LEN):
        start = (j // CMP_STRIDE) * pitch + j % CMP_STRIDE
        for n, (srcs, pos_ref, w1_ref, _) in enumerate(streams):
            xj = jnp.concatenate([r[pl.ds(start, LANES, stride=pitch), :] for r in srcs], axis=0) + pos_ref[j:j + 1, :]
            accs[n] = accs[n] + _nn(xj.astype(BF16), w1_ref[j])
    return [_nn(_silu(acc).astype(BF16), w2_ref[...]) for acc, (_, _, _, w2_ref) in zip(accs, streams)]


def _compress_prompt_body(kc_ref, vc_ref, posk_ref, w1k_ref, w2k_ref, posv_ref, w1v_ref, w2v_ref,
                          ko_ref, vo_ref, kbuf_ref, vbuf_ref):
    t = kc_ref.shape[0]
    for buf_ref, src_ref in ((kbuf_ref, kc_ref), (vbuf_ref, vc_ref)):
        buf_ref[t:, :] = jnp.zeros((buf_ref.shape[0] - t, LANES), F32)
        buf_ref[:t, :] = src_ref[...]
    ko_ref[...], vo_ref[...] = _compress([([kbuf_ref], posk_ref, w1k_ref, w2k_ref),
                                          ([vbuf_ref], posv_ref, w1v_ref, w2v_ref)], CMP_STRIDE)


def _block_diag2(w):
    z = jnp.zeros_like(w)
    return jnp.concatenate([jnp.concatenate([w, z], axis=-1), jnp.concatenate([z, w], axis=-1)], axis=-2)


def _compress_weights(pos, w1, w2):
    pos2 = jnp.concatenate([pos, pos], axis=-1)
    w1bd = _block_diag2(w1.reshape(CMP_LEN, NSA_HD, CMP_HID)).astype(BF16)
    w2bd = _block_diag2(w2).astype(BF16)
    return pos2, w1bd, w2bd


def _const_spec(shape, single_buffer=False):
    nd = len(shape)
    if single_buffer:
        return pl.BlockSpec(shape, lambda *_: (0,) * nd, pipeline_mode=pl.Buffered(1))
    return pl.BlockSpec(shape, lambda *_: (0,) * nd)


def _compress_prompt(proj3, cw_k, cw_v):
    b, t, _ = proj3.shape
    col = C_KV // LANES
    w_specs = [_const_spec(a.shape) for a in cw_k + cw_v]
    return pl.pallas_call(
        _compress_prompt_body,
        grid=(b,),
        in_specs=[pl.BlockSpec((None, t, LANES), lambda i: (i, 0, col)),
                  pl.BlockSpec((None, t, LANES), lambda i: (i, 0, col + 1))] + w_specs,
        out_specs=[pl.BlockSpec((None, LANES, LANES), lambda i: (i, 0, 0))] * 2,
        out_shape=[jax.ShapeDtypeStruct((b, LANES, LANES), F32)] * 2,
        scratch_shapes=[pltpu.VMEM((max(t + CMP_LEN, CMP_SRC_ROWS), LANES), F32)] * 2,
        compiler_params=_cparams(1),
        name="compress_prompt",
    )(proj3, proj3, *cw_k, *cw_v)


def _select_blocks(imp, qpos, nb, axis):
    j = lax.broadcasted_iota(jnp.int32, imp.shape, axis)
    cur = qpos // SEL_BLOCK
    valid = j * SEL_BLOCK <= qpos
    forced = (j == 0) | (j == cur) | (j == cur - 1)
    score = jnp.where(valid, jnp.where(forced, FORCED, imp), -1.0)
    score = jnp.where(j < nb, score, -2.0)
    rank = jnp.zeros(imp.shape, F32)
    for i in range(nb):
        s_i = lax.slice_in_dim(score, i, i + 1, axis=axis)
        tie = jnp.where(i < j, 1.0, 0.0)
        rank = rank + jnp.where(s_i > score, 1.0, jnp.where(s_i == score, tie, 0.0))
    return jnp.where((rank < min(SEL_TOPN, nb)) & (j < nb), 1.0, 0.0)


def _masked_softmax(s, mask):
    s = jnp.where(mask, s, NEG)
    m = jnp.max(s, axis=-1, keepdims=True)
    e = jnp.exp(s - m)
    p = e / jnp.sum(e, axis=-1, keepdims=True)
    return jnp.where(mask, p, 0.0)


def _stack_queries(q_ref, qbd_ref, rows):
    lane = lax.broadcasted_iota(jnp.int32, (rows, LANES), 1)
    for r in range(NSA_REP):
        blk = q_ref[:, r * LANES:(r + 1) * LANES] * (NSA_HD ** -0.5)
        qbd_ref[(2 * r) * rows:(2 * r + 1) * rows, :] = jnp.where(lane < NSA_HD, blk, 0.0).astype(BF16)
        qbd_ref[(2 * r + 1) * rows:(2 * r + 2) * rows, :] = jnp.where(lane >= NSA_HD, blk, 0.0).astype(BF16)


def _gate_and_store(gate_ref, za_ref, o_ref, o_cmp, o_sel, o_win, rows):
    gs = _sigmoid(gate_ref[...])
    lane = lax.broadcasted_iota(jnp.int32, (rows, LANES), 1)
    for r in range(NSA_REP):
        parts = []
        for g in range(NSA_KV):
            rb = 2 * r + g
            h = _rowblk_head(rb)
            o = (gs[:, h:h + 1] * o_cmp[rb] + gs[:, NSA_HEADS + h:NSA_HEADS + h + 1] * o_sel[rb]
                 + gs[:, 2 * NSA_HEADS + h:2 * NSA_HEADS + h + 1] * o_win[rb])
            parts.append(o)
        blk = jnp.where(lane < NSA_HD, parts[0], parts[1])
        o_ref[:, r * LANES:(r + 1) * LANES] = blk * _silu(za_ref[:, r * LANES:(r + 1) * LANES])


SEL_TILES_PER_ITER = 2


def _nsa_prompt_body(q_ref, gate_ref, za_ref, ks_ref, vs_ref, kw_ref, vw_ref, kcmp_ref, vcmp_ref,
                     bcmp_ref, btile_ref, covert_ref, o_ref,
                     qt_ref, ksb_ref, vst_ref, kwb_ref, vwt_ref, vcmpt_ref, mask_ref, s_ref, m_ref, l_ref, acc_ref, res_ref,
                     *, nc, nb):
    qb = pl.program_id(1)
    nkt = ksb_ref.shape[0]
    sub = lax.broadcasted_iota(jnp.int32, (LANES, TQ), 0)
    tok = lax.broadcasted_iota(jnp.int32, (LANES, TQ), 1)

    @pl.when(qb == 0)
    def _():
        for kt in range(nkt):
            rows = slice(kt * TQ, (kt + 1) * TQ)
            ksb_ref[kt] = ks_ref[rows, :].astype(BF16)
            kwb_ref[kt] = kw_ref[rows, :].astype(BF16)
            vst_ref[kt] = vs_ref[rows, :].T.astype(BF16)
            vwt_ref[kt] = vw_ref[rows, :].T.astype(BF16)
        vcmpt_ref[...] = vcmp_ref[...].T.astype(BF16)

    for r in range(NSA_REP):
        qt = (q_ref[:, r * LANES:(r + 1) * LANES] * (NSA_HD ** -0.5)).T
        qt_ref[2 * r] = jnp.where(sub < NSA_HD, qt, 0.0).astype(BF16)
        qt_ref[2 * r + 1] = jnp.where(sub >= NSA_HD, qt, 0.0).astype(BF16)

    valid = ((CMP_STRIDE * sub + CMP_LEN - 1) <= qb * TQ + tok) & (sub < nc)
    kcb = kcmp_ref[...].astype(BF16)
    psum = [None] * NSA_KV
    for rb in range(N_ROWBLK):
        s = jnp.where(valid, _nn(kcb, qt_ref[rb]) + bcmp_ref[rb], NEG)
        e = jnp.exp(s - jnp.max(s, axis=0, keepdims=True))
        p = jnp.where(valid, e / jnp.sum(e, axis=0, keepdims=True), 0.0)
        res_ref[0, rb] = _nn(vcmpt_ref[...], p.astype(BF16))
        g = rb % NSA_KV
        psum[g] = p if psum[g] is None else psum[g] + p

    nbp = _round_up(nb, SUBLANES)
    upper = sub < SEL_BLOCK
    for g in range(NSA_KV):
        p_hi, p_lo = _split2(psum[g])
        imp = _nn(covert_ref[...], p_hi) + _nn(covert_ref[...], p_lo)
        sel = _select_blocks(imp[:nbp], qb * TQ + tok[:1], nb, 0)
        for kt in range(nkt):
            picked = jnp.where(upper, sel[2 * kt:2 * kt + 1, :], sel[2 * kt + 1:2 * kt + 2, :]) > 0.5
            mask_ref[kt, g] = jnp.where(kt <= qb, jnp.where(picked, 0.0, NEG), NEG)

    m_ref[...] = jnp.full(m_ref.shape, NEG, F32)
    l_ref[...] = jnp.zeros(l_ref.shape, F32)
    acc_ref[...] = jnp.zeros(acc_ref.shape, F32)

    def sel_scores(pair, buf):
        for u in range(SEL_TILES_PER_ITER):
            kt = jnp.minimum(pair * SEL_TILES_PER_ITER + u, nkt - 1)
            k_t = ksb_ref[kt]
            bidx = jnp.clip(qb - kt, 0, 2)
            for g in range(NSA_KV):
                madd = mask_ref[kt, g]
                for rb in range(g, N_ROWBLK, NSA_KV):
                    s_ref[buf, u, rb] = _nn(k_t, qt_ref[rb]) + btile_ref[bidx, rb] + madd

    def sel_update(pair, buf):
        v_ts = [vst_ref[pair * SEL_TILES_PER_ITER + u] for u in range(SEL_TILES_PER_ITER)]
        for rb in range(N_ROWBLK):
            ss = [s_ref[buf, u, rb] for u in range(SEL_TILES_PER_ITER)]
            m_old = m_ref[rb]
            m_new = m_old
            for s in ss:
                m_new = jnp.maximum(m_new, jnp.max(s, axis=0, keepdims=True))
            alpha = jnp.exp(m_old - m_new)
            l_new = alpha * l_ref[rb]
            acc = alpha * acc_ref[rb]
            for s, v_t in zip(ss, v_ts):
                p = jnp.exp(s - m_new)
                l_new = l_new + jnp.sum(p, axis=0, keepdims=True)
                acc = acc + _nn(v_t, p.astype(BF16))
            l_ref[rb] = l_new
            acc_ref[rb] = acc
            m_ref[rb] = m_new

    def sel_body(it, carry):
        sel_scores(2 * it + 1, 1)
        sel_update(2 * it, 0)
        sel_scores(2 * it + 2, 0)
        sel_update(2 * it + 1, 1)
        return carry

    sel_scores(0, 0)
    lax.fori_loop(0, qb // (2 * SEL_TILES_PER_ITER) + 1, sel_body, 0)
    res_ref[1] = acc_ref[...] / l_ref[...]

    n_wt = WINDOW // TQ + 1
    for rb in range(N_ROWBLK):
        qc = qt_ref[rb]
        ss = []
        for dist in range(n_wt):
            kt = jnp.maximum(qb - dist, 0)
            bidx = jnp.where(qb >= dist, dist, n_wt)
            ss.append(_nn(kwb_ref[kt], qc) + btile_ref[bidx, rb])
        m = jnp.max(ss[0], axis=0, keepdims=True)
        for s in ss[1:]:
            m = jnp.maximum(m, jnp.max(s, axis=0, keepdims=True))
        l_w = None
        acc = None
        for dist, s in enumerate(ss):
            p = jnp.exp(s - m)
            ps = jnp.sum(p, axis=0, keepdims=True)
            pv = _nn(vwt_ref[jnp.maximum(qb - dist, 0)], p.astype(BF16))
            l_w = ps if l_w is None else l_w + ps
            acc = pv if acc is None else acc + pv
        res_ref[2, rb] = acc / l_w

    gt = _sigmoid(gate_ref[...]).T
    for r in range(NSA_REP):
        parts = []
        for g in range(NSA_KV):
            rb = 2 * r + g
            h = _rowblk_head(rb)
            parts.append(gt[h:h + 1, :] * res_ref[0, rb]
                         + gt[NSA_HEADS + h:NSA_HEADS + h + 1, :] * res_ref[1, rb]
                         + gt[2 * NSA_HEADS + h:2 * NSA_HEADS + h + 1, :] * res_ref[2, rb])
        blk = jnp.where(sub < NSA_HD, parts[0], parts[1]).T
        o_ref[:, r * LANES:(r + 1) * LANES] = (blk * _silu(za_ref[:, r * LANES:(r + 1) * LANES])).astype(o_ref.dtype)


def _cover_matrix(nc, nb):
    cs = np.arange(nc) * CMP_STRIDE
    ce = cs + CMP_LEN
    bs = np.arange(nb) * SEL_BLOCK
    be = bs + SEL_BLOCK
    ov = np.clip(np.minimum(ce[:, None], be[None, :]) - np.maximum(cs[:, None], bs[None, :]), 0, None) / CMP_STRIDE
    out = np.zeros((LANES, LANES), np.float32)
    out[:nc, :nb] = ov
    return jnp.asarray(out, BF16)


def _expand_matrix(nkeys):
    j = np.arange(LANES)[:, None]
    k = np.arange(nkeys)[None, :]
    return jnp.asarray((k // SEL_BLOCK == j).astype(np.float32), BF16)


def _nsa_prompt(proj3, kcmp, vcmp, bias_cmp, bias_tiles):
    b, t, _ = proj3.shape
    nq = t // TQ
    nc = (t - CMP_LEN) // CMP_STRIDE + 1
    nb = t // SEL_BLOCK
    assert nc <= LANES and nb <= LANES and t % (TQ * 2 * SEL_TILES_PER_ITER) == 0
    kv_col = C_KV // LANES
    kv_spec = lambda c: pl.BlockSpec((None, t, LANES), lambda i, j: (i, 0, kv_col + c))
    cover_t = _cover_matrix(nc, nb).T
    tile_bf16 = pltpu.VMEM((nq, TQ, LANES), BF16)
    return pl.pallas_call(
        functools.partial(_nsa_prompt_body, nc=nc, nb=nb),
        grid=(b, nq),
        in_specs=[pl.BlockSpec((None, TQ, A_WIDTH), lambda i, j: (i, j, C_Q // A_WIDTH)),
                  pl.BlockSpec((None, TQ, LANES), lambda i, j: (i, j, C_GATE // LANES)),
                  pl.BlockSpec((None, TQ, A_WIDTH), lambda i, j: (i, j, C_ZA // A_WIDTH)),
                  kv_spec(2), kv_spec(3), kv_spec(4), kv_spec(5),
                  pl.BlockSpec((None, LANES, LANES), lambda i, j: (i, 0, 0)),
                  pl.BlockSpec((None, LANES, LANES), lambda i, j: (i, 0, 0)),
                  pl.BlockSpec((None, N_ROWBLK, LANES, TQ), lambda i, j: (j, 0, 0, 0)),
                  _const_spec(bias_tiles.shape), _const_spec(cover_t.shape)],
        out_specs=pl.BlockSpec((None, TQ, A_WIDTH), lambda i, j: (i, j, 0)),
        out_shape=jax.ShapeDtypeStruct((b, t, A_WIDTH), BF16),
        scratch_shapes=[pltpu.VMEM((N_ROWBLK, LANES, TQ), BF16),
                        tile_bf16, tile_bf16, tile_bf16, tile_bf16,
                        pltpu.VMEM((LANES, LANES), BF16),
                        pltpu.VMEM((nq, NSA_KV, TQ, TQ), F32),
                        pltpu.VMEM((2, SEL_TILES_PER_ITER, N_ROWBLK, TQ, TQ), F32),
                        pltpu.VMEM((N_ROWBLK, 1, TQ), F32),
                        pltpu.VMEM((N_ROWBLK, 1, TQ), F32),
                        pltpu.VMEM((N_ROWBLK, LANES, TQ), F32),
                        pltpu.VMEM((3, N_ROWBLK, LANES, TQ), F32)],
        compiler_params=_cparams(2),
        name="nsa_prompt",
    )(proj3, proj3, proj3, proj3, proj3, proj3, proj3, kcmp, vcmp, bias_cmp, bias_tiles, cover_t)


N_DMA_PRIORITIES = 2
NSA_SEQ_PER_STEP = 4


def _group_page_copy(cache_refs, pt_ref, pbuf, sems, step, slot, s, c, pg, n_pages):
    seq = step * NSA_SEQ_PER_STEP + s
    return pltpu.make_async_copy(cache_refs[c].at[pt_ref[seq, pg]], pbuf.at[slot, s, c * n_pages + pg],
                                 sems.at[slot])


def _nsa_sample_group_body(pt_ref, *refs, n_pages, **static):
    cache_refs = refs[:N_CACHED_STREAMS]
    per_seq = refs[N_CACHED_STREAMS:N_CACHED_STREAMS + 11]
    consts = refs[N_CACHED_STREAMS + 11:N_CACHED_STREAMS + 22]
    outs = refs[N_CACHED_STREAMS + 22:N_CACHED_STREAMS + 25]
    scratch = refs[N_CACHED_STREAMS + 25:-2]
    pbuf, sems = refs[-2:]
    i = pl.program_id(0)
    slot = i % 2

    def all_pages(step, slot_):
        return [_group_page_copy(cache_refs, pt_ref, pbuf, sems, step, slot_, s, c, pg, n_pages)
                for s in range(NSA_SEQ_PER_STEP) for c in range(N_CACHED_STREAMS) for pg in range(n_pages)]

    def start_all(copies):
        for n, cp in enumerate(copies):
            cp.start(priority=n % N_DMA_PRIORITIES)

    @pl.when(i == 0)
    def _():
        start_all(all_pages(0, 0))

    @pl.when(i + 1 < pl.num_programs(0))
    def _():
        start_all(all_pages(i + 1, 1 - slot))

    for cp in all_pages(i, slot):
        cp.wait()
    pages = [[[pbuf.at[slot, s, c * n_pages + pg] for pg in range(n_pages)] for c in range(N_CACHED_STREAMS)]
             for s in range(NSA_SEQ_PER_STEP)]
    _nsa_sample_group(pages, per_seq, consts, outs, scratch, n_pages=n_pages, **static)


def _nsa_sample_group(pages, per_seq, consts, outs, scratch, *, n_pages, past, wbuf, nc, nb):
    q_ref, kc_ref, vc_ref, ks_ref, vs_ref, kw_ref, vw_ref, gate_ref, za_ref, wk_ref, wv_ref = per_seq
    (posk_ref, w1k_ref, w2k_ref, posv_ref, w1v_ref, w2v_ref, bcmp_ref, bsel_ref, bwin_ref,
     cover_ref, emat_ref) = consts
    o_ref, swk_ref, swv_ref = outs
    kc_all, vc_all, newt_ref, qbd_ref = scratch
    nseq = NSA_SEQ_PER_STEP
    seqs = range(nseq)
    n_keys = bsel_ref.shape[2]
    n_new = T_PAD // 2
    chunks_per_page = PAGE_SIZE // CMP_STRIDE
    nrow = N_ROWBLK * T_PAD

    for dst, cache, new_ref in ((kc_all, 0, kc_ref), (vc_all, 1, vc_ref)):
        for s in seqs:
            for pg in range(n_pages):
                rows = pages[s][cache][pg][...].T
                for c in range(chunks_per_page):
                    r0 = (pg * chunks_per_page + c) * CMP_PITCH
                    dst[s, r0:r0 + CMP_STRIDE, :] = rows[c * CMP_STRIDE:(c + 1) * CMP_STRIDE, :]
            r0 = (past // CMP_STRIDE) * CMP_PITCH
            dst[s, r0:r0 + T_PAD, :] = new_ref[s]
            dst[s, r0 + T_PAD:, :] = jnp.zeros((dst.shape[1] - r0 - T_PAD, LANES), F32)
    kcmp, vcmp = [c.astype(BF16) for c in _compress(
        [([kc_all.at[s] for s in seqs], posk_ref, w1k_ref, w2k_ref),
         ([vc_all.at[s] for s in seqs], posv_ref, w1v_ref, w2v_ref)], CMP_PITCH)]
    per_s = lambda x, s: x[s * LANES:(s + 1) * LANES]

    newt_ref[...] = jnp.zeros(newt_ref.shape, F32)
    for s in seqs:
        for idx, new_ref in enumerate((ks_ref, vs_ref, kw_ref, vw_ref)):
            newt_ref[s, idx, 0:T_PAD, :] = new_ref[s]
        _stack_queries(q_ref.at[s], qbd_ref.at[s], T_PAD)
    qbd = [qbd_ref[s] for s in seqs]
    tok = lax.broadcasted_iota(jnp.int32, (T_PAD, LANES), 0)
    lane = lax.broadcasted_iota(jnp.int32, (T_PAD, LANES), 1)
    qpos = past + tok
    stack = lambda xs: jnp.concatenate(xs, axis=0)
    by_head = lambda x: x.reshape(nseq, NSA_REP, NSA_KV, T_PAD, LANES)

    s_c = by_head(stack([_nt(qbd[s], per_s(kcmp, s)) for s in seqs])) + _by_head_bias(bcmp_ref[...])
    mask_c = ((CMP_STRIDE * lane + CMP_LEN - 1) <= qpos) & (lane < nc)
    p = _masked_softmax(s_c, mask_c[None, None, None]).reshape(nseq, nrow, LANES)
    o_cmp = [_nn(p[s].astype(BF16), per_s(vcmp, s)).reshape(N_ROWBLK, T_PAD, LANES) for s in seqs]

    psum = p.reshape(nseq, NSA_REP, NSA_KV * T_PAD, LANES).sum(axis=1).reshape(nseq * NSA_KV * T_PAD, LANES)
    p_hi, p_lo = _split2(psum)
    imp = _nn(p_hi, cover_ref[...]) + _nn(p_lo, cover_ref[...])
    sel = _select_blocks(imp.reshape(nseq, NSA_KV, T_PAD, LANES), qpos[None, None], nb, 3)
    selk = _nn(sel.reshape(nseq * NSA_KV * T_PAD, LANES).astype(BF16), emat_ref[...])
    selk = selk.reshape(nseq, NSA_KV, T_PAD, n_keys)

    def softmax_pv(scores, values):
        m = scores[0]
        for s_t in scores[1:]:
            m = jnp.maximum(m, s_t)
        m = jnp.max(m, axis=-1, keepdims=True)
        acc = [None] * nseq
        tot = None
        for s_t, (v_ts, channel_major) in zip(scores, values):
            p_t = jnp.exp(s_t - m)
            pb = p_t.reshape(nseq, nrow, LANES).astype(BF16)
            for s in seqs:
                pv = _nt(pb[s], v_ts[s]) if channel_major else _nn(pb[s], v_ts[s])
                acc[s] = pv if acc[s] is None else acc[s] + pv
            tot = p_t if tot is None else tot + p_t
        inv = 1.0 / jnp.sum(tot, axis=-1, keepdims=True).reshape(nseq, nrow, 1)
        return [(acc[s] * inv[s]).reshape(N_ROWBLK, T_PAD, LANES) for s in seqs]

    scores, values = [], []
    for t in range(n_pages + 1):
        cols = slice(t * LANES, (t + 1) * LANES)
        allowed = ((t * LANES + lane) <= qpos)[None, None] & (selk[:, :, :, cols] > 0.5)
        madd = jnp.where(allowed, 0.0, NEG)[:, None]
        if t < n_pages:
            raw = [_nn(qbd[s], pages[s][2][t][...].astype(BF16)) for s in seqs]
            values.append(([pages[s][3][t][...].astype(BF16) for s in seqs], True))
        else:
            raw = [_nt(qbd[s], newt_ref[s, 0].astype(BF16)) for s in seqs]
            values.append(([newt_ref[s, 1].astype(BF16) for s in seqs], False))
        scores.append(by_head(stack(raw)) + _by_head_bias(bsel_ref[:, :, cols]) + madd)
    o_sel = softmax_pv(scores, values)

    scores, values = [], []
    kwt = [wk_ref[s].astype(BF16) for s in seqs]
    vwt = [wv_ref[s].astype(BF16) for s in seqs]
    for t in range(wbuf // LANES + 1):
        cols = slice(t * LANES, (t + 1) * LANES)
        rel_w = wbuf + tok - (t * LANES + lane)
        madd = jnp.where((rel_w >= 0) & (rel_w < WINDOW), 0.0, NEG)[None, None, None]
        if t < wbuf // LANES:
            raw = [_nn(qbd[s], kwt[s][:, cols]) for s in seqs]
            values.append(([vwt[s][:, cols] for s in seqs], True))
        else:
            raw = [_nt(qbd[s], newt_ref[s, 2].astype(BF16)) for s in seqs]
            values.append(([newt_ref[s, 3].astype(BF16) for s in seqs], False))
        scores.append(by_head(stack(raw)) + _by_head_bias(bwin_ref[:, :, cols]) + madd)
    o_win = softmax_pv(scores, values)

    lane_full = lax.broadcasted_iota(jnp.int32, (LANES, LANES), 1)
    for s in seqs:
        _gate_and_store(gate_ref.at[s], za_ref.at[s], o_ref.at[s], o_cmp[s], o_sel[s], o_win[s], T_PAD)
        for idx, w_ref, out_ref in ((2, wk_ref, swk_ref), (3, wv_ref, swv_ref)):
            rolled = pltpu.roll(w_ref[s], wbuf - n_new, axis=1)
            tail = pltpu.roll(newt_ref[s, idx], LANES - n_new, axis=0).T
            out_ref[s, :, :wbuf - LANES] = rolled[:, :wbuf - LANES]
            out_ref[s, :, wbuf - LANES:] = jnp.where(lane_full >= LANES - n_new, tail, rolled[:, wbuf - LANES:])


def _by_head_bias(b):
    return b.reshape(1, NSA_REP, NSA_KV, T_PAD, b.shape[-1])


def _round_up(x, m):
    return (x + m - 1) // m * m


def _nsa_sample(projs3, caches, wk, wv, page_table, cw_k, cw_v, rel_bias, n_new):
    nseq, n_pages = page_table.shape
    past = n_pages * PAGE_SIZE
    wbuf = wk.shape[2]
    length = past + n_new
    nc = (length - CMP_LEN) // CMP_STRIDE + 1
    nb = -(-length // SEL_BLOCK)
    n_keys = past + LANES
    n_win = wbuf + LANES
    pitched = lambda p: (p // CMP_STRIDE) * CMP_PITCH + p % CMP_STRIDE
    n_cmp_rows = _round_up(max(pitched(past) + T_PAD, pitched(CMP_SRC_ROWS - 1) + 1), SUBLANES)
    assert nc <= LANES and nb <= LANES and n_new == T_PAD // 2 and nb * SEL_BLOCK <= n_keys
    assert wbuf % LANES == 0 and PAGE_SIZE == LANES

    bias_cmp = _bias_table_sample(rel_bias, past - (CMP_LEN - 1), CMP_STRIDE, LANES)
    bias_sel = _bias_table_sample(rel_bias, past, 1, n_keys)
    bias_win = _bias_table_sample(rel_bias, wbuf, 1, n_win)
    cover = _cover_matrix(nc, nb)
    emat = _expand_matrix(n_keys)

    sps = NSA_SEQ_PER_STEP
    assert nseq % sps == 0

    def new_spec(width, col):
        return pl.BlockSpec((sps, T_PAD, width), lambda i, pt: (i, 0, col))

    kv_col = C_KV // LANES
    in_specs = [pl.BlockSpec(memory_space=pl.ANY)] * N_CACHED_STREAMS
    in_specs += [new_spec(A_WIDTH, C_Q // A_WIDTH)]
    in_specs += [new_spec(LANES, kv_col + c) for c in range(N_KV_STREAMS)]
    in_specs += [new_spec(LANES, C_GATE // LANES), new_spec(A_WIDTH, C_ZA // A_WIDTH)]
    win_spec = pl.BlockSpec((sps, LANES, wbuf), lambda i, pt: (i, 0, 0))
    in_specs += [win_spec] * 2
    consts = list(cw_k) + list(cw_v) + [bias_cmp, bias_sel, bias_win, cover, emat]
    in_specs += [_const_spec(a.shape, single_buffer=True) for a in consts]
    operands = list(caches) + [projs3] * 9 + [wk, wv] + consts

    grid_spec = pltpu.PrefetchScalarGridSpec(
        num_scalar_prefetch=1,
        grid=(nseq // sps,),
        in_specs=in_specs,
        out_specs=[pl.BlockSpec((sps, T_PAD, A_WIDTH), lambda i, pt: (i, 0, 0)), win_spec, win_spec],
        scratch_shapes=[pltpu.VMEM((sps, n_cmp_rows, LANES), F32), pltpu.VMEM((sps, n_cmp_rows, LANES), F32),
                        pltpu.VMEM((sps, 4, LANES, LANES), F32),
                        pltpu.VMEM((sps, N_ROWBLK * T_PAD, LANES), BF16),
                        pltpu.VMEM((2, sps, N_CACHED_STREAMS * n_pages, LANES, PAGE_SIZE), F32),
                        pltpu.SemaphoreType.DMA((2,))])
    return pl.pallas_call(
        functools.partial(_nsa_sample_group_body, n_pages=n_pages, past=past, wbuf=wbuf, nc=nc, nb=nb),
        grid_spec=grid_spec,
        out_shape=[jax.ShapeDtypeStruct((nseq, T_PAD, A_WIDTH), F32),
                   jax.ShapeDtypeStruct((nseq, LANES, wbuf), F32),
                   jax.ShapeDtypeStruct((nseq, LANES, wbuf), F32)],
        compiler_params=_cparams(1),
        name="nsa_sample",
    )(page_table, *operands)


def _hgrn_constants(rows, seg_stride, seg_len):
    idx = np.arange(rows)
    seg = idx // seg_stride
    pos = idx % seg_stride
    same = seg[:, None] == seg[None, :]
    pt, pu = pos[:, None], pos[None, :]
    mats = [same & (pu <= pt),
            same & (pu > pt) & (pu <= seg_len - 1)]
    levels = []
    m = seg_len // 2
    while m >= 2:
        levels.append(m)
        m //= 2
    for m in levels:
        mats.append(same & (pu >= (pt // m) * m) & (pu <= pt))
        mats.append(same & (pu > pt) & (pu <= (pt // m) * m + m - 1))
    wstack = np.concatenate(mats, axis=0).astype(np.float32)
    wstack = np.concatenate([wstack] * 3, axis=1)
    masks = []
    for m in levels + [1]:
        masks.append(same & (pt // (2 * m) == pu // (2 * m)) & ((pt // m) % 2 == 1) & ((pu // m) % 2 == 0))
    masks.append(idx[:, None] == idx[None, :])
    masks = np.stack(masks).astype(np.float32)
    return jnp.asarray(wstack, BF16), jnp.asarray(masks, F32), len(levels)


def _hgrn_gates(f_pre, hl_ref, wst_ref):
    hl = hl_ref[...]
    e = jnp.exp(hl - jnp.max(hl, axis=0, keepdims=True))
    lb = e[0:1] / jnp.sum(e, axis=0, keepdims=True)
    f = lb + (1.0 - lb) * _sigmoid(f_pre)
    g = jnp.log(f)
    k = 1.0 - f
    g_hi = g.astype(BF16)
    r1 = g - g_hi.astype(F32)
    g_mid = r1.astype(BF16)
    g_lo = (r1 - g_mid.astype(F32)).astype(BF16)
    sums = _nn(wst_ref[...], jnp.concatenate([g_hi, g_mid, g_lo], axis=0))
    return f, k, sums


def _hgrn_head(q, f, k, sums, masks_ref, hs, n_levels):
    rows = q.shape[0]
    qh, fh, kh = q[:, hs], f[:, hs], k[:, hs]
    part = lambda i: sums[i * rows:(i + 1) * rows, hs]
    big_g = part(0)
    qg = qh * jnp.exp(big_g)
    kd = kh * jnp.exp(part(1))
    a = None
    for li in range(n_levels + 2):
        if li < n_levels:
            ql, kl = qh * jnp.exp(part(2 + 2 * li)), kh * jnp.exp(part(3 + 2 * li))
        elif li == n_levels:
            ql, kl = qh * fh, kh
        else:
            ql, kl = qh, kh
        term = masks_ref[li] * _nt(ql.astype(BF16), kl.astype(BF16))
        a = term if a is None else a + term
    return big_g, qg, kd, a


def _hgrn_out(o, ng, z):
    o = o * lax.rsqrt(jnp.mean(o * o, axis=-1, keepdims=True) + EPS)
    return o * ng * _silu(z)


def _hgrn_prompt_body(q_ref, f_ref, i_ref, z_ref, hl_ref, ng_ref, wst_ref, masks_ref, o_ref, s_ref, st_ref,
                      *, n_levels, rows):
    c = pl.program_id(1)

    @pl.when(c == 0)
    def _():
        st_ref[...] = jnp.zeros(st_ref.shape, F32)

    chunks = []
    for ci in range(q_ref.shape[0] // rows):
        rs = slice(ci * rows, (ci + 1) * rows)
        v = i_ref[rs, :]
        q = q_ref[rs, :]
        f, k, sums = _hgrn_gates(f_ref[rs, :], hl_ref, wst_ref)
        per_head = []
        for h in range(HG_HEADS):
            hs = slice(h * HG_DK, (h + 1) * HG_DK)
            vb = v[:, hs].astype(BF16)
            big_g, qg, kd, a = _hgrn_head(q, f, k, sums, masks_ref, hs, n_levels)
            per_head.append((qg.astype(BF16), _nn(a.astype(BF16), vb),
                             jnp.exp(big_g[rows - 1:rows]), _tn(vb, kd.astype(BF16))))
        chunks.append(per_head)

    states = [st_ref[h] for h in range(HG_HEADS)]
    for ci, per_head in enumerate(chunks):
        rs = slice(ci * rows, (ci + 1) * rows)
        for h in range(HG_HEADS):
            hs = slice(h * HG_DK, (h + 1) * HG_DK)
            qgb, o_intra, decay, kv = per_head[h]
            o = _nt(qgb, states[h].astype(BF16)) + o_intra
            o_ref[rs, hs] = _hgrn_out(o, ng_ref[:, hs], z_ref[rs, hs]).astype(o_ref.dtype)
            states[h] = states[h] * decay + kv
    for h in range(HG_HEADS):
        st_ref[h] = states[h]

    @pl.when(c == pl.num_programs(1) - 1)
    def _():
        for h in range(HG_HEADS):
            s_ref[h] = states[h].T


HGRN_CHUNKS_PER_STEP = 16


def _hgrn_prompt(proj3, hg_lower, hg_norm_g):
    b, t, _ = proj3.shape
    chunk = math.gcd(t, HG_CHUNK)
    wst, masks, n_levels = _hgrn_constants(chunk, chunk, chunk)
    rows = chunk * math.gcd(t // chunk, HGRN_CHUNKS_PER_STEP)
    col = lambda c: pl.BlockSpec((None, rows, B_KEY), lambda i, j: (i, j, c // B_KEY))
    return pl.pallas_call(
        functools.partial(_hgrn_prompt_body, n_levels=n_levels, rows=chunk),
        grid=(b, t // rows),
        in_specs=[col(C_QB), col(C_FB), col(C_IB), col(C_ZB),
                  _const_spec(hg_lower.shape), _const_spec(hg_norm_g.shape),
                  _const_spec(wst.shape), _const_spec(masks.shape)],
        out_specs=[pl.BlockSpec((None, rows, B_WIDTH), lambda i, j: (i, j, 0)),
                   pl.BlockSpec((None, HG_HEADS, HG_DK, HG_DV), lambda i, j: (i, 0, 0, 0))],
        out_shape=[jax.ShapeDtypeStruct((b, t, B_WIDTH), BF16),
                   jax.ShapeDtypeStruct((b, HG_HEADS, HG_DK, HG_DV), F32)],
        scratch_shapes=[pltpu.VMEM((HG_HEADS, HG_DV, HG_DK), F32)],
        compiler_params=_cparams(2),
        name="hgrn_prompt",
    )(proj3, proj3, proj3, proj3, hg_lower, hg_norm_g, wst, masks)


SEQ_PER_STEP = 16


def _hgrn_sample_body(q_ref, f_ref, i_ref, z_ref, s0_ref, hl_ref, ng_ref, wst_ref, masks_ref, o_ref, s_ref,
                      *, n_levels, n_new):
    v = i_ref[...]
    q = q_ref[...]
    f, k, sums = _hgrn_gates(f_ref[...], hl_ref, wst_ref)
    for h in range(HG_HEADS):
        hs = slice(h * HG_DK, (h + 1) * HG_DK)
        vb = v[:, hs].astype(BF16)
        big_g, qg, kd, a = _hgrn_head(q, f, k, sums, masks_ref, hs, n_levels)
        o_intra = _nn(a.astype(BF16), vb)
        for s in range(SEQ_PER_STEP):
            rs = slice(s * T_PAD, (s + 1) * T_PAD)
            s0 = s0_ref[s, h]
            o = _nn(qg[rs].astype(BF16), s0.astype(BF16)) + o_intra[rs]
            o_ref[rs, hs] = _hgrn_out(o, ng_ref[:, hs], z_ref[rs, hs])
            decay = jnp.exp(big_g[rs]).T[:, n_new - 1:n_new]
            s_ref[s, h] = s0 * decay + _tn(kd[rs].astype(BF16), v[rs, hs].astype(BF16))


def _hgrn_sample(projs, state, hg_lower, hg_norm_g, n_new):
    nseq = state.shape[0]
    rows = SEQ_PER_STEP * T_PAD
    wst, masks, n_levels = _hgrn_constants(rows, T_PAD, n_new)
    col = lambda c: pl.BlockSpec((rows, B_KEY), lambda i: (i, c // B_KEY))
    st_spec = pl.BlockSpec((SEQ_PER_STEP, HG_HEADS, HG_DK, HG_DV), lambda i: (i, 0, 0, 0))
    return pl.pallas_call(
        functools.partial(_hgrn_sample_body, n_levels=n_levels, n_new=n_new),
        grid=(nseq // SEQ_PER_STEP,),
        in_specs=[col(C_QB), col(C_FB), col(C_IB), col(C_ZB), st_spec,
                  _const_spec(hg_lower.shape), _const_spec(hg_norm_g.shape),
                  _const_spec(wst.shape), _const_spec(masks.shape)],
        out_specs=[pl.BlockSpec((rows, B_WIDTH), lambda i: (i, 0)), st_spec],
        out_shape=[jax.ShapeDtypeStruct((nseq * T_PAD, B_WIDTH), F32),
                   jax.ShapeDtypeStruct(state.shape, F32)],
        compiler_params=_cparams(1),
        name="hgrn_sample",
    )(projs, projs, projs, projs, state, hg_lower, hg_norm_g, wst, masks)


def _merge_body(oa_ref, ob_ref, ma_ref, mb_ref, x_ref, wba_ref, wbb_ref, wout_ref, fg_ref, y_ref):
    a = _nn(oa_ref[...].astype(BF16), wba_ref[...])
    b = _nn(ob_ref[...].astype(BF16), wbb_ref[...])
    y = _sigmoid(ma_ref[...]) * a + _sigmoid(mb_ref[...]) * b
    h = x_ref[...] + _nn(y.astype(BF16), wout_ref[...])
    ms = jnp.mean(h * h, axis=-1, keepdims=True)
    y_ref[...] = h * lax.rsqrt(ms + EPS) * fg_ref[...]


MERGE_ROWS = 1024


def _merge(o_a, o_b, proj, x2d, w_ba, w_bb, w_out, final_g):
    m, d = x2d.shape
    tm = min(m, MERGE_ROWS)
    row = lambda w, c: pl.BlockSpec((tm, w), lambda i: (i, c))
    return pl.pallas_call(
        _merge_body,
        grid=(m // tm,),
        in_specs=[row(A_WIDTH, 0), row(B_WIDTH, 0), row(d, C_MA // d), row(d, C_MA // d + 1), row(d, 0),
                  _const_spec(w_ba.shape), _const_spec(w_bb.shape), _const_spec(w_out.shape),
                  _const_spec((1, d))],
        out_specs=row(d, 0),
        out_shape=jax.ShapeDtypeStruct((m, d), F32),
        compiler_params=_cparams(1),
        name="merge_out",
    )(o_a, o_b, proj, proj, x2d, w_ba, w_bb, w_out, final_g.reshape(1, d))


def kernel(x_prompt, x_sample, cache_cmp_k, cache_cmp_v, cache_sel_k, cache_sel_v, state_win_k, state_win_v,
           state_hgrn, page_table, norm_g, w_in, cmp_pos_k, cmp_w1_k, cmp_w2_k, cmp_pos_v, cmp_w1_v, cmp_w2_v,
           rel_bias, hg_lower, hg_norm_g, w_branch_a, w_branch_b, w_out, final_g):
    depth = norm_g.shape[0]
    assert depth == 1, "single-layer trunk"
    b, t, d = x_prompt.shape
    nseq, n_new, _ = x_sample.shape
    assert C_MA % d == 0

    w = _arrange_w_in(w_in[0])
    n_cols = w.shape[0]
    w_ba = _permute_heads_rows(w_branch_a[0]).astype(BF16)
    w_bb = w_branch_b[0].astype(BF16)
    w_o = w_out[0].astype(BF16)
    cw_k = _compress_weights(cmp_pos_k[0], cmp_w1_k[0], cmp_w2_k[0])
    cw_v = _compress_weights(cmp_pos_v[0], cmp_w1_v[0], cmp_w2_v[0])
    ng = hg_norm_g[0].reshape(1, B_WIDTH)
    kv_cols = lambda a, c: a[..., C_KV + c * A_KV:C_KV + (c + 1) * A_KV]

    xp2 = x_prompt.reshape(b * t, d)
    proj, *kv_t = _project(xp2, norm_g[0], w, seq_len=t)
    proj3 = proj.reshape(b, t, n_cols)
    nc_p = (t - CMP_LEN) // CMP_STRIDE + 1
    kcmp, vcmp = _compress_prompt(proj3, cw_k, cw_v)
    bias_cmp = _bias_table(rel_bias, base=-(CMP_LEN - 1), step=TQ, row_stride=-CMP_STRIDE, col_stride=1,
                           rows=LANES, cols=TQ, steps=t // TQ)
    bias_tiles = _bias_table(rel_bias, base=0, step=TQ, row_stride=-1, col_stride=1, rows=TQ, cols=TQ,
                             steps=WINDOW // TQ + 2, lo=0, hi=WINDOW)
    o_a = _nsa_prompt(proj3, kcmp, vcmp, bias_cmp, bias_tiles)
    o_b, p_hgrn = _hgrn_prompt(proj3, hg_lower, ng)
    y_prompt = _merge(o_a.reshape(b * t, A_WIDTH), o_b.reshape(b * t, B_WIDTH), proj, xp2,
                      w_ba, w_bb, w_o, final_g).reshape(b, t, d)
    wb_p = min(WINDOW, t)
    chan_major = lambda a: jnp.transpose(a, (0, 2, 3, 1)).reshape(a.shape[0], A_KV, a.shape[1])
    row_major = lambda a: jnp.transpose(a.reshape(a.shape[0], NSA_KV, NSA_HD, a.shape[2]), (0, 3, 1, 2))
    p_kv = ([row_major(kv_t[c])[None] for c in range(N_CACHED_STREAMS)]
            + [row_major(kv_t[c][:, :, t - wb_p:])[None] for c in range(N_CACHED_STREAMS, N_KV_STREAMS)])
    p_state = p_kv + [p_hgrn[None]]

    xs_pad = jnp.pad(x_sample, ((0, 0), (0, T_PAD - n_new), (0, 0))).reshape(nseq * T_PAD, d)
    projs = _project(xs_pad, norm_g[0], w)
    projs3 = projs.reshape(nseq, T_PAD, n_cols)
    pool = cache_cmp_k.shape[1]
    caches =[chan_major(c[0]) for c in (cache_cmp_k, cache_cmp_v, cache_sel_k, cache_sel_v)]
    wbuf = state_win_k.shape[2]
    o_as, s_win_k, s_win_v = _nsa_sample(projs3, caches, chan_major(state_win_k[0]), chan_major(state_win_v[0]),
                                         page_table, cw_k, cw_v, rel_bias, n_new)
    s_win_k, s_win_v = row_major(s_win_k), row_major(s_win_v)
    o_bs, s_hgrn = _hgrn_sample(projs, state_hgrn[0], hg_lower, ng, n_new)
    y_s = _merge(o_as.reshape(nseq * T_PAD, A_WIDTH), o_bs, projs, xs_pad, w_ba, w_bb, w_o, final_g)
    y_sample = y_s.reshape(nseq, T_PAD, d)[:, :n_new]
    s_kv = [kv_cols(projs3, c)[:, :n_new].reshape(1, nseq, n_new, NSA_KV, NSA_HD) for c in range(N_CACHED_STREAMS)]
    s_state = s_kv + [s_win_k.reshape(1, nseq, wbuf, NSA_KV, NSA_HD),
                      s_win_v.reshape(1, nseq, wbuf, NSA_KV, NSA_HD), s_hgrn[None]]

    return (y_prompt, y_sample, *p_state, *s_state)
```

```python
import functools
import math

import numpy as np
import jax
import jax.numpy as jnp
from jax import lax
from jax.experimental import pallas as pl
from jax.experimental.pallas import tpu as pltpu

F32 = jnp.float32
BF16 = jnp.bfloat16

NSA_HEADS = 8
NSA_KV = 2
NSA_HD = 64
NSA_REP = NSA_HEADS // NSA_KV
CMP_LEN = 32
CMP_STRIDE = 16
CMP_HID = 64
SEL_BLOCK = 64
SEL_TOPN = 8
WINDOW = 512
HG_HEADS = 4
HG_DK = 128
HG_DV = 128
HG_CHUNK = 64
N_BUCKETS = 32
MAX_DIST = 128
EPS = 1e-6
NEG = -1e30
FORCED = 1e9
LOG2E = math.log2(math.e)
PAGE_SIZE = 128

A_WIDTH = NSA_HEADS * NSA_HD
A_KV = NSA_KV * NSA_HD
B_WIDTH = HG_HEADS * HG_DV
B_KEY = HG_HEADS * HG_DK
N_KV_STREAMS = 6
N_CACHED_STREAMS = 4

LANES = 128
SUBLANES = 8
VMEM_LIMIT_BYTES = 56 * 1024 * 1024

TQ = 128
T_PAD = SUBLANES
N_ROWBLK = NSA_HEADS

C_Q = 0
C_KV = C_Q + A_WIDTH
C_GATE = C_KV + N_KV_STREAMS * A_KV
C_ZA = C_GATE + 2 * LANES
C_QB = C_ZA + A_WIDTH
C_FB = C_QB + B_KEY
C_IB = C_FB + B_KEY
C_ZB = C_IB + B_WIDTH
C_MA = C_ZB + B_WIDTH

CMP_SRC_ROWS = (LANES - 1) * CMP_STRIDE + CMP_LEN
CMP_PITCH = CMP_STRIDE + 1


def _nn(a, b):
    return jnp.dot(a, b, preferred_element_type=F32)


def _nt(a, b):
    return lax.dot_general(a, b, (((1,), (1,)), ((), ())), preferred_element_type=F32)


def _tn(a, b):
    return lax.dot_general(a, b, (((0,), (0,)), ((), ())), preferred_element_type=F32)


def _sigmoid(x):
    return 1.0 / (1.0 + jnp.exp(-x))


def _silu(x):
    return x * _sigmoid(x)


def _split2(x):
    hi = x.astype(BF16)
    lo = (x - hi.astype(F32)).astype(BF16)
    return hi, lo


def _cparams(n_grid):
    return pltpu.CompilerParams(dimension_semantics=("arbitrary",) * n_grid,
                                vmem_limit_bytes=VMEM_LIMIT_BYTES)


def _t5_bucket(rel):
    n = jnp.maximum(rel, 0)
    exact = N_BUCKETS // 2
    nf = jnp.maximum(n, 1).astype(F32)
    scaled = jnp.log(nf / exact) / math.log(MAX_DIST / exact) * (N_BUCKETS - exact)
    large = exact + jnp.floor(scaled).astype(jnp.int32)
    large = jnp.minimum(large, N_BUCKETS - 1)
    return jnp.where(n < exact, n, large)


def _rowblk_head(rb):
    return NSA_REP * (rb % NSA_KV) + rb // NSA_KV


def _bias_body(tab_ref, o_ref, *, base, step, row_stride, col_stride, lo, hi, cols, scale):
    sub_r = min(o_ref.shape[1], 4 * SUBLANES)
    sub_c = min(cols, LANES)
    shape = (sub_r, sub_c)
    row = lax.broadcasted_iota(jnp.int32, shape, 0)
    col = lax.broadcasted_iota(jnp.int32, shape, 1)
    for r0 in range(0, o_ref.shape[1], sub_r):
        for c0 in range(0, cols, sub_c):
            rel = base + step * pl.program_id(0) + row_stride * (row + r0) + col_stride * (col + c0)
            bucket = _t5_bucket(rel)
            accs = [jnp.zeros(shape, F32)] * N_ROWBLK
            for k in range(N_BUCKETS):
                hit = bucket == k
                accs = [jnp.where(hit, tab_ref[k, _rowblk_head(rb)] * scale, accs[rb]) for rb in range(N_ROWBLK)]
            for rb in range(N_ROWBLK):
                acc = accs[rb]
                if lo is not None:
                    acc = jnp.where((rel >= lo) & (rel < hi), acc, NEG)
                o_ref[rb, r0:r0 + sub_r, c0:c0 + sub_c] = acc


def _bias_table(rel_bias, *, base, step, row_stride, col_stride, rows, cols, steps, lo=None, hi=None, scale=1.0):
    return pl.pallas_call(
        functools.partial(_bias_body, base=base, step=step, row_stride=row_stride, col_stride=col_stride,
                          lo=lo, hi=hi, cols=cols, scale=scale),
        grid=(steps,),
        in_specs=[pl.BlockSpec(memory_space=pltpu.SMEM)],
        out_specs=pl.BlockSpec((None, N_ROWBLK, rows, cols), lambda i: (i, 0, 0, 0)),
        out_shape=jax.ShapeDtypeStruct((steps, N_ROWBLK, rows, cols), F32),
        compiler_params=_cparams(1),
        name="bias_table",
    )(rel_bias)


def _bias_table_sample(rel_bias, base, col_stride, cols):
    return _bias_table(rel_bias, base=base, step=0, row_stride=1, col_stride=-col_stride, rows=T_PAD, cols=cols,
                       steps=1)[0]


def _proj_body(x_ref, g_ref, w_ref, o_ref, *kvt_refs):
    x = x_ref[...]
    ms = jnp.mean(x * x, axis=-1, keepdims=True)
    xn = (x * lax.rsqrt(ms + EPS) * g_ref[...]).astype(BF16)
    o = _nt(xn, w_ref[...])
    o_ref[...] = o
    for c, kvt_ref in enumerate(kvt_refs):
        kvt_ref[...] = o[:, C_KV + c * A_KV:C_KV + (c + 1) * A_KV].T


PROJ_ROWS = 512


def _project(x2d, norm_g, w, seq_len=None):
    m, d = x2d.shape
    n = w.shape[0]
    tm = min(m, PROJ_ROWS)
    out_specs = [pl.BlockSpec((tm, n), lambda i: (i, 0))]
    out_shape = [jax.ShapeDtypeStruct((m, n), F32)]
    if seq_len is not None:
        per_seq = seq_len // tm
        assert seq_len % tm == 0
        out_specs += [pl.BlockSpec((None, A_KV, tm), lambda i: (i // per_seq, 0, i % per_seq))] * N_KV_STREAMS
        out_shape += [jax.ShapeDtypeStruct((m // seq_len, A_KV, seq_len), F32)] * N_KV_STREAMS
    outs = pl.pallas_call(
        _proj_body,
        grid=(m // tm,),
        in_specs=[pl.BlockSpec((tm, d), lambda i: (i, 0)),
                  pl.BlockSpec((1, d), lambda i: (0, 0)),
                  pl.BlockSpec((n, d), lambda i: (0, 0), pipeline_mode=pl.Buffered(1))],
        out_specs=out_specs,
        out_shape=out_shape,
        compiler_params=_cparams(1),
        name="in_proj",
    )(x2d, norm_g.reshape(1, d), w)
    return outs if seq_len is not None else outs[0]


def _permute_heads_rows(w):
    return w.reshape(NSA_KV, NSA_REP, NSA_HD, w.shape[-1]).swapaxes(0, 1).reshape(A_WIDTH, w.shape[-1])


def _arrange_w_in(w_in):
    d = w_in.shape[0]
    wt = w_in.T
    sizes = (A_WIDTH, N_KV_STREAMS * A_KV, 3 * NSA_HEADS, A_WIDTH, 2 * B_KEY + 2 * B_WIDTH + 2 * d)
    offs = np.concatenate([[0], np.cumsum(sizes)])
    q_a, kv_a, g_a, z_a, rest = [wt[offs[i]:offs[i + 1]] for i in range(5)]
    gate_pad = jnp.zeros((2 * LANES - 3 * NSA_HEADS, d), w_in.dtype)
    return jnp.concatenate([_permute_heads_rows(q_a), kv_a, g_a, gate_pad, _permute_heads_rows(z_a), rest],
                           axis=0).astype(BF16)


def _compress(streams, pitch):
    accs = [jnp.zeros((len(srcs) * LANES, LANES), F32) for srcs, _, _, _ in streams]
    for j in range(CMP_LEN):
        start = (j // CMP_STRIDE) * pitch + j % CMP_STRIDE
        for n, (srcs, pos_ref, w1_ref, _) in enumerate(streams):
            xj = jnp.concatenate([r[pl.ds(start, LANES, stride=pitch), :] for r in srcs], axis=0) + pos_ref[j:j + 1, :]
            accs[n] = accs[n] + _nn(xj.astype(BF16), w1_ref[j])
    return [_nn(_silu(acc).astype(BF16), w2_ref[...]) for acc, (_, _, _, w2_ref) in zip(accs, streams)]


def _compress_prompt_body(kc_ref, vc_ref, posk_ref, w1k_ref, w2k_ref, posv_ref, w1v_ref, w2v_ref,
                          ko_ref, vo_ref, kbuf_ref, vbuf_ref):
    t = kc_ref.shape[0]
    for buf_ref, src_ref in ((kbuf_ref, kc_ref), (vbuf_ref, vc_ref)):
        buf_ref[t:, :] = jnp.zeros((buf_ref.shape[0] - t, LANES), F32)
        buf_ref[:t, :] = src_ref[...]
    ko_ref[...], vo_ref[...] = _compress([([kbuf_ref], posk_ref, w1k_ref, w2k_ref),
                                          ([vbuf_ref], posv_ref, w1v_ref, w2v_ref)], CMP_STRIDE)


def _block_diag2(w):
    z = jnp.zeros_like(w)
    return jnp.concatenate([jnp.concatenate([w, z], axis=-1), jnp.concatenate([z, w], axis=-1)], axis=-2)


def _compress_weights(pos, w1, w2):
    pos2 = jnp.concatenate([pos, pos], axis=-1)
    w1bd = _block_diag2(w1.reshape(CMP_LEN, NSA_HD, CMP_HID)).astype(BF16)
    w2bd = _block_diag2(w2).astype(BF16)
    return pos2, w1bd, w2bd


def _const_spec(shape, single_buffer=False):
    nd = len(shape)
    if single_buffer:
        return pl.BlockSpec(shape, lambda *_: (0,) * nd, pipeline_mode=pl.Buffered(1))
    return pl.BlockSpec(shape, lambda *_: (0,) * nd)


def _compress_prompt(proj3, cw_k, cw_v):
    b, t, _ = proj3.shape
    col = C_KV // LANES
    w_specs = [_const_spec(a.shape) for a in cw_k + cw_v]
    return pl.pallas_call(
        _compress_prompt_body,
        grid=(b,),
        in_specs=[pl.BlockSpec((None, t, LANES), lambda i: (i, 0, col)),
                  pl.BlockSpec((None, t, LANES), lambda i: (i, 0, col + 1))] + w_specs,
        out_specs=[pl.BlockSpec((None, LANES, LANES), lambda i: (i, 0, 0))] * 2,
        out_shape=[jax.ShapeDtypeStruct((b, LANES, LANES), F32)] * 2,
        scratch_shapes=[pltpu.VMEM((max(t + CMP_LEN, CMP_SRC_ROWS), LANES), F32)] * 2,
        compiler_params=_cparams(1),
        name="compress_prompt",
    )(proj3, proj3, *cw_k, *cw_v)


def _select_blocks(imp, qpos, nb, axis):
    j = lax.broadcasted_iota(jnp.int32, imp.shape, axis)
    cur = qpos // SEL_BLOCK
    valid = j * SEL_BLOCK <= qpos
    forced = (j == 0) | (j == cur) | (j == cur - 1)
    score = jnp.where(valid, jnp.where(forced, FORCED, imp), -1.0)
    score = jnp.where(j < nb, score, -2.0)
    rank = jnp.zeros(imp.shape, F32)
    for i in range(nb):
        s_i = lax.slice_in_dim(score, i, i + 1, axis=axis)
        tie = jnp.where(i < j, 1.0, 0.0)
        rank = rank + jnp.where(s_i > score, 1.0, jnp.where(s_i == score, tie, 0.0))
    return jnp.where((rank < min(SEL_TOPN, nb)) & (j < nb), 1.0, 0.0)


def _masked_softmax(s, mask):
    s = jnp.where(mask, s, NEG)
    m = jnp.max(s, axis=-1, keepdims=True)
    e = jnp.exp(s - m)
    p = e / jnp.sum(e, axis=-1, keepdims=True)
    return jnp.where(mask, p, 0.0)


def _stack_queries(q_ref, qbd_ref, rows):
    lane = lax.broadcasted_iota(jnp.int32, (rows, LANES), 1)
    for r in range(NSA_REP):
        blk = q_ref[:, r * LANES:(r + 1) * LANES] * (NSA_HD ** -0.5)
        qbd_ref[(2 * r) * rows:(2 * r + 1) * rows, :] = jnp.where(lane < NSA_HD, blk, 0.0).astype(BF16)
        qbd_ref[(2 * r + 1) * rows:(2 * r + 2) * rows, :] = jnp.where(lane >= NSA_HD, blk, 0.0).astype(BF16)


def _gate_and_store(gate_ref, za_ref, o_ref, o_cmp, o_sel, o_win, rows):
    gs = _sigmoid(gate_ref[...])
    lane = lax.broadcasted_iota(jnp.int32, (rows, LANES), 1)
    for r in range(NSA_REP):
        parts = []
        for g in range(NSA_KV):
            rb = 2 * r + g
            h = _rowblk_head(rb)
            o = (gs[:, h:h + 1] * o_cmp[rb] + gs[:, NSA_HEADS + h:NSA_HEADS + h + 1] * o_sel[rb]
                 + gs[:, 2 * NSA_HEADS + h:2 * NSA_HEADS + h + 1] * o_win[rb])
            parts.append(o)
        blk = jnp.where(lane < NSA_HD, parts[0], parts[1])
        o_ref[:, r * LANES:(r + 1) * LANES] = blk * _silu(za_ref[:, r * LANES:(r + 1) * LANES])


SEL_TILES_PER_ITER = 2


def _nsa_prompt_body(q_ref, gate_ref, za_ref, ks_ref, vs_ref, kw_ref, vw_ref, kcmp_ref, vcmp_ref,
                     bcmp_ref, btile_ref, covert_ref, o_ref,
                     qt_ref, ksb_ref, vst_ref, kwb_ref, vwt_ref, vcmpt_ref, mask_ref, s_ref, m_ref, l_ref, acc_ref, res_ref,
                     *, nc, nb):
    qb = pl.program_id(1)
    nkt = ksb_ref.shape[0]
    sub = lax.broadcasted_iota(jnp.int32, (LANES, TQ), 0)
    tok = lax.broadcasted_iota(jnp.int32, (LANES, TQ), 1)

    @pl.when(qb == 0)
    def _():
        for kt in range(nkt):
            rows = slice(kt * TQ, (kt + 1) * TQ)
            ksb_ref[kt] = ks_ref[rows, :].astype(BF16)
            kwb_ref[kt] = kw_ref[rows, :].astype(BF16)
            vst_ref[kt] = vs_ref[rows, :].T.astype(BF16)
            vwt_ref[kt] = vw_ref[rows, :].T.astype(BF16)
        vcmpt_ref[...] = vcmp_ref[...].T.astype(BF16)

    for r in range(NSA_REP):
        qt = (q_ref[:, r * LANES:(r + 1) * LANES] * (NSA_HD ** -0.5 * LOG2E)).T
        qt_ref[2 * r] = jnp.where(sub < NSA_HD, qt, 0.0).astype(BF16)
        qt_ref[2 * r + 1] = jnp.where(sub >= NSA_HD, qt, 0.0).astype(BF16)

    valid = ((CMP_STRIDE * sub + CMP_LEN - 1) <= qb * TQ + tok) & (sub < nc)
    kcb = kcmp_ref[...].astype(BF16)
    psum = [None] * NSA_KV
    for rb in range(N_ROWBLK):
        s = jnp.where(valid, _nn(kcb, qt_ref[rb]) + bcmp_ref[rb], NEG)
        e = jnp.exp2(s - jnp.max(s, axis=0, keepdims=True))
        p = jnp.where(valid, e / jnp.sum(e, axis=0, keepdims=True), 0.0)
        res_ref[0, rb] = _nn(vcmpt_ref[...], p.astype(BF16))
        g = rb % NSA_KV
        psum[g] = p if psum[g] is None else psum[g] + p

    nbp = _round_up(nb, SUBLANES)
    upper = sub < SEL_BLOCK
    for g in range(NSA_KV):
        p_hi, p_lo = _split2(psum[g])
        imp = _nn(covert_ref[...], p_hi) + _nn(covert_ref[...], p_lo)
        sel = _select_blocks(imp[:nbp], qb * TQ + tok[:1], nb, 0)
        for kt in range(nkt):
            picked = jnp.where(upper, sel[2 * kt:2 * kt + 1, :], sel[2 * kt + 1:2 * kt + 2, :]) > 0.5
            mask_ref[kt, g] = jnp.where(kt <= qb, jnp.where(picked, 0.0, NEG), NEG)

    m_ref[...] = jnp.full(m_ref.shape, NEG, F32)
    l_ref[...] = jnp.zeros(l_ref.shape, F32)
    acc_ref[...] = jnp.zeros(acc_ref.shape, F32)

    def sel_scores(pair, buf):
        for u in range(SEL_TILES_PER_ITER):
            kt = jnp.minimum(pair * SEL_TILES_PER_ITER + u, nkt - 1)
            k_t = ksb_ref[kt]
            bidx = jnp.clip(qb - kt, 0, 2)
            for g in range(NSA_KV):
                madd = mask_ref[kt, g]
                for rb in range(g, N_ROWBLK, NSA_KV):
                    s_ref[buf, u, rb] = _nn(k_t, qt_ref[rb]) + btile_ref[bidx, rb] + madd

    def sel_update(pair, buf):
        v_ts = [vst_ref[pair * SEL_TILES_PER_ITER + u] for u in range(SEL_TILES_PER_ITER)]
        for rb in range(N_ROWBLK):
            ss = [s_ref[buf, u, rb] for u in range(SEL_TILES_PER_ITER)]
            m_old = m_ref[rb]
            m_new = m_old
            for s in ss:
                m_new = jnp.maximum(m_new, jnp.max(s, axis=0, keepdims=True))
            alpha = jnp.exp2(m_old - m_new)
            l_new = alpha * l_ref[rb]
            acc = alpha * acc_ref[rb]
            for s, v_t in zip(ss, v_ts):
                p = jnp.exp2(s - m_new)
                l_new = l_new + jnp.sum(p, axis=0, keepdims=True)
                acc = acc + _nn(v_t, p.astype(BF16))
            l_ref[rb] = l_new
            acc_ref[rb] = acc
            m_ref[rb] = m_new

    def sel_body(it, carry):
        sel_scores(2 * it + 1, 1)
        sel_update(2 * it, 0)
        sel_scores(2 * it + 2, 0)
        sel_update(2 * it + 1, 1)
        return carry

    sel_scores(0, 0)
    lax.fori_loop(0, qb // (2 * SEL_TILES_PER_ITER) + 1, sel_body, 0)
    res_ref[1] = acc_ref[...] / l_ref[...]

    n_wt = WINDOW // TQ + 1
    for rb in range(N_ROWBLK):
        qc = qt_ref[rb]
        ss = []
        for dist in range(n_wt):
            kt = jnp.maximum(qb - dist, 0)
            bidx = jnp.where(qb >= dist, dist, n_wt)
            ss.append(_nn(kwb_ref[kt], qc) + btile_ref[bidx, rb])
        m = jnp.max(ss[0], axis=0, keepdims=True)
        for s in ss[1:]:
            m = jnp.maximum(m, jnp.max(s, axis=0, keepdims=True))
        l_w = None
        acc = None
        for dist, s in enumerate(ss):
            p = jnp.exp2(s - m)
            ps = jnp.sum(p, axis=0, keepdims=True)
            pv = _nn(vwt_ref[jnp.maximum(qb - dist, 0)], p.astype(BF16))
            l_w = ps if l_w is None else l_w + ps
            acc = pv if acc is None else acc + pv
        res_ref[2, rb] = acc / l_w

    gt = _sigmoid(gate_ref[...]).T
    for r in range(NSA_REP):
        parts = []
        for g in range(NSA_KV):
            rb = 2 * r + g
            h = _rowblk_head(rb)
            parts.append(gt[h:h + 1, :] * res_ref[0, rb]
                         + gt[NSA_HEADS + h:NSA_HEADS + h + 1, :] * res_ref[1, rb]
                         + gt[2 * NSA_HEADS + h:2 * NSA_HEADS + h + 1, :] * res_ref[2, rb])
        blk = jnp.where(sub < NSA_HD, parts[0], parts[1]).T
        o_ref[:, r * LANES:(r + 1) * LANES] = (blk * _silu(za_ref[:, r * LANES:(r + 1) * LANES])).astype(o_ref.dtype)


def _cover_matrix(nc, nb):
    cs = np.arange(nc) * CMP_STRIDE
    ce = cs + CMP_LEN
    bs = np.arange(nb) * SEL_BLOCK
    be = bs + SEL_BLOCK
    ov = np.clip(np.minimum(ce[:, None], be[None, :]) - np.maximum(cs[:, None], bs[None, :]), 0, None) / CMP_STRIDE
    out = np.zeros((LANES, LANES), np.float32)
    out[:nc, :nb] = ov
    return jnp.asarray(out, BF16)


def _expand_matrix(nkeys):
    j = np.arange(LANES)[:, None]
    k = np.arange(nkeys)[None, :]
    return jnp.asarray((k // SEL_BLOCK == j).astype(np.float32), BF16)


def _nsa_prompt(proj3, kcmp, vcmp, bias_cmp, bias_tiles):
    b, t, _ = proj3.shape
    nq = t // TQ
    nc = (t - CMP_LEN) // CMP_STRIDE + 1
    nb = t // SEL_BLOCK
    assert nc <= LANES and nb <= LANES and t % (TQ * 2 * SEL_TILES_PER_ITER) == 0
    kv_col = C_KV // LANES
    kv_spec = lambda c: pl.BlockSpec((None, t, LANES), lambda i, j: (i, 0, kv_col + c))
    cover_t = _cover_matrix(nc, nb).T
    tile_bf16 = pltpu.VMEM((nq, TQ, LANES), BF16)
    return pl.pallas_call(
        functools.partial(_nsa_prompt_body, nc=nc, nb=nb),
        grid=(b, nq),
        in_specs=[pl.BlockSpec((None, TQ, A_WIDTH), lambda i, j: (i, j, C_Q // A_WIDTH)),
                  pl.BlockSpec((None, TQ, LANES), lambda i, j: (i, j, C_GATE // LANES)),
                  pl.BlockSpec((None, TQ, A_WIDTH), lambda i, j: (i, j, C_ZA // A_WIDTH)),
                  kv_spec(2), kv_spec(3), kv_spec(4), kv_spec(5),
                  pl.BlockSpec((None, LANES, LANES), lambda i, j: (i, 0, 0)),
                  pl.BlockSpec((None, LANES, LANES), lambda i, j: (i, 0, 0)),
                  pl.BlockSpec((None, N_ROWBLK, LANES, TQ), lambda i, j: (j, 0, 0, 0)),
                  _const_spec(bias_tiles.shape), _const_spec(cover_t.shape)],
        out_specs=pl.BlockSpec((None, TQ, A_WIDTH), lambda i, j: (i, j, 0)),
        out_shape=jax.ShapeDtypeStruct((b, t, A_WIDTH), BF16),
        scratch_shapes=[pltpu.VMEM((N_ROWBLK, LANES, TQ), BF16),
                        tile_bf16, tile_bf16, tile_bf16, tile_bf16,
                        pltpu.VMEM((LANES, LANES), BF16),
                        pltpu.VMEM((nq, NSA_KV, TQ, TQ), F32),
                        pltpu.VMEM((2, SEL_TILES_PER_ITER, N_ROWBLK, TQ, TQ), F32),
                        pltpu.VMEM((N_ROWBLK, 1, TQ), F32),
                        pltpu.VMEM((N_ROWBLK, 1, TQ), F32),
                        pltpu.VMEM((N_ROWBLK, LANES, TQ), F32),
                        pltpu.VMEM((3, N_ROWBLK, LANES, TQ), F32)],
        compiler_params=_cparams(2),
        name="nsa_prompt",
    )(proj3, proj3, proj3, proj3, proj3, proj3, proj3, kcmp, vcmp, bias_cmp, bias_tiles, cover_t)


NSA_SEQ_PER_STEP = 4


def _group_page_copy(cache_refs, pt_ref, pbuf, sems, step, slot, s, c, pg, n_pages):
    seq = step * NSA_SEQ_PER_STEP + s
    return pltpu.make_async_copy(cache_refs[c].at[pt_ref[seq, pg]], pbuf.at[slot, s, c * n_pages + pg],
                                 sems.at[slot])


def _nsa_sample_group_body(pt_ref, *refs, n_pages, **static):
    cache_refs = refs[:N_CACHED_STREAMS]
    per_seq = refs[N_CACHED_STREAMS:N_CACHED_STREAMS + 11]
    consts = refs[N_CACHED_STREAMS + 11:N_CACHED_STREAMS + 22]
    outs = refs[N_CACHED_STREAMS + 22:N_CACHED_STREAMS + 25]
    scratch = refs[N_CACHED_STREAMS + 25:-2]
    pbuf, sems = refs[-2:]
    i = pl.program_id(0)
    slot = i % 2

    def all_pages(step, slot_):
        return [_group_page_copy(cache_refs, pt_ref, pbuf, sems, step, slot_, s, c, pg, n_pages)
                for s in range(NSA_SEQ_PER_STEP) for c in range(N_CACHED_STREAMS) for pg in range(n_pages)]

    @pl.when(i == 0)
    def _():
        for cp in all_pages(0, 0):
            cp.start()

    @pl.when(i + 1 < pl.num_programs(0))
    def _():
        for cp in all_pages(i + 1, 1 - slot):
            cp.start()

    for cp in all_pages(i, slot):
        cp.wait()
    pages = [[[pbuf.at[slot, s, c * n_pages + pg] for pg in range(n_pages)] for c in range(N_CACHED_STREAMS)]
             for s in range(NSA_SEQ_PER_STEP)]
    _nsa_sample_group(pages, per_seq, consts, outs, scratch, n_pages=n_pages, **static)


def _nsa_sample_group(pages, per_seq, consts, outs, scratch, *, n_pages, past, wbuf, nc, nb):
    q_ref, kc_ref, vc_ref, ks_ref, vs_ref, kw_ref, vw_ref, gate_ref, za_ref, wk_ref, wv_ref = per_seq
    (posk_ref, w1k_ref, w2k_ref, posv_ref, w1v_ref, w2v_ref, bcmp_ref, bsel_ref, bwin_ref,
     cover_ref, emat_ref) = consts
    o_ref, swk_ref, swv_ref = outs
    kc_all, vc_all, newt_ref, qbd_ref = scratch
    nseq = NSA_SEQ_PER_STEP
    seqs = range(nseq)
    n_keys = bsel_ref.shape[2]
    n_new = T_PAD // 2
    chunks_per_page = PAGE_SIZE // CMP_STRIDE
    nrow = N_ROWBLK * T_PAD

    for dst, cache, new_ref in ((kc_all, 0, kc_ref), (vc_all, 1, vc_ref)):
        for s in seqs:
            for pg in range(n_pages):
                rows = pages[s][cache][pg][...].T
                for c in range(chunks_per_page):
                    r0 = (pg * chunks_per_page + c) * CMP_PITCH
                    dst[s, r0:r0 + CMP_STRIDE, :] = rows[c * CMP_STRIDE:(c + 1) * CMP_STRIDE, :]
            r0 = (past // CMP_STRIDE) * CMP_PITCH
            dst[s, r0:r0 + T_PAD, :] = new_ref[s]
            dst[s, r0 + T_PAD:, :] = jnp.zeros((dst.shape[1] - r0 - T_PAD, LANES), F32)
    kcmp, vcmp = [c.astype(BF16) for c in _compress(
        [([kc_all.at[s] for s in seqs], posk_ref, w1k_ref, w2k_ref),
         ([vc_all.at[s] for s in seqs], posv_ref, w1v_ref, w2v_ref)], CMP_PITCH)]
    per_s = lambda x, s: x[s * LANES:(s + 1) * LANES]

    newt_ref[...] = jnp.zeros(newt_ref.shape, F32)
    for s in seqs:
        for idx, new_ref in enumerate((ks_ref, vs_ref, kw_ref, vw_ref)):
            newt_ref[s, idx, 0:T_PAD, :] = new_ref[s]
        _stack_queries(q_ref.at[s], qbd_ref.at[s], T_PAD)
    qbd = [qbd_ref[s] for s in seqs]
    tok = lax.broadcasted_iota(jnp.int32, (T_PAD, LANES), 0)
    lane = lax.broadcasted_iota(jnp.int32, (T_PAD, LANES), 1)
    qpos = past + tok
    stack = lambda xs: jnp.concatenate(xs, axis=0)
    by_head = lambda x: x.reshape(nseq, NSA_REP, NSA_KV, T_PAD, LANES)

    s_c = by_head(stack([_nt(qbd[s], per_s(kcmp, s)) for s in seqs])) + _by_head_bias(bcmp_ref[...])
    mask_c = ((CMP_STRIDE * lane + CMP_LEN - 1) <= qpos) & (lane < nc)
    p = _masked_softmax(s_c, mask_c[None, None, None]).reshape(nseq, nrow, LANES)
    o_cmp = [_nn(p[s].astype(BF16), per_s(vcmp, s)).reshape(N_ROWBLK, T_PAD, LANES) for s in seqs]

    psum = p.reshape(nseq, NSA_REP, NSA_KV * T_PAD, LANES).sum(axis=1).reshape(nseq * NSA_KV * T_PAD, LANES)
    p_hi, p_lo = _split2(psum)
    imp = _nn(p_hi, cover_ref[...]) + _nn(p_lo, cover_ref[...])
    sel = _select_blocks(imp.reshape(nseq, NSA_KV, T_PAD, LANES), qpos[None, None], nb, 3)
    selk = _nn(sel.reshape(nseq * NSA_KV * T_PAD, LANES).astype(BF16), emat_ref[...])
    selk = selk.reshape(nseq, NSA_KV, T_PAD, n_keys)

    def softmax_pv(scores, values):
        m = scores[0]
        for s_t in scores[1:]:
            m = jnp.maximum(m, s_t)
        m = jnp.max(m, axis=-1, keepdims=True)
        acc = [None] * nseq
        tot = None
        for s_t, (v_ts, channel_major) in zip(scores, values):
            p_t = jnp.exp(s_t - m)
            pb = p_t.reshape(nseq, nrow, LANES).astype(BF16)
            for s in seqs:
                pv = _nt(pb[s], v_ts[s]) if channel_major else _nn(pb[s], v_ts[s])
                acc[s] = pv if acc[s] is None else acc[s] + pv
            tot = p_t if tot is None else tot + p_t
        inv = 1.0 / jnp.sum(tot, axis=-1, keepdims=True).reshape(nseq, nrow, 1)
        return [(acc[s] * inv[s]).reshape(N_ROWBLK, T_PAD, LANES) for s in seqs]

    scores, values = [], []
    for t in range(n_pages + 1):
        cols = slice(t * LANES, (t + 1) * LANES)
        allowed = ((t * LANES + lane) <= qpos)[None, None] & (selk[:, :, :, cols] > 0.5)
        madd = jnp.where(allowed, 0.0, NEG)[:, None]
        if t < n_pages:
            raw = [_nn(qbd[s], pages[s][2][t][...].astype(BF16)) for s in seqs]
            values.append(([pages[s][3][t][...].astype(BF16) for s in seqs], True))
        else:
            raw = [_nt(qbd[s], newt_ref[s, 0].astype(BF16)) for s in seqs]
            values.append(([newt_ref[s, 1].astype(BF16) for s in seqs], False))
        scores.append(by_head(stack(raw)) + _by_head_bias(bsel_ref[:, :, cols]) + madd)
    o_sel = softmax_pv(scores, values)

    scores, values = [], []
    kwt = [wk_ref[s].astype(BF16) for s in seqs]
    vwt = [wv_ref[s].astype(BF16) for s in seqs]
    for t in range(wbuf // LANES + 1):
        cols = slice(t * LANES, (t + 1) * LANES)
        rel_w = wbuf + tok - (t * LANES + lane)
        madd = jnp.where((rel_w >= 0) & (rel_w < WINDOW), 0.0, NEG)[None, None, None]
        if t < wbuf // LANES:
            raw = [_nn(qbd[s], kwt[s][:, cols]) for s in seqs]
            values.append(([vwt[s][:, cols] for s in seqs], True))
        else:
            raw = [_nt(qbd[s], newt_ref[s, 2].astype(BF16)) for s in seqs]
            values.append(([newt_ref[s, 3].astype(BF16) for s in seqs], False))
        scores.append(by_head(stack(raw)) + _by_head_bias(bwin_ref[:, :, cols]) + madd)
    o_win = softmax_pv(scores, values)

    lane_full = lax.broadcasted_iota(jnp.int32, (LANES, LANES), 1)
    for s in seqs:
        _gate_and_store(gate_ref.at[s], za_ref.at[s], o_ref.at[s], o_cmp[s], o_sel[s], o_win[s], T_PAD)
        for idx, w_ref, out_ref in ((2, wk_ref, swk_ref), (3, wv_ref, swv_ref)):
            rolled = pltpu.roll(w_ref[s], wbuf - n_new, axis=1)
            tail = pltpu.roll(newt_ref[s, idx], LANES - n_new, axis=0).T
            out_ref[s, :, :wbuf - LANES] = rolled[:, :wbuf - LANES]
            out_ref[s, :, wbuf - LANES:] = jnp.where(lane_full >= LANES - n_new, tail, rolled[:, wbuf - LANES:])


def _by_head_bias(b):
    return b.reshape(1, NSA_REP, NSA_KV, T_PAD, b.shape[-1])


def _round_up(x, m):
    return (x + m - 1) // m * m


def _nsa_sample(projs3, caches, wk, wv, page_table, cw_k, cw_v, rel_bias, n_new):
    nseq, n_pages = page_table.shape
    past = n_pages * PAGE_SIZE
    wbuf = wk.shape[2]
    length = past + n_new
    nc = (length - CMP_LEN) // CMP_STRIDE + 1
    nb = -(-length // SEL_BLOCK)
    n_keys = past + LANES
    n_win = wbuf + LANES
    pitched = lambda p: (p // CMP_STRIDE) * CMP_PITCH + p % CMP_STRIDE
    n_cmp_rows = _round_up(max(pitched(past) + T_PAD, pitched(CMP_SRC_ROWS - 1) + 1), SUBLANES)
    assert nc <= LANES and nb <= LANES and n_new == T_PAD // 2 and nb * SEL_BLOCK <= n_keys
    assert wbuf % LANES == 0 and PAGE_SIZE == LANES

    bias_cmp = _bias_table_sample(rel_bias, past - (CMP_LEN - 1), CMP_STRIDE, LANES)
    bias_sel = _bias_table_sample(rel_bias, past, 1, n_keys)
    bias_win = _bias_table_sample(rel_bias, wbuf, 1, n_win)
    cover = _cover_matrix(nc, nb)
    emat = _expand_matrix(n_keys)

    sps = NSA_SEQ_PER_STEP
    assert nseq % sps == 0

    def new_spec(width, col):
        return pl.BlockSpec((sps, T_PAD, width), lambda i, pt: (i, 0, col))

    kv_col = C_KV // LANES
    in_specs = [pl.BlockSpec(memory_space=pl.ANY)] * N_CACHED_STREAMS
    in_specs += [new_spec(A_WIDTH, C_Q // A_WIDTH)]
    in_specs += [new_spec(LANES, kv_col + c) for c in range(N_KV_STREAMS)]
    in_specs += [new_spec(LANES, C_GATE // LANES), new_spec(A_WIDTH, C_ZA // A_WIDTH)]
    win_spec = pl.BlockSpec((sps, LANES, wbuf), lambda i, pt: (i, 0, 0))
    in_specs += [win_spec] * 2
    consts = list(cw_k) + list(cw_v) + [bias_cmp, bias_sel, bias_win, cover, emat]
    in_specs += [_const_spec(a.shape, single_buffer=True) for a in consts]
    operands = list(caches) + [projs3] * 9 + [wk, wv] + consts

    grid_spec = pltpu.PrefetchScalarGridSpec(
        num_scalar_prefetch=1,
        grid=(nseq // sps,),
        in_specs=in_specs,
        out_specs=[pl.BlockSpec((sps, T_PAD, A_WIDTH), lambda i, pt: (i, 0, 0)), win_spec, win_spec],
        scratch_shapes=[pltpu.VMEM((sps, n_cmp_rows, LANES), F32), pltpu.VMEM((sps, n_cmp_rows, LANES), F32),
                        pltpu.VMEM((sps, 4, LANES, LANES), F32),
                        pltpu.VMEM((sps, N_ROWBLK * T_PAD, LANES), BF16),
                        pltpu.VMEM((2, sps, N_CACHED_STREAMS * n_pages, LANES, PAGE_SIZE), F32),
                        pltpu.SemaphoreType.DMA((2,))])
    return pl.pallas_call(
        functools.partial(_nsa_sample_group_body, n_pages=n_pages, past=past, wbuf=wbuf, nc=nc, nb=nb),
        grid_spec=grid_spec,
        out_shape=[jax.ShapeDtypeStruct((nseq, T_PAD, A_WIDTH), F32),
                   jax.ShapeDtypeStruct((nseq, LANES, wbuf), F32),
                   jax.ShapeDtypeStruct((nseq, LANES, wbuf), F32)],
        compiler_params=_cparams(1),
        name="nsa_sample",
    )(page_table, *operands)


def _hgrn_constants(rows, seg_stride, seg_len):
    idx = np.arange(rows)
    seg = idx // seg_stride
    pos = idx % seg_stride
    same = seg[:, None] == seg[None, :]
    pt, pu = pos[:, None], pos[None, :]
    mats = [same & (pu <= pt),
            same & (pu > pt) & (pu <= seg_len - 1)]
    levels = []
    m = seg_len // 2
    while m >= 2:
        levels.append(m)
        m //= 2
    for m in levels:
        mats.append(same & (pu >= (pt // m) * m) & (pu <= pt))
        mats.append(same & (pu > pt) & (pu <= (pt // m) * m + m - 1))
    wstack = np.concatenate(mats, axis=0).astype(np.float32)
    wstack = np.concatenate([wstack] * 3, axis=1)
    masks = []
    for m in levels + [1]:
        masks.append(same & (pt // (2 * m) == pu // (2 * m)) & ((pt // m) % 2 == 1) & ((pu // m) % 2 == 0))
    masks.append(idx[:, None] == idx[None, :])
    masks = np.stack(masks).astype(np.float32)
    return jnp.asarray(wstack, BF16), jnp.asarray(masks, F32), len(levels)


def _hgrn_gates(f_pre, hl_ref, wst_ref):
    hl = hl_ref[...]
    e = jnp.exp(hl - jnp.max(hl, axis=0, keepdims=True))
    lb = e[0:1] / jnp.sum(e, axis=0, keepdims=True)
    f = lb + (1.0 - lb) * _sigmoid(f_pre)
    g = jnp.log(f)
    k = 1.0 - f
    g_hi = g.astype(BF16)
    r1 = g - g_hi.astype(F32)
    g_mid = r1.astype(BF16)
    g_lo = (r1 - g_mid.astype(F32)).astype(BF16)
    sums = _nn(wst_ref[...], jnp.concatenate([g_hi, g_mid, g_lo], axis=0))
    return f, k, sums


def _hgrn_head(q, f, k, sums, masks_ref, hs, n_levels):
    rows = q.shape[0]
    qh, fh, kh = q[:, hs], f[:, hs], k[:, hs]
    part = lambda i: sums[i * rows:(i + 1) * rows, hs]
    big_g = part(0)
    qg = qh * jnp.exp(big_g)
    kd = kh * jnp.exp(part(1))
    a = None
    for li in range(n_levels + 2):
        if li < n_levels:
            ql, kl = qh * jnp.exp(part(2 + 2 * li)), kh * jnp.exp(part(3 + 2 * li))
        elif li == n_levels:
            ql, kl = qh * fh, kh
        else:
            ql, kl = qh, kh
        term = masks_ref[li] * _nt(ql.astype(BF16), kl.astype(BF16))
        a = term if a is None else a + term
    return big_g, qg, kd, a


def _hgrn_out(o, ng, z):
    o = o * lax.rsqrt(jnp.mean(o * o, axis=-1, keepdims=True) + EPS)
    return o * ng * _silu(z)


def _hgrn_prompt_body(q_ref, f_ref, i_ref, z_ref, hl_ref, ng_ref, wst_ref, masks_ref, o_ref, s_ref, st_ref,
                      *, n_levels, rows):
    c = pl.program_id(1)

    @pl.when(c == 0)
    def _():
        st_ref[...] = jnp.zeros(st_ref.shape, F32)

    chunks = []
    for ci in range(q_ref.shape[0] // rows):
        rs = slice(ci * rows, (ci + 1) * rows)
        v = i_ref[rs, :]
        q = q_ref[rs, :]
        f, k, sums = _hgrn_gates(f_ref[rs, :], hl_ref, wst_ref)
        per_head = []
        for h in range(HG_HEADS):
            hs = slice(h * HG_DK, (h + 1) * HG_DK)
            vb = v[:, hs].astype(BF16)
            big_g, qg, kd, a = _hgrn_head(q, f, k, sums, masks_ref, hs, n_levels)
            per_head.append((qg.astype(BF16), _nn(a.astype(BF16), vb),
                             jnp.exp(big_g[rows - 1:rows]), _tn(vb, kd.astype(BF16))))
        chunks.append(per_head)

    states = [st_ref[h] for h in range(HG_HEADS)]
    for ci, per_head in enumerate(chunks):
        rs = slice(ci * rows, (ci + 1) * rows)
        for h in range(HG_HEADS):
            hs = slice(h * HG_DK, (h + 1) * HG_DK)
            qgb, o_intra, decay, kv = per_head[h]
            o = _nt(qgb, states[h].astype(BF16)) + o_intra
            o_ref[rs, hs] = _hgrn_out(o, ng_ref[:, hs], z_ref[rs, hs]).astype(o_ref.dtype)
            states[h] = states[h] * decay + kv
    for h in range(HG_HEADS):
        st_ref[h] = states[h]

    @pl.when(c == pl.num_programs(1) - 1)
    def _():
        for h in range(HG_HEADS):
            s_ref[h] = states[h].T


HGRN_CHUNKS_PER_STEP = 16


def _hgrn_prompt(proj3, hg_lower, hg_norm_g):
    b, t, _ = proj3.shape
    chunk = math.gcd(t, HG_CHUNK)
    wst, masks, n_levels = _hgrn_constants(chunk, chunk, chunk)
    rows = chunk * math.gcd(t // chunk, HGRN_CHUNKS_PER_STEP)
    col = lambda c: pl.BlockSpec((None, rows, B_KEY), lambda i, j: (i, j, c // B_KEY))
    return pl.pallas_call(
        functools.partial(_hgrn_prompt_body, n_levels=n_levels, rows=chunk),
        grid=(b, t // rows),
        in_specs=[col(C_QB), col(C_FB), col(C_IB), col(C_ZB),
                  _const_spec(hg_lower.shape), _const_spec(hg_norm_g.shape),
                  _const_spec(wst.shape), _const_spec(masks.shape)],
        out_specs=[pl.BlockSpec((None, rows, B_WIDTH), lambda i, j: (i, j, 0)),
                   pl.BlockSpec((None, HG_HEADS, HG_DK, HG_DV), lambda i, j: (i, 0, 0, 0))],
        out_shape=[jax.ShapeDtypeStruct((b, t, B_WIDTH), BF16),
                   jax.ShapeDtypeStruct((b, HG_HEADS, HG_DK, HG_DV), F32)],
        scratch_shapes=[pltpu.VMEM((HG_HEADS, HG_DV, HG_DK), F32)],
        compiler_params=_cparams(2),
        name="hgrn_prompt",
    )(proj3, proj3, proj3, proj3, hg_lower, hg_norm_g, wst, masks)


SEQ_PER_STEP = 16


def _hgrn_sample_body(q_ref, f_ref, i_ref, z_ref, s0_ref, hl_ref, ng_ref, wst_ref, masks_ref, o_ref, s_ref,
                      *, n_levels, n_new):
    v = i_ref[...]
    q = q_ref[...]
    f, k, sums = _hgrn_gates(f_ref[...], hl_ref, wst_ref)
    for h in range(HG_HEADS):
        hs = slice(h * HG_DK, (h + 1) * HG_DK)
        vb = v[:, hs].astype(BF16)
        big_g, qg, kd, a = _hgrn_head(q, f, k, sums, masks_ref, hs, n_levels)
        o_intra = _nn(a.astype(BF16), vb)
        for s in range(SEQ_PER_STEP):
            rs = slice(s * T_PAD, (s + 1) * T_PAD)
            s0 = s0_ref[s, h]
            o = _nn(qg[rs].astype(BF16), s0.astype(BF16)) + o_intra[rs]
            o_ref[rs, hs] = _hgrn_out(o, ng_ref[:, hs], z_ref[rs, hs])
            decay = jnp.exp(big_g[rs]).T[:, n_new - 1:n_new]
            s_ref[s, h] = s0 * decay + _tn(kd[rs].astype(BF16), v[rs, hs].astype(BF16))


def _hgrn_sample(projs, state, hg_lower, hg_norm_g, n_new):
    nseq = state.shape[0]
    rows = SEQ_PER_STEP * T_PAD
    wst, masks, n_levels = _hgrn_constants(rows, T_PAD, n_new)
    col = lambda c: pl.BlockSpec((rows, B_KEY), lambda i: (i, c // B_KEY))
    st_spec = pl.BlockSpec((SEQ_PER_STEP, HG_HEADS, HG_DK, HG_DV), lambda i: (i, 0, 0, 0))
    return pl.pallas_call(
        functools.partial(_hgrn_sample_body, n_levels=n_levels, n_new=n_new),
        grid=(nseq // SEQ_PER_STEP,),
        in_specs=[col(C_QB), col(C_FB), col(C_IB), col(C_ZB), st_spec,
                  _const_spec(hg_lower.shape), _const_spec(hg_norm_g.shape),
                  _const_spec(wst.shape), _const_spec(masks.shape)],
        out_specs=[pl.BlockSpec((rows, B_WIDTH), lambda i: (i, 0)), st_spec],
        out_shape=[jax.ShapeDtypeStruct((nseq * T_PAD, B_WIDTH), F32),
                   jax.ShapeDtypeStruct(state.shape, F32)],
        compiler_params=_cparams(1),
        name="hgrn_sample",
    )(projs, projs, projs, projs, state, hg_lower, hg_norm_g, wst, masks)


def _merge_body(oa_ref, ob_ref, ma_ref, mb_ref, x_ref, wba_ref, wbb_ref, wout_ref, fg_ref, y_ref):
    a = _nn(oa_ref[...].astype(BF16), wba_ref[...])
    b = _nn(ob_ref[...].astype(BF16), wbb_ref[...])
    y = _sigmoid(ma_ref[...]) * a + _sigmoid(mb_ref[...]) * b
    h = x_ref[...] + _nn(y.astype(BF16), wout_ref[...])
    ms = jnp.mean(h * h, axis=-1, keepdims=True)
    y_ref[...] = h * lax.rsqrt(ms + EPS) * fg_ref[...]


MERGE_ROWS = 1024


def _merge(o_a, o_b, proj, x2d, w_ba, w_bb, w_out, final_g):
    m, d = x2d.shape
    tm = min(m, MERGE_ROWS)
    row = lambda w, c: pl.BlockSpec((tm, w), lambda i: (i, c))
    return pl.pallas_call(
        _merge_body,
        grid=(m // tm,),
        in_specs=[row(A_WIDTH, 0), row(B_WIDTH, 0), row(d, C_MA // d), row(d, C_MA // d + 1), row(d, 0),
                  _const_spec(w_ba.shape), _const_spec(w_bb.shape), _const_spec(w_out.shape),
                  _const_spec((1, d))],
        out_specs=row(d, 0),
        out_shape=jax.ShapeDtypeStruct((m, d), F32),
        compiler_params=_cparams(1),
        name="merge_out",
    )(o_a, o_b, proj, proj, x2d, w_ba, w_bb, w_out, final_g.reshape(1, d))


def kernel(x_prompt, x_sample, cache_cmp_k, cache_cmp_v, cache_sel_k, cache_sel_v, state_win_k, state_win_v,
           state_hgrn, page_table, norm_g, w_in, cmp_pos_k, cmp_w1_k, cmp_w2_k, cmp_pos_v, cmp_w1_v, cmp_w2_v,
           rel_bias, hg_lower, hg_norm_g, w_branch_a, w_branch_b, w_out, final_g):
    depth = norm_g.shape[0]
    assert depth == 1, "single-layer trunk"
    b, t, d = x_prompt.shape
    nseq, n_new, _ = x_sample.shape
    assert C_MA % d == 0

    w = _arrange_w_in(w_in[0])
    n_cols = w.shape[0]
    w_ba = _permute_heads_rows(w_branch_a[0]).astype(BF16)
    w_bb = w_branch_b[0].astype(BF16)
    w_o = w_out[0].astype(BF16)
    cw_k = _compress_weights(cmp_pos_k[0], cmp_w1_k[0], cmp_w2_k[0])
    cw_v = _compress_weights(cmp_pos_v[0], cmp_w1_v[0], cmp_w2_v[0])
    ng = hg_norm_g[0].reshape(1, B_WIDTH)
    kv_cols = lambda a, c: a[..., C_KV + c * A_KV:C_KV + (c + 1) * A_KV]

    xp2 = x_prompt.reshape(b * t, d)
    proj, *kv_t = _project(xp2, norm_g[0], w, seq_len=t)
    proj3 = proj.reshape(b, t, n_cols)
    nc_p = (t - CMP_LEN) // CMP_STRIDE + 1
    kcmp, vcmp = _compress_prompt(proj3, cw_k, cw_v)
    bias_cmp = _bias_table(rel_bias, base=-(CMP_LEN - 1), step=TQ, row_stride=-CMP_STRIDE, col_stride=1,
                           rows=LANES, cols=TQ, steps=t // TQ, scale=LOG2E)
    bias_tiles = _bias_table(rel_bias, base=0, step=TQ, row_stride=-1, col_stride=1, rows=TQ, cols=TQ,
                             steps=WINDOW // TQ + 2, lo=0, hi=WINDOW, scale=LOG2E)
    o_a = _nsa_prompt(proj3, kcmp, vcmp, bias_cmp, bias_tiles)
    o_b, p_hgrn = _hgrn_prompt(proj3, hg_lower, ng)
    y_prompt = _merge(o_a.reshape(b * t, A_WIDTH), o_b.reshape(b * t, B_WIDTH), proj, xp2,
                      w_ba, w_bb, w_o, final_g).reshape(b, t, d)
    wb_p = min(WINDOW, t)
    chan_major = lambda a: jnp.transpose(a, (0, 2, 3, 1)).reshape(a.shape[0], A_KV, a.shape[1])
    row_major = lambda a: jnp.transpose(a.reshape(a.shape[0], NSA_KV, NSA_HD, a.shape[2]), (0, 3, 1, 2))
    p_kv = ([row_major(kv_t[c])[None] for c in range(N_CACHED_STREAMS)]
            + [row_major(kv_t[c][:, :, t - wb_p:])[None] for c in range(N_CACHED_STREAMS, N_KV_STREAMS)])
    p_state = p_kv + [p_hgrn[None]]

    xs_pad = jnp.pad(x_sample, ((0, 0), (0, T_PAD - n_new), (0, 0))).reshape(nseq * T_PAD, d)
    projs = _project(xs_pad, norm_g[0], w)
    projs3 = projs.reshape(nseq, T_PAD, n_cols)
    pool = cache_cmp_k.shape[1]
    caches =[chan_major(c[0]) for c in (cache_cmp_k, cache_cmp_v, cache_sel_k, cache_sel_v)]
    wbuf = state_win_k.shape[2]
    o_as, s_win_k, s_win_v = _nsa_sample(projs3, caches, chan_major(state_win_k[0]), chan_major(state_win_v[0]),
                                         page_table, cw_k, cw_v, rel_bias, n_new)
    s_win_k, s_win_v = row_major(s_win_k), row_major(s_win_v)
    o_bs, s_hgrn = _hgrn_sample(projs, state_hgrn[0], hg_lower, ng, n_new)
    y_s = _merge(o_as.reshape(nseq * T_PAD, A_WIDTH), o_bs, projs, xs_pad, w_ba, w_bb, w_o, final_g)
    y_sample = y_s.reshape(nseq, T_PAD, d)[:, :n_new]
    s_kv = [kv_cols(projs3, c)[:, :n_new].reshape(1, nseq, n_new, NSA_KV, NSA_HD) for c in range(N_CACHED_STREAMS)]
    s_state = s_kv + [s_win_k.reshape(1, nseq, wbuf, NSA_KV, NSA_HD),
                      s_win_v.reshape(1, nseq, wbuf, NSA_KV, NSA_HD), s_hgrn[None]]

    return (y_prompt, y_sample, *p_state, *s_state)
```
